```python
import jax, jax.numpy as jnp
from jax import lax
import numpy as np

D_MODEL = 1024
BATCH = 8
SEQ = 8192
DEPTH = 2

GRID_W = 64
N_MEM = 256
HEAD_DIM = 64
NA_HEADS = 8
NA_WIDTH = NA_HEADS * HEAD_DIM
NA_KH = 8
NA_KW = 16
RW_HEADS = 4
RW_WIDTH = RW_HEADS * HEAD_DIM
RW_LORA_W = 64
RW_LORA_A = 64
RW_LORA_G = 128
RW_PROJ = 3 * RW_WIDTH + 2 * RW_LORA_W + 2 * RW_LORA_A + RW_LORA_G
MEM_HEADS = 4
MEM_WIDTH = MEM_HEADS * HEAD_DIM
D_MIX = NA_WIDTH + RW_WIDTH + MEM_WIDTH
P_IN = 3 * NA_WIDTH + RW_PROJ + MEM_WIDTH
MOE_GROUPS = 4
MOE_EXPERTS_PER_GROUP = 4
MOE_EXPERTS = MOE_GROUPS * MOE_EXPERTS_PER_GROUP
MOE_TOP_K = 2
MOE_D_FF = 512
MOE_BLOCK = 128
RMS_EPS = 1e-6
RW_GN_EPS = 64e-5

kernel_name = 'hybrid_natten_rwkv7_hmoe_encoder'


def rms_norm(x, w):
    xf = x.astype(jnp.float32)
    y = xf * lax.rsqrt(jnp.mean(xf * xf, axis=-1, keepdims=True) + RMS_EPS)
    return (y * w.astype(jnp.float32)).astype(x.dtype)


def neighborhood_attention(q, k, v, rpb):
    B, S, H, dh = q.shape
    rows = S // GRID_W
    kh = min(NA_KH, rows)
    col = np.arange(GRID_W)
    col_start = np.clip(col - NA_KW // 2, 0, GRID_W - NA_KW)
    col_idx = col_start[:, None] + np.arange(NA_KW)[None, :]
    dc = col_idx - col[:, None]
    qg = q.reshape(B, rows, GRID_W, H, dh)
    kg = k.reshape(B, rows, GRID_W, H, dh)
    vg = v.reshape(B, rows, GRID_W, H, dh)
    scale = HEAD_DIM ** -0.5

    def one_row(r):
        rs = jnp.clip(r - kh // 2, 0, rows - kh)
        k_rows = lax.dynamic_slice_in_dim(kg, rs, kh, axis=1)
        v_rows = lax.dynamic_slice_in_dim(vg, rs, kh, axis=1)
        k_nb = k_rows[:, :, col_idx]
        v_nb = v_rows[:, :, col_idx]
        q_r = lax.dynamic_index_in_dim(qg, r, axis=1, keepdims=False)
        dr = rs + jnp.arange(kh) - r
        bias = rpb[:, dr[:, None, None] + NA_KH - 1, dc[None, :, :] + NA_KW - 1]
        s = jnp.einsum('bqhd,bnqmhd->bhqnm', q_r, k_nb).astype(jnp.float32) * scale
        s = s + jnp.transpose(bias, (0, 2, 1, 3)).astype(jnp.float32)[None]
        p = jax.nn.softmax(s.reshape(B, H, GRID_W, kh * NA_KW), axis=-1)
        p = p.reshape(B, H, GRID_W, kh, NA_KW).astype(v.dtype)
        return jnp.einsum('bhqnm,bnqmhd->bqhd', p, v_nb)

    out = lax.map(one_row, jnp.arange(rows))
    return jnp.transpose(out, (1, 0, 2, 3, 4)).reshape(B, S, H * dh)


def centred_shift(s, mu_prev, mu_next):
    prev = jnp.pad(s, ((0, 0), (1, 0), (0, 0)))[:, :-1]
    nxt = jnp.pad(s, ((0, 0), (0, 1), (0, 0)))[:, 1:]
    return s + mu_prev * (prev - s) + mu_next * (nxt - s)


def wkv7_scan(r, w, k, v, kk, a, reverse):
    B, S, H, N = r.shape

    def step(state, inp):
        r_t, w_t, k_t, v_t, kk_t, a_t = inp
        sa = jnp.einsum('bhvk,bhk->bhv', state, -kk_t)
        state = (state * w_t[:, :, None, :]
                 + sa[..., None] * (kk_t * a_t)[:, :, None, :]
                 + v_t[..., None] * k_t[:, :, None, :])
        return state, jnp.einsum('bhvk,bhk->bhv', state, r_t)

    xs = tuple(jnp.swapaxes(t, 0, 1) for t in (r, w, k, v, kk, a))
    state0 = jnp.zeros((B, H, N, N), jnp.float32)
    _, y = lax.scan(step, state0, xs, reverse=reverse)
    return jnp.swapaxes(y, 0, 1)


def rwkv7_bidirectional(rw, mu_prev, mu_next, w0, w2, a0, a2, g2, k_k, k_a, r_k, ln_w, ln_b):
    f32 = jnp.float32
    B, S, _ = rw.shape
    rw = centred_shift(rw.astype(f32), mu_prev.astype(f32), mu_next.astype(f32))
    splits = [RW_WIDTH, 2 * RW_WIDTH, 3 * RW_WIDTH,
              3 * RW_WIDTH + RW_LORA_W, 3 * RW_WIDTH + 2 * RW_LORA_W,
              3 * RW_WIDTH + 2 * RW_LORA_W + RW_LORA_A,
              3 * RW_WIDTH + 2 * RW_LORA_W + 2 * RW_LORA_A]
    r, k, v, lw_f, lw_b, la_f, la_b, lg = jnp.split(rw, splits, axis=-1)
    w0, w2, a0, a2, g2 = (t.astype(f32) for t in (w0, w2, a0, a2, g2))

    def heads(t):
        return t.reshape(B, S, RW_HEADS, HEAD_DIM)

    g = jax.nn.sigmoid(lg) @ g2
    kk = heads(k * k_k.astype(f32))
    kk = kk / jnp.maximum(jnp.sqrt(jnp.sum(kk * kk, axis=-1, keepdims=True)), 1e-12)
    rh, vh = heads(r), heads(v)
    ys = []
    kds = []
    for d, (lw, la, rev) in enumerate(((lw_f, la_f, False), (lw_b, la_b, True))):
        wl = -jax.nn.softplus(-(w0[d] + jnp.tanh(lw) @ w2[d])) - 0.5
        decay = jnp.exp(-jnp.exp(wl))
        a = jax.nn.sigmoid(a0[d] + la @ a2[d])
        kd = k * (1.0 + (a - 1.0) * k_a.astype(f32))
        ys.append(wkv7_scan(rh, heads(decay), heads(kd), vh, kk, heads(a), rev))
        kds.append(kd)
    y = ys[0] + ys[1]
    mu = jnp.mean(y, axis=-1, keepdims=True)
    var = jnp.mean((y - mu) ** 2, axis=-1, keepdims=True)
    yn = ((y - mu) * lax.rsqrt(var + RW_GN_EPS)).reshape(B, S, RW_WIDTH)
    yn = yn * ln_w.astype(f32) + ln_b.astype(f32)
    bonus = jnp.sum(rh * heads(kds[0] + kds[1]) * r_k.astype(f32), axis=-1, keepdims=True) * vh
    return (yn + bonus.reshape(B, S, RW_WIDTH)) * g


def memory_cross_attention(mq, mem_n, w_mem_kv, q_norm_w, k_norm_w):
    B, S, _ = mq.shape
    M = mem_n.shape[1]
    q = rms_norm(mq.reshape(B, S, MEM_HEADS, HEAD_DIM), q_norm_w)
    k, v = jnp.split(mem_n @ w_mem_kv, 2, axis=-1)
    k = rms_norm(k.reshape(B, M, MEM_HEADS, HEAD_DIM), k_norm_w)
    v = v.reshape(B, M, MEM_HEADS, HEAD_DIM)
    s = jnp.einsum('bshd,bmhd->bhsm', q, k).astype(jnp.float32) * (HEAD_DIM ** -0.5)
    p = jax.nn.softmax(s, axis=-1).astype(v.dtype)
    return jnp.einsum('bhsm,bmhd->bshd', p, v).reshape(B, S, MEM_WIDTH)


def hybrid_mixer(h, mem_n, w_in, na_q_norm_w, na_k_norm_w, na_rpb,
                 rw_mu_prev, rw_mu_next, rw_w0, rw_w2, rw_a0, rw_a2, rw_g2,
                 rw_k_k, rw_k_a, rw_r_k, rw_ln_w, rw_ln_b,
                 w_mem_kv, mem_q_norm_w, mem_k_norm_w, w_out):
    B, S, _ = h.shape
    proj = h @ w_in
    na_qkv, rw, mq = jnp.split(proj, [3 * NA_WIDTH, 3 * NA_WIDTH + RW_PROJ], axis=-1)
    q, k, v = [t.reshape(B, S, NA_HEADS, HEAD_DIM) for t in jnp.split(na_qkv, 3, axis=-1)]
    q = rms_norm(q, na_q_norm_w)
    k = rms_norm(k, na_k_norm_w)
    y_na = neighborhood_attention(q, k, v, na_rpb)
    y_rw = rwkv7_bidirectional(rw, rw_mu_prev, rw_mu_next, rw_w0, rw_w2, rw_a0, rw_a2, rw_g2,
                               rw_k_k, rw_k_a, rw_r_k, rw_ln_w, rw_ln_b).astype(h.dtype)
    y_mem = memory_cross_attention(mq, mem_n, w_mem_kv, mem_q_norm_w, mem_k_norm_w)
    return jnp.concatenate([y_na, y_rw, y_mem], axis=-1) @ w_out


def hierarchical_moe(h, w_group, b_group, w_expert, b_expert, w_gate, w_up, w_down):
    B, S, D = h.shape
    n = B * S
    h2 = h.reshape(n, D)
    f32 = jnp.float32
    glogits = (h2 @ w_group).astype(f32) + b_group.astype(f32)
    g_idx = jnp.argmax(glogits, axis=-1)
    g_w = jnp.take_along_axis(jax.nn.softmax(glogits, axis=-1), g_idx[:, None], axis=1)[:, 0]
    elogits = ((h2 @ w_expert).astype(f32) + b_expert.astype(f32)).reshape(n, MOE_GROUPS, MOE_EXPERTS_PER_GROUP)
    sel = jnp.take_along_axis(elogits, g_idx[:, None, None], axis=1)[:, 0]
    top_p, top_i = lax.top_k(jax.nn.softmax(sel, axis=-1), MOE_TOP_K)
    top_p = top_p / jnp.sum(top_p, axis=-1, keepdims=True)
    weights = g_w[:, None] * top_p
    expert = g_idx[:, None] * MOE_EXPERTS_PER_GROUP + top_i
    nk = n * MOE_TOP_K
    flat_e = expert.reshape(-1).astype(jnp.int32)
    flat_w = weights.reshape(-1)
    flat_t = jnp.repeat(jnp.arange(n, dtype=jnp.int32), MOE_TOP_K)
    order = jnp.argsort(flat_e)
    se, st, sw = flat_e[order], flat_t[order], flat_w[order]
    counts = jnp.bincount(flat_e, length=MOE_EXPERTS).astype(jnp.int32)
    starts = jnp.cumsum(counts) - counts
    padded = (counts + MOE_BLOCK - 1) // MOE_BLOCK * MOE_BLOCK
    pends = jnp.cumsum(padded)
    pstarts = pends - padded
    dest = pstarts[se] + jnp.arange(nk, dtype=jnp.int32) - starts[se]
    n_blocks = -(-nk // MOE_BLOCK) + MOE_EXPERTS
    cap = n_blocks * MOE_BLOCK
    tok_buf = jnp.zeros((cap,), jnp.int32).at[dest].set(st)
    w_buf = jnp.zeros((cap,), h2.dtype).at[dest].set(sw.astype(h2.dtype))
    block_start = jnp.arange(n_blocks, dtype=jnp.int32) * MOE_BLOCK
    block_e = jnp.clip(jnp.searchsorted(pends, block_start, side='right'), 0, MOE_EXPERTS - 1)

    def run_block(args):
        e, toks, wts = args
        xb = h2[toks]
        hid = jax.nn.silu(xb @ w_gate[e]) * (xb @ w_up[e])
        return (hid @ w_down[e]) * wts[:, None]

    out = lax.map(run_block, (block_e, tok_buf.reshape(n_blocks, MOE_BLOCK),
                              w_buf.reshape(n_blocks, MOE_BLOCK)))
    y = jnp.zeros((n, D), h2.dtype).at[tok_buf].add(out.reshape(cap, D))
    return y.reshape(B, S, D)


def setup_inputs(seed: int = 0) -> dict:
    key = jax.random.key(seed)
    ks = iter(jax.random.split(key, 48))
    L = DEPTH

    def nrm(shape, scale):
        return jax.random.normal(next(ks), shape, jnp.float32) * scale

    def gain(shape):
        return 1.0 + nrm(shape, 0.02)

    def unif(shape, lo, hi):
        return jax.random.uniform(next(ks), shape, jnp.float32, lo, hi)

    return {
        'x': nrm((BATCH, SEQ, D_MODEL), 1.0),
        'mem': nrm((BATCH, N_MEM, D_MODEL), 1.0),
        'attn_norm_w': gain((L, D_MODEL)),
        'w_in': nrm((L, D_MODEL, P_IN), D_MODEL ** -0.5),
        'na_q_norm_w': gain((L, HEAD_DIM)),
        'na_k_norm_w': gain((L, HEAD_DIM)),
        'na_rpb': nrm((L, NA_HEADS, 2 * NA_KH - 1, 2 * NA_KW - 1), 0.02),
        'rw_mu_prev': unif((L, RW_PROJ), 0.1, 0.5),
        'rw_mu_next': unif((L, RW_PROJ), 0.1, 0.5),
        'rw_w0': unif((L, 2, RW_WIDTH), -5.0, 0.5),
        'rw_w2': nrm((L, 2, RW_LORA_W, RW_WIDTH), 0.1 * RW_LORA_W ** -0.5),
        'rw_a0': nrm((L, 2, RW_WIDTH), 0.1),
        'rw_a2': nrm((L, 2, RW_LORA_A, RW_WIDTH), 0.5 * RW_LORA_A ** -0.5),
        'rw_g2': nrm((L, RW_LORA_G, RW_WIDTH), RW_LORA_G ** -0.5),
        'rw_k_k': 0.85 + nrm((L, RW_WIDTH), 0.02),
        'rw_k_a': 1.0 + nrm((L, RW_WIDTH), 0.02),
        'rw_r_k': nrm((L, RW_HEADS, HEAD_DIM), 0.1),
        'rw_ln_w': gain((L, RW_WIDTH)),
        'rw_ln_b': nrm((L, RW_WIDTH), 0.02),
        'mem_norm_w': gain((L, D_MODEL)),
        'w_mem_kv': nrm((L, D_MODEL, 2 * MEM_WIDTH), D_MODEL ** -0.5),
        'mem_q_norm_w': gain((L, HEAD_DIM)),
        'mem_k_norm_w': gain((L, HEAD_DIM)),
        'w_out': nrm((L, D_MIX, D_MODEL), D_MIX ** -0.5),
        'ffn_norm_w': gain((L, D_MODEL)),
        'moe_w_group': nrm((L, D_MODEL, MOE_GROUPS), D_MODEL ** -0.5),
        'moe_b_group': nrm((L, MOE_GROUPS), 0.01),
        'moe_w_expert': nrm((L, D_MODEL, MOE_EXPERTS), D_MODEL ** -0.5),
        'moe_b_expert': nrm((L, MOE_EXPERTS), 0.01),
        'moe_w_gate': nrm((L, MOE_EXPERTS, D_MODEL, MOE_D_FF), D_MODEL ** -0.5),
        'moe_w_up': nrm((L, MOE_EXPERTS, D_MODEL, MOE_D_FF), D_MODEL ** -0.5),
        'moe_w_down': nrm((L, MOE_EXPERTS, MOE_D_FF, D_MODEL), MOE_D_FF ** -0.5),
    }


def reference(x, mem, attn_norm_w, w_in, na_q_norm_w, na_k_norm_w, na_rpb,
              rw_mu_prev, rw_mu_next, rw_w0, rw_w2, rw_a0, rw_a2, rw_g2,
              rw_k_k, rw_k_a, rw_r_k, rw_ln_w, rw_ln_b,
              mem_norm_w, w_mem_kv, mem_q_norm_w, mem_k_norm_w, w_out,
              ffn_norm_w, moe_w_group, moe_b_group, moe_w_expert, moe_b_expert,
              moe_w_gate, moe_w_up, moe_w_down):
    for l in range(DEPTH):
        h = rms_norm(x, attn_norm_w[l])
        mem_n = rms_norm(mem, mem_norm_w[l])
        x = x + hybrid_mixer(h, mem_n, w_in[l], na_q_norm_w[l], na_k_norm_w[l], na_rpb[l],
                             rw_mu_prev[l], rw_mu_next[l], rw_w0[l], rw_w2[l], rw_a0[l], rw_a2[l],
                             rw_g2[l], rw_k_k[l], rw_k_a[l], rw_r_k[l], rw_ln_w[l], rw_ln_b[l],
                             w_mem_kv[l], mem_q_norm_w[l], mem_k_norm_w[l], w_out[l])
        x = x + hierarchical_moe(rms_norm(x, ffn_norm_w[l]), moe_w_group[l], moe_b_group[l],
                                 moe_w_expert[l], moe_b_expert[l], moe_w_gate[l],
                                 moe_w_up[l], moe_w_down[l])
    return x
```

```python
import functools

import jax
import jax.numpy as jnp
from jax import lax
from jax.experimental import pallas as pl
from jax.experimental.pallas import tpu as pltpu

F32 = jnp.float32
BF16 = jnp.bfloat16

GRID_W = 64
HEAD_DIM = 64
NA_HEADS = 8
NA_WIDTH = NA_HEADS * HEAD_DIM
NA_KH = 8
NA_KW = 16
RW_HEADS = 4
RW_WIDTH = RW_HEADS * HEAD_DIM
RW_LORA_W = 64
RW_LORA_A = 64
RW_LORA_G = 128
RW_PROJ = 3 * RW_WIDTH + 2 * RW_LORA_W + 2 * RW_LORA_A + RW_LORA_G
MEM_HEADS = 4
MEM_WIDTH = MEM_HEADS * HEAD_DIM
MOE_GROUPS = 4
MOE_EXPERTS_PER_GROUP = 4
MOE_EXPERTS = MOE_GROUPS * MOE_EXPERTS_PER_GROUP
MOE_D_FF = 512
RMS_EPS = 1e-6
RW_GN_EPS = 64e-5

LANES = 128
WKV_CHUNK = 64
NEG_BIG = -1e30
VMEM_LIMIT = 56 * 1024 * 1024


def _cparams(sem):
    return pltpu.CompilerParams(dimension_semantics=sem, vmem_limit_bytes=VMEM_LIMIT)


def _dot(a, b):
    return jnp.dot(a, b, preferred_element_type=F32)


def _dot_nt(a, b):
    return lax.dot_general(a, b, (((1,), (1,)), ((), ())), preferred_element_type=F32)


def _dot_tn(a, b):
    return lax.dot_general(a, b, (((0,), (0,)), ((), ())), preferred_element_type=F32)


def _split_dot(t, m):
    hi = t.astype(BF16)
    lo = (t - hi.astype(F32)).astype(BF16)
    return _dot(hi, m) + _dot(lo, m)


def _head_ones(width):
    i = lax.broadcasted_iota(jnp.int32, (width, width), 0) // HEAD_DIM
    j = lax.broadcasted_iota(jnp.int32, (width, width), 1) // HEAD_DIM
    return (i == j).astype(BF16)


def _row(v):
    return v.reshape(1, -1).astype(F32)


def _proj_kernel(x_ref, nw_ref, w_ref, qw_ref, kw_ref, mw_ref, q_ref, k_ref, v_ref, rw_ref, mq_ref):
    x = x_ref[...]
    ms = jnp.mean(x * x, axis=-1, keepdims=True)
    h = (x * lax.rsqrt(ms + RMS_EPS) * nw_ref[...]).astype(BF16)
    ones_na = _head_ones(NA_WIDTH)

    def head_norm(t, w, ones):
        ms = _split_dot(t * t, ones) * (1.0 / HEAD_DIM)
        return t * lax.rsqrt(ms + RMS_EPS) * w

    o = 0
    q = _dot(h, w_ref[:, o:o + NA_WIDTH])
    q_ref[...] = (head_norm(q, qw_ref[...], ones_na) * (HEAD_DIM ** -0.5)).astype(BF16)
    o += NA_WIDTH
    k = _dot(h, w_ref[:, o:o + NA_WIDTH])
    k_ref[...] = head_norm(k, kw_ref[...], ones_na).astype(BF16)
    o += NA_WIDTH
    v_ref[...] = _dot(h, w_ref[:, o:o + NA_WIDTH]).astype(BF16)
    o += NA_WIDTH
    rw_ref[...] = _dot(h, w_ref[:, o:o + RW_PROJ])
    o += RW_PROJ
    mq = _dot(h, w_ref[:, o:o + MEM_WIDTH])
    mq_ref[...] = (head_norm(mq, mw_ref[...], _head_ones(MEM_WIDTH)) * (HEAD_DIM ** -0.5)).astype(BF16)


def _proj(x2, nw, w_in, qw, kw, mw, tm):
    n, d = x2.shape
    p_in = w_in.shape[1]
    tok = lambda w: pl.BlockSpec((tm, w), lambda i: (i, 0))
    full = lambda a: pl.BlockSpec(a.shape, lambda i: (0,) * a.ndim)
    return pl.pallas_call(
        _proj_kernel,
        grid=(n // tm,),
        in_specs=[tok(d), full(nw), full(w_in), full(qw), full(kw), full(mw)],
        out_specs=[tok(NA_WIDTH), tok(NA_WIDTH), tok(NA_WIDTH), tok(RW_PROJ), tok(MEM_WIDTH)],
        out_shape=[jax.ShapeDtypeStruct((n, NA_WIDTH), BF16)] * 3
        + [jax.ShapeDtypeStruct((n, RW_PROJ), F32), jax.ShapeDtypeStruct((n, MEM_WIDTH), BF16)],
        compiler_params=_cparams(("parallel",)),
    )(x2, nw, w_in, qw, kw, mw)


NA_BLOCK_ROWS = 8


def _na_bias_table(rpb):
    col = jnp.arange(GRID_W)
    cs = jnp.clip(col - NA_KW // 2, 0, GRID_W - NA_KW)
    dc = col[None, :] - col[:, None]
    inside = (col[None, :] >= cs[:, None]) & (col[None, :] < cs[:, None] + NA_KW)
    dci = jnp.clip(dc + NA_KW - 1, 0, 2 * NA_KW - 2)
    off = jnp.arange(NA_KH) - (NA_KH - 1)
    dri = off[:, None] + jnp.arange(NA_KH)[None, :] + NA_KH - 1
    t = rpb[:, dri][:, :, :, dci]
    t = jnp.where(inside[None, None, None], t, NEG_BIG)
    t = jnp.transpose(t, (1, 0, 3, 2, 4))
    return t.reshape(NA_KH, NA_HEADS, GRID_W, NA_KH * GRID_W).astype(F32)


def _na_kernel(q_ref, kp_ref, kc_ref, kn_ref, vp_ref, vc_ref, vn_ref, bias_ref, o_ref, kwin, vwin, *, rows):
    j = pl.program_id(1)
    blk = NA_BLOCK_ROWS * GRID_W
    kwin[0:blk] = kp_ref[0]
    kwin[blk:2 * blk] = kc_ref[0]
    kwin[2 * blk:3 * blk] = kn_ref[0]
    vwin[0:blk] = vp_ref[0]
    vwin[blk:2 * blk] = vc_ref[0]
    vwin[2 * blk:3 * blk] = vn_ref[0]
    lane = lax.broadcasted_iota(jnp.int32, (1, LANES), 1)
    first = lane < HEAD_DIM

    def row_body(i, carry):
        r = j * NA_BLOCK_ROWS + i
        rs = jnp.clip(r - NA_KH // 2, 0, rows - NA_KH)
        oi = rs - r + (NA_KH - 1)
        start = pl.multiple_of((rs - (j - 1) * NA_BLOCK_ROWS) * GRID_W, GRID_W)
        qs = pl.multiple_of(i * GRID_W, GRID_W)
        for p in range(NA_HEADS // 2):
            ls = slice(p * LANES, (p + 1) * LANES)
            qp = q_ref[0, pl.ds(qs, GRID_W), ls]
            kw = kwin[pl.ds(start, NA_KH * GRID_W), ls]
            vw = vwin[pl.ds(start, NA_KH * GRID_W), ls]
            outs = []
            for hh in range(2):
                sel = first if hh == 0 else jnp.logical_not(first)
                qm = jnp.where(sel, qp, jnp.zeros_like(qp))
                s = _dot_nt(qm, kw) + bias_ref[oi, 2 * p + hh]
                m = jnp.max(s, axis=-1, keepdims=True)
                e = jnp.exp(s - m)
                l = jnp.sum(e, axis=-1, keepdims=True)
                outs.append(_dot(e.astype(BF16), vw) / l)
            o_ref[0, pl.ds(qs, GRID_W), ls] = jnp.where(first, outs[0], outs[1]).astype(o_ref.dtype)
        return carry

    lax.fori_loop(0, NA_BLOCK_ROWS, row_body, 0)


def _na(q, k, v, bias):
    b, s, w = q.shape
    rows = s // GRID_W
    nblk = rows // NA_BLOCK_ROWS
    blk = NA_BLOCK_ROWS * GRID_W
    cur = pl.BlockSpec((1, blk, w), lambda bi, j: (bi, j, 0))
    prv = pl.BlockSpec((1, blk, w), lambda bi, j: (bi, jnp.maximum(j - 1, 0), 0))
    nxt = pl.BlockSpec((1, blk, w), lambda bi, j: (bi, jnp.minimum(j + 1, nblk - 1), 0))
    return pl.pallas_call(
        functools.partial(_na_kernel, rows=rows),
        grid=(b, nblk),
        in_specs=[cur, prv, cur, nxt, prv, cur, nxt,
                  pl.BlockSpec(bias.shape, lambda bi, j: (0, 0, 0, 0))],
        out_specs=cur,
        out_shape=jax.ShapeDtypeStruct((b, s, w), BF16),
        scratch_shapes=[pltpu.VMEM((3 * blk, w), BF16), pltpu.VMEM((3 * blk, w), BF16)],
        compiler_params=_cparams(("parallel", "parallel")),
    )(q, k, k, k, v, v, v, bias)


def _rwprep_kernel(rw_ref, prev_ref, next_ref, mup_ref, mun_ref, w0_ref, w2_ref, a0_ref, a2_ref, g2_ref,
                   kk_w_ref, ka_ref, rk_ref,
                   r_ref, v_ref, kk_ref, kd_ref, b_ref, lw_ref, g_ref, bonus_ref, *, nblk):
    i = pl.program_id(1)
    s = rw_ref[0]
    ts = s.shape[0]
    rowi = lax.broadcasted_iota(jnp.int32, (ts, 1), 0)
    halo_p = jnp.where(i > 0, prev_ref[0, 7:8, :], 0.0)
    halo_n = jnp.where(i < nblk - 1, next_ref[0, 0:1, :], 0.0)
    prev = jnp.where(rowi == 0, halo_p, pltpu.roll(s, 1, 0))
    nxt = jnp.where(rowi == ts - 1, halo_n, pltpu.roll(s, ts - 1, 0))
    s = s + mup_ref[...] * (prev - s) + mun_ref[...] * (nxt - s)

    c = RW_WIDTH
    r = s[:, 0:c]
    k = s[:, c:2 * c]
    v = s[:, 2 * c:3 * c]
    lw = s[:, 3 * c:3 * c + 2 * RW_LORA_W]
    la = s[:, 3 * c + 2 * RW_LORA_W:3 * c + 2 * RW_LORA_W + 2 * RW_LORA_A]
    lg = s[:, 3 * c + 2 * RW_LORA_W + 2 * RW_LORA_A:]
    ones = _head_ones(c)

    g_ref[0] = _dot(jax.nn.sigmoid(lg).astype(BF16), g2_ref[...])
    kk = k * kk_w_ref[...]
    nrm = jnp.sqrt(_split_dot(kk * kk, ones))
    kk = kk / jnp.maximum(nrm, 1e-12)
    wl_pre = w0_ref[...] + _dot(jnp.tanh(lw).astype(BF16), w2_ref[...])
    a_all = jax.nn.sigmoid(a0_ref[...] + _dot(la.astype(BF16), a2_ref[...]))
    wl = -jax.nn.softplus(-wl_pre) - 0.5
    logdecay = -jnp.exp(wl)
    r_ref[0] = r
    v_ref[0] = v
    kk_ref[0] = kk
    kd_sum = jnp.zeros_like(k)
    for d in range(2):
        a = a_all[:, d * c:(d + 1) * c]
        kd = k * (1.0 + (a - 1.0) * ka_ref[...])
        kd_ref[d, 0] = kd
        b_ref[d, 0] = kk * a
        lw_ref[d, 0] = logdecay[:, d * c:(d + 1) * c]
        kd_sum = kd_sum + kd
    bonus_ref[0] = _split_dot(r * kd_sum * rk_ref[...], ones) * v


def _blockdiag2(m):
    z = jnp.zeros_like(m[0])
    return jnp.concatenate([jnp.concatenate([m[0], z], 1), jnp.concatenate([z, m[1]], 1)], 0)


def _rwprep(rw, mu_prev, mu_next, w0, w2, a0, a2, g2, k_k, k_a, r_k, ts):
    b, s, pw = rw.shape
    nblk = s // ts
    c = RW_WIDTH
    cur = pl.BlockSpec((1, ts, pw), lambda bi, i: (bi, i, 0))
    prv = pl.BlockSpec((1, 8, pw), lambda bi, i: (bi, jnp.maximum(i * (ts // 8) - 1, 0), 0))
    nxt = pl.BlockSpec((1, 8, pw), lambda bi, i: (bi, jnp.minimum((i + 1) * (ts // 8), s // 8 - 1), 0))
    params = [_row(mu_prev), _row(mu_next), _row(w0), _blockdiag2(w2).astype(BF16), _row(a0),
              _blockdiag2(a2).astype(BF16), g2.astype(BF16), _row(k_k), _row(k_a), _row(r_k)]
    full = lambda a: pl.BlockSpec(a.shape, lambda bi, i: (0,) * a.ndim)
    one = pl.BlockSpec((1, ts, c), lambda bi, i: (bi, i, 0))
    two = pl.BlockSpec((2, 1, ts, c), lambda bi, i: (0, bi, i, 0))
    s1 = jax.ShapeDtypeStruct((b, s, c), F32)
    s2 = jax.ShapeDtypeStruct((2, b, s, c), F32)
    return pl.pallas_call(
        functools.partial(_rwprep_kernel, nblk=nblk),
        grid=(b, nblk),
        in_specs=[cur, prv, nxt] + [full(p) for p in params],
        out_specs=[one, one, one, two, two, two, one, one],
        out_shape=[s1, s1, s1, s2, s2, s2, s1, s1],
        compiler_params=_cparams(("parallel", "parallel")),
    )(rw, rw, rw, *params)


WKV_STEP_CHUNKS = 4


def _wkv_kernel(r_ref, v_ref, kk_ref, kd_ref, b_ref, lw_ref, y_ref, s_ref, *, reverse):
    cc = WKV_CHUNK

    @pl.when(pl.program_id(1) == 0)
    def _():
        s_ref[...] = jnp.zeros_like(s_ref)

    row = lax.broadcasted_iota(jnp.int32, (cc, cc), 0)
    col = lax.broadcasted_iota(jnp.int32, (cc, cc), 1)
    strict = (col > row) if reverse else (col < row)
    incl = (col >= row) if reverse else (col <= row)
    eye = (col == row).astype(F32)
    rowc = lax.broadcasted_iota(jnp.int32, (cc, LANES), 0)
    first = lax.broadcasted_iota(jnp.int32, (1, LANES), 1) < HEAD_DIM
    same_head = ((lax.broadcasted_iota(jnp.int32, (LANES, LANES), 0) < HEAD_DIM)
                 == (lax.broadcasted_iota(jnp.int32, (LANES, LANES), 1) < HEAD_DIM))

    for step in range(WKV_STEP_CHUNKS):
        j = WKV_STEP_CHUNKS - 1 - step if reverse else step
        sl = slice(j * cc, (j + 1) * cc)
        for p in range(RW_HEADS // 2):
            ls = slice(p * LANES, (p + 1) * LANES)
            r = r_ref[0, sl, ls]
            v = v_ref[0, sl, ls]
            kk = kk_ref[0, sl, ls]
            kd = kd_ref[0, 0, sl, ls]
            b = b_ref[0, 0, sl, ls]
            lw = lw_ref[0, 0, sl, ls]
            cum = lw
            sh = 1
            while sh < cc:
                if reverse:
                    cum = cum + jnp.where(rowc < cc - sh, pltpu.roll(cum, cc - sh, 0), 0.0)
                else:
                    cum = cum + jnp.where(rowc >= sh, pltpu.roll(cum, sh, 0), 0.0)
                sh *= 2
            tot = cum[0:1] if reverse else cum[cc - 1:cc]
            at = -kk * jnp.exp(cum - lw)
            rt = r * jnp.exp(cum)
            einv = jnp.exp(-cum)
            kt = (kd * einv).astype(BF16)
            bt = (b * einv).astype(BF16)
            eh = jnp.exp(tot - cum)
            kh = kd * eh
            bh = b * eh
            vb = v.astype(BF16)
            qq = jnp.concatenate([at, rt], axis=0)
            atp, u0, y0, arb = [], [], [], []
            for hh in range(2):
                sel = first if hh == 0 else jnp.logical_not(first)
                qm = jnp.where(sel, qq, 0.0).astype(BF16)
                mb = _dot_nt(qm, bt)
                mk = _dot_nt(qm, kt)
                a_ab = jnp.where(strict, mb[:cc], 0.0)
                a_rb = jnp.where(incl, mb[cc:], 0.0)
                a_ak = jnp.where(strict, mk[:cc], 0.0)
                a_rk = jnp.where(incl, mk[cc:], 0.0)
                t = eye + a_ab
                pw = a_ab
                for _ in range(cc.bit_length() - 2):
                    pw = _dot(pw.astype(BF16), pw.astype(BF16))
                    t = t + _dot(t.astype(BF16), pw.astype(BF16))
                av = _dot(jnp.concatenate([a_ak, a_rk], axis=0).astype(BF16), vb)
                tx = _dot(t.astype(BF16), jnp.concatenate([at, av[:cc]], axis=1).astype(BF16))
                atp.append(tx[:, :LANES])
                u0.append(tx[:, LANES:])
                y0.append(av[cc:])
                arb.append(a_rb.astype(BF16))
            atp = jnp.where(first, atp[0], atp[1])
            u0 = jnp.where(first, u0[0], u0[1])
            y0 = jnp.where(first, y0[0], y0[1])
            st = s_ref[p]
            qs = _dot_nt(jnp.concatenate([atp, rt], axis=0).astype(BF16), st.astype(BF16))
            u = qs[:cc] + u0
            ub = u.astype(BF16)
            ru = jnp.where(first, _dot(arb[0], ub), _dot(arb[1], ub))
            y_ref[0, sl, ls] = qs[cc:] + y0 + ru
            upd = _dot_tn(jnp.concatenate([vb, ub], axis=0),
                          jnp.concatenate([kh, bh], axis=0).astype(BF16))
            s_ref[p] = st * jnp.exp(tot) + jnp.where(same_head, upd, 0.0)


def _wkv(r, v, kk, kd, bb, lw, d):
    b, s, c = r.shape
    tt = WKV_CHUNK * WKV_STEP_CHUNKS
    nblk = s // tt
    reverse = d == 1
    tix = (lambda i: nblk - 1 - i) if reverse else (lambda i: i)
    one = pl.BlockSpec((1, tt, c), lambda bi, i: (bi, tix(i), 0))
    two = pl.BlockSpec((1, 1, tt, c), lambda bi, i: (d, bi, tix(i), 0))
    return pl.pallas_call(
        functools.partial(_wkv_kernel, reverse=reverse),
        grid=(b, nblk),
        in_specs=[one, one, one, two, two, two],
        out_specs=one,
        out_shape=jax.ShapeDtypeStruct((b, s, c), F32),
        scratch_shapes=[pltpu.VMEM((RW_HEADS // 2, LANES, LANES), F32)],
        compiler_params=_cparams(("parallel", "arbitrary")),
    )(r, v, kk, kd, bb, lw)


def _memkv_kernel(mem_ref, nw_ref, w_ref, kw_ref, k_ref, v_ref):
    x = mem_ref[0]
    ms = jnp.mean(x * x, axis=-1, keepdims=True)
    h = (x * lax.rsqrt(ms + RMS_EPS) * nw_ref[...]).astype(BF16)
    kv = _dot(h, w_ref[...])
    k = kv[:, :MEM_WIDTH]
    ms = _split_dot(k * k, _head_ones(MEM_WIDTH)) * (1.0 / HEAD_DIM)
    k_ref[0] = (k * lax.rsqrt(ms + RMS_EPS) * kw_ref[...]).astype(BF16)
    v_ref[0] = kv[:, MEM_WIDTH:].astype(BF16)


def _memkv(mem, nw, w_kv, kw):
    b, m, d = mem.shape
    full = lambda a: pl.BlockSpec(a.shape, lambda bi: (0,) * a.ndim)
    out = pl.BlockSpec((1, m, MEM_WIDTH), lambda bi: (bi, 0, 0))
    return pl.pallas_call(
        _memkv_kernel,
        grid=(b,),
        in_specs=[pl.BlockSpec((1, m, d), lambda bi: (bi, 0, 0)), full(nw), full(w_kv), full(kw)],
        out_specs=[out, out],
        out_shape=[jax.ShapeDtypeStruct((b, m, MEM_WIDTH), BF16)] * 2,
        compiler_params=_cparams(("parallel",)),
    )(mem, nw, w_kv, kw)


def _mixout_kernel(x_ref, na_ref, yf_ref, yb_ref, bonus_ref, g_ref, mq_ref, mk_ref, mv_ref,
                   lnw_ref, lnb_ref, wo_ref, o_ref):
    ones = _head_ones(RW_WIDTH)
    y = yf_ref[0] + yb_ref[0]
    mu = _split_dot(y, ones) * (1.0 / HEAD_DIM)
    yc = y - mu
    var = _split_dot(yc * yc, ones) * (1.0 / HEAD_DIM)
    yn = yc * lax.rsqrt(var + RW_GN_EPS) * lnw_ref[...] + lnb_ref[...]
    y_rw = ((yn + bonus_ref[0]) * g_ref[0]).astype(BF16)

    first = lax.broadcasted_iota(jnp.int32, (1, LANES), 1) < HEAD_DIM
    mems = []
    for p in range(MEM_HEADS // 2):
        ls = slice(p * LANES, (p + 1) * LANES)
        qp = mq_ref[0, :, ls]
        kp = mk_ref[0, :, ls]
        vp = mv_ref[0, :, ls]
        outs = []
        for hh in range(2):
            sel = first if hh == 0 else jnp.logical_not(first)
            s = _dot_nt(jnp.where(sel, qp, jnp.zeros_like(qp)), kp)
            m = jnp.max(s, axis=-1, keepdims=True)
            e = jnp.exp(s - m)
            l = jnp.sum(e, axis=-1, keepdims=True)
            outs.append(_dot(e.astype(BF16), vp) / l)
        mems.append(jnp.where(first, outs[0], outs[1]).astype(BF16))
    y_mem = jnp.concatenate(mems, axis=1)

    acc = _dot(na_ref[0], wo_ref[0:NA_WIDTH, :])
    acc = acc + _dot(y_rw, wo_ref[NA_WIDTH:NA_WIDTH + RW_WIDTH, :])
    acc = acc + _dot(y_mem, wo_ref[NA_WIDTH + RW_WIDTH:, :])
    o_ref[0] = x_ref[0] + acc


def _mixout(x, y_na, y_f, y_b, bonus, g, mq, mk, mv, ln_w, ln_b, w_out, tm):
    b, s, d = x.shape
    m = mk.shape[1]
    tok = lambda w: pl.BlockSpec((1, tm, w), lambda bi, i: (bi, i, 0))
    full = lambda a: pl.BlockSpec(a.shape, lambda bi, i: (0,) * a.ndim)
    memb = pl.BlockSpec((1, m, MEM_WIDTH), lambda bi, i: (bi, 0, 0))
    return pl.pallas_call(
        _mixout_kernel,
        grid=(b, s // tm),
        in_specs=[tok(d), tok(NA_WIDTH), tok(RW_WIDTH), tok(RW_WIDTH), tok(RW_WIDTH), tok(RW_WIDTH),
                  tok(MEM_WIDTH), memb, memb, full(ln_w), full(ln_b), full(w_out)],
        out_specs=tok(d),
        out_shape=jax.ShapeDtypeStruct((b, s, d), F32),
        compiler_params=_cparams(("parallel", "parallel")),
    )(x, y_na, y_f, y_b, bonus, g, mq, mk, mv, ln_w, ln_b, w_out)


ROUTER_LANES = 128


def _router_kernel(x_ref, nw_ref, whi_ref, wlo_ref, b_ref, h_ref, lg_ref):
    x = x_ref[...]
    ms = jnp.mean(x * x, axis=-1, keepdims=True)
    h = x * lax.rsqrt(ms + RMS_EPS) * nw_ref[...]
    hi = h.astype(BF16)
    lo = (h - hi.astype(F32)).astype(BF16)
    h_ref[...] = hi
    lg_ref[...] = _dot(hi, whi_ref[...]) + _dot(hi, wlo_ref[...]) + _dot(lo, whi_ref[...]) + b_ref[...]


def _router(x2, nw, w_group, b_group, w_expert, b_expert, tm):
    n, d = x2.shape
    pad = ROUTER_LANES - MOE_GROUPS - MOE_EXPERTS
    w = jnp.concatenate([w_group, w_expert, jnp.zeros((d, pad), F32)], axis=1)
    whi = w.astype(BF16)
    wlo = (w - whi.astype(F32)).astype(BF16)
    bias = jnp.concatenate([b_group, b_expert, jnp.zeros((pad,), F32)]).reshape(1, -1)
    full = lambda a: pl.BlockSpec(a.shape, lambda i: (0,) * a.ndim)
    return pl.pallas_call(
        _router_kernel,
        grid=(n // tm,),
        in_specs=[pl.BlockSpec((tm, d), lambda i: (i, 0)), full(nw), full(whi), full(wlo), full(bias)],
        out_specs=[pl.BlockSpec((tm, d), lambda i: (i, 0)), pl.BlockSpec((tm, ROUTER_LANES), lambda i: (i, 0))],
        out_shape=[jax.ShapeDtypeStruct((n, d), BF16), jax.ShapeDtypeStruct((n, ROUTER_LANES), F32)],
        compiler_params=_cparams(("parallel",)),
    )(x2, nw, whi, wlo, bias)


def _route_weights(logits):
    n = logits.shape[0]
    gl = logits[:, :MOE_GROUPS]
    el = logits[:, MOE_GROUPS:MOE_GROUPS + MOE_EXPERTS].reshape(n, MOE_GROUPS, MOE_EXPERTS_PER_GROUP)
    g_idx = jnp.argmax(gl, axis=-1)
    g_w = jnp.take_along_axis(jax.nn.softmax(gl, axis=-1), g_idx[:, None], axis=1)[:, 0]
    sel = jnp.take_along_axis(el, g_idx[:, None, None], axis=1)[:, 0]
    top_p, top_i = lax.top_k(jax.nn.softmax(sel, axis=-1), 2)
    top_p = top_p / jnp.sum(top_p, axis=-1, keepdims=True)
    w = g_w[:, None] * top_p
    e = g_idx[:, None] * MOE_EXPERTS_PER_GROUP + top_i
    onehot = (e[:, :, None] == jnp.arange(MOE_EXPERTS)[None, None, :]).astype(F32)
    return jnp.sum(onehot * w[:, :, None], axis=1)


def _experts_kernel(x_ref, h_ref, rw_ref, wg_ref, wu_ref, wd_ref, o_ref):
    e = pl.program_id(1)

    @pl.when(e == 0)
    def _():
        o_ref[...] = x_ref[...]

    h = h_ref[...]
    hid = jax.nn.silu(_dot(h, wg_ref[0])) * _dot(h, wu_ref[0])
    lane = lax.broadcasted_iota(jnp.int32, rw_ref.shape, 1)
    wt = jnp.sum(jnp.where(lane == e, rw_ref[...], 0.0), axis=1, keepdims=True)
    o_ref[...] += _dot(hid.astype(BF16), wd_ref[0]) * wt


def _experts(x2, h2, rwt, w_gate, w_up, w_down, tm):
    n, d = x2.shape
    ne = w_gate.shape[0]
    tok = lambda w: pl.BlockSpec((tm, w), lambda i, e: (i, 0))
    return pl.pallas_call(
        _experts_kernel,
        grid=(n // tm, ne),
        in_specs=[tok(d), tok(d), tok(ne),
                  pl.BlockSpec((1, d, MOE_D_FF), lambda i, e: (e, 0, 0)),
                  pl.BlockSpec((1, d, MOE_D_FF), lambda i, e: (e, 0, 0)),
                  pl.BlockSpec((1, MOE_D_FF, d), lambda i, e: (e, 0, 0))],
        out_specs=tok(d),
        out_shape=jax.ShapeDtypeStruct((n, d), F32),
        compiler_params=_cparams(("parallel", "arbitrary")),
    )(x2, h2, rwt, w_gate, w_up, w_down)


def _tile(n, want):
    t = min(n, want)
    assert n % t == 0
    return t


def kernel(x, mem, attn_norm_w, w_in, na_q_norm_w, na_k_norm_w, na_rpb, rw_mu_prev, rw_mu_next, rw_w0, rw_w2, rw_a0, rw_a2, rw_g2, rw_k_k, rw_k_a, rw_r_k, rw_ln_w, rw_ln_b, mem_norm_w, w_mem_kv, mem_q_norm_w, mem_k_norm_w, w_out, ffn_norm_w, moe_w_group, moe_b_group, moe_w_expert, moe_b_expert, moe_w_gate, moe_w_up, moe_w_down):
    b, s, d = x.shape
    n = b * s
    depth = w_in.shape[0]
    assert s % (NA_BLOCK_ROWS * GRID_W) == 0 and s // GRID_W >= 2 * NA_KH
    tm = _tile(n, 512)
    ts = _tile(s, 512)
    for l in range(depth):
        q, k, v, rw, mq = _proj(
            x.reshape(n, d), _row(attn_norm_w[l]), w_in[l].astype(BF16),
            _row(jnp.tile(na_q_norm_w[l], NA_HEADS)), _row(jnp.tile(na_k_norm_w[l], NA_HEADS)),
            _row(jnp.tile(mem_q_norm_w[l], MEM_HEADS)), tm)
        y_na = _na(q.reshape(b, s, NA_WIDTH), k.reshape(b, s, NA_WIDTH), v.reshape(b, s, NA_WIDTH),
                   _na_bias_table(na_rpb[l]))
        r, vv, kk, kd, bb, lw, g, bonus = _rwprep(
            rw.reshape(b, s, RW_PROJ), rw_mu_prev[l], rw_mu_next[l], rw_w0[l], rw_w2[l], rw_a0[l],
            rw_a2[l], rw_g2[l], rw_k_k[l], rw_k_a[l], rw_r_k[l], ts)
        y_f = _wkv(r, vv, kk, kd, bb, lw, 0)
        y_b = _wkv(r, vv, kk, kd, bb, lw, 1)
        mk, mv = _memkv(mem, _row(mem_norm_w[l]), w_mem_kv[l].astype(BF16),
                        _row(jnp.tile(mem_k_norm_w[l], MEM_HEADS)))
        x = _mixout(x, y_na, y_f, y_b, bonus, g, mq.reshape(b, s, MEM_WIDTH), mk, mv,
                    _row(rw_ln_w[l]), _row(rw_ln_b[l]), w_out[l].astype(BF16), ts)
        h2, logits = _router(x.reshape(n, d), _row(ffn_norm_w[l]), moe_w_group[l], moe_b_group[l],
                             moe_w_expert[l], moe_b_expert[l], tm)
        x = _experts(x.reshape(n, d), h2, _route_weights(logits), moe_w_gate[l].astype(BF16),
                     moe_w_up[l].astype(BF16), moe_w_down[l].astype(BF16), _tile(n, 1024)).reshape(b, s, d)
    return x
```

```python
import functools

import jax
import jax.numpy as jnp
from jax import lax
from jax.experimental import pallas as pl
from jax.experimental.pallas import tpu as pltpu

F32 = jnp.float32
BF16 = jnp.bfloat16

GRID_W = 64
HEAD_DIM = 64
NA_HEADS = 8
NA_WIDTH = NA_HEADS * HEAD_DIM
NA_KH = 8
NA_KW = 16
RW_HEADS = 4
RW_WIDTH = RW_HEADS * HEAD_DIM
RW_LORA_W = 64
RW_LORA_A = 64
RW_LORA_G = 128
RW_PROJ = 3 * RW_WIDTH + 2 * RW_LORA_W + 2 * RW_LORA_A + RW_LORA_G
MEM_HEADS = 4
MEM_WIDTH = MEM_HEADS * HEAD_DIM
MOE_GROUPS = 4
MOE_EXPERTS_PER_GROUP = 4
MOE_EXPERTS = MOE_GROUPS * MOE_EXPERTS_PER_GROUP
MOE_D_FF = 512
RMS_EPS = 1e-6
RW_GN_EPS = 64e-5

LANES = 128
WKV_CHUNK = 64
NEG_BIG = -1e30
VMEM_LIMIT = 56 * 1024 * 1024


def _cparams(sem):
    return pltpu.CompilerParams(dimension_semantics=sem, vmem_limit_bytes=VMEM_LIMIT)


def _dot(a, b):
    return jnp.dot(a, b, preferred_element_type=F32)


def _dot_nt(a, b):
    return lax.dot_general(a, b, (((1,), (1,)), ((), ())), preferred_element_type=F32)


def _dot_tn(a, b):
    return lax.dot_general(a, b, (((0,), (0,)), ((), ())), preferred_element_type=F32)


def _split_dot(t, m):
    hi = t.astype(BF16)
    lo = (t - hi.astype(F32)).astype(BF16)
    return _dot(hi, m) + _dot(lo, m)


def _head_ones(width):
    i = lax.broadcasted_iota(jnp.int32, (width, width), 0) // HEAD_DIM
    j = lax.broadcasted_iota(jnp.int32, (width, width), 1) // HEAD_DIM
    return (i == j).astype(BF16)


def _row(v):
    return v.reshape(1, -1).astype(F32)


def _proj_kernel(x_ref, nw_ref, w_ref, qw_ref, kw_ref, mw_ref, q_ref, k_ref, v_ref, rw_ref, mq_ref):
    x = x_ref[...]
    ms = jnp.mean(x * x, axis=-1, keepdims=True)
    h = (x * lax.rsqrt(ms + RMS_EPS) * nw_ref[...]).astype(BF16)
    ones_na = _head_ones(NA_WIDTH)

    def head_norm(t, w, ones):
        ms = _split_dot(t * t, ones) * (1.0 / HEAD_DIM)
        return t * lax.rsqrt(ms + RMS_EPS) * w

    o = 0
    q = _dot(h, w_ref[:, o:o + NA_WIDTH])
    q_ref[...] = (head_norm(q, qw_ref[...], ones_na) * (HEAD_DIM ** -0.5)).astype(BF16)
    o += NA_WIDTH
    k = _dot(h, w_ref[:, o:o + NA_WIDTH])
    k_ref[...] = head_norm(k, kw_ref[...], ones_na).astype(BF16)
    o += NA_WIDTH
    v_ref[...] = _dot(h, w_ref[:, o:o + NA_WIDTH]).astype(BF16)
    o += NA_WIDTH
    rw_ref[...] = _dot(h, w_ref[:, o:o + RW_PROJ])
    o += RW_PROJ
    mq = _dot(h, w_ref[:, o:o + MEM_WIDTH])
    mq_ref[...] = (head_norm(mq, mw_ref[...], _head_ones(MEM_WIDTH)) * (HEAD_DIM ** -0.5)).astype(BF16)


def _proj(x2, nw, w_in, qw, kw, mw, tm):
    n, d = x2.shape
    p_in = w_in.shape[1]
    tok = lambda w: pl.BlockSpec((tm, w), lambda i: (i, 0))
    full = lambda a: pl.BlockSpec(a.shape, lambda i: (0,) * a.ndim)
    return pl.pallas_call(
        _proj_kernel,
        grid=(n // tm,),
        in_specs=[tok(d), full(nw), full(w_in), full(qw), full(kw), full(mw)],
        out_specs=[tok(NA_WIDTH), tok(NA_WIDTH), tok(NA_WIDTH), tok(RW_PROJ), tok(MEM_WIDTH)],
        out_shape=[jax.ShapeDtypeStruct((n, NA_WIDTH), BF16)] * 3
        + [jax.ShapeDtypeStruct((n, RW_PROJ), F32), jax.ShapeDtypeStruct((n, MEM_WIDTH), BF16)],
        compiler_params=_cparams(("parallel",)),
    )(x2, nw, w_in, qw, kw, mw)


NA_BLOCK_ROWS = 8


def _na_bias_table(rpb):
    col = jnp.arange(GRID_W)
    cs = jnp.clip(col - NA_KW // 2, 0, GRID_W - NA_KW)
    dc = col[None, :] - col[:, None]
    inside = (col[None, :] >= cs[:, None]) & (col[None, :] < cs[:, None] + NA_KW)
    dci = jnp.clip(dc + NA_KW - 1, 0, 2 * NA_KW - 2)
    off = jnp.arange(NA_KH) - (NA_KH - 1)
    dri = off[:, None] + jnp.arange(NA_KH)[None, :] + NA_KH - 1
    t = rpb[:, dri][:, :, :, dci]
    t = jnp.where(inside[None, None, None], t, NEG_BIG)
    t = jnp.transpose(t, (1, 0, 3, 2, 4))
    return t.reshape(NA_KH, NA_HEADS, GRID_W, NA_KH * GRID_W).astype(F32)


def _na_kernel(q_ref, kp_ref, kc_ref, kn_ref, vp_ref, vc_ref, vn_ref, bias_ref, o_ref, kwin, vwin, *, rows):
    j = pl.program_id(1)
    blk = NA_BLOCK_ROWS * GRID_W
    kwin[0:blk] = kp_ref[0]
    kwin[blk:2 * blk] = kc_ref[0]
    kwin[2 * blk:3 * blk] = kn_ref[0]
    vwin[0:blk] = vp_ref[0]
    vwin[blk:2 * blk] = vc_ref[0]
    vwin[2 * blk:3 * blk] = vn_ref[0]
    lane = lax.broadcasted_iota(jnp.int32, (1, LANES), 1)
    first = lane < HEAD_DIM

    def row_body(i, carry):
        r = j * NA_BLOCK_ROWS + i
        rs = jnp.clip(r - NA_KH // 2, 0, rows - NA_KH)
        oi = rs - r + (NA_KH - 1)
        start = pl.multiple_of((rs - (j - 1) * NA_BLOCK_ROWS) * GRID_W, GRID_W)
        qs = pl.multiple_of(i * GRID_W, GRID_W)
        for p in range(NA_HEADS // 2):
            ls = slice(p * LANES, (p + 1) * LANES)
            qp = q_ref[0, pl.ds(qs, GRID_W), ls]
            kw = kwin[pl.ds(start, NA_KH * GRID_W), ls]
            vw = vwin[pl.ds(start, NA_KH * GRID_W), ls]
            outs = []
            for hh in range(2):
                sel = first if hh == 0 else jnp.logical_not(first)
                qm = jnp.where(sel, qp, jnp.zeros_like(qp))
                s = _dot_nt(qm, kw) + bias_ref[oi, 2 * p + hh]
                m = jnp.max(s, axis=-1, keepdims=True)
                e = jnp.exp(s - m)
                l = jnp.sum(e, axis=-1, keepdims=True)
                outs.append(_dot(e.astype(BF16), vw) / l)
            o_ref[0, pl.ds(qs, GRID_W), ls] = jnp.where(first, outs[0], outs[1]).astype(o_ref.dtype)
        return carry

    lax.fori_loop(0, NA_BLOCK_ROWS, row_body, 0)


def _na(q, k, v, bias):
    b, s, w = q.shape
    rows = s // GRID_W
    nblk = rows // NA_BLOCK_ROWS
    blk = NA_BLOCK_ROWS * GRID_W
    cur = pl.BlockSpec((1, blk, w), lambda bi, j: (bi, j, 0))
    prv = pl.BlockSpec((1, blk, w), lambda bi, j: (bi, jnp.maximum(j - 1, 0), 0))
    nxt = pl.BlockSpec((1, blk, w), lambda bi, j: (bi, jnp.minimum(j + 1, nblk - 1), 0))
    return pl.pallas_call(
        functools.partial(_na_kernel, rows=rows),
        grid=(b, nblk),
        in_specs=[cur, prv, cur, nxt, prv, cur, nxt,
                  pl.BlockSpec(bias.shape, lambda bi, j: (0, 0, 0, 0))],
        out_specs=cur,
        out_shape=jax.ShapeDtypeStruct((b, s, w), BF16),
        scratch_shapes=[pltpu.VMEM((3 * blk, w), BF16), pltpu.VMEM((3 * blk, w), BF16)],
        compiler_params=_cparams(("parallel", "parallel")),
    )(q, k, k, k, v, v, v, bias)


def _rwprep_kernel(rw_ref, prev_ref, next_ref, mup_ref, mun_ref, w0_ref, w2_ref, a0_ref, a2_ref, g2_ref,
                   kk_w_ref, ka_ref, rk_ref,
                   r_ref, v_ref, kk_ref, kd_ref, b_ref, lw_ref, g_ref, bonus_ref, *, nblk):
    i = pl.program_id(1)
    s = rw_ref[0]
    ts = s.shape[0]
    rowi = lax.broadcasted_iota(jnp.int32, (ts, 1), 0)
    halo_p = jnp.where(i > 0, prev_ref[0, 7:8, :], 0.0)
    halo_n = jnp.where(i < nblk - 1, next_ref[0, 0:1, :], 0.0)
    prev = jnp.where(rowi == 0, halo_p, pltpu.roll(s, 1, 0))
    nxt = jnp.where(rowi == ts - 1, halo_n, pltpu.roll(s, ts - 1, 0))
    s = s + mup_ref[...] * (prev - s) + mun_ref[...] * (nxt - s)

    c = RW_WIDTH
    r = s[:, 0:c]
    k = s[:, c:2 * c]
    v = s[:, 2 * c:3 * c]
    lw = s[:, 3 * c:3 * c + 2 * RW_LORA_W]
    la = s[:, 3 * c + 2 * RW_LORA_W:3 * c + 2 * RW_LORA_W + 2 * RW_LORA_A]
    lg = s[:, 3 * c + 2 * RW_LORA_W + 2 * RW_LORA_A:]
    ones = _head_ones(c)

    g_ref[0] = _dot(jax.nn.sigmoid(lg).astype(BF16), g2_ref[...])
    kk = k * kk_w_ref[...]
    nrm = jnp.sqrt(_split_dot(kk * kk, ones))
    kk = kk / jnp.maximum(nrm, 1e-12)
    wl_pre = w0_ref[...] + _dot(jnp.tanh(lw).astype(BF16), w2_ref[...])
    a_all = jax.nn.sigmoid(a0_ref[...] + _dot(la.astype(BF16), a2_ref[...]))
    wl = -jax.nn.softplus(-wl_pre) - 0.5
    logdecay = -jnp.exp(wl)
    r_ref[0] = r
    v_ref[0] = v
    kk_ref[0] = kk
    kd_sum = jnp.zeros_like(k)
    for d in range(2):
        a = a_all[:, d * c:(d + 1) * c]
        kd = k * (1.0 + (a - 1.0) * ka_ref[...])
        kd_ref[d, 0] = kd
        b_ref[d, 0] = kk * a
        lw_ref[d, 0] = logdecay[:, d * c:(d + 1) * c]
        kd_sum = kd_sum + kd
    bonus_ref[0] = _split_dot(r * kd_sum * rk_ref[...], ones) * v


def _blockdiag2(m):
    z = jnp.zeros_like(m[0])
    return jnp.concatenate([jnp.concatenate([m[0], z], 1), jnp.concatenate([z, m[1]], 1)], 0)


def _rwprep(rw, mu_prev, mu_next, w0, w2, a0, a2, g2, k_k, k_a, r_k, ts):
    b, s, pw = rw.shape
    nblk = s // ts
    c = RW_WIDTH
    cur = pl.BlockSpec((1, ts, pw), lambda bi, i: (bi, i, 0))
    prv = pl.BlockSpec((1, 8, pw), lambda bi, i: (bi, jnp.maximum(i * (ts // 8) - 1, 0), 0))
    nxt = pl.BlockSpec((1, 8, pw), lambda bi, i: (bi, jnp.minimum((i + 1) * (ts // 8), s // 8 - 1), 0))
    params = [_row(mu_prev), _row(mu_next), _row(w0), _blockdiag2(w2).astype(BF16), _row(a0),
              _blockdiag2(a2).astype(BF16), g2.astype(BF16), _row(k_k), _row(k_a), _row(r_k)]
    full = lambda a: pl.BlockSpec(a.shape, lambda bi, i: (0,) * a.ndim)
    one = pl.BlockSpec((1, ts, c), lambda bi, i: (bi, i, 0))
    two = pl.BlockSpec((2, 1, ts, c), lambda bi, i: (0, bi, i, 0))
    s1 = jax.ShapeDtypeStruct((b, s, c), F32)
    s2 = jax.ShapeDtypeStruct((2, b, s, c), F32)
    return pl.pallas_call(
        functools.partial(_rwprep_kernel, nblk=nblk),
        grid=(b, nblk),
        in_specs=[cur, prv, nxt] + [full(p) for p in params],
        out_specs=[one, one, one, two, two, two, one, one],
        out_shape=[s1, s1, s1, s2, s2, s2, s1, s1],
        compiler_params=_cparams(("parallel", "parallel")),
    )(rw, rw, rw, *params)


WKV_STEP_CHUNKS = 4


def _wkv_kernel(r_ref, v_ref, kk_ref, kd_ref, b_ref, lw_ref, y_ref, s_ref, *, reverse):
    cc = WKV_CHUNK

    @pl.when(pl.program_id(1) == 0)
    def _():
        s_ref[...] = jnp.zeros_like(s_ref)

    row = lax.broadcasted_iota(jnp.int32, (cc, cc), 0)
    col = lax.broadcasted_iota(jnp.int32, (cc, cc), 1)
    strict = (col > row) if reverse else (col < row)
    incl = (col >= row) if reverse else (col <= row)
    eye = (col == row).astype(F32)
    rowc = lax.broadcasted_iota(jnp.int32, (cc, LANES), 0)
    first = lax.broadcasted_iota(jnp.int32, (1, LANES), 1) < HEAD_DIM
    same_head = ((lax.broadcasted_iota(jnp.int32, (LANES, LANES), 0) < HEAD_DIM)
                 == (lax.broadcasted_iota(jnp.int32, (LANES, LANES), 1) < HEAD_DIM))

    for step in range(WKV_STEP_CHUNKS):
        j = WKV_STEP_CHUNKS - 1 - step if reverse else step
        sl = slice(j * cc, (j + 1) * cc)
        for p in range(RW_HEADS // 2):
            ls = slice(p * LANES, (p + 1) * LANES)
            r = r_ref[0, sl, ls]
            v = v_ref[0, sl, ls]
            kk = kk_ref[0, sl, ls]
            kd = kd_ref[0, 0, sl, ls]
            b = b_ref[0, 0, sl, ls]
            lw = lw_ref[0, 0, sl, ls]
            cum = lw
            sh = 1
            while sh < cc:
                if reverse:
                    cum = cum + jnp.where(rowc < cc - sh, pltpu.roll(cum, cc - sh, 0), 0.0)
                else:
                    cum = cum + jnp.where(rowc >= sh, pltpu.roll(cum, sh, 0), 0.0)
                sh *= 2
            tot = cum[0:1] if reverse else cum[cc - 1:cc]
            at = -kk * jnp.exp(cum - lw)
            rt = r * jnp.exp(cum)
            einv = jnp.exp(-cum)
            kt = (kd * einv).astype(BF16)
            bt = (b * einv).astype(BF16)
            eh = jnp.exp(tot - cum)
            kh = kd * eh
            bh = b * eh
            vb = v.astype(BF16)
            qq = jnp.concatenate([at, rt], axis=0)
            atp, u0, y0, arb = [], [], [], []
            for hh in range(2):
                sel = first if hh == 0 else jnp.logical_not(first)
                qm = jnp.where(sel, qq, 0.0).astype(BF16)
                mb = _dot_nt(qm, bt)
                mk = _dot_nt(qm, kt)
                a_ab = jnp.where(strict, mb[:cc], 0.0)
                a_rb = jnp.where(incl, mb[cc:], 0.0)
                a_ak = jnp.where(strict, mk[:cc], 0.0)
                a_rk = jnp.where(incl, mk[cc:], 0.0)
                t = eye + a_ab
                pw = a_ab
                for _ in range(cc.bit_length() - 2):
                    pw = _dot(pw.astype(BF16), pw.astype(BF16))
                    t = t + _dot(t.astype(BF16), pw.astype(BF16))
                av = _dot(jnp.concatenate([a_ak, a_rk], axis=0).astype(BF16), vb)
                tx = _dot(t.astype(BF16), jnp.concatenate([at, av[:cc]], axis=1).astype(BF16))
                atp.append(tx[:, :LANES])
                u0.append(tx[:, LANES:])
                y0.append(av[cc:])
                arb.append(a_rb.astype(BF16))
            atp = jnp.where(first, atp[0], atp[1])
            u0 = jnp.where(first, u0[0], u0[1])
            y0 = jnp.where(first, y0[0], y0[1])
            st = s_ref[p]
            qs = _dot_nt(jnp.concatenate([atp, rt], axis=0).astype(BF16), st.astype(BF16))
            u = qs[:cc] + u0
            ub = u.astype(BF16)
            ru = jnp.where(first, _dot(arb[0], ub), _dot(arb[1], ub))
            y_ref[0, sl, ls] = qs[cc:] + y0 + ru
            upd = _dot_tn(jnp.concatenate([vb, ub], axis=0),
                          jnp.concatenate([kh, bh], axis=0).astype(BF16))
            s_ref[p] = st * jnp.exp(tot) + jnp.where(same_head, upd, 0.0)


def _wkv(r, v, kk, kd, bb, lw, d):
    b, s, c = r.shape
    tt = WKV_CHUNK * WKV_STEP_CHUNKS
    nblk = s // tt
    reverse = d == 1
    tix = (lambda i: nblk - 1 - i) if reverse else (lambda i: i)
    one = pl.BlockSpec((1, tt, c), lambda bi, i: (bi, tix(i), 0))
    two = pl.BlockSpec((1, 1, tt, c), lambda bi, i: (d, bi, tix(i), 0))
    return pl.pallas_call(
        functools.partial(_wkv_kernel, reverse=reverse),
        grid=(b, nblk),
        in_specs=[one, one, one, two, two, two],
        out_specs=one,
        out_shape=jax.ShapeDtypeStruct((b, s, c), F32),
        scratch_shapes=[pltpu.VMEM((RW_HEADS // 2, LANES, LANES), F32)],
        compiler_params=_cparams(("parallel", "arbitrary")),
    )(r, v, kk, kd, bb, lw)


WKV2_BATCH = 2


def _wkv2_kernel(rf, vf, kkf, rb, vb, kkb, kdf, bf, lwf, kdb, bb, lwb, yf_ref, yb_ref, s_ref):
    cc = WKV_CHUNK
    nb = rf.shape[0]
    nchunk = rf.shape[1] // cc

    @pl.when(pl.program_id(1) == 0)
    def _():
        s_ref[...] = jnp.zeros_like(s_ref)

    ri = lax.broadcasted_iota(jnp.int32, (LANES, LANES), 0)
    ci = lax.broadcasted_iota(jnp.int32, (LANES, LANES), 1)
    same_head = (ri < HEAD_DIM) == (ci < HEAD_DIM)
    rt_, ct_ = ri % cc, ci % cc
    eye = (ri == ci).astype(F32)
    strict = (same_head & (ct_ < rt_), same_head & (ct_ > rt_))
    incl = (same_head & (ct_ <= rt_), same_head & (ct_ >= rt_))
    rowc = lax.broadcasted_iota(jnp.int32, (cc, LANES), 0)
    first = lax.broadcasted_iota(jnp.int32, (1, LANES), 1) < HEAD_DIM
    second = jnp.logical_not(first)

    def stack_heads(t):
        return jnp.concatenate([jnp.where(first, t, 0.0), jnp.where(second, t, 0.0)], axis=0)

    def fold_heads(t):
        return t[:cc] + t[cc:]

    for c in range(nchunk):
        chains = []
        for bi in range(nb):
            for d in range(2):
                j = nchunk - 1 - c if d else c
                sl = slice(j * cc, (j + 1) * cc)
                src = (rb, vb, kkb, kdb, bb, lwb) if d else (rf, vf, kkf, kdf, bf, lwf)
                for p in range(RW_HEADS // 2):
                    ls = slice(p * LANES, (p + 1) * LANES)
                    chains.append(dict(
                        bi=bi, d=d, p=p, sl=sl, ls=ls,
                        r=src[0][bi, sl, ls], v=src[1][bi, sl, ls], kk=src[2][bi, sl, ls],
                        kd=src[3][0, bi, sl, ls], b=src[4][0, bi, sl, ls], lw=src[5][0, bi, sl, ls]))

        for ch in chains:
            cum = ch["lw"]
            sh = 1
            while sh < cc:
                if ch["d"]:
                    cum = cum + jnp.where(rowc < cc - sh, pltpu.roll(cum, cc - sh, 0), 0.0)
                else:
                    cum = cum + jnp.where(rowc >= sh, pltpu.roll(cum, sh, 0), 0.0)
                sh *= 2
            tot = cum[0:1] if ch["d"] else cum[cc - 1:cc]
            at = -ch["kk"] * jnp.exp(cum - ch["lw"])
            rt = ch["r"] * jnp.exp(cum)
            einv = jnp.exp(-cum)
            eh = jnp.exp(tot - cum)
            ch["etot"] = jnp.exp(tot)
            ch["rt"] = rt.astype(BF16)
            ch["vb"] = ch["v"].astype(BF16)
            ch["at_st"] = stack_heads(at).astype(BF16)
            ch["lhs"] = jnp.concatenate([ch["at_st"], stack_heads(rt).astype(BF16)], axis=0)
            ch["rhs"] = jnp.concatenate([stack_heads(ch["b"] * einv), stack_heads(ch["kd"] * einv)],
                                        axis=0).astype(BF16)
            ch["v_st"] = stack_heads(ch["v"]).astype(BF16)
            ch["kb_hat"] = jnp.concatenate([ch["kd"] * eh, ch["b"] * eh], axis=0).astype(BF16)
        for ch in chains:
            m1 = _dot_nt(ch["lhs"], ch["rhs"])
            d = ch["d"]
            ch["n"] = jnp.where(strict[d], m1[:LANES, :LANES], 0.0)
            a_ak = jnp.where(strict[d], m1[:LANES, LANES:], 0.0)
            ch["a_rb"] = jnp.where(incl[d], m1[LANES:, :LANES], 0.0).astype(BF16)
            a_rk = jnp.where(incl[d], m1[LANES:, LANES:], 0.0)
            ch["a_k"] = jnp.concatenate([a_ak, a_rk], axis=0).astype(BF16)
        for ch in chains:
            nbf = ch["n"].astype(BF16)
            ch["pw"] = _dot(nbf, nbf)
            ch["t"] = eye + ch["n"]
        for _ in range(cc.bit_length() - 3):
            for ch in chains:
                pwb = ch["pw"].astype(BF16)
                tp = _dot(jnp.concatenate([ch["t"].astype(BF16), pwb], axis=0), pwb)
                ch["t"] = ch["t"] + tp[:LANES]
                ch["pw"] = tp[LANES:]
        for ch in chains:
            ch["t"] = (ch["t"] + _dot(ch["t"].astype(BF16), ch["pw"].astype(BF16))).astype(BF16)
        for ch in chains:
            ch["av"] = _dot(ch["a_k"], ch["v_st"])
        for ch in chains:
            tx = _dot(ch["t"], jnp.concatenate([ch["at_st"], ch["av"][:LANES].astype(BF16)], axis=1))
            ch["atp"] = fold_heads(tx[:, :LANES]).astype(BF16)
            ch["u0"] = fold_heads(tx[:, LANES:])
            ch["y0"] = fold_heads(ch["av"][LANES:])

        for ch in chains:
            ch["st"] = s_ref[ch["bi"], ch["d"], ch["p"]]
            qs = _dot_nt(jnp.concatenate([ch["atp"], ch["rt"]], axis=0), ch["st"].astype(BF16))
            ch["u"] = qs[:cc] + ch["u0"]
            ch["ys"] = qs[cc:] + ch["y0"]
        for ch in chains:
            upd = _dot_tn(jnp.concatenate([ch["vb"], ch["u"].astype(BF16)], axis=0), ch["kb_hat"])
            s_ref[ch["bi"], ch["d"], ch["p"]] = ch["st"] * ch["etot"] + jnp.where(same_head, upd, 0.0)
        for ch in chains:
            ru = _dot(ch["a_rb"], stack_heads(ch["u"]).astype(BF16))
            y_ref = yb_ref if ch["d"] else yf_ref
            y_ref[ch["bi"], ch["sl"], ch["ls"]] = ch["ys"] + fold_heads(ru)


def _wkv2(r, v, kk, kd, bb, lw):
    b, s, c = r.shape
    tt = WKV_CHUNK * WKV_STEP_CHUNKS
    nblk = s // tt
    nb = WKV2_BATCH if b % WKV2_BATCH == 0 else 1
    fwd = pl.BlockSpec((nb, tt, c), lambda bi, i: (bi, i, 0))
    bwd = pl.BlockSpec((nb, tt, c), lambda bi, i: (bi, nblk - 1 - i, 0))
    fwd2 = pl.BlockSpec((1, nb, tt, c), lambda bi, i: (0, bi, i, 0))
    bwd2 = pl.BlockSpec((1, nb, tt, c), lambda bi, i: (1, bi, nblk - 1 - i, 0))
    return pl.pallas_call(
        _wkv2_kernel,
        grid=(b // nb, nblk),
        in_specs=[fwd, fwd, fwd, bwd, bwd, bwd, fwd2, fwd2, fwd2, bwd2, bwd2, bwd2],
        out_specs=[fwd, bwd],
        out_shape=[jax.ShapeDtypeStruct((b, s, c), F32)] * 2,
        scratch_shapes=[pltpu.VMEM((nb, 2, RW_HEADS // 2, LANES, LANES), F32)],
        compiler_params=_cparams(("parallel", "arbitrary")),
    )(r, v, kk, r, v, kk, kd, bb, lw, kd, bb, lw)


def _memkv_kernel(mem_ref, nw_ref, w_ref, kw_ref, k_ref, v_ref):
    x = mem_ref[0]
    ms = jnp.mean(x * x, axis=-1, keepdims=True)
    h = (x * lax.rsqrt(ms + RMS_EPS) * nw_ref[...]).astype(BF16)
    kv = _dot(h, w_ref[...])
    k = kv[:, :MEM_WIDTH]
    ms = _split_dot(k * k, _head_ones(MEM_WIDTH)) * (1.0 / HEAD_DIM)
    k_ref[0] = (k * lax.rsqrt(ms + RMS_EPS) * kw_ref[...]).astype(BF16)
    v_ref[0] = kv[:, MEM_WIDTH:].astype(BF16)


def _memkv(mem, nw, w_kv, kw):
    b, m, d = mem.shape
    full = lambda a: pl.BlockSpec(a.shape, lambda bi: (0,) * a.ndim)
    out = pl.BlockSpec((1, m, MEM_WIDTH), lambda bi: (bi, 0, 0))
    return pl.pallas_call(
        _memkv_kernel,
        grid=(b,),
        in_specs=[pl.BlockSpec((1, m, d), lambda bi: (bi, 0, 0)), full(nw), full(w_kv), full(kw)],
        out_specs=[out, out],
        out_shape=[jax.ShapeDtypeStruct((b, m, MEM_WIDTH), BF16)] * 2,
        compiler_params=_cparams(("parallel",)),
    )(mem, nw, w_kv, kw)


def _mixout_kernel(x_ref, na_ref, yf_ref, yb_ref, bonus_ref, g_ref, mq_ref, mk_ref, mv_ref,
                   lnw_ref, lnb_ref, wo_ref, o_ref):
    ones = _head_ones(RW_WIDTH)
    y = yf_ref[0] + yb_ref[0]
    mu = _split_dot(y, ones) * (1.0 / HEAD_DIM)
    yc = y - mu
    var = _split_dot(yc * yc, ones) * (1.0 / HEAD_DIM)
    yn = yc * lax.rsqrt(var + RW_GN_EPS) * lnw_ref[...] + lnb_ref[...]
    y_rw = ((yn + bonus_ref[0]) * g_ref[0]).astype(BF16)

    first = lax.broadcasted_iota(jnp.int32, (1, LANES), 1) < HEAD_DIM
    mems = []
    for p in range(MEM_HEADS // 2):
        ls = slice(p * LANES, (p + 1) * LANES)
        qp = mq_ref[0, :, ls]
        kp = mk_ref[0, :, ls]
        vp = mv_ref[0, :, ls]
        outs = []
        for hh in range(2):
            sel = first if hh == 0 else jnp.logical_not(first)
            s = _dot_nt(jnp.where(sel, qp, jnp.zeros_like(qp)), kp)
            m = jnp.max(s, axis=-1, keepdims=True)
            e = jnp.exp(s - m)
            l = jnp.sum(e, axis=-1, keepdims=True)
            outs.append(_dot(e.astype(BF16), vp) / l)
        mems.append(jnp.where(first, outs[0], outs[1]).astype(BF16))
    y_mem = jnp.concatenate(mems, axis=1)

    acc = _dot(na_ref[0], wo_ref[0:NA_WIDTH, :])
    acc = acc + _dot(y_rw, wo_ref[NA_WIDTH:NA_WIDTH + RW_WIDTH, :])
    acc = acc + _dot(y_mem, wo_ref[NA_WIDTH + RW_WIDTH:, :])
    o_ref[0] = x_ref[0] + acc


def _mixout(x, y_na, y_f, y_b, bonus, g, mq, mk, mv, ln_w, ln_b, w_out, tm):
    b, s, d = x.shape
    m = mk.shape[1]
    tok = lambda w: pl.BlockSpec((1, tm, w), lambda bi, i: (bi, i, 0))
    full = lambda a: pl.BlockSpec(a.shape, lambda bi, i: (0,) * a.ndim)
    memb = pl.BlockSpec((1, m, MEM_WIDTH), lambda bi, i: (bi, 0, 0))
    return pl.pallas_call(
        _mixout_kernel,
        grid=(b, s // tm),
        in_specs=[tok(d), tok(NA_WIDTH), tok(RW_WIDTH), tok(RW_WIDTH), tok(RW_WIDTH), tok(RW_WIDTH),
                  tok(MEM_WIDTH), memb, memb, full(ln_w), full(ln_b), full(w_out)],
        out_specs=tok(d),
        out_shape=jax.ShapeDtypeStruct((b, s, d), F32),
        compiler_params=_cparams(("parallel", "parallel")),
    )(x, y_na, y_f, y_b, bonus, g, mq, mk, mv, ln_w, ln_b, w_out)


ROUTER_LANES = 128


def _router_kernel(x_ref, nw_ref, whi_ref, wlo_ref, b_ref, h_ref, lg_ref):
    x = x_ref[...]
    ms = jnp.mean(x * x, axis=-1, keepdims=True)
    h = x * lax.rsqrt(ms + RMS_EPS) * nw_ref[...]
    hi = h.astype(BF16)
    lo = (h - hi.astype(F32)).astype(BF16)
    h_ref[...] = hi
    lg_ref[...] = _dot(hi, whi_ref[...]) + _dot(hi, wlo_ref[...]) + _dot(lo, whi_ref[...]) + b_ref[...]


def _router(x2, nw, w_group, b_group, w_expert, b_expert, tm):
    n, d = x2.shape
    pad = ROUTER_LANES - MOE_GROUPS - MOE_EXPERTS
    w = jnp.concatenate([w_group, w_expert, jnp.zeros((d, pad), F32)], axis=1)
    whi = w.astype(BF16)
    wlo = (w - whi.astype(F32)).astype(BF16)
    bias = jnp.concatenate([b_group, b_expert, jnp.zeros((pad,), F32)]).reshape(1, -1)
    full = lambda a: pl.BlockSpec(a.shape, lambda i: (0,) * a.ndim)
    return pl.pallas_call(
        _router_kernel,
        grid=(n // tm,),
        in_specs=[pl.BlockSpec((tm, d), lambda i: (i, 0)), full(nw), full(whi), full(wlo), full(bias)],
        out_specs=[pl.BlockSpec((tm, d), lambda i: (i, 0)), pl.BlockSpec((tm, ROUTER_LANES), lambda i: (i, 0))],
        out_shape=[jax.ShapeDtypeStruct((n, d), BF16), jax.ShapeDtypeStruct((n, ROUTER_LANES), F32)],
        compiler_params=_cparams(("parallel",)),
    )(x2, nw, whi, wlo, bias)


def _route_weights(logits):
    n = logits.shape[0]
    gl = logits[:, :MOE_GROUPS]
    el = logits[:, MOE_GROUPS:MOE_GROUPS + MOE_EXPERTS].reshape(n, MOE_GROUPS, MOE_EXPERTS_PER_GROUP)
    g_idx = jnp.argmax(gl, axis=-1)
    g_w = jnp.take_along_axis(jax.nn.softmax(gl, axis=-1), g_idx[:, None], axis=1)[:, 0]
    sel = jnp.take_along_axis(el, g_idx[:, None, None], axis=1)[:, 0]
    top_p, top_i = lax.top_k(jax.nn.softmax(sel, axis=-1), 2)
    top_p = top_p / jnp.sum(top_p, axis=-1, keepdims=True)
    w = g_w[:, None] * top_p
    e = g_idx[:, None] * MOE_EXPERTS_PER_GROUP + top_i
    onehot = (e[:, :, None] == jnp.arange(MOE_EXPERTS)[None, None, :]).astype(F32)
    return jnp.sum(onehot * w[:, :, None], axis=1)


def _experts_kernel(x_ref, h_ref, rw_ref, wg_ref, wu_ref, wd_ref, o_ref):
    e = pl.program_id(1)

    @pl.when(e == 0)
    def _():
        o_ref[...] = x_ref[...]

    h = h_ref[...]
    hid = jax.nn.silu(_dot(h, wg_ref[0])) * _dot(h, wu_ref[0])
    lane = lax.broadcasted_iota(jnp.int32, rw_ref.shape, 1)
    wt = jnp.sum(jnp.where(lane == e, rw_ref[...], 0.0), axis=1, keepdims=True)
    o_ref[...] += _dot(hid.astype(BF16), wd_ref[0]) * wt


def _experts(x2, h2, rwt, w_gate, w_up, w_down, tm):
    n, d = x2.shape
    ne = w_gate.shape[0]
    tok = lambda w: pl.BlockSpec((tm, w), lambda i, e: (i, 0))
    return pl.pallas_call(
        _experts_kernel,
        grid=(n // tm, ne),
        in_specs=[tok(d), tok(d), tok(ne),
                  pl.BlockSpec((1, d, MOE_D_FF), lambda i, e: (e, 0, 0)),
                  pl.BlockSpec((1, d, MOE_D_FF), lambda i, e: (e, 0, 0)),
                  pl.BlockSpec((1, MOE_D_FF, d), lambda i, e: (e, 0, 0))],
        out_specs=tok(d),
        out_shape=jax.ShapeDtypeStruct((n, d), F32),
        compiler_params=_cparams(("parallel", "arbitrary")),
    )(x2, h2, rwt, w_gate, w_up, w_down)


def _tile(n, want):
    t = min(n, want)
    assert n % t == 0
    return t


def kernel(x, mem, attn_norm_w, w_in, na_q_norm_w, na_k_norm_w, na_rpb, rw_mu_prev, rw_mu_next, rw_w0, rw_w2, rw_a0, rw_a2, rw_g2, rw_k_k, rw_k_a, rw_r_k, rw_ln_w, rw_ln_b, mem_norm_w, w_mem_kv, mem_q_norm_w, mem_k_norm_w, w_out, ffn_norm_w, moe_w_group, moe_b_group, moe_w_expert, moe_b_expert, moe_w_gate, moe_w_up, moe_w_down):
    b, s, d = x.shape
    n = b * s
    depth = w_in.shape[0]
    assert s % (NA_BLOCK_ROWS * GRID_W) == 0 and s // GRID_W >= 2 * NA_KH
    tm = _tile(n, 512)
    ts = _tile(s, 512)
    for l in range(depth):
        q, k, v, rw, mq = _proj(
            x.reshape(n, d), _row(attn_norm_w[l]), w_in[l].astype(BF16),
            _row(jnp.tile(na_q_norm_w[l], NA_HEADS)), _row(jnp.tile(na_k_norm_w[l], NA_HEADS)),
            _row(jnp.tile(mem_q_norm_w[l], MEM_HEADS)), tm)
        y_na = _na(q.reshape(b, s, NA_WIDTH), k.reshape(b, s, NA_WIDTH), v.reshape(b, s, NA_WIDTH),
                   _na_bias_table(na_rpb[l]))
        r, vv, kk, kd, bb, lw, g, bonus = _rwprep(
            rw.reshape(b, s, RW_PROJ), rw_mu_prev[l], rw_mu_next[l], rw_w0[l], rw_w2[l], rw_a0[l],
            rw_a2[l], rw_g2[l], rw_k_k[l], rw_k_a[l], rw_r_k[l], ts)
        y_f, y_b = _wkv2(r, vv, kk, kd, bb, lw)
        mk, mv = _memkv(mem, _row(mem_norm_w[l]), w_mem_kv[l].astype(BF16),
                        _row(jnp.tile(mem_k_norm_w[l], MEM_HEADS)))
        x = _mixout(x, y_na, y_f, y_b, bonus, g, mq.reshape(b, s, MEM_WIDTH), mk, mv,
                    _row(rw_ln_w[l]), _row(rw_ln_b[l]), w_out[l].astype(BF16), ts)
        h2, logits = _router(x.reshape(n, d), _row(ffn_norm_w[l]), moe_w_group[l], moe_b_group[l],
                             moe_w_expert[l], moe_b_expert[l], tm)
        x = _experts(x.reshape(n, d), h2, _route_weights(logits), moe_w_gate[l].astype(BF16),
                     moe_w_up[l].astype(BF16), moe_w_down[l].astype(BF16), _tile(n, 1024)).reshape(b, s, d)
    return x
```

```python
import functools

import jax
import jax.numpy as jnp
from jax import lax
from jax.experimental import pallas as pl
from jax.experimental.pallas import tpu as pltpu

F32 = jnp.float32
BF16 = jnp.bfloat16

GRID_W = 64
HEAD_DIM = 64
NA_HEADS = 8
NA_WIDTH = NA_HEADS * HEAD_DIM
NA_KH = 8
NA_KW = 16
RW_HEADS = 4
RW_WIDTH = RW_HEADS * HEAD_DIM
RW_LORA_W = 64
RW_LORA_A = 64
RW_LORA_G = 128
RW_PROJ = 3 * RW_WIDTH + 2 * RW_LORA_W + 2 * RW_LORA_A + RW_LORA_G
MEM_HEADS = 4
MEM_WIDTH = MEM_HEADS * HEAD_DIM
MOE_GROUPS = 4
MOE_EXPERTS_PER_GROUP = 4
MOE_EXPERTS = MOE_GROUPS * MOE_EXPERTS_PER_GROUP
MOE_D_FF = 512
RMS_EPS = 1e-6
RW_GN_EPS = 64e-5

LANES = 128
WKV_CHUNK = 64
NEG_BIG = -1e30
VMEM_LIMIT = 56 * 1024 * 1024


def _cparams(sem):
    return pltpu.CompilerParams(dimension_semantics=sem, vmem_limit_bytes=VMEM_LIMIT)


def _dot(a, b):
    return jnp.dot(a, b, preferred_element_type=F32)


def _dot_nt(a, b):
    return lax.dot_general(a, b, (((1,), (1,)), ((), ())), preferred_element_type=F32)


def _dot_tn(a, b):
    return lax.dot_general(a, b, (((0,), (0,)), ((), ())), preferred_element_type=F32)


def _split_dot(t, m):
    hi = t.astype(BF16)
    lo = (t - hi.astype(F32)).astype(BF16)
    return _dot(hi, m) + _dot(lo, m)


def _head_ones(width):
    i = lax.broadcasted_iota(jnp.int32, (width, width), 0) // HEAD_DIM
    j = lax.broadcasted_iota(jnp.int32, (width, width), 1) // HEAD_DIM
    return (i == j).astype(BF16)


def _row(v):
    return v.reshape(1, -1).astype(F32)


def _proj_kernel(x_ref, nw_ref, w_ref, qw_ref, kw_ref, mw_ref, q_ref, k_ref, v_ref, rw_ref, mq_ref):
    x = x_ref[...]
    ms = jnp.mean(x * x, axis=-1, keepdims=True)
    h = (x * lax.rsqrt(ms + RMS_EPS) * nw_ref[...]).astype(BF16)
    ones_na = _head_ones(NA_WIDTH)

    def head_norm(t, w, ones):
        ms = _split_dot(t * t, ones) * (1.0 / HEAD_DIM)
        return t * lax.rsqrt(ms + RMS_EPS) * w

    o = 0
    q = _dot(h, w_ref[:, o:o + NA_WIDTH])
    q_ref[...] = (head_norm(q, qw_ref[...], ones_na) * (HEAD_DIM ** -0.5)).astype(BF16)
    o += NA_WIDTH
    k = _dot(h, w_ref[:, o:o + NA_WIDTH])
    k_ref[...] = head_norm(k, kw_ref[...], ones_na).astype(BF16)
    o += NA_WIDTH
    v_ref[...] = _dot(h, w_ref[:, o:o + NA_WIDTH]).astype(BF16)
    o += NA_WIDTH
    rw_ref[...] = _dot(h, w_ref[:, o:o + RW_PROJ])
    o += RW_PROJ
    mq = _dot(h, w_ref[:, o:o + MEM_WIDTH])
    mq_ref[...] = (head_norm(mq, mw_ref[...], _head_ones(MEM_WIDTH)) * (HEAD_DIM ** -0.5)).astype(BF16)


def _proj(x2, nw, w_in, qw, kw, mw, tm):
    n, d = x2.shape
    p_in = w_in.shape[1]
    tok = lambda w: pl.BlockSpec((tm, w), lambda i: (i, 0))
    full = lambda a: pl.BlockSpec(a.shape, lambda i: (0,) * a.ndim)
    return pl.pallas_call(
        _proj_kernel,
        grid=(n // tm,),
        in_specs=[tok(d), full(nw), full(w_in), full(qw), full(kw), full(mw)],
        out_specs=[tok(NA_WIDTH), tok(NA_WIDTH), tok(NA_WIDTH), tok(RW_PROJ), tok(MEM_WIDTH)],
        out_shape=[jax.ShapeDtypeStruct((n, NA_WIDTH), BF16)] * 3
        + [jax.ShapeDtypeStruct((n, RW_PROJ), F32), jax.ShapeDtypeStruct((n, MEM_WIDTH), BF16)],
        compiler_params=_cparams(("parallel",)),
    )(x2, nw, w_in, qw, kw, mw)


NA_BLOCK_ROWS = 8


def _na_bias_table(rpb):
    col = jnp.arange(GRID_W)
    cs = jnp.clip(col - NA_KW // 2, 0, GRID_W - NA_KW)
    dc = col[None, :] - col[:, None]
    inside = (col[None, :] >= cs[:, None]) & (col[None, :] < cs[:, None] + NA_KW)
    dci = jnp.clip(dc + NA_KW - 1, 0, 2 * NA_KW - 2)
    off = jnp.arange(NA_KH) - (NA_KH - 1)
    dri = off[:, None] + jnp.arange(NA_KH)[None, :] + NA_KH - 1
    t = rpb[:, dri][:, :, :, dci]
    t = jnp.where(inside[None, None, None], t, NEG_BIG)
    t = jnp.transpose(t, (1, 0, 3, 2, 4))
    return t.reshape(NA_KH, NA_HEADS // 2, 2 * GRID_W, NA_KH * GRID_W).astype(F32)


def _na_kernel(q_ref, kp_ref, kc_ref, kn_ref, vp_ref, vc_ref, vn_ref, bias_ref, o_ref, kwin, vwin, *, rows):
    j = pl.program_id(1)
    blk = NA_BLOCK_ROWS * GRID_W
    kwin[0:blk] = kp_ref[0]
    kwin[blk:2 * blk] = kc_ref[0]
    kwin[2 * blk:3 * blk] = kn_ref[0]
    vwin[0:blk] = vp_ref[0]
    vwin[blk:2 * blk] = vc_ref[0]
    vwin[2 * blk:3 * blk] = vn_ref[0]
    lane = lax.broadcasted_iota(jnp.int32, (1, LANES), 1)
    first = lane < HEAD_DIM

    def row_body(i, carry):
        r = j * NA_BLOCK_ROWS + i
        rs = jnp.clip(r - NA_KH // 2, 0, rows - NA_KH)
        oi = rs - r + (NA_KH - 1)
        start = pl.multiple_of((rs - (j - 1) * NA_BLOCK_ROWS) * GRID_W, GRID_W)
        qs = pl.multiple_of(i * GRID_W, GRID_W)
        pairs = [slice(p * LANES, (p + 1) * LANES) for p in range(NA_HEADS // 2)]
        scores = []
        for p, ls in enumerate(pairs):
            qp = q_ref[0, pl.ds(qs, GRID_W), ls]
            zero = jnp.zeros_like(qp)
            qst = jnp.concatenate([jnp.where(first, qp, zero), jnp.where(first, zero, qp)], axis=0)
            scores.append(_dot_nt(qst, kwin[pl.ds(start, NA_KH * GRID_W), ls]) + bias_ref[oi, p])
        probs, norms = [], []
        for s in scores:
            e = jnp.exp(s - jnp.max(s, axis=-1, keepdims=True))
            norms.append(jnp.sum(e, axis=-1, keepdims=True))
            probs.append(e.astype(BF16))
        for ls, e, l in zip(pairs, probs, norms):
            o = _dot(e, vwin[pl.ds(start, NA_KH * GRID_W), ls]) / l
            o_ref[0, pl.ds(qs, GRID_W), ls] = jnp.where(first, o[:GRID_W], o[GRID_W:]).astype(o_ref.dtype)
        return carry

    lax.fori_loop(0, NA_BLOCK_ROWS, row_body, 0)


def _na(q, k, v, bias):
    b, s, w = q.shape
    rows = s // GRID_W
    nblk = rows // NA_BLOCK_ROWS
    blk = NA_BLOCK_ROWS * GRID_W
    cur = pl.BlockSpec((1, blk, w), lambda bi, j: (bi, j, 0))
    prv = pl.BlockSpec((1, blk, w), lambda bi, j: (bi, jnp.maximum(j - 1, 0), 0))
    nxt = pl.BlockSpec((1, blk, w), lambda bi, j: (bi, jnp.minimum(j + 1, nblk - 1), 0))
    return pl.pallas_call(
        functools.partial(_na_kernel, rows=rows),
        grid=(b, nblk),
        in_specs=[cur, prv, cur, nxt, prv, cur, nxt,
                  pl.BlockSpec(bias.shape, lambda bi, j: (0, 0, 0, 0))],
        out_specs=cur,
        out_shape=jax.ShapeDtypeStruct((b, s, w), BF16),
        scratch_shapes=[pltpu.VMEM((3 * blk, w), BF16), pltpu.VMEM((3 * blk, w), BF16)],
        compiler_params=_cparams(("parallel", "parallel")),
    )(q, k, k, k, v, v, v, bias)


def _rwprep_kernel(rw_ref, prev_ref, next_ref, mup_ref, mun_ref, w0_ref, w2_ref, a0_ref, a2_ref, g2_ref,
                   kk_w_ref, ka_ref, rk_ref,
                   r_ref, v_ref, kk_ref, kd_ref, b_ref, lw_ref, g_ref, bonus_ref, *, nblk):
    i = pl.program_id(1)
    s = rw_ref[0]
    ts = s.shape[0]
    rowi = lax.broadcasted_iota(jnp.int32, (ts, 1), 0)
    halo_p = jnp.where(i > 0, prev_ref[0, 7:8, :], 0.0)
    halo_n = jnp.where(i < nblk - 1, next_ref[0, 0:1, :], 0.0)
    prev = jnp.where(rowi == 0, halo_p, pltpu.roll(s, 1, 0))
    nxt = jnp.where(rowi == ts - 1, halo_n, pltpu.roll(s, ts - 1, 0))
    s = s + mup_ref[...] * (prev - s) + mun_ref[...] * (nxt - s)

    c = RW_WIDTH
    r = s[:, 0:c]
    k = s[:, c:2 * c]
    v = s[:, 2 * c:3 * c]
    lw = s[:, 3 * c:3 * c + 2 * RW_LORA_W]
    la = s[:, 3 * c + 2 * RW_LORA_W:3 * c + 2 * RW_LORA_W + 2 * RW_LORA_A]
    lg = s[:, 3 * c + 2 * RW_LORA_W + 2 * RW_LORA_A:]
    ones = _head_ones(c)

    g_ref[0] = _dot(jax.nn.sigmoid(lg).astype(BF16), g2_ref[...])
    kk = k * kk_w_ref[...]
    nrm = jnp.sqrt(_split_dot(kk * kk, ones))
    kk = kk / jnp.maximum(nrm, 1e-12)
    wl_pre = w0_ref[...] + _dot(jnp.tanh(lw).astype(BF16), w2_ref[...])
    a_all = jax.nn.sigmoid(a0_ref[...] + _dot(la.astype(BF16), a2_ref[...]))
    wl = -jax.nn.softplus(-wl_pre) - 0.5
    logdecay = -jnp.exp(wl)
    r_ref[0] = r
    v_ref[0] = v
    kk_ref[0] = kk
    kd_sum = jnp.zeros_like(k)
    for d in range(2):
        a = a_all[:, d * c:(d + 1) * c]
        kd = k * (1.0 + (a - 1.0) * ka_ref[...])
        kd_ref[d, 0] = kd
        b_ref[d, 0] = kk * a
        lw_ref[d, 0] = logdecay[:, d * c:(d + 1) * c]
        kd_sum = kd_sum + kd
    bonus_ref[0] = _split_dot(r * kd_sum * rk_ref[...], ones) * v


def _blockdiag2(m):
    z = jnp.zeros_like(m[0])
    return jnp.concatenate([jnp.concatenate([m[0], z], 1), jnp.concatenate([z, m[1]], 1)], 0)


def _rwprep(rw, mu_prev, mu_next, w0, w2, a0, a2, g2, k_k, k_a, r_k, ts):
    b, s, pw = rw.shape
    nblk = s // ts
    c = RW_WIDTH
    cur = pl.BlockSpec((1, ts, pw), lambda bi, i: (bi, i, 0))
    prv = pl.BlockSpec((1, 8, pw), lambda bi, i: (bi, jnp.maximum(i * (ts // 8) - 1, 0), 0))
    nxt = pl.BlockSpec((1, 8, pw), lambda bi, i: (bi, jnp.minimum((i + 1) * (ts // 8), s // 8 - 1), 0))
    params = [_row(mu_prev), _row(mu_next), _row(w0), _blockdiag2(w2).astype(BF16), _row(a0),
              _blockdiag2(a2).astype(BF16), g2.astype(BF16), _row(k_k), _row(k_a), _row(r_k)]
    full = lambda a: pl.BlockSpec(a.shape, lambda bi, i: (0,) * a.ndim)
    one = pl.BlockSpec((1, ts, c), lambda bi, i: (bi, i, 0))
    two = pl.BlockSpec((2, 1, ts, c), lambda bi, i: (0, bi, i, 0))
    s1 = jax.ShapeDtypeStruct((b, s, c), F32)
    s2 = jax.ShapeDtypeStruct((2, b, s, c), F32)
    return pl.pallas_call(
        functools.partial(_rwprep_kernel, nblk=nblk),
        grid=(b, nblk),
        in_specs=[cur, prv, nxt] + [full(p) for p in params],
        out_specs=[one, one, one, two, two, two, one, one],
        out_shape=[s1, s1, s1, s2, s2, s2, s1, s1],
        compiler_params=_cparams(("parallel", "parallel")),
    )(rw, rw, rw, *params)


WKV_STEP_CHUNKS = 4


def _wkv_kernel(r_ref, v_ref, kk_ref, kd_ref, b_ref, lw_ref, y_ref, s_ref, *, reverse):
    cc = WKV_CHUNK

    @pl.when(pl.program_id(1) == 0)
    def _():
        s_ref[...] = jnp.zeros_like(s_ref)

    row = lax.broadcasted_iota(jnp.int32, (cc, cc), 0)
    col = lax.broadcasted_iota(jnp.int32, (cc, cc), 1)
    strict = (col > row) if reverse else (col < row)
    incl = (col >= row) if reverse else (col <= row)
    eye = (col == row).astype(F32)
    rowc = lax.broadcasted_iota(jnp.int32, (cc, LANES), 0)
    first = lax.broadcasted_iota(jnp.int32, (1, LANES), 1) < HEAD_DIM
    same_head = ((lax.broadcasted_iota(jnp.int32, (LANES, LANES), 0) < HEAD_DIM)
                 == (lax.broadcasted_iota(jnp.int32, (LANES, LANES), 1) < HEAD_DIM))

    for step in range(WKV_STEP_CHUNKS):
        j = WKV_STEP_CHUNKS - 1 - step if reverse else step
        sl = slice(j * cc, (j + 1) * cc)
        for p in range(RW_HEADS // 2):
            ls = slice(p * LANES, (p + 1) * LANES)
            r = r_ref[0, sl, ls]
            v = v_ref[0, sl, ls]
            kk = kk_ref[0, sl, ls]
            kd = kd_ref[0, 0, sl, ls]
            b = b_ref[0, 0, sl, ls]
            lw = lw_ref[0, 0, sl, ls]
            cum = lw
            sh = 1
            while sh < cc:
                if reverse:
                    cum = cum + jnp.where(rowc < cc - sh, pltpu.roll(cum, cc - sh, 0), 0.0)
                else:
                    cum = cum + jnp.where(rowc >= sh, pltpu.roll(cum, sh, 0), 0.0)
                sh *= 2
            tot = cum[0:1] if reverse else cum[cc - 1:cc]
            at = -kk * jnp.exp(cum - lw)
            rt = r * jnp.exp(cum)
            einv = jnp.exp(-cum)
            kt = (kd * einv).astype(BF16)
            bt = (b * einv).astype(BF16)
            eh = jnp.exp(tot - cum)
            kh = kd * eh
            bh = b * eh
            vb = v.astype(BF16)
            qq = jnp.concatenate([at, rt], axis=0)
            atp, u0, y0, arb = [], [], [], []
            for hh in range(2):
                sel = first if hh == 0 else jnp.logical_not(first)
                qm = jnp.where(sel, qq, 0.0).astype(BF16)
                mb = _dot_nt(qm, bt)
                mk = _dot_nt(qm, kt)
                a_ab = jnp.where(strict, mb[:cc], 0.0)
                a_rb = jnp.where(incl, mb[cc:], 0.0)
                a_ak = jnp.where(strict, mk[:cc], 0.0)
                a_rk = jnp.where(incl, mk[cc:], 0.0)
                t = eye + a_ab
                pw = a_ab
                for _ in range(cc.bit_length() - 2):
                    pw = _dot(pw.astype(BF16), pw.astype(BF16))
                    t = t + _dot(t.astype(BF16), pw.astype(BF16))
                av = _dot(jnp.concatenate([a_ak, a_rk], axis=0).astype(BF16), vb)
                tx = _dot(t.astype(BF16), jnp.concatenate([at, av[:cc]], axis=1).astype(BF16))
                atp.append(tx[:, :LANES])
                u0.append(tx[:, LANES:])
                y0.append(av[cc:])
                arb.append(a_rb.astype(BF16))
            atp = jnp.where(first, atp[0], atp[1])
            u0 = jnp.where(first, u0[0], u0[1])
            y0 = jnp.where(first, y0[0], y0[1])
            st = s_ref[p]
            qs = _dot_nt(jnp.concatenate([atp, rt], axis=0).astype(BF16), st.astype(BF16))
            u = qs[:cc] + u0
            ub = u.astype(BF16)
            ru = jnp.where(first, _dot(arb[0], ub), _dot(arb[1], ub))
            y_ref[0, sl, ls] = qs[cc:] + y0 + ru
            upd = _dot_tn(jnp.concatenate([vb, ub], axis=0),
                          jnp.concatenate([kh, bh], axis=0).astype(BF16))
            s_ref[p] = st * jnp.exp(tot) + jnp.where(same_head, upd, 0.0)


def _wkv(r, v, kk, kd, bb, lw, d):
    b, s, c = r.shape
    tt = WKV_CHUNK * WKV_STEP_CHUNKS
    nblk = s // tt
    reverse = d == 1
    tix = (lambda i: nblk - 1 - i) if reverse else (lambda i: i)
    one = pl.BlockSpec((1, tt, c), lambda bi, i: (bi, tix(i), 0))
    two = pl.BlockSpec((1, 1, tt, c), lambda bi, i: (d, bi, tix(i), 0))
    return pl.pallas_call(
        functools.partial(_wkv_kernel, reverse=reverse),
        grid=(b, nblk),
        in_specs=[one, one, one, two, two, two],
        out_specs=one,
        out_shape=jax.ShapeDtypeStruct((b, s, c), F32),
        scratch_shapes=[pltpu.VMEM((RW_HEADS // 2, LANES, LANES), F32)],
        compiler_params=_cparams(("parallel", "arbitrary")),
    )(r, v, kk, kd, bb, lw)


WKV2_BATCH = 2


def _wkv2_kernel(rf, vf, kkf, rb, vb, kkb, kdf, bf, lwf, kdb, bb, lwb, yf_ref, yb_ref, s_ref):
    cc = WKV_CHUNK
    nb = rf.shape[0]
    nchunk = rf.shape[1] // cc

    @pl.when(pl.program_id(1) == 0)
    def _():
        s_ref[...] = jnp.zeros_like(s_ref)

    ri = lax.broadcasted_iota(jnp.int32, (LANES, LANES), 0)
    ci = lax.broadcasted_iota(jnp.int32, (LANES, LANES), 1)
    same_head = (ri < HEAD_DIM) == (ci < HEAD_DIM)
    rt_, ct_ = ri % cc, ci % cc
    eye = (ri == ci).astype(F32)
    strict = (same_head & (ct_ < rt_), same_head & (ct_ > rt_))
    incl = (same_head & (ct_ <= rt_), same_head & (ct_ >= rt_))
    rowc = lax.broadcasted_iota(jnp.int32, (cc, LANES), 0)
    first = lax.broadcasted_iota(jnp.int32, (1, LANES), 1) < HEAD_DIM
    second = jnp.logical_not(first)

    def stack_heads(t):
        return jnp.concatenate([jnp.where(first, t, 0.0), jnp.where(second, t, 0.0)], axis=0)

    def fold_heads(t):
        return t[:cc] + t[cc:]

    for c in range(nchunk):
        chains = []
        for bi in range(nb):
            for d in range(2):
                j = nchunk - 1 - c if d else c
                sl = slice(j * cc, (j + 1) * cc)
                src = (rb, vb, kkb, kdb, bb, lwb) if d else (rf, vf, kkf, kdf, bf, lwf)
                for p in range(RW_HEADS // 2):
                    ls = slice(p * LANES, (p + 1) * LANES)
                    chains.append(dict(
                        bi=bi, d=d, p=p, sl=sl, ls=ls,
                        r=src[0][bi, sl, ls], v=src[1][bi, sl, ls], kk=src[2][bi, sl, ls],
                        kd=src[3][0, bi, sl, ls], b=src[4][0, bi, sl, ls], lw=src[5][0, bi, sl, ls]))

        for ch in chains:
            cum = ch["lw"]
            sh = 1
            while sh < cc:
                if ch["d"]:
                    cum = cum + jnp.where(rowc < cc - sh, pltpu.roll(cum, cc - sh, 0), 0.0)
                else:
                    cum = cum + jnp.where(rowc >= sh, pltpu.roll(cum, sh, 0), 0.0)
                sh *= 2
            tot = cum[0:1] if ch["d"] else cum[cc - 1:cc]
            at = -ch["kk"] * jnp.exp(cum - ch["lw"])
            rt = ch["r"] * jnp.exp(cum)
            einv = jnp.exp(-cum)
            eh = jnp.exp(tot - cum)
            ch["etot"] = jnp.exp(tot)
            ch["rt"] = rt.astype(BF16)
            ch["vb"] = ch["v"].astype(BF16)
            ch["at_st"] = stack_heads(at).astype(BF16)
            ch["lhs"] = jnp.concatenate([ch["at_st"], stack_heads(rt).astype(BF16)], axis=0)
            ch["rhs"] = jnp.concatenate([stack_heads(ch["b"] * einv), stack_heads(ch["kd"] * einv)],
                                        axis=0).astype(BF16)
            ch["v_st"] = stack_heads(ch["v"]).astype(BF16)
            ch["kb_hat"] = jnp.concatenate([ch["kd"] * eh, ch["b"] * eh], axis=0).astype(BF16)
        for ch in chains:
            m1 = _dot_nt(ch["lhs"], ch["rhs"])
            d = ch["d"]
            ch["n"] = jnp.where(strict[d], m1[:LANES, :LANES], 0.0)
            a_ak = jnp.where(strict[d], m1[:LANES, LANES:], 0.0)
            ch["a_rb"] = jnp.where(incl[d], m1[LANES:, :LANES], 0.0).astype(BF16)
            a_rk = jnp.where(incl[d], m1[LANES:, LANES:], 0.0)
            ch["a_k"] = jnp.concatenate([a_ak, a_rk], axis=0).astype(BF16)
        for ch in chains:
            nbf = ch["n"].astype(BF16)
            ch["pw"] = _dot(nbf, nbf)
            ch["t"] = eye + ch["n"]
        for _ in range(cc.bit_length() - 3):
            for ch in chains:
                pwb = ch["pw"].astype(BF16)
                tp = _dot(jnp.concatenate([ch["t"].astype(BF16), pwb], axis=0), pwb)
                ch["t"] = ch["t"] + tp[:LANES]
                ch["pw"] = tp[LANES:]
        for ch in chains:
            ch["t"] = (ch["t"] + _dot(ch["t"].astype(BF16), ch["pw"].astype(BF16))).astype(BF16)
        for ch in chains:
            ch["av"] = _dot(ch["a_k"], ch["v_st"])
        for ch in chains:
            tx = _dot(ch["t"], jnp.concatenate([ch["at_st"], ch["av"][:LANES].astype(BF16)], axis=1))
            ch["atp"] = fold_heads(tx[:, :LANES]).astype(BF16)
            ch["u0"] = fold_heads(tx[:, LANES:])
            ch["y0"] = fold_heads(ch["av"][LANES:])

        for ch in chains:
            ch["st"] = s_ref[ch["bi"], ch["d"], ch["p"]]
            qs = _dot_nt(jnp.concatenate([ch["atp"], ch["rt"]], axis=0), ch["st"].astype(BF16))
            ch["u"] = qs[:cc] + ch["u0"]
            ch["ys"] = qs[cc:] + ch["y0"]
        for ch in chains:
            upd = _dot_tn(jnp.concatenate([ch["vb"], ch["u"].astype(BF16)], axis=0), ch["kb_hat"])
            s_ref[ch["bi"], ch["d"], ch["p"]] = ch["st"] * ch["etot"] + jnp.where(same_head, upd, 0.0)
        for ch in chains:
            ru = _dot(ch["a_rb"], stack_heads(ch["u"]).astype(BF16))
            y_ref = yb_ref if ch["d"] else yf_ref
            y_ref[ch["bi"], ch["sl"], ch["ls"]] = ch["ys"] + fold_heads(ru)


def _wkv2(r, v, kk, kd, bb, lw):
    b, s, c = r.shape
    tt = WKV_CHUNK * WKV_STEP_CHUNKS
    nblk = s // tt
    nb = WKV2_BATCH if b % WKV2_BATCH == 0 else 1
    fwd = pl.BlockSpec((nb, tt, c), lambda bi, i: (bi, i, 0))
    bwd = pl.BlockSpec((nb, tt, c), lambda bi, i: (bi, nblk - 1 - i, 0))
    fwd2 = pl.BlockSpec((1, nb, tt, c), lambda bi, i: (0, bi, i, 0))
    bwd2 = pl.BlockSpec((1, nb, tt, c), lambda bi, i: (1, bi, nblk - 1 - i, 0))
    return pl.pallas_call(
        _wkv2_kernel,
        grid=(b // nb, nblk),
        in_specs=[fwd, fwd, fwd, bwd, bwd, bwd, fwd2, fwd2, fwd2, bwd2, bwd2, bwd2],
        out_specs=[fwd, bwd],
        out_shape=[jax.ShapeDtypeStruct((b, s, c), F32)] * 2,
        scratch_shapes=[pltpu.VMEM((nb, 2, RW_HEADS // 2, LANES, LANES), F32)],
        compiler_params=_cparams(("parallel", "arbitrary")),
    )(r, v, kk, r, v, kk, kd, bb, lw, kd, bb, lw)


def _memkv_kernel(mem_ref, nw_ref, w_ref, kw_ref, k_ref, v_ref):
    x = mem_ref[0]
    ms = jnp.mean(x * x, axis=-1, keepdims=True)
    h = (x * lax.rsqrt(ms + RMS_EPS) * nw_ref[...]).astype(BF16)
    kv = _dot(h, w_ref[...])
    k = kv[:, :MEM_WIDTH]
    ms = _split_dot(k * k, _head_ones(MEM_WIDTH)) * (1.0 / HEAD_DIM)
    k_ref[0] = (k * lax.rsqrt(ms + RMS_EPS) * kw_ref[...]).astype(BF16)
    v_ref[0] = kv[:, MEM_WIDTH:].astype(BF16)


def _memkv(mem, nw, w_kv, kw):
    b, m, d = mem.shape
    full = lambda a: pl.BlockSpec(a.shape, lambda bi: (0,) * a.ndim)
    out = pl.BlockSpec((1, m, MEM_WIDTH), lambda bi: (bi, 0, 0))
    return pl.pallas_call(
        _memkv_kernel,
        grid=(b,),
        in_specs=[pl.BlockSpec((1, m, d), lambda bi: (bi, 0, 0)), full(nw), full(w_kv), full(kw)],
        out_specs=[out, out],
        out_shape=[jax.ShapeDtypeStruct((b, m, MEM_WIDTH), BF16)] * 2,
        compiler_params=_cparams(("parallel",)),
    )(mem, nw, w_kv, kw)


def _mixout_kernel(x_ref, na_ref, yf_ref, yb_ref, bonus_ref, g_ref, mq_ref, mk_ref, mv_ref,
                   lnw_ref, lnb_ref, wo_ref, o_ref):
    ones = _head_ones(RW_WIDTH)
    y = yf_ref[0] + yb_ref[0]
    mu = _split_dot(y, ones) * (1.0 / HEAD_DIM)
    yc = y - mu
    var = _split_dot(yc * yc, ones) * (1.0 / HEAD_DIM)
    yn = yc * lax.rsqrt(var + RW_GN_EPS) * lnw_ref[...] + lnb_ref[...]
    y_rw = ((yn + bonus_ref[0]) * g_ref[0]).astype(BF16)

    first = lax.broadcasted_iota(jnp.int32, (1, LANES), 1) < HEAD_DIM
    mems = []
    for p in range(MEM_HEADS // 2):
        ls = slice(p * LANES, (p + 1) * LANES)
        qp = mq_ref[0, :, ls]
        kp = mk_ref[0, :, ls]
        vp = mv_ref[0, :, ls]
        outs = []
        for hh in range(2):
            sel = first if hh == 0 else jnp.logical_not(first)
            s = _dot_nt(jnp.where(sel, qp, jnp.zeros_like(qp)), kp)
            m = jnp.max(s, axis=-1, keepdims=True)
            e = jnp.exp(s - m)
            l = jnp.sum(e, axis=-1, keepdims=True)
            outs.append(_dot(e.astype(BF16), vp) / l)
        mems.append(jnp.where(first, outs[0], outs[1]).astype(BF16))
    y_mem = jnp.concatenate(mems, axis=1)

    acc = _dot(na_ref[0], wo_ref[0:NA_WIDTH, :])
    acc = acc + _dot(y_rw, wo_ref[NA_WIDTH:NA_WIDTH + RW_WIDTH, :])
    acc = acc + _dot(y_mem, wo_ref[NA_WIDTH + RW_WIDTH:, :])
    o_ref[0] = x_ref[0] + acc


def _mixout(x, y_na, y_f, y_b, bonus, g, mq, mk, mv, ln_w, ln_b, w_out, tm):
    b, s, d = x.shape
    m = mk.shape[1]
    tok = lambda w: pl.BlockSpec((1, tm, w), lambda bi, i: (bi, i, 0))
    full = lambda a: pl.BlockSpec(a.shape, lambda bi, i: (0,) * a.ndim)
    memb = pl.BlockSpec((1, m, MEM_WIDTH), lambda bi, i: (bi, 0, 0))
    return pl.pallas_call(
        _mixout_kernel,
        grid=(b, s // tm),
        in_specs=[tok(d), tok(NA_WIDTH), tok(RW_WIDTH), tok(RW_WIDTH), tok(RW_WIDTH), tok(RW_WIDTH),
                  tok(MEM_WIDTH), memb, memb, full(ln_w), full(ln_b), full(w_out)],
        out_specs=tok(d),
        out_shape=jax.ShapeDtypeStruct((b, s, d), F32),
        compiler_params=_cparams(("parallel", "parallel")),
    )(x, y_na, y_f, y_b, bonus, g, mq, mk, mv, ln_w, ln_b, w_out)


ROUTER_LANES = 128


def _router_kernel(x_ref, nw_ref, whi_ref, wlo_ref, b_ref, h_ref, lg_ref):
    x = x_ref[...]
    ms = jnp.mean(x * x, axis=-1, keepdims=True)
    h = x * lax.rsqrt(ms + RMS_EPS) * nw_ref[...]
    hi = h.astype(BF16)
    lo = (h - hi.astype(F32)).astype(BF16)
    h_ref[...] = hi
    lg_ref[...] = _dot(hi, whi_ref[...]) + _dot(hi, wlo_ref[...]) + _dot(lo, whi_ref[...]) + b_ref[...]


def _router(x2, nw, w_group, b_group, w_expert, b_expert, tm):
    n, d = x2.shape
    pad = ROUTER_LANES - MOE_GROUPS - MOE_EXPERTS
    w = jnp.concatenate([w_group, w_expert, jnp.zeros((d, pad), F32)], axis=1)
    whi = w.astype(BF16)
    wlo = (w - whi.astype(F32)).astype(BF16)
    bias = jnp.concatenate([b_group, b_expert, jnp.zeros((pad,), F32)]).reshape(1, -1)
    full = lambda a: pl.BlockSpec(a.shape, lambda i: (0,) * a.ndim)
    return pl.pallas_call(
        _router_kernel,
        grid=(n // tm,),
        in_specs=[pl.BlockSpec((tm, d), lambda i: (i, 0)), full(nw), full(whi), full(wlo), full(bias)],
        out_specs=[pl.BlockSpec((tm, d), lambda i: (i, 0)), pl.BlockSpec((tm, ROUTER_LANES), lambda i: (i, 0))],
        out_shape=[jax.ShapeDtypeStruct((n, d), BF16), jax.ShapeDtypeStruct((n, ROUTER_LANES), F32)],
        compiler_params=_cparams(("parallel",)),
    )(x2, nw, whi, wlo, bias)


def _route_weights(logits):
    n = logits.shape[0]
    gl = logits[:, :MOE_GROUPS]
    el = logits[:, MOE_GROUPS:MOE_GROUPS + MOE_EXPERTS].reshape(n, MOE_GROUPS, MOE_EXPERTS_PER_GROUP)
    g_idx = jnp.argmax(gl, axis=-1)
    g_w = jnp.take_along_axis(jax.nn.softmax(gl, axis=-1), g_idx[:, None], axis=1)[:, 0]
    sel = jnp.take_along_axis(el, g_idx[:, None, None], axis=1)[:, 0]
    top_p, top_i = lax.top_k(jax.nn.softmax(sel, axis=-1), 2)
    top_p = top_p / jnp.sum(top_p, axis=-1, keepdims=True)
    w = g_w[:, None] * top_p
    e = g_idx[:, None] * MOE_EXPERTS_PER_GROUP + top_i
    onehot = (e[:, :, None] == jnp.arange(MOE_EXPERTS)[None, None, :]).astype(F32)
    return jnp.sum(onehot * w[:, :, None], axis=1)


def _experts_kernel(x_ref, h_ref, rw_ref, wg_ref, wu_ref, wd_ref, o_ref):
    e = pl.program_id(1)

    @pl.when(e == 0)
    def _():
        o_ref[...] = x_ref[...]

    h = h_ref[...]
    hid = jax.nn.silu(_dot(h, wg_ref[0])) * _dot(h, wu_ref[0])
    lane = lax.broadcasted_iota(jnp.int32, rw_ref.shape, 1)
    wt = jnp.sum(jnp.where(lane == e, rw_ref[...], 0.0), axis=1, keepdims=True)
    o_ref[...] += _dot(hid.astype(BF16), wd_ref[0]) * wt


def _experts(x2, h2, rwt, w_gate, w_up, w_down, tm):
    n, d = x2.shape
    ne = w_gate.shape[0]
    tok = lambda w: pl.BlockSpec((tm, w), lambda i, e: (i, 0))
    return pl.pallas_call(
        _experts_kernel,
        grid=(n // tm, ne),
        in_specs=[tok(d), tok(d), tok(ne),
                  pl.BlockSpec((1, d, MOE_D_FF), lambda i, e: (e, 0, 0)),
                  pl.BlockSpec((1, d, MOE_D_FF), lambda i, e: (e, 0, 0)),
                  pl.BlockSpec((1, MOE_D_FF, d), lambda i, e: (e, 0, 0))],
        out_specs=tok(d),
        out_shape=jax.ShapeDtypeStruct((n, d), F32),
        compiler_params=_cparams(("parallel", "arbitrary")),
    )(x2, h2, rwt, w_gate, w_up, w_down)


def _tile(n, want):
    t = min(n, want)
    assert n % t == 0
    return t


def kernel(x, mem, attn_norm_w, w_in, na_q_norm_w, na_k_norm_w, na_rpb, rw_mu_prev, rw_mu_next, rw_w0, rw_w2, rw_a0, rw_a2, rw_g2, rw_k_k, rw_k_a, rw_r_k, rw_ln_w, rw_ln_b, mem_norm_w, w_mem_kv, mem_q_norm_w, mem_k_norm_w, w_out, ffn_norm_w, moe_w_group, moe_b_group, moe_w_expert, moe_b_expert, moe_w_gate, moe_w_up, moe_w_down):
    b, s, d = x.shape
    n = b * s
    depth = w_in.shape[0]
    assert s % (NA_BLOCK_ROWS * GRID_W) == 0 and s // GRID_W >= 2 * NA_KH
    tm = _tile(n, 512)
    ts = _tile(s, 512)
    for l in range(depth):
        q, k, v, rw, mq = _proj(
            x.reshape(n, d), _row(attn_norm_w[l]), w_in[l].astype(BF16),
            _row(jnp.tile(na_q_norm_w[l], NA_HEADS)), _row(jnp.tile(na_k_norm_w[l], NA_HEADS)),
            _row(jnp.tile(mem_q_norm_w[l], MEM_HEADS)), tm)
        y_na = _na(q.reshape(b, s, NA_WIDTH), k.reshape(b, s, NA_WIDTH), v.reshape(b, s, NA_WIDTH),
                   _na_bias_table(na_rpb[l]))
        r, vv, kk, kd, bb, lw, g, bonus = _rwprep(
            rw.reshape(b, s, RW_PROJ), rw_mu_prev[l], rw_mu_next[l], rw_w0[l], rw_w2[l], rw_a0[l],
            rw_a2[l], rw_g2[l], rw_k_k[l], rw_k_a[l], rw_r_k[l], ts)
        y_f, y_b = _wkv2(r, vv, kk, kd, bb, lw)
        mk, mv = _memkv(mem, _row(mem_norm_w[l]), w_mem_kv[l].astype(BF16),
                        _row(jnp.tile(mem_k_norm_w[l], MEM_HEADS)))
        x = _mixout(x, y_na, y_f, y_b, bonus, g, mq.reshape(b, s, MEM_WIDTH), mk, mv,
                    _row(rw_ln_w[l]), _row(rw_ln_b[l]), w_out[l].astype(BF16), ts)
        h2, logits = _router(x.reshape(n, d), _row(ffn_norm_w[l]), moe_w_group[l], moe_b_group[l],
                             moe_w_expert[l], moe_b_expert[l], tm)
        x = _experts(x.reshape(n, d), h2, _route_weights(logits), moe_w_gate[l].astype(BF16),
                     moe_w_up[l].astype(BF16), moe_w_down[l].astype(BF16), _tile(n, 1024)).reshape(b, s, d)
    return x
```

```python
import functools

import jax
import jax.numpy as jnp
from jax import lax
from jax.experimental import pallas as pl
from jax.experimental.pallas import tpu as pltpu

F32 = jnp.float32
BF16 = jnp.bfloat16

GRID_W = 64
HEAD_DIM = 64
NA_HEADS = 8
NA_WIDTH = NA_HEADS * HEAD_DIM
NA_KH = 8
NA_KW = 16
RW_HEADS = 4
RW_WIDTH = RW_HEADS * HEAD_DIM
RW_LORA_W = 64
RW_LORA_A = 64
RW_LORA_G = 128
RW_PROJ = 3 * RW_WIDTH + 2 * RW_LORA_W + 2 * RW_LORA_A + RW_LORA_G
MEM_HEADS = 4
MEM_WIDTH = MEM_HEADS * HEAD_DIM
MOE_GROUPS = 4
MOE_EXPERTS_PER_GROUP = 4
MOE_EXPERTS = MOE_GROUPS * MOE_EXPERTS_PER_GROUP
MOE_D_FF = 512
RMS_EPS = 1e-6
RW_GN_EPS = 64e-5

LANES = 128
WKV_CHUNK = 64
NEG_BIG = -1e30
VMEM_LIMIT = 56 * 1024 * 1024


def _cparams(sem):
    return pltpu.CompilerParams(dimension_semantics=sem, vmem_limit_bytes=VMEM_LIMIT)


def _dot(a, b):
    return jnp.dot(a, b, preferred_element_type=F32)


def _dot_nt(a, b):
    return lax.dot_general(a, b, (((1,), (1,)), ((), ())), preferred_element_type=F32)


def _dot_tn(a, b):
    return lax.dot_general(a, b, (((0,), (0,)), ((), ())), preferred_element_type=F32)


def _split_dot(t, m):
    hi = t.astype(BF16)
    lo = (t - hi.astype(F32)).astype(BF16)
    return _dot(hi, m) + _dot(lo, m)


def _head_ones(width):
    i = lax.broadcasted_iota(jnp.int32, (width, width), 0) // HEAD_DIM
    j = lax.broadcasted_iota(jnp.int32, (width, width), 1) // HEAD_DIM
    return (i == j).astype(BF16)


def _row(v):
    return v.reshape(1, -1).astype(F32)


def _proj_kernel(x_ref, nw_ref, w_ref, qw_ref, kw_ref, mw_ref, q_ref, k_ref, v_ref, rw_ref, mq_ref):
    x = x_ref[...]
    ms = jnp.mean(x * x, axis=-1, keepdims=True)
    h = (x * lax.rsqrt(ms + RMS_EPS) * nw_ref[...]).astype(BF16)
    ones_na = _head_ones(NA_WIDTH)

    def head_norm(t, w, ones):
        ms = _split_dot(t * t, ones) * (1.0 / HEAD_DIM)
        return t * lax.rsqrt(ms + RMS_EPS) * w

    o = 0
    q = _dot(h, w_ref[:, o:o + NA_WIDTH])
    q_ref[...] = (head_norm(q, qw_ref[...], ones_na) * (HEAD_DIM ** -0.5)).astype(BF16)
    o += NA_WIDTH
    k = _dot(h, w_ref[:, o:o + NA_WIDTH])
    k_ref[...] = head_norm(k, kw_ref[...], ones_na).astype(BF16)
    o += NA_WIDTH
    v_ref[...] = _dot(h, w_ref[:, o:o + NA_WIDTH]).astype(BF16)
    o += NA_WIDTH
    rw_ref[...] = _dot(h, w_ref[:, o:o + RW_PROJ])
    o += RW_PROJ
    mq = _dot(h, w_ref[:, o:o + MEM_WIDTH])
    mq_ref[...] = (head_norm(mq, mw_ref[...], _head_ones(MEM_WIDTH)) * (HEAD_DIM ** -0.5)).astype(BF16)


def _proj(x2, nw, w_in, qw, kw, mw, tm):
    n, d = x2.shape
    p_in = w_in.shape[1]
    tok = lambda w: pl.BlockSpec((tm, w), lambda i: (i, 0))
    full = lambda a: pl.BlockSpec(a.shape, lambda i: (0,) * a.ndim)
    return pl.pallas_call(
        _proj_kernel,
        grid=(n // tm,),
        in_specs=[tok(d), full(nw), full(w_in), full(qw), full(kw), full(mw)],
        out_specs=[tok(NA_WIDTH), tok(NA_WIDTH), tok(NA_WIDTH), tok(RW_PROJ), tok(MEM_WIDTH)],
        out_shape=[jax.ShapeDtypeStruct((n, NA_WIDTH), BF16)] * 3
        + [jax.ShapeDtypeStruct((n, RW_PROJ), F32), jax.ShapeDtypeStruct((n, MEM_WIDTH), BF16)],
        compiler_params=_cparams(("parallel",)),
    )(x2, nw, w_in, qw, kw, mw)


NA_BLOCK_ROWS = 8


def _na_bias_table(rpb):
    col = jnp.arange(GRID_W)
    cs = jnp.clip(col - NA_KW // 2, 0, GRID_W - NA_KW)
    dc = col[None, :] - col[:, None]
    inside = (col[None, :] >= cs[:, None]) & (col[None, :] < cs[:, None] + NA_KW)
    dci = jnp.clip(dc + NA_KW - 1, 0, 2 * NA_KW - 2)
    off = jnp.arange(NA_KH) - (NA_KH - 1)
    dri = off[:, None] + jnp.arange(NA_KH)[None, :] + NA_KH - 1
    t = rpb[:, dri][:, :, :, dci]
    t = jnp.where(inside[None, None, None], t, NEG_BIG)
    t = jnp.transpose(t, (1, 0, 3, 2, 4))
    return t.reshape(NA_KH, NA_HEADS // 2, 2 * GRID_W, NA_KH * GRID_W).astype(F32)


def _na_kernel(q_ref, kp_ref, kc_ref, kn_ref, vp_ref, vc_ref, vn_ref, bias_ref, o_ref, kwin, vwin, *, rows):
    j = pl.program_id(1)
    blk = NA_BLOCK_ROWS * GRID_W
    kwin[0:blk] = kp_ref[0]
    kwin[blk:2 * blk] = kc_ref[0]
    kwin[2 * blk:3 * blk] = kn_ref[0]
    vwin[0:blk] = vp_ref[0]
    vwin[blk:2 * blk] = vc_ref[0]
    vwin[2 * blk:3 * blk] = vn_ref[0]
    lane = lax.broadcasted_iota(jnp.int32, (1, LANES), 1)
    first = lane < HEAD_DIM

    def row_body(i, carry):
        r = j * NA_BLOCK_ROWS + i
        rs = jnp.clip(r - NA_KH // 2, 0, rows - NA_KH)
        oi = rs - r + (NA_KH - 1)
        start = pl.multiple_of((rs - (j - 1) * NA_BLOCK_ROWS) * GRID_W, GRID_W)
        qs = pl.multiple_of(i * GRID_W, GRID_W)
        pairs = [slice(p * LANES, (p + 1) * LANES) for p in range(NA_HEADS // 2)]
        scores = []
        for p, ls in enumerate(pairs):
            qp = q_ref[0, pl.ds(qs, GRID_W), ls]
            zero = jnp.zeros_like(qp)
            qst = jnp.concatenate([jnp.where(first, qp, zero), jnp.where(first, zero, qp)], axis=0)
            scores.append(_dot_nt(qst, kwin[pl.ds(start, NA_KH * GRID_W), ls]) + bias_ref[oi, p])
        probs, norms = [], []
        for s in scores:
            e = jnp.exp(s - jnp.max(s, axis=-1, keepdims=True))
            norms.append(jnp.sum(e, axis=-1, keepdims=True))
            probs.append(e.astype(BF16))
        for ls, e, l in zip(pairs, probs, norms):
            o = _dot(e, vwin[pl.ds(start, NA_KH * GRID_W), ls]) / l
            o_ref[0, pl.ds(qs, GRID_W), ls] = jnp.where(first, o[:GRID_W], o[GRID_W:]).astype(o_ref.dtype)
        return carry

    lax.fori_loop(0, NA_BLOCK_ROWS, row_body, 0)


def _na(q, k, v, bias):
    b, s, w = q.shape
    rows = s // GRID_W
    nblk = rows // NA_BLOCK_ROWS
    blk = NA_BLOCK_ROWS * GRID_W
    cur = pl.BlockSpec((1, blk, w), lambda bi, j: (bi, j, 0))
    prv = pl.BlockSpec((1, blk, w), lambda bi, j: (bi, jnp.maximum(j - 1, 0), 0))
    nxt = pl.BlockSpec((1, blk, w), lambda bi, j: (bi, jnp.minimum(j + 1, nblk - 1), 0))
    return pl.pallas_call(
        functools.partial(_na_kernel, rows=rows),
        grid=(b, nblk),
        in_specs=[cur, prv, cur, nxt, prv, cur, nxt,
                  pl.BlockSpec(bias.shape, lambda bi, j: (0, 0, 0, 0))],
        out_specs=cur,
        out_shape=jax.ShapeDtypeStruct((b, s, w), BF16),
        scratch_shapes=[pltpu.VMEM((3 * blk, w), BF16), pltpu.VMEM((3 * blk, w), BF16)],
        compiler_params=_cparams(("parallel", "parallel")),
    )(q, k, k, k, v, v, v, bias)


def _rwprep_kernel(rw_ref, prev_ref, next_ref, mup_ref, mun_ref, w0_ref, w2_ref, a0_ref, a2_ref, g2_ref,
                   kk_w_ref, ka_ref, rk_ref,
                   r_ref, v_ref, kk_ref, kd_ref, b_ref, lw_ref, g_ref, bonus_ref, *, nblk):
    i = pl.program_id(1)
    s = rw_ref[0]
    ts = s.shape[0]
    rowi = lax.broadcasted_iota(jnp.int32, (ts, 1), 0)
    halo_p = jnp.where(i > 0, prev_ref[0, 7:8, :], 0.0)
    halo_n = jnp.where(i < nblk - 1, next_ref[0, 0:1, :], 0.0)
    prev = jnp.where(rowi == 0, halo_p, pltpu.roll(s, 1, 0))
    nxt = jnp.where(rowi == ts - 1, halo_n, pltpu.roll(s, ts - 1, 0))
    s = s + mup_ref[...] * (prev - s) + mun_ref[...] * (nxt - s)

    c = RW_WIDTH
    r = s[:, 0:c]
    k = s[:, c:2 * c]
    v = s[:, 2 * c:3 * c]
    lw = s[:, 3 * c:3 * c + 2 * RW_LORA_W]
    la = s[:, 3 * c + 2 * RW_LORA_W:3 * c + 2 * RW_LORA_W + 2 * RW_LORA_A]
    lg = s[:, 3 * c + 2 * RW_LORA_W + 2 * RW_LORA_A:]
    ones = _head_ones(c)

    g_ref[0] = _dot(jax.nn.sigmoid(lg).astype(BF16), g2_ref[...])
    kk = k * kk_w_ref[...]
    nrm = jnp.sqrt(_split_dot(kk * kk, ones))
    kk = kk / jnp.maximum(nrm, 1e-12)
    wl_pre = w0_ref[...] + _dot(jnp.tanh(lw).astype(BF16), w2_ref[...])
    a_all = jax.nn.sigmoid(a0_ref[...] + _dot(la.astype(BF16), a2_ref[...]))
    wl = -jax.nn.softplus(-wl_pre) - 0.5
    logdecay = -jnp.exp(wl)
    r_ref[0] = r
    v_ref[0] = v
    kk_ref[0] = kk
    kd_sum = jnp.zeros_like(k)
    for d in range(2):
        a = a_all[:, d * c:(d + 1) * c]
        kd = k * (1.0 + (a - 1.0) * ka_ref[...])
        kd_ref[d, 0] = kd
        b_ref[d, 0] = kk * a
        lw_ref[d, 0] = logdecay[:, d * c:(d + 1) * c]
        kd_sum = kd_sum + kd
    bonus_ref[0] = _split_dot(r * kd_sum * rk_ref[...], ones) * v


def _blockdiag2(m):
    z = jnp.zeros_like(m[0])
    return jnp.concatenate([jnp.concatenate([m[0], z], 1), jnp.concatenate([z, m[1]], 1)], 0)


def _rwprep(rw, mu_prev, mu_next, w0, w2, a0, a2, g2, k_k, k_a, r_k, ts):
    b, s, pw = rw.shape
    nblk = s // ts
    c = RW_WIDTH
    cur = pl.BlockSpec((1, ts, pw), lambda bi, i: (bi, i, 0))
    prv = pl.BlockSpec((1, 8, pw), lambda bi, i: (bi, jnp.maximum(i * (ts // 8) - 1, 0), 0))
    nxt = pl.BlockSpec((1, 8, pw), lambda bi, i: (bi, jnp.minimum((i + 1) * (ts // 8), s // 8 - 1), 0))
    params = [_row(mu_prev), _row(mu_next), _row(w0), _blockdiag2(w2).astype(BF16), _row(a0),
              _blockdiag2(a2).astype(BF16), g2.astype(BF16), _row(k_k), _row(k_a), _row(r_k)]
    full = lambda a: pl.BlockSpec(a.shape, lambda bi, i: (0,) * a.ndim)
    one = pl.BlockSpec((1, ts, c), lambda bi, i: (bi, i, 0))
    two = pl.BlockSpec((2, 1, ts, c), lambda bi, i: (0, bi, i, 0))
    s1 = jax.ShapeDtypeStruct((b, s, c), F32)
    s2 = jax.ShapeDtypeStruct((2, b, s, c), F32)
    return pl.pallas_call(
        functools.partial(_rwprep_kernel, nblk=nblk),
        grid=(b, nblk),
        in_specs=[cur, prv, nxt] + [full(p) for p in params],
        out_specs=[one, one, one, two, two, two, one, one],
        out_shape=[s1, s1, s1, s2, s2, s2, s1, s1],
        compiler_params=_cparams(("parallel", "parallel")),
    )(rw, rw, rw, *params)


WKV_STEP_CHUNKS = 4


def _wkv_kernel(r_ref, v_ref, kk_ref, kd_ref, b_ref, lw_ref, y_ref, s_ref, *, reverse):
    cc = WKV_CHUNK

    @pl.when(pl.program_id(1) == 0)
    def _():
        s_ref[...] = jnp.zeros_like(s_ref)

    row = lax.broadcasted_iota(jnp.int32, (cc, cc), 0)
    col = lax.broadcasted_iota(jnp.int32, (cc, cc), 1)
    strict = (col > row) if reverse else (col < row)
    incl = (col >= row) if reverse else (col <= row)
    eye = (col == row).astype(F32)
    rowc = lax.broadcasted_iota(jnp.int32, (cc, LANES), 0)
    first = lax.broadcasted_iota(jnp.int32, (1, LANES), 1) < HEAD_DIM
    same_head = ((lax.broadcasted_iota(jnp.int32, (LANES, LANES), 0) < HEAD_DIM)
                 == (lax.broadcasted_iota(jnp.int32, (LANES, LANES), 1) < HEAD_DIM))

    for step in range(WKV_STEP_CHUNKS):
        j = WKV_STEP_CHUNKS - 1 - step if reverse else step
        sl = slice(j * cc, (j + 1) * cc)
        for p in range(RW_HEADS // 2):
            ls = slice(p * LANES, (p + 1) * LANES)
            r = r_ref[0, sl, ls]
            v = v_ref[0, sl, ls]
            kk = kk_ref[0, sl, ls]
            kd = kd_ref[0, 0, sl, ls]
            b = b_ref[0, 0, sl, ls]
            lw = lw_ref[0, 0, sl, ls]
            cum = lw
            sh = 1
            while sh < cc:
                if reverse:
                    cum = cum + jnp.where(rowc < cc - sh, pltpu.roll(cum, cc - sh, 0), 0.0)
                else:
                    cum = cum + jnp.where(rowc >= sh, pltpu.roll(cum, sh, 0), 0.0)
                sh *= 2
            tot = cum[0:1] if reverse else cum[cc - 1:cc]
            at = -kk * jnp.exp(cum - lw)
            rt = r * jnp.exp(cum)
            einv = jnp.exp(-cum)
            kt = (kd * einv).astype(BF16)
            bt = (b * einv).astype(BF16)
            eh = jnp.exp(tot - cum)
            kh = kd * eh
            bh = b * eh
            vb = v.astype(BF16)
            qq = jnp.concatenate([at, rt], axis=0)
            atp, u0, y0, arb = [], [], [], []
            for hh in range(2):
                sel = first if hh == 0 else jnp.logical_not(first)
                qm = jnp.where(sel, qq, 0.0).astype(BF16)
                mb = _dot_nt(qm, bt)
                mk = _dot_nt(qm, kt)
                a_ab = jnp.where(strict, mb[:cc], 0.0)
                a_rb = jnp.where(incl, mb[cc:], 0.0)
                a_ak = jnp.where(strict, mk[:cc], 0.0)
                a_rk = jnp.where(incl, mk[cc:], 0.0)
                t = eye + a_ab
                pw = a_ab
                for _ in range(cc.bit_length() - 2):
                    pw = _dot(pw.astype(BF16), pw.astype(BF16))
                    t = t + _dot(t.astype(BF16), pw.astype(BF16))
                av = _dot(jnp.concatenate([a_ak, a_rk], axis=0).astype(BF16), vb)
                tx = _dot(t.astype(BF16), jnp.concatenate([at, av[:cc]], axis=1).astype(BF16))
                atp.append(tx[:, :LANES])
                u0.append(tx[:, LANES:])
                y0.append(av[cc:])
                arb.append(a_rb.astype(BF16))
            atp = jnp.where(first, atp[0], atp[1])
            u0 = jnp.where(first, u0[0], u0[1])
            y0 = jnp.where(first, y0[0], y0[1])
            st = s_ref[p]
            qs = _dot_nt(jnp.concatenate([atp, rt], axis=0).astype(BF16), st.astype(BF16))
            u = qs[:cc] + u0
            ub = u.astype(BF16)
            ru = jnp.where(first, _dot(arb[0], ub), _dot(arb[1], ub))
            y_ref[0, sl, ls] = qs[cc:] + y0 + ru
            upd = _dot_tn(jnp.concatenate([vb, ub], axis=0),
                          jnp.concatenate([kh, bh], axis=0).astype(BF16))
            s_ref[p] = st * jnp.exp(tot) + jnp.where(same_head, upd, 0.0)


def _wkv(r, v, kk, kd, bb, lw, d):
    b, s, c = r.shape
    tt = WKV_CHUNK * WKV_STEP_CHUNKS
    nblk = s // tt
    reverse = d == 1
    tix = (lambda i: nblk - 1 - i) if reverse else (lambda i: i)
    one = pl.BlockSpec((1, tt, c), lambda bi, i: (bi, tix(i), 0))
    two = pl.BlockSpec((1, 1, tt, c), lambda bi, i: (d, bi, tix(i), 0))
    return pl.pallas_call(
        functools.partial(_wkv_kernel, reverse=reverse),
        grid=(b, nblk),
        in_specs=[one, one, one, two, two, two],
        out_specs=one,
        out_shape=jax.ShapeDtypeStruct((b, s, c), F32),
        scratch_shapes=[pltpu.VMEM((RW_HEADS // 2, LANES, LANES), F32)],
        compiler_params=_cparams(("parallel", "arbitrary")),
    )(r, v, kk, kd, bb, lw)


WKV2_BATCH = 2


def _wkv2_kernel(rf, vf, kkf, rb, vb, kkb, kdf, bf, lwf, kdb, bb, lwb, yf_ref, yb_ref, s_ref):
    cc = WKV_CHUNK
    nb = rf.shape[0]
    nchunk = rf.shape[1] // cc

    @pl.when(pl.program_id(1) == 0)
    def _():
        s_ref[...] = jnp.zeros_like(s_ref)

    ri = lax.broadcasted_iota(jnp.int32, (LANES, LANES), 0)
    ci = lax.broadcasted_iota(jnp.int32, (LANES, LANES), 1)
    same_head = (ri < HEAD_DIM) == (ci < HEAD_DIM)
    rt_, ct_ = ri % cc, ci % cc
    eye = (ri == ci).astype(F32)
    strict = (same_head & (ct_ < rt_), same_head & (ct_ > rt_))
    incl = (same_head & (ct_ <= rt_), same_head & (ct_ >= rt_))
    rowc = lax.broadcasted_iota(jnp.int32, (cc, LANES), 0)
    first = lax.broadcasted_iota(jnp.int32, (1, LANES), 1) < HEAD_DIM
    second = jnp.logical_not(first)

    def stack_heads(t):
        return jnp.concatenate([jnp.where(first, t, 0.0), jnp.where(second, t, 0.0)], axis=0)

    def fold_heads(t):
        return t[:cc] + t[cc:]

    for c in range(nchunk):
        chains = []
        for bi in range(nb):
            for d in range(2):
                j = nchunk - 1 - c if d else c
                sl = slice(j * cc, (j + 1) * cc)
                src = (rb, vb, kkb, kdb, bb, lwb) if d else (rf, vf, kkf, kdf, bf, lwf)
                for p in range(RW_HEADS // 2):
                    ls = slice(p * LANES, (p + 1) * LANES)
                    chains.append(dict(
                        bi=bi, d=d, p=p, sl=sl, ls=ls,
                        r=src[0][bi, sl, ls], v=src[1][bi, sl, ls], kk=src[2][bi, sl, ls],
                        kd=src[3][0, bi, sl, ls], b=src[4][0, bi, sl, ls], lw=src[5][0, bi, sl, ls]))

        for ch in chains:
            cum = ch["lw"]
            sh = 1
            while sh < cc:
                if ch["d"]:
                    cum = cum + jnp.where(rowc < cc - sh, pltpu.roll(cum, cc - sh, 0), 0.0)
                else:
                    cum = cum + jnp.where(rowc >= sh, pltpu.roll(cum, sh, 0), 0.0)
                sh *= 2
            tot = cum[0:1] if ch["d"] else cum[cc - 1:cc]
            at = -ch["kk"] * jnp.exp(cum - ch["lw"])
            rt = ch["r"] * jnp.exp(cum)
            einv = jnp.exp(-cum)
            eh = jnp.exp(tot - cum)
            ch["etot"] = jnp.exp(tot)
            ch["rt"] = rt.astype(BF16)
            ch["vb"] = ch["v"].astype(BF16)
            ch["at_st"] = stack_heads(at).astype(BF16)
            ch["lhs"] = jnp.concatenate([ch["at_st"], stack_heads(rt).astype(BF16)], axis=0)
            ch["rhs"] = jnp.concatenate([stack_heads(ch["b"] * einv), stack_heads(ch["kd"] * einv)],
                                        axis=0).astype(BF16)
            ch["v_st"] = stack_heads(ch["v"]).astype(BF16)
            ch["kb_hat"] = jnp.concatenate([ch["kd"] * eh, ch["b"] * eh], axis=0).astype(BF16)
        for ch in chains:
            m1 = _dot_nt(ch["lhs"], ch["rhs"])
            d = ch["d"]
            ch["n"] = jnp.where(strict[d], m1[:LANES, :LANES], 0.0)
            a_ak = jnp.where(strict[d], m1[:LANES, LANES:], 0.0)
            ch["a_rb"] = jnp.where(incl[d], m1[LANES:, :LANES], 0.0).astype(BF16)
            a_rk = jnp.where(incl[d], m1[LANES:, LANES:], 0.0)
            ch["a_k"] = jnp.concatenate([a_ak, a_rk], axis=0).astype(BF16)
        for ch in chains:
            nbf = ch["n"].astype(BF16)
            ch["pw"] = _dot(nbf, nbf)
            ch["t"] = eye + ch["n"]
        for _ in range(cc.bit_length() - 3):
            for ch in chains:
                pwb = ch["pw"].astype(BF16)
                tp = _dot(jnp.concatenate([ch["t"].astype(BF16), pwb], axis=0), pwb)
                ch["t"] = ch["t"] + tp[:LANES]
                ch["pw"] = tp[LANES:]
        for ch in chains:
            ch["t"] = (ch["t"] + _dot(ch["t"].astype(BF16), ch["pw"].astype(BF16))).astype(BF16)
        for ch in chains:
            ch["av"] = _dot(ch["a_k"], ch["v_st"])
        for ch in chains:
            tx = _dot(ch["t"], jnp.concatenate([ch["at_st"], ch["av"][:LANES].astype(BF16)], axis=1))
            ch["atp"] = fold_heads(tx[:, :LANES]).astype(BF16)
            ch["u0"] = fold_heads(tx[:, LANES:])
            ch["y0"] = fold_heads(ch["av"][LANES:])

        for ch in chains:
            ch["st"] = s_ref[ch["bi"], ch["d"], ch["p"]]
            qs = _dot_nt(jnp.concatenate([ch["atp"], ch["rt"]], axis=0), ch["st"].astype(BF16))
            ch["u"] = qs[:cc] + ch["u0"]
            ch["ys"] = qs[cc:] + ch["y0"]
        for ch in chains:
            upd = _dot_tn(jnp.concatenate([ch["vb"], ch["u"].astype(BF16)], axis=0), ch["kb_hat"])
            s_ref[ch["bi"], ch["d"], ch["p"]] = ch["st"] * ch["etot"] + jnp.where(same_head, upd, 0.0)
        for ch in chains:
            ru = _dot(ch["a_rb"], stack_heads(ch["u"]).astype(BF16))
            y_ref = yb_ref if ch["d"] else yf_ref
            y_ref[ch["bi"], ch["sl"], ch["ls"]] = ch["ys"] + fold_heads(ru)


def _wkv2(r, v, kk, kd, bb, lw):
    b, s, c = r.shape
    tt = WKV_CHUNK * WKV_STEP_CHUNKS
    nblk = s // tt
    nb = WKV2_BATCH if b % WKV2_BATCH == 0 else 1
    fwd = pl.BlockSpec((nb, tt, c), lambda bi, i: (bi, i, 0))
    bwd = pl.BlockSpec((nb, tt, c), lambda bi, i: (bi, nblk - 1 - i, 0))
    fwd2 = pl.BlockSpec((1, nb, tt, c), lambda bi, i: (0, bi, i, 0))
    bwd2 = pl.BlockSpec((1, nb, tt, c), lambda bi, i: (1, bi, nblk - 1 - i, 0))
    return pl.pallas_call(
        _wkv2_kernel,
        grid=(b // nb, nblk),
        in_specs=[fwd, fwd, fwd, bwd, bwd, bwd, fwd2, fwd2, fwd2, bwd2, bwd2, bwd2],
        out_specs=[fwd, bwd],
        out_shape=[jax.ShapeDtypeStruct((b, s, c), F32)] * 2,
        scratch_shapes=[pltpu.VMEM((nb, 2, RW_HEADS // 2, LANES, LANES), F32)],
        compiler_params=_cparams(("parallel", "arbitrary")),
    )(r, v, kk, r, v, kk, kd, bb, lw, kd, bb, lw)


def _memkv_kernel(mem_ref, nw_ref, w_ref, kw_ref, k_ref, v_ref):
    x = mem_ref[0]
    ms = jnp.mean(x * x, axis=-1, keepdims=True)
    h = (x * lax.rsqrt(ms + RMS_EPS) * nw_ref[...]).astype(BF16)
    kv = _dot(h, w_ref[...])
    k = kv[:, :MEM_WIDTH]
    ms = _split_dot(k * k, _head_ones(MEM_WIDTH)) * (1.0 / HEAD_DIM)
    k_ref[0] = (k * lax.rsqrt(ms + RMS_EPS) * kw_ref[...]).astype(BF16)
    v_ref[0] = kv[:, MEM_WIDTH:].astype(BF16)


def _memkv(mem, nw, w_kv, kw):
    b, m, d = mem.shape
    full = lambda a: pl.BlockSpec(a.shape, lambda bi: (0,) * a.ndim)
    out = pl.BlockSpec((1, m, MEM_WIDTH), lambda bi: (bi, 0, 0))
    return pl.pallas_call(
        _memkv_kernel,
        grid=(b,),
        in_specs=[pl.BlockSpec((1, m, d), lambda bi: (bi, 0, 0)), full(nw), full(w_kv), full(kw)],
        out_specs=[out, out],
        out_shape=[jax.ShapeDtypeStruct((b, m, MEM_WIDTH), BF16)] * 2,
        compiler_params=_cparams(("parallel",)),
    )(mem, nw, w_kv, kw)


def _mixout_kernel(x_ref, na_ref, yf_ref, yb_ref, bonus_ref, g_ref, mq_ref, mk_ref, mv_ref,
                   lnw_ref, lnb_ref, wo_ref, o_ref):
    ones = _head_ones(RW_WIDTH)
    y = yf_ref[0] + yb_ref[0]
    mu = _split_dot(y, ones) * (1.0 / HEAD_DIM)
    yc = y - mu
    var = _split_dot(yc * yc, ones) * (1.0 / HEAD_DIM)
    yn = yc * lax.rsqrt(var + RW_GN_EPS) * lnw_ref[...] + lnb_ref[...]
    y_rw = ((yn + bonus_ref[0]) * g_ref[0]).astype(BF16)

    first = lax.broadcasted_iota(jnp.int32, (1, LANES), 1) < HEAD_DIM
    mems = []
    for p in range(MEM_HEADS // 2):
        ls = slice(p * LANES, (p + 1) * LANES)
        qp = mq_ref[0, :, ls]
        kp = mk_ref[0, :, ls]
        vp = mv_ref[0, :, ls]
        outs = []
        for hh in range(2):
            sel = first if hh == 0 else jnp.logical_not(first)
            s = _dot_nt(jnp.where(sel, qp, jnp.zeros_like(qp)), kp)
            m = jnp.max(s, axis=-1, keepdims=True)
            e = jnp.exp(s - m)
            l = jnp.sum(e, axis=-1, keepdims=True)
            outs.append(_dot(e.astype(BF16), vp) / l)
        mems.append(jnp.where(first, outs[0], outs[1]).astype(BF16))
    y_mem = jnp.concatenate(mems, axis=1)

    acc = _dot(na_ref[0], wo_ref[0:NA_WIDTH, :])
    acc = acc + _dot(y_rw, wo_ref[NA_WIDTH:NA_WIDTH + RW_WIDTH, :])
    acc = acc + _dot(y_mem, wo_ref[NA_WIDTH + RW_WIDTH:, :])
    o_ref[0] = x_ref[0] + acc


def _mixout(x, y_na, y_f, y_b, bonus, g, mq, mk, mv, ln_w, ln_b, w_out, tm):
    b, s, d = x.shape
    m = mk.shape[1]
    tok = lambda w: pl.BlockSpec((1, tm, w), lambda bi, i: (bi, i, 0))
    full = lambda a: pl.BlockSpec(a.shape, lambda bi, i: (0,) * a.ndim)
    memb = pl.BlockSpec((1, m, MEM_WIDTH), lambda bi, i: (bi, 0, 0))
    return pl.pallas_call(
        _mixout_kernel,
        grid=(b, s // tm),
        in_specs=[tok(d), tok(NA_WIDTH), tok(RW_WIDTH), tok(RW_WIDTH), tok(RW_WIDTH), tok(RW_WIDTH),
                  tok(MEM_WIDTH), memb, memb, full(ln_w), full(ln_b), full(w_out)],
        out_specs=tok(d),
        out_shape=jax.ShapeDtypeStruct((b, s, d), F32),
        compiler_params=_cparams(("parallel", "parallel")),
    )(x, y_na, y_f, y_b, bonus, g, mq, mk, mv, ln_w, ln_b, w_out)


ROUTER_LANES = 128


def _router_kernel(x_ref, nw_ref, whi_ref, wlo_ref, b_ref, h_ref, lg_ref):
    x = x_ref[...]
    ms = jnp.mean(x * x, axis=-1, keepdims=True)
    h = x * lax.rsqrt(ms + RMS_EPS) * nw_ref[...]
    hi = h.astype(BF16)
    lo = (h - hi.astype(F32)).astype(BF16)
    h_ref[...] = hi
    lg_ref[...] = _dot(hi, whi_ref[...]) + _dot(hi, wlo_ref[...]) + _dot(lo, whi_ref[...]) + b_ref[...]


def _router(x2, nw, w_group, b_group, w_expert, b_expert, tm):
    n, d = x2.shape
    pad = ROUTER_LANES - MOE_GROUPS - MOE_EXPERTS
    w = jnp.concatenate([w_group, w_expert, jnp.zeros((d, pad), F32)], axis=1)
    whi = w.astype(BF16)
    wlo = (w - whi.astype(F32)).astype(BF16)
    bias = jnp.concatenate([b_group, b_expert, jnp.zeros((pad,), F32)]).reshape(1, -1)
    full = lambda a: pl.BlockSpec(a.shape, lambda i: (0,) * a.ndim)
    return pl.pallas_call(
        _router_kernel,
        grid=(n // tm,),
        in_specs=[pl.BlockSpec((tm, d), lambda i: (i, 0)), full(nw), full(whi), full(wlo), full(bias)],
        out_specs=[pl.BlockSpec((tm, d), lambda i: (i, 0)), pl.BlockSpec((tm, ROUTER_LANES), lambda i: (i, 0))],
        out_shape=[jax.ShapeDtypeStruct((n, d), BF16), jax.ShapeDtypeStruct((n, ROUTER_LANES), F32)],
        compiler_params=_cparams(("parallel",)),
    )(x2, nw, whi, wlo, bias)


MOE_SUB = 256
MOE_CHUNK = 16
MOE_LOCAL_CHUNKS = 48
MOE_BLOCK_CHUNKS = 32
MOE_TOP_K = 2
assert MOE_LOCAL_CHUNKS >= MOE_TOP_K * MOE_SUB // MOE_CHUNK + MOE_EXPERTS - 1


def _route(logits):
    n = logits.shape[0]
    gl = logits[:, :MOE_GROUPS]
    el = logits[:, MOE_GROUPS:MOE_GROUPS + MOE_EXPERTS].reshape(n, MOE_GROUPS, MOE_EXPERTS_PER_GROUP)
    g_idx = jnp.argmax(gl, axis=-1)
    g_w = jnp.take_along_axis(jax.nn.softmax(gl, axis=-1), g_idx[:, None], axis=1)[:, 0]
    sel = jnp.take_along_axis(el, g_idx[:, None, None], axis=1)[:, 0]
    top_p, top_i = lax.top_k(jax.nn.softmax(sel, axis=-1), MOE_TOP_K)
    top_p = top_p / jnp.sum(top_p, axis=-1, keepdims=True)
    return (g_idx[:, None] * MOE_EXPERTS_PER_GROUP + top_i).astype(jnp.int32), g_w[:, None] * top_p


def _moe_plan(e, n):
    ns = n // MOE_SUB
    per = MOE_TOP_K * MOE_SUB
    oh = (e.reshape(ns, per, 1) == jnp.arange(MOE_EXPERTS, dtype=jnp.int32)).astype(jnp.int32)
    csum = jnp.cumsum(oh, axis=1)
    rank = jnp.sum((csum - oh) * oh, axis=-1)
    nch = (csum[:, -1, :] + MOE_CHUNK - 1) // MOE_CHUNK
    lc_end = jnp.cumsum(nch, axis=1)
    lc_start = lc_end - nch
    nloc = lc_end[:, -1]
    pos = jnp.sum(oh * lc_start[:, None, :], axis=-1) * MOE_CHUNK + rank
    pos = jnp.transpose(pos.reshape(ns, MOE_SUB, MOE_TOP_K), (0, 2, 1))
    pos = jnp.concatenate([pos, jnp.zeros((ns, 8 - MOE_TOP_K, MOE_SUB), jnp.int32)], axis=1)

    nblk = (jnp.sum(nch, axis=0) + MOE_BLOCK_CHUNKS - 1) // MOE_BLOCK_CHUNKS
    bend = jnp.cumsum(nblk)
    gch = ((bend - nblk) * MOE_BLOCK_CHUNKS)[None, :] + jnp.cumsum(nch, axis=0) - nch
    c = jnp.arange(MOE_LOCAL_CHUNKS, dtype=jnp.int32)
    e_c = jnp.minimum(jnp.sum(c[None, :, None] >= lc_end[:, None, :], axis=-1), MOE_EXPERTS - 1)
    dest = (jnp.take_along_axis(gch, e_c, axis=1) + c[None, :] - jnp.take_along_axis(lc_start, e_c, axis=1))
    dest = jnp.where(c[None, :] < nloc[:, None], dest, 0)

    nblocks = _moe_blocks(n)
    b = jnp.arange(nblocks, dtype=jnp.int32)
    block_e = jnp.minimum(jnp.sum(b[:, None] >= bend[None, :], axis=-1), MOE_EXPERTS - 1)
    return (pos.astype(jnp.int32), dest.astype(jnp.int32), nloc.astype(jnp.int32),
            block_e.astype(jnp.int32), bend[-1:].astype(jnp.int32))


def _moe_blocks(n):
    ns = n // MOE_SUB
    chunks = ns * (MOE_TOP_K * MOE_SUB // MOE_CHUNK + MOE_EXPERTS - 1) + MOE_EXPERTS * (MOE_BLOCK_CHUNKS - 1)
    return -(-chunks // MOE_BLOCK_CHUNKS)


def _local_onehot(pos_rows, shape, row_axis):
    idx = lax.broadcasted_iota(jnp.int32, shape, row_axis)
    return (idx == pos_rows[0]) | (idx == pos_rows[1])


def _dispatch_kernel(dest_ref, nloc_ref, h_ref, pos_ref, zeros_ref, xs_ref, buf, sem):
    del zeros_ref
    s = pl.program_id(0)
    ns = pl.num_programs(0)
    slot = s % 2

    def chunk_copy(sl, c, step):
        src = buf.at[sl, pl.ds(pl.multiple_of(c * MOE_CHUNK, MOE_CHUNK), MOE_CHUNK)]
        dst = xs_ref.at[pl.ds(pl.multiple_of(dest_ref[step, c] * MOE_CHUNK, MOE_CHUNK), MOE_CHUNK)]
        return pltpu.make_async_copy(src, dst, sem.at[sl])

    def wait_step(step, sl):
        def body(c, carry):
            chunk_copy(sl, c, step).wait()
            return carry
        lax.fori_loop(0, nloc_ref[step], body, 0)

    @pl.when(s >= 2)
    def _():
        wait_step(s - 2, slot)

    pos = pos_ref[0]
    onehot = _local_onehot((pos[0:1], pos[1:2]), (MOE_LOCAL_CHUNKS * MOE_CHUNK, MOE_SUB), 0)
    buf[slot] = _dot(jnp.where(onehot, 1.0, 0.0).astype(BF16), h_ref[...]).astype(BF16)

    def issue(c, carry):
        chunk_copy(slot, c, s).start()
        return carry
    lax.fori_loop(0, nloc_ref[s], issue, 0)

    @pl.when(s == ns - 1)
    def _():
        @pl.when(s >= 1)
        def _():
            wait_step(s - 1, 1 - slot)
        wait_step(s, slot)


def _dispatch(h2, pos, dest, nloc):
    n, d = h2.shape
    ns = n // MOE_SUB
    cap = _moe_blocks(n) * MOE_BLOCK_CHUNKS * MOE_CHUNK
    rows = MOE_LOCAL_CHUNKS * MOE_CHUNK
    return pl.pallas_call(
        _dispatch_kernel,
        grid_spec=pltpu.PrefetchScalarGridSpec(
            num_scalar_prefetch=2, grid=(ns,),
            in_specs=[pl.BlockSpec((MOE_SUB, d), lambda s, *_: (s, 0)),
                      pl.BlockSpec((1, 8, MOE_SUB), lambda s, *_: (s, 0, 0)),
                      pl.BlockSpec(memory_space=pl.ANY)],
            out_specs=pl.BlockSpec(memory_space=pl.ANY),
            scratch_shapes=[pltpu.VMEM((2, rows, d), BF16), pltpu.SemaphoreType.DMA((2,))]),
        out_shape=jax.ShapeDtypeStruct((cap, d), BF16),
        input_output_aliases={4: 0},
        compiler_params=_cparams(("arbitrary",)),
    )(dest, nloc, h2, pos, jnp.zeros((cap, d), BF16))


def _experts_kernel(be_ref, nv_ref, x_ref, wg_ref, wu_ref, wd_ref, o_ref):
    del be_ref
    valid = pl.program_id(0) < nv_ref[0]

    @pl.when(valid)
    def _():
        x = x_ref[...]
        hid = jax.nn.silu(_dot(x, wg_ref[0])) * _dot(x, wu_ref[0])
        o_ref[...] = _dot(hid.astype(BF16), wd_ref[0])

    @pl.when(jnp.logical_not(valid))
    def _():
        o_ref[...] = jnp.zeros_like(o_ref)


def _experts(xs, block_e, nvalid, w_gate, w_up, w_down):
    cap, d = xs.shape
    bm = MOE_BLOCK_CHUNKS * MOE_CHUNK
    blk = lambda b, be, nv: (jnp.minimum(b, nv[0] - 1), 0)
    wsel = lambda b, be, nv: (be[jnp.minimum(b, nv[0] - 1)], 0, 0)
    return pl.pallas_call(
        _experts_kernel,
        grid_spec=pltpu.PrefetchScalarGridSpec(
            num_scalar_prefetch=2, grid=(cap // bm,),
            in_specs=[pl.BlockSpec((bm, d), blk),
                      pl.BlockSpec((1, d, MOE_D_FF), wsel),
                      pl.BlockSpec((1, d, MOE_D_FF), wsel),
                      pl.BlockSpec((1, MOE_D_FF, d), wsel)],
            out_specs=pl.BlockSpec((bm, d), lambda b, be, nv: (b, 0))),
        out_shape=jax.ShapeDtypeStruct((cap, d), F32),
        compiler_params=_cparams(("arbitrary",)),
    )(block_e, nvalid, xs, w_gate, w_up, w_down)


def _combine_kernel(dest_ref, x_ref, pos_ref, w_ref, ys_ref, o_ref, buf, sem):
    s = pl.program_id(0)
    ns = pl.num_programs(0)
    slot = s % 2

    def chunk_copy(sl, c, step):
        src = ys_ref.at[pl.ds(pl.multiple_of(dest_ref[step, c] * MOE_CHUNK, MOE_CHUNK), MOE_CHUNK)]
        dst = buf.at[sl, pl.ds(pl.multiple_of(c * MOE_CHUNK, MOE_CHUNK), MOE_CHUNK)]
        return pltpu.make_async_copy(src, dst, sem.at[sl])

    def fetch(step, sl):
        def body(c, carry):
            chunk_copy(sl, c, step).start()
            return carry
        lax.fori_loop(0, MOE_LOCAL_CHUNKS, body, 0)

    @pl.when(s == 0)
    def _():
        fetch(0, 0)

    @pl.when(s + 1 < ns)
    def _():
        fetch(s + 1, 1 - slot)

    def wait(c, carry):
        chunk_copy(slot, c, s).wait()
        return carry
    lax.fori_loop(0, MOE_LOCAL_CHUNKS, wait, 0)

    pos = jnp.transpose(pos_ref[0])
    wts = jnp.transpose(w_ref[0])
    idx = lax.broadcasted_iota(jnp.int32, (MOE_SUB, MOE_LOCAL_CHUNKS * MOE_CHUNK), 1)
    pw = jnp.where(idx == pos[:, 0:1], wts[:, 0:1], 0.0) + jnp.where(idx == pos[:, 1:2], wts[:, 1:2], 0.0)
    ys = buf[slot]
    p_hi = pw.astype(BF16)
    p_lo = (pw - p_hi.astype(F32)).astype(BF16)
    y_hi = ys.astype(BF16)
    y_lo = (ys - y_hi.astype(F32)).astype(BF16)
    o_ref[...] = x_ref[...] + _dot(p_hi, y_hi) + _dot(p_hi, y_lo) + _dot(p_lo, y_hi)


def _combine(x2, ys, pos, wts, dest):
    n, d = x2.shape
    ns = n // MOE_SUB
    rows = MOE_LOCAL_CHUNKS * MOE_CHUNK
    small = pl.BlockSpec((1, 8, MOE_SUB), lambda s, *_: (s, 0, 0))
    return pl.pallas_call(
        _combine_kernel,
        grid_spec=pltpu.PrefetchScalarGridSpec(
            num_scalar_prefetch=1, grid=(ns,),
            in_specs=[pl.BlockSpec((MOE_SUB, d), lambda s, *_: (s, 0)), small, small,
                      pl.BlockSpec(memory_space=pl.ANY)],
            out_specs=pl.BlockSpec((MOE_SUB, d), lambda s, *_: (s, 0)),
            scratch_shapes=[pltpu.VMEM((2, rows, d), F32), pltpu.SemaphoreType.DMA((2,))]),
        out_shape=jax.ShapeDtypeStruct((n, d), F32),
        compiler_params=_cparams(("arbitrary",)),
    )(dest, x2, pos, wts, ys)


def _moe(x2, h2, logits, w_gate, w_up, w_down):
    n = x2.shape[0]
    e, w = _route(logits)
    pos, dest, nloc, block_e, nvalid = _moe_plan(e, n)
    wts = jnp.transpose(w.reshape(n // MOE_SUB, MOE_SUB, MOE_TOP_K), (0, 2, 1))
    wts = jnp.concatenate([wts, jnp.zeros((n // MOE_SUB, 8 - MOE_TOP_K, MOE_SUB), F32)], axis=1)
    xs = _dispatch(h2, pos, dest, nloc)
    ys = _experts(xs, block_e, nvalid, w_gate, w_up, w_down)
    return _combine(x2, ys, pos, wts, dest)


def _tile(n, want):
    t = min(n, want)
    assert n % t == 0
    return t


def kernel(x, mem, attn_norm_w, w_in, na_q_norm_w, na_k_norm_w, na_rpb, rw_mu_prev, rw_mu_next, rw_w0, rw_w2, rw_a0, rw_a2, rw_g2, rw_k_k, rw_k_a, rw_r_k, rw_ln_w, rw_ln_b, mem_norm_w, w_mem_kv, mem_q_norm_w, mem_k_norm_w, w_out, ffn_norm_w, moe_w_group, moe_b_group, moe_w_expert, moe_b_expert, moe_w_gate, moe_w_up, moe_w_down):
    b, s, d = x.shape
    n = b * s
    depth = w_in.shape[0]
    assert s % (NA_BLOCK_ROWS * GRID_W) == 0 and s // GRID_W >= 2 * NA_KH
    tm = _tile(n, 512)
    ts = _tile(s, 512)
    for l in range(depth):
        q, k, v, rw, mq = _proj(
            x.reshape(n, d), _row(attn_norm_w[l]), w_in[l].astype(BF16),
            _row(jnp.tile(na_q_norm_w[l], NA_HEADS)), _row(jnp.tile(na_k_norm_w[l], NA_HEADS)),
            _row(jnp.tile(mem_q_norm_w[l], MEM_HEADS)), tm)
        y_na = _na(q.reshape(b, s, NA_WIDTH), k.reshape(b, s, NA_WIDTH), v.reshape(b, s, NA_WIDTH),
                   _na_bias_table(na_rpb[l]))
        r, vv, kk, kd, bb, lw, g, bonus = _rwprep(
            rw.reshape(b, s, RW_PROJ), rw_mu_prev[l], rw_mu_next[l], rw_w0[l], rw_w2[l], rw_a0[l],
            rw_a2[l], rw_g2[l], rw_k_k[l], rw_k_a[l], rw_r_k[l], ts)
        y_f, y_b = _wkv2(r, vv, kk, kd, bb, lw)
        mk, mv = _memkv(mem, _row(mem_norm_w[l]), w_mem_kv[l].astype(BF16),
                        _row(jnp.tile(mem_k_norm_w[l], MEM_HEADS)))
        x = _mixout(x, y_na, y_f, y_b, bonus, g, mq.reshape(b, s, MEM_WIDTH), mk, mv,
                    _row(rw_ln_w[l]), _row(rw_ln_b[l]), w_out[l].astype(BF16), ts)
        h2, logits = _router(x.reshape(n, d), _row(ffn_norm_w[l]), moe_w_group[l], moe_b_group[l],
                             moe_w_expert[l], moe_b_expert[l], tm)
        x = _moe(x.reshape(n, d), h2, logits, moe_w_gate[l].astype(BF16),
                 moe_w_up[l].astype(BF16), moe_w_down[l].astype(BF16)).reshape(b, s, d)
    return x
```

```python
import functools

import jax
import jax.numpy as jnp
from jax import lax
from jax.experimental import pallas as pl
from jax.experimental.pallas import tpu as pltpu

F32 = jnp.float32
BF16 = jnp.bfloat16

GRID_W = 64
HEAD_DIM = 64
NA_HEADS = 8
NA_WIDTH = NA_HEADS * HEAD_DIM
NA_KH = 8
NA_KW = 16
RW_HEADS = 4
RW_WIDTH = RW_HEADS * HEAD_DIM
RW_LORA_W = 64
RW_LORA_A = 64
RW_LORA_G = 128
RW_PROJ = 3 * RW_WIDTH + 2 * RW_LORA_W + 2 * RW_LORA_A + RW_LORA_G
MEM_HEADS = 4
MEM_WIDTH = MEM_HEADS * HEAD_DIM
MOE_GROUPS = 4
MOE_EXPERTS_PER_GROUP = 4
MOE_EXPERTS = MOE_GROUPS * MOE_EXPERTS_PER_GROUP
MOE_D_FF = 512
RMS_EPS = 1e-6
RW_GN_EPS = 64e-5

LANES = 128
WKV_CHUNK = 64
NEG_BIG = -1e30
VMEM_LIMIT = 56 * 1024 * 1024


def _cparams(sem):
    return pltpu.CompilerParams(dimension_semantics=sem, vmem_limit_bytes=VMEM_LIMIT)


def _dot(a, b):
    return jnp.dot(a, b, preferred_element_type=F32)


def _dot_nt(a, b):
    return lax.dot_general(a, b, (((1,), (1,)), ((), ())), preferred_element_type=F32)


def _dot_tn(a, b):
    return lax.dot_general(a, b, (((0,), (0,)), ((), ())), preferred_element_type=F32)


def _split_dot(t, m):
    hi = t.astype(BF16)
    lo = (t - hi.astype(F32)).astype(BF16)
    return _dot(hi, m) + _dot(lo, m)


def _head_ones(width):
    i = lax.broadcasted_iota(jnp.int32, (width, width), 0) // HEAD_DIM
    j = lax.broadcasted_iota(jnp.int32, (width, width), 1) // HEAD_DIM
    return (i == j).astype(BF16)


def _row(v):
    return v.reshape(1, -1).astype(F32)


def _proj_kernel(x_ref, nw_ref, w_ref, qw_ref, kw_ref, mw_ref, q_ref, k_ref, v_ref, rw_ref, mq_ref):
    x = x_ref[...]
    ms = jnp.mean(x * x, axis=-1, keepdims=True)
    h = (x * lax.rsqrt(ms + RMS_EPS) * nw_ref[...]).astype(BF16)
    ones_na = _head_ones(NA_WIDTH)

    def head_norm(t, w, ones):
        ms = _dot((t * t).astype(BF16), ones) * (1.0 / HEAD_DIM)
        return t * lax.rsqrt(ms + RMS_EPS) * w

    o = 0
    q = _dot(h, w_ref[:, o:o + NA_WIDTH])
    q_ref[...] = (head_norm(q, qw_ref[...], ones_na) * (HEAD_DIM ** -0.5)).astype(BF16)
    o += NA_WIDTH
    k = _dot(h, w_ref[:, o:o + NA_WIDTH])
    k_ref[...] = head_norm(k, kw_ref[...], ones_na).astype(BF16)
    o += NA_WIDTH
    v_ref[...] = _dot(h, w_ref[:, o:o + NA_WIDTH]).astype(BF16)
    o += NA_WIDTH
    rw_ref[...] = _dot(h, w_ref[:, o:o + RW_PROJ])
    o += RW_PROJ
    mq = _dot(h, w_ref[:, o:o + MEM_WIDTH])
    mq_ref[...] = (head_norm(mq, mw_ref[...], _head_ones(MEM_WIDTH)) * (HEAD_DIM ** -0.5)).astype(BF16)


def _proj(x2, nw, w_in, qw, kw, mw, tm):
    n, d = x2.shape
    p_in = w_in.shape[1]
    tok = lambda w: pl.BlockSpec((tm, w), lambda i: (i, 0))
    full = lambda a: pl.BlockSpec(a.shape, lambda i: (0,) * a.ndim)
    return pl.pallas_call(
        _proj_kernel,
        grid=(n // tm,),
        in_specs=[tok(d), full(nw), full(w_in), full(qw), full(kw), full(mw)],
        out_specs=[tok(NA_WIDTH), tok(NA_WIDTH), tok(NA_WIDTH), tok(RW_PROJ), tok(MEM_WIDTH)],
        out_shape=[jax.ShapeDtypeStruct((n, NA_WIDTH), BF16)] * 3
        + [jax.ShapeDtypeStruct((n, RW_PROJ), F32), jax.ShapeDtypeStruct((n, MEM_WIDTH), BF16)],
        compiler_params=_cparams(("parallel",)),
    )(x2, nw, w_in, qw, kw, mw)


NA_BLOCK_ROWS = 8


def _na_bias_table(rpb):
    col = jnp.arange(GRID_W)
    cs = jnp.clip(col - NA_KW // 2, 0, GRID_W - NA_KW)
    dc = col[None, :] - col[:, None]
    inside = (col[None, :] >= cs[:, None]) & (col[None, :] < cs[:, None] + NA_KW)
    dci = jnp.clip(dc + NA_KW - 1, 0, 2 * NA_KW - 2)
    off = jnp.arange(NA_KH) - (NA_KH - 1)
    dri = off[:, None] + jnp.arange(NA_KH)[None, :] + NA_KH - 1
    t = rpb[:, dri][:, :, :, dci]
    t = jnp.where(inside[None, None, None], t, NEG_BIG)
    t = jnp.transpose(t, (1, 0, 3, 2, 4))
    return t.reshape(NA_KH, NA_HEADS // 2, 2 * GRID_W, NA_KH * GRID_W).astype(F32)


def _na_kernel(q_ref, kp_ref, kc_ref, kn_ref, vp_ref, vc_ref, vn_ref, bias_ref, o_ref, kwin, vwin, *, rows):
    j = pl.program_id(1)
    blk = NA_BLOCK_ROWS * GRID_W
    kwin[0:blk] = kp_ref[0]
    kwin[blk:2 * blk] = kc_ref[0]
    kwin[2 * blk:3 * blk] = kn_ref[0]
    vwin[0:blk] = vp_ref[0]
    vwin[blk:2 * blk] = vc_ref[0]
    vwin[2 * blk:3 * blk] = vn_ref[0]
    lane = lax.broadcasted_iota(jnp.int32, (1, LANES), 1)
    first = lane < HEAD_DIM

    def row_body(i, carry):
        r = j * NA_BLOCK_ROWS + i
        rs = jnp.clip(r - NA_KH // 2, 0, rows - NA_KH)
        oi = rs - r + (NA_KH - 1)
        start = pl.multiple_of((rs - (j - 1) * NA_BLOCK_ROWS) * GRID_W, GRID_W)
        qs = pl.multiple_of(i * GRID_W, GRID_W)
        pairs = [slice(p * LANES, (p + 1) * LANES) for p in range(NA_HEADS // 2)]
        scores = []
        for p, ls in enumerate(pairs):
            qp = q_ref[0, pl.ds(qs, GRID_W), ls]
            zero = jnp.zeros_like(qp)
            qst = jnp.concatenate([jnp.where(first, qp, zero), jnp.where(first, zero, qp)], axis=0)
            scores.append(_dot_nt(qst, kwin[pl.ds(start, NA_KH * GRID_W), ls]) + bias_ref[oi, p])
        probs, norms = [], []
        for s in scores:
            e = jnp.exp(s - jnp.max(s, axis=-1, keepdims=True))
            norms.append(jnp.sum(e, axis=-1, keepdims=True))
            probs.append(e.astype(BF16))
        for ls, e, l in zip(pairs, probs, norms):
            o = _dot(e, vwin[pl.ds(start, NA_KH * GRID_W), ls]) / l
            o_ref[0, pl.ds(qs, GRID_W), ls] = jnp.where(first, o[:GRID_W], o[GRID_W:]).astype(o_ref.dtype)
        return carry

    lax.fori_loop(0, NA_BLOCK_ROWS, row_body, 0)


def _na(q, k, v, bias):
    b, s, w = q.shape
    rows = s // GRID_W
    nblk = rows // NA_BLOCK_ROWS
    blk = NA_BLOCK_ROWS * GRID_W
    cur = pl.BlockSpec((1, blk, w), lambda bi, j: (bi, j, 0))
    prv = pl.BlockSpec((1, blk, w), lambda bi, j: (bi, jnp.maximum(j - 1, 0), 0))
    nxt = pl.BlockSpec((1, blk, w), lambda bi, j: (bi, jnp.minimum(j + 1, nblk - 1), 0))
    return pl.pallas_call(
        functools.partial(_na_kernel, rows=rows),
        grid=(b, nblk),
        in_specs=[cur, prv, cur, nxt, prv, cur, nxt,
                  pl.BlockSpec(bias.shape, lambda bi, j: (0, 0, 0, 0))],
        out_specs=cur,
        out_shape=jax.ShapeDtypeStruct((b, s, w), BF16),
        scratch_shapes=[pltpu.VMEM((3 * blk, w), BF16), pltpu.VMEM((3 * blk, w), BF16)],
        compiler_params=_cparams(("parallel", "parallel")),
    )(q, k, k, k, v, v, v, bias)


def _rwprep_kernel(rw_ref, prev_ref, next_ref, mup_ref, mun_ref, w0_ref, w2_ref, a0_ref, a2_ref, g2_ref,
                   kk_w_ref, ka_ref, rk_ref,
                   r_ref, v_ref, kk_ref, kd_ref, b_ref, lw_ref, g_ref, bonus_ref, *, nblk):
    i = pl.program_id(1)
    s = rw_ref[0]
    ts = s.shape[0]
    rowi = lax.broadcasted_iota(jnp.int32, (ts, 1), 0)
    halo_p = jnp.where(i > 0, prev_ref[0, 7:8, :], 0.0)
    halo_n = jnp.where(i < nblk - 1, next_ref[0, 0:1, :], 0.0)
    prev = jnp.where(rowi == 0, halo_p, pltpu.roll(s, 1, 0))
    nxt = jnp.where(rowi == ts - 1, halo_n, pltpu.roll(s, ts - 1, 0))
    s = s + mup_ref[...] * (prev - s) + mun_ref[...] * (nxt - s)

    c = RW_WIDTH
    r = s[:, 0:c]
    k = s[:, c:2 * c]
    v = s[:, 2 * c:3 * c]
    lw = s[:, 3 * c:3 * c + 2 * RW_LORA_W]
    la = s[:, 3 * c + 2 * RW_LORA_W:3 * c + 2 * RW_LORA_W + 2 * RW_LORA_A]
    lg = s[:, 3 * c + 2 * RW_LORA_W + 2 * RW_LORA_A:]
    ones = _head_ones(c)

    g_ref[0] = _dot(jax.nn.sigmoid(lg).astype(BF16), g2_ref[...])
    kk = k * kk_w_ref[...]
    nrm = jnp.sqrt(_split_dot(kk * kk, ones))
    kk = kk / jnp.maximum(nrm, 1e-12)
    wl_pre = w0_ref[...] + _dot(jnp.tanh(lw).astype(BF16), w2_ref[...])
    a_all = jax.nn.sigmoid(a0_ref[...] + _dot(la.astype(BF16), a2_ref[...]))
    wl = -jax.nn.softplus(-wl_pre) - 0.5
    logdecay = -jnp.exp(wl)
    r_ref[0] = r
    v_ref[0] = v
    kk_ref[0] = kk
    kd_sum = jnp.zeros_like(k)
    for d in range(2):
        a = a_all[:, d * c:(d + 1) * c]
        kd = k * (1.0 + (a - 1.0) * ka_ref[...])
        kd_ref[d, 0] = kd
        b_ref[d, 0] = kk * a
        lw_ref[d, 0] = logdecay[:, d * c:(d + 1) * c]
        kd_sum = kd_sum + kd
    bonus_ref[0] = _split_dot(r * kd_sum * rk_ref[...], ones) * v


def _blockdiag2(m):
    z = jnp.zeros_like(m[0])
    return jnp.concatenate([jnp.concatenate([m[0], z], 1), jnp.concatenate([z, m[1]], 1)], 0)


def _rwprep(rw, mu_prev, mu_next, w0, w2, a0, a2, g2, k_k, k_a, r_k, ts):
    b, s, pw = rw.shape
    nblk = s // ts
    c = RW_WIDTH
    cur = pl.BlockSpec((1, ts, pw), lambda bi, i: (bi, i, 0))
    prv = pl.BlockSpec((1, 8, pw), lambda bi, i: (bi, jnp.maximum(i * (ts // 8) - 1, 0), 0))
    nxt = pl.BlockSpec((1, 8, pw), lambda bi, i: (bi, jnp.minimum((i + 1) * (ts // 8), s // 8 - 1), 0))
    params = [_row(mu_prev), _row(mu_next), _row(w0), _blockdiag2(w2).astype(BF16), _row(a0),
              _blockdiag2(a2).astype(BF16), g2.astype(BF16), _row(k_k), _row(k_a), _row(r_k)]
    full = lambda a: pl.BlockSpec(a.shape, lambda bi, i: (0,) * a.ndim)
    one = pl.BlockSpec((1, ts, c), lambda bi, i: (bi, i, 0))
    two = pl.BlockSpec((2, 1, ts, c), lambda bi, i: (0, bi, i, 0))
    s1 = jax.ShapeDtypeStruct((b, s, c), F32)
    s2 = jax.ShapeDtypeStruct((2, b, s, c), F32)
    return pl.pallas_call(
        functools.partial(_rwprep_kernel, nblk=nblk),
        grid=(b, nblk),
        in_specs=[cur, prv, nxt] + [full(p) for p in params],
        out_specs=[one, one, one, two, two, two, one, one],
        out_shape=[s1, s1, s1, s2, s2, s2, s1, s1],
        compiler_params=_cparams(("parallel", "parallel")),
    )(rw, rw, rw, *params)


WKV_STEP_CHUNKS = 4


def _wkv_kernel(r_ref, v_ref, kk_ref, kd_ref, b_ref, lw_ref, y_ref, s_ref, *, reverse):
    cc = WKV_CHUNK

    @pl.when(pl.program_id(1) == 0)
    def _():
        s_ref[...] = jnp.zeros_like(s_ref)

    row = lax.broadcasted_iota(jnp.int32, (cc, cc), 0)
    col = lax.broadcasted_iota(jnp.int32, (cc, cc), 1)
    strict = (col > row) if reverse else (col < row)
    incl = (col >= row) if reverse else (col <= row)
    eye = (col == row).astype(F32)
    rowc = lax.broadcasted_iota(jnp.int32, (cc, LANES), 0)
    first = lax.broadcasted_iota(jnp.int32, (1, LANES), 1) < HEAD_DIM
    same_head = ((lax.broadcasted_iota(jnp.int32, (LANES, LANES), 0) < HEAD_DIM)
                 == (lax.broadcasted_iota(jnp.int32, (LANES, LANES), 1) < HEAD_DIM))

    for step in range(WKV_STEP_CHUNKS):
        j = WKV_STEP_CHUNKS - 1 - step if reverse else step
        sl = slice(j * cc, (j + 1) * cc)
        for p in range(RW_HEADS // 2):
            ls = slice(p * LANES, (p + 1) * LANES)
            r = r_ref[0, sl, ls]
            v = v_ref[0, sl, ls]
            kk = kk_ref[0, sl, ls]
            kd = kd_ref[0, 0, sl, ls]
            b = b_ref[0, 0, sl, ls]
            lw = lw_ref[0, 0, sl, ls]
            cum = lw
            sh = 1
            while sh < cc:
                if reverse:
                    cum = cum + jnp.where(rowc < cc - sh, pltpu.roll(cum, cc - sh, 0), 0.0)
                else:
                    cum = cum + jnp.where(rowc >= sh, pltpu.roll(cum, sh, 0), 0.0)
                sh *= 2
            tot = cum[0:1] if reverse else cum[cc - 1:cc]
            at = -kk * jnp.exp(cum - lw)
            rt = r * jnp.exp(cum)
            einv = jnp.exp(-cum)
            kt = (kd * einv).astype(BF16)
            bt = (b * einv).astype(BF16)
            eh = jnp.exp(tot - cum)
            kh = kd * eh
            bh = b * eh
            vb = v.astype(BF16)
            qq = jnp.concatenate([at, rt], axis=0)
            atp, u0, y0, arb = [], [], [], []
            for hh in range(2):
                sel = first if hh == 0 else jnp.logical_not(first)
                qm = jnp.where(sel, qq, 0.0).astype(BF16)
                mb = _dot_nt(qm, bt)
                mk = _dot_nt(qm, kt)
                a_ab = jnp.where(strict, mb[:cc], 0.0)
                a_rb = jnp.where(incl, mb[cc:], 0.0)
                a_ak = jnp.where(strict, mk[:cc], 0.0)
                a_rk = jnp.where(incl, mk[cc:], 0.0)
                t = eye + a_ab
                pw = a_ab
                for _ in range(cc.bit_length() - 2):
                    pw = _dot(pw.astype(BF16), pw.astype(BF16))
                    t = t + _dot(t.astype(BF16), pw.astype(BF16))
                av = _dot(jnp.concatenate([a_ak, a_rk], axis=0).astype(BF16), vb)
                tx = _dot(t.astype(BF16), jnp.concatenate([at, av[:cc]], axis=1).astype(BF16))
                atp.append(tx[:, :LANES])
                u0.append(tx[:, LANES:])
                y0.append(av[cc:])
                arb.append(a_rb.astype(BF16))
            atp = jnp.where(first, atp[0], atp[1])
            u0 = jnp.where(first, u0[0], u0[1])
            y0 = jnp.where(first, y0[0], y0[1])
            st = s_ref[p]
            qs = _dot_nt(jnp.concatenate([atp, rt], axis=0).astype(BF16), st.astype(BF16))
            u = qs[:cc] + u0
            ub = u.astype(BF16)
            ru = jnp.where(first, _dot(arb[0], ub), _dot(arb[1], ub))
            y_ref[0, sl, ls] = qs[cc:] + y0 + ru
            upd = _dot_tn(jnp.concatenate([vb, ub], axis=0),
                          jnp.concatenate([kh, bh], axis=0).astype(BF16))
            s_ref[p] = st * jnp.exp(tot) + jnp.where(same_head, upd, 0.0)


def _wkv(r, v, kk, kd, bb, lw, d):
    b, s, c = r.shape
    tt = WKV_CHUNK * WKV_STEP_CHUNKS
    nblk = s // tt
    reverse = d == 1
    tix = (lambda i: nblk - 1 - i) if reverse else (lambda i: i)
    one = pl.BlockSpec((1, tt, c), lambda bi, i: (bi, tix(i), 0))
    two = pl.BlockSpec((1, 1, tt, c), lambda bi, i: (d, bi, tix(i), 0))
    return pl.pallas_call(
        functools.partial(_wkv_kernel, reverse=reverse),
        grid=(b, nblk),
        in_specs=[one, one, one, two, two, two],
        out_specs=one,
        out_shape=jax.ShapeDtypeStruct((b, s, c), F32),
        scratch_shapes=[pltpu.VMEM((RW_HEADS // 2, LANES, LANES), F32)],
        compiler_params=_cparams(("parallel", "arbitrary")),
    )(r, v, kk, kd, bb, lw)


WKV2_BATCH = 2


def _wkv2_kernel(rf, vf, kkf, rb, vb, kkb, kdf, bf, lwf, kdb, bb, lwb, yf_ref, yb_ref, s_ref):
    cc = WKV_CHUNK
    nb = rf.shape[0]
    nchunk = rf.shape[1] // cc

    @pl.when(pl.program_id(1) == 0)
    def _():
        s_ref[...] = jnp.zeros_like(s_ref)

    same_head = ((lax.broadcasted_iota(jnp.int32, (LANES, LANES), 0) < HEAD_DIM)
                 == (lax.broadcasted_iota(jnp.int32, (LANES, LANES), 1) < HEAD_DIM))
    rowc = lax.broadcasted_iota(jnp.int32, (cc, LANES), 0)
    colc = lax.broadcasted_iota(jnp.int32, (cc, LANES), 1) % cc
    eye = (colc == rowc).astype(F32)
    strict = (colc < rowc, colc > rowc)
    incl = (colc <= rowc, colc >= rowc)
    first = lax.broadcasted_iota(jnp.int32, (1, LANES), 1) < HEAD_DIM
    second = jnp.logical_not(first)

    def stack_heads(t):
        z = jnp.zeros_like(t)
        return jnp.concatenate([jnp.where(first, t, z), jnp.where(second, t, z)], axis=0)

    def make_chains(c):
        chains = []
        for bi in range(nb):
            for d in range(2):
                j = nchunk - 1 - c if d else c
                sl = slice(j * cc, (j + 1) * cc)
                src = (rb, vb, kkb, kdb, bb, lwb) if d else (rf, vf, kkf, kdf, bf, lwf)
                for p in range(RW_HEADS // 2):
                    ls = slice(p * LANES, (p + 1) * LANES)
                    chains.append(dict(
                        bi=bi, d=d, p=p, sl=sl, ls=ls,
                        r=src[0][bi, sl, ls], v=src[1][bi, sl, ls], kk=src[2][bi, sl, ls],
                        kd=src[3][0, bi, sl, ls], b=src[4][0, bi, sl, ls], lw=src[5][0, bi, sl, ls]))
        return chains

    def solve_chunks(chains):
        for ch in chains:
            cum = ch["lw"]
            sh = 1
            while sh < cc:
                if ch["d"]:
                    cum = cum + jnp.where(rowc < cc - sh, pltpu.roll(cum, cc - sh, 0), 0.0)
                else:
                    cum = cum + jnp.where(rowc >= sh, pltpu.roll(cum, sh, 0), 0.0)
                sh *= 2
            tot = cum[0:1] if ch["d"] else cum[cc - 1:cc]
            at = (-ch["kk"] * jnp.exp(cum - ch["lw"])).astype(BF16)
            einv = jnp.exp(-cum)
            eh = jnp.exp(tot - cum)
            ch["etot"] = jnp.exp(tot)
            ch["rt"] = (ch["r"] * jnp.exp(cum)).astype(BF16)
            ch["vb"] = ch["v"].astype(BF16)
            ch["v_st"] = stack_heads(ch["vb"])
            ch["at_st"] = stack_heads(at)
            ch["lhs"] = jnp.concatenate([at, ch["rt"]], axis=0)
            ch["rhs"] = jnp.concatenate([stack_heads((ch["b"] * einv).astype(BF16)),
                                         stack_heads((ch["kd"] * einv).astype(BF16))], axis=0)
            ch["kb_hat"] = jnp.concatenate([ch["kd"] * eh, ch["b"] * eh], axis=0).astype(BF16)
        for ch in chains:
            m1 = _dot_nt(ch["lhs"], ch["rhs"])
            d = ch["d"]
            ch["n"] = jnp.where(strict[d], m1[:cc, :LANES], 0.0)
            a_ak = jnp.where(strict[d], m1[:cc, LANES:], 0.0)
            ch["a_rb"] = jnp.where(incl[d], m1[cc:, :LANES], 0.0).astype(BF16)
            a_rk = jnp.where(incl[d], m1[cc:, LANES:], 0.0)
            ch["a_k"] = jnp.concatenate([a_ak, a_rk], axis=0).astype(BF16)
        for ch in chains:
            nbf = ch["n"].astype(BF16)
            ch["pw"] = _dot(nbf, stack_heads(nbf))
            ch["t"] = eye + ch["n"]
        for _ in range(cc.bit_length() - 3):
            for ch in chains:
                pwb = ch["pw"].astype(BF16)
                tp = _dot(jnp.concatenate([ch["t"].astype(BF16), pwb], axis=0), stack_heads(pwb))
                ch["t"] = ch["t"] + tp[:cc]
                ch["pw"] = tp[cc:]
        for ch in chains:
            ch["t"] = (ch["t"] + _dot(ch["t"].astype(BF16), stack_heads(ch["pw"].astype(BF16)))).astype(BF16)
        for ch in chains:
            ch["av"] = _dot(ch["a_k"], ch["v_st"])
        for ch in chains:
            akv = stack_heads(ch["av"][:cc].astype(BF16))
            tx = _dot(ch["t"], jnp.concatenate([ch["at_st"], akv], axis=1))
            ch["atp"] = tx[:, :LANES].astype(BF16)
            ch["u0"] = tx[:, LANES:]

    def advance_state(chains):
        for ch in chains:
            ch["st"] = s_ref[ch["bi"], ch["d"], ch["p"]]
            qs = _dot_nt(jnp.concatenate([ch["atp"], ch["rt"]], axis=0), ch["st"].astype(BF16))
            ch["u"] = (qs[:cc] + ch["u0"]).astype(BF16)
            ch["ys"] = qs[cc:] + ch["av"][cc:]
        for ch in chains:
            upd = _dot_tn(jnp.concatenate([ch["vb"], ch["u"]], axis=0), ch["kb_hat"])
            s_ref[ch["bi"], ch["d"], ch["p"]] = ch["st"] * ch["etot"] + jnp.where(same_head, upd, 0.0)
        for ch in chains:
            y_ref = yb_ref if ch["d"] else yf_ref
            y_ref[ch["bi"], ch["sl"], ch["ls"]] = ch["ys"] + _dot(ch["a_rb"], stack_heads(ch["u"]))

    per_chunk = [make_chains(c) for c in range(nchunk)]
    solve_chunks([ch for chains in per_chunk for ch in chains])
    for chains in per_chunk:
        advance_state(chains)


def _wkv2(r, v, kk, kd, bb, lw):
    b, s, c = r.shape
    tt = WKV_CHUNK * WKV_STEP_CHUNKS
    nblk = s // tt
    nb = WKV2_BATCH if b % WKV2_BATCH == 0 else 1
    fwd = pl.BlockSpec((nb, tt, c), lambda bi, i: (bi, i, 0))
    bwd = pl.BlockSpec((nb, tt, c), lambda bi, i: (bi, nblk - 1 - i, 0))
    fwd2 = pl.BlockSpec((1, nb, tt, c), lambda bi, i: (0, bi, i, 0))
    bwd2 = pl.BlockSpec((1, nb, tt, c), lambda bi, i: (1, bi, nblk - 1 - i, 0))
    return pl.pallas_call(
        _wkv2_kernel,
        grid=(b // nb, nblk),
        in_specs=[fwd, fwd, fwd, bwd, bwd, bwd, fwd2, fwd2, fwd2, bwd2, bwd2, bwd2],
        out_specs=[fwd, bwd],
        out_shape=[jax.ShapeDtypeStruct((b, s, c), F32)] * 2,
        scratch_shapes=[pltpu.VMEM((nb, 2, RW_HEADS // 2, LANES, LANES), F32)],
        compiler_params=_cparams(("parallel", "arbitrary")),
    )(r, v, kk, r, v, kk, kd, bb, lw, kd, bb, lw)


def _memkv_kernel(mem_ref, nw_ref, w_ref, kw_ref, k_ref, v_ref):
    x = mem_ref[0]
    ms = jnp.mean(x * x, axis=-1, keepdims=True)
    h = (x * lax.rsqrt(ms + RMS_EPS) * nw_ref[...]).astype(BF16)
    kv = _dot(h, w_ref[...])
    k = kv[:, :MEM_WIDTH]
    ms = _split_dot(k * k, _head_ones(MEM_WIDTH)) * (1.0 / HEAD_DIM)
    k_ref[0] = (k * lax.rsqrt(ms + RMS_EPS) * kw_ref[...]).astype(BF16)
    v_ref[0] = kv[:, MEM_WIDTH:].astype(BF16)


def _memkv(mem, nw, w_kv, kw):
    b, m, d = mem.shape
    full = lambda a: pl.BlockSpec(a.shape, lambda bi: (0,) * a.ndim)
    out = pl.BlockSpec((1, m, MEM_WIDTH), lambda bi: (bi, 0, 0))
    return pl.pallas_call(
        _memkv_kernel,
        grid=(b,),
        in_specs=[pl.BlockSpec((1, m, d), lambda bi: (bi, 0, 0)), full(nw), full(w_kv), full(kw)],
        out_specs=[out, out],
        out_shape=[jax.ShapeDtypeStruct((b, m, MEM_WIDTH), BF16)] * 2,
        compiler_params=_cparams(("parallel",)),
    )(mem, nw, w_kv, kw)


def _mixout_kernel(x_ref, na_ref, yf_ref, yb_ref, bonus_ref, g_ref, mq_ref, mk_ref, mv_ref,
                   lnw_ref, lnb_ref, wo_ref, o_ref):
    ones = _head_ones(RW_WIDTH)
    y = yf_ref[0] + yb_ref[0]
    mu = _split_dot(y, ones) * (1.0 / HEAD_DIM)
    yc = y - mu
    var = _split_dot(yc * yc, ones) * (1.0 / HEAD_DIM)
    yn = yc * lax.rsqrt(var + RW_GN_EPS) * lnw_ref[...] + lnb_ref[...]
    y_rw = ((yn + bonus_ref[0]) * g_ref[0]).astype(BF16)

    first = lax.broadcasted_iota(jnp.int32, (1, LANES), 1) < HEAD_DIM
    mems = []
    for p in range(MEM_HEADS // 2):
        ls = slice(p * LANES, (p + 1) * LANES)
        qp = mq_ref[0, :, ls]
        kp = mk_ref[0, :, ls]
        vp = mv_ref[0, :, ls]
        outs = []
        for hh in range(2):
            sel = first if hh == 0 else jnp.logical_not(first)
            s = _dot_nt(jnp.where(sel, qp, jnp.zeros_like(qp)), kp)
            m = jnp.max(s, axis=-1, keepdims=True)
            e = jnp.exp(s - m)
            l = jnp.sum(e, axis=-1, keepdims=True)
            outs.append(_dot(e.astype(BF16), vp) / l)
        mems.append(jnp.where(first, outs[0], outs[1]).astype(BF16))
    y_mem = jnp.concatenate(mems, axis=1)

    acc = _dot(na_ref[0], wo_ref[0:NA_WIDTH, :])
    acc = acc + _dot(y_rw, wo_ref[NA_WIDTH:NA_WIDTH + RW_WIDTH, :])
    acc = acc + _dot(y_mem, wo_ref[NA_WIDTH + RW_WIDTH:, :])
    o_ref[0] = x_ref[0] + acc


def _mixout(x, y_na, y_f, y_b, bonus, g, mq, mk, mv, ln_w, ln_b, w_out, tm):
    b, s, d = x.shape
    m = mk.shape[1]
    tok = lambda w: pl.BlockSpec((1, tm, w), lambda bi, i: (bi, i, 0))
    full = lambda a: pl.BlockSpec(a.shape, lambda bi, i: (0,) * a.ndim)
    memb = pl.BlockSpec((1, m, MEM_WIDTH), lambda bi, i: (bi, 0, 0))
    return pl.pallas_call(
        _mixout_kernel,
        grid=(b, s // tm),
        in_specs=[tok(d), tok(NA_WIDTH), tok(RW_WIDTH), tok(RW_WIDTH), tok(RW_WIDTH), tok(RW_WIDTH),
                  tok(MEM_WIDTH), memb, memb, full(ln_w), full(ln_b), full(w_out)],
        out_specs=tok(d),
        out_shape=jax.ShapeDtypeStruct((b, s, d), F32),
        compiler_params=_cparams(("parallel", "parallel")),
    )(x, y_na, y_f, y_b, bonus, g, mq, mk, mv, ln_w, ln_b, w_out)


ROUTER_LANES = 128


MOE_SUB = 256
MOE_CHUNK = 16
MOE_LOCAL_CHUNKS = 48
MOE_BLOCK_CHUNKS = 32
MOE_TOP_K = 2
ROUTE_ROWS = 8
assert MOE_LOCAL_CHUNKS >= MOE_TOP_K * MOE_SUB // MOE_CHUNK + MOE_EXPERTS - 1


def _route_subtile(lg):
    t = lg.shape[0]
    lane = lax.broadcasted_iota(jnp.int32, (1, ROUTER_LANES), 1)
    lanef = lane.astype(F32)
    big = float(ROUTER_LANES)

    def argmax_lane(v):
        m = jnp.max(v, axis=-1, keepdims=True)
        return m, jnp.min(jnp.where(v == m, lanef, big), axis=-1, keepdims=True)

    is_group = lane < MOE_GROUPS
    gmax, gidx = argmax_lane(jnp.where(is_group, lg, NEG_BIG))
    gsum = jnp.sum(jnp.where(is_group, jnp.exp(lg - gmax), 0.0), axis=-1, keepdims=True)
    lo = MOE_GROUPS + MOE_EXPERTS_PER_GROUP * gidx
    el = jnp.where((lanef >= lo) & (lanef < lo + MOE_EXPERTS_PER_GROUP), lg, NEG_BIG)
    m1, i1 = argmax_lane(el)
    m2, i2 = argmax_lane(jnp.where(lanef == i1, NEG_BIG, el))
    ratio = jnp.exp(m2 - m1)
    w0 = 1.0 / (gsum * (1.0 + ratio))
    w1 = w0 * ratio

    oh0 = (lanef == i1 - MOE_GROUPS).astype(F32)
    oh1 = (lanef == i2 - MOE_GROUPS).astype(F32)
    row = lax.broadcasted_iota(jnp.int32, (t, t), 0)
    col = lax.broadcasted_iota(jnp.int32, (t, t), 1)
    before = (col < row).astype(BF16)
    cnt0 = jnp.sum(oh0, axis=0, keepdims=True)
    cnt1 = jnp.sum(oh1, axis=0, keepdims=True)
    rank0 = _dot(before, oh0.astype(BF16))
    rank1 = _dot(before, oh1.astype(BF16)) + cnt0
    nch = jnp.floor((cnt0 + cnt1 + (MOE_CHUNK - 1.0)) * (1.0 / MOE_CHUNK))
    li = lax.broadcasted_iota(jnp.int32, (ROUTER_LANES, ROUTER_LANES), 0)
    lj = lax.broadcasted_iota(jnp.int32, (ROUTER_LANES, ROUTER_LANES), 1)
    start = _dot(jnp.broadcast_to(nch, (8, ROUTER_LANES)).astype(BF16), (li < lj).astype(BF16))[0:1]
    pos0 = jnp.sum((start * MOE_CHUNK + rank0) * oh0, axis=-1, keepdims=True)
    pos1 = jnp.sum((start * MOE_CHUNK + rank1) * oh1, axis=-1, keepdims=True)
    rec = (jnp.where(lane == 0, pos0, 0.0) + jnp.where(lane == 1, pos1, 0.0)
           + jnp.where(lane == 2, w0, 0.0) + jnp.where(lane == 3, w1, 0.0))
    return jnp.transpose(rec)[0:ROUTE_ROWS], nch


def _router_kernel(x_ref, nw_ref, whi_ref, wlo_ref, b_ref, h_ref, route_ref, nch_ref):
    x = x_ref[...]
    ms = jnp.mean(x * x, axis=-1, keepdims=True)
    h = x * lax.rsqrt(ms + RMS_EPS) * nw_ref[...]
    hi = h.astype(BF16)
    lo = (h - hi.astype(F32)).astype(BF16)
    h_ref[...] = hi
    lg = _dot(hi, whi_ref[...]) + _dot(hi, wlo_ref[...]) + _dot(lo, whi_ref[...]) + b_ref[...]
    for j in range(x.shape[0] // MOE_SUB):
        route_ref[j], nch_ref[j] = _route_subtile(lg[j * MOE_SUB:(j + 1) * MOE_SUB])


def _router(x2, nw, w_group, b_group, w_expert, b_expert, tm):
    n, d = x2.shape
    pad = ROUTER_LANES - MOE_GROUPS - MOE_EXPERTS
    w = jnp.concatenate([w_group, w_expert, jnp.zeros((d, pad), F32)], axis=1)
    whi = w.astype(BF16)
    wlo = (w - whi.astype(F32)).astype(BF16)
    bias = jnp.concatenate([b_group, b_expert, jnp.zeros((pad,), F32)]).reshape(1, -1)
    full = lambda a: pl.BlockSpec(a.shape, lambda i: (0,) * a.ndim)
    sub = tm // MOE_SUB
    return pl.pallas_call(
        _router_kernel,
        grid=(n // tm,),
        in_specs=[pl.BlockSpec((tm, d), lambda i: (i, 0)), full(nw), full(whi), full(wlo), full(bias)],
        out_specs=[pl.BlockSpec((tm, d), lambda i: (i, 0)),
                   pl.BlockSpec((sub, ROUTE_ROWS, MOE_SUB), lambda i: (i, 0, 0)),
                   pl.BlockSpec((sub, 1, ROUTER_LANES), lambda i: (i, 0, 0))],
        out_shape=[jax.ShapeDtypeStruct((n, d), BF16),
                   jax.ShapeDtypeStruct((n // MOE_SUB, ROUTE_ROWS, MOE_SUB), F32),
                   jax.ShapeDtypeStruct((n // MOE_SUB, 1, ROUTER_LANES), F32)],
        compiler_params=_cparams(("parallel",)),
    )(x2, nw, whi, wlo, bias)


def _moe_plan(nch, n):
    nch = nch[:, 0, :MOE_EXPERTS].astype(jnp.int32)
    lc_end = jnp.cumsum(nch, axis=1)
    lc_start = lc_end - nch
    nloc = lc_end[:, -1]
    nblk = (jnp.sum(nch, axis=0) + MOE_BLOCK_CHUNKS - 1) // MOE_BLOCK_CHUNKS
    bend = jnp.cumsum(nblk)
    gch = ((bend - nblk) * MOE_BLOCK_CHUNKS)[None, :] + jnp.cumsum(nch, axis=0) - nch
    c = jnp.arange(MOE_LOCAL_CHUNKS, dtype=jnp.int32)
    e_c = jnp.minimum(jnp.sum(c[None, :, None] >= lc_end[:, None, :], axis=-1), MOE_EXPERTS - 1)
    dest = (jnp.take_along_axis(gch, e_c, axis=1) + c[None, :] - jnp.take_along_axis(lc_start, e_c, axis=1))
    dest = jnp.where(c[None, :] < nloc[:, None], dest, 0)

    nblocks = _moe_blocks(n)
    b = jnp.arange(nblocks, dtype=jnp.int32)
    block_e = jnp.minimum(jnp.sum(b[:, None] >= bend[None, :], axis=-1), MOE_EXPERTS - 1)
    return (dest.astype(jnp.int32), nloc.astype(jnp.int32), block_e.astype(jnp.int32),
            bend[-1:].astype(jnp.int32))


def _moe_blocks(n):
    ns = n // MOE_SUB
    chunks = ns * (MOE_TOP_K * MOE_SUB // MOE_CHUNK + MOE_EXPERTS - 1) + MOE_EXPERTS * (MOE_BLOCK_CHUNKS - 1)
    return -(-chunks // MOE_BLOCK_CHUNKS)


def _local_onehot(pos_rows, shape, row_axis):
    idx = lax.broadcasted_iota(jnp.int32, shape, row_axis)
    return (idx == pos_rows[0]) | (idx == pos_rows[1])


def _dispatch_kernel(dest_ref, nloc_ref, h_ref, pos_ref, zeros_ref, xs_ref, buf, sem):
    del zeros_ref
    s = pl.program_id(0)
    ns = pl.num_programs(0)
    slot = s % 2

    def chunk_copy(sl, c, step):
        src = buf.at[sl, pl.ds(pl.multiple_of(c * MOE_CHUNK, MOE_CHUNK), MOE_CHUNK)]
        dst = xs_ref.at[pl.ds(pl.multiple_of(dest_ref[step, c] * MOE_CHUNK, MOE_CHUNK), MOE_CHUNK)]
        return pltpu.make_async_copy(src, dst, sem.at[sl])

    def wait_step(step, sl):
        def body(c, carry):
            chunk_copy(sl, c, step).wait()
            return carry
        lax.fori_loop(0, nloc_ref[step], body, 0)

    @pl.when(s >= 2)
    def _():
        wait_step(s - 2, slot)

    pos = pos_ref[0, 0:MOE_TOP_K].astype(jnp.int32)
    onehot = _local_onehot((pos[0:1], pos[1:2]), (MOE_LOCAL_CHUNKS * MOE_CHUNK, MOE_SUB), 0)
    buf[slot] = _dot(jnp.where(onehot, 1.0, 0.0).astype(BF16), h_ref[...]).astype(BF16)

    def issue(c, carry):
        chunk_copy(slot, c, s).start()
        return carry
    lax.fori_loop(0, nloc_ref[s], issue, 0)

    @pl.when(s == ns - 1)
    def _():
        @pl.when(s >= 1)
        def _():
            wait_step(s - 1, 1 - slot)
        wait_step(s, slot)


def _dispatch(h2, pos, dest, nloc):
    n, d = h2.shape
    ns = n // MOE_SUB
    cap = _moe_blocks(n) * MOE_BLOCK_CHUNKS * MOE_CHUNK
    rows = MOE_LOCAL_CHUNKS * MOE_CHUNK
    return pl.pallas_call(
        _dispatch_kernel,
        grid_spec=pltpu.PrefetchScalarGridSpec(
            num_scalar_prefetch=2, grid=(ns,),
            in_specs=[pl.BlockSpec((MOE_SUB, d), lambda s, *_: (s, 0)),
                      pl.BlockSpec((1, ROUTE_ROWS, MOE_SUB), lambda s, *_: (s, 0, 0)),
                      pl.BlockSpec(memory_space=pl.ANY)],
            out_specs=pl.BlockSpec(memory_space=pl.ANY),
            scratch_shapes=[pltpu.VMEM((2, rows, d), BF16), pltpu.SemaphoreType.DMA((2,))]),
        out_shape=jax.ShapeDtypeStruct((cap, d), BF16),
        input_output_aliases={4: 0},
        compiler_params=_cparams(("arbitrary",)),
    )(dest, nloc, h2, pos, jnp.zeros((cap, d), BF16))


def _experts_kernel(be_ref, nv_ref, x_ref, wg_ref, wu_ref, wd_ref, o_ref):
    del be_ref
    valid = pl.program_id(0) < nv_ref[0]

    @pl.when(valid)
    def _():
        x = x_ref[...]
        hid = jax.nn.silu(_dot(x, wg_ref[0])) * _dot(x, wu_ref[0])
        o_ref[...] = _dot(hid.astype(BF16), wd_ref[0])

    @pl.when(jnp.logical_not(valid))
    def _():
        o_ref[...] = jnp.zeros_like(o_ref)


def _experts(xs, block_e, nvalid, w_gate, w_up, w_down):
    cap, d = xs.shape
    bm = MOE_BLOCK_CHUNKS * MOE_CHUNK
    blk = lambda b, be, nv: (jnp.minimum(b, nv[0] - 1), 0)
    wsel = lambda b, be, nv: (be[jnp.minimum(b, nv[0] - 1)], 0, 0)
    return pl.pallas_call(
        _experts_kernel,
        grid_spec=pltpu.PrefetchScalarGridSpec(
            num_scalar_prefetch=2, grid=(cap // bm,),
            in_specs=[pl.BlockSpec((bm, d), blk),
                      pl.BlockSpec((1, d, MOE_D_FF), wsel),
                      pl.BlockSpec((1, d, MOE_D_FF), wsel),
                      pl.BlockSpec((1, MOE_D_FF, d), wsel)],
            out_specs=pl.BlockSpec((bm, d), lambda b, be, nv: (b, 0))),
        out_shape=jax.ShapeDtypeStruct((cap, d), F32),
        compiler_params=_cparams(("arbitrary",)),
    )(block_e, nvalid, xs, w_gate, w_up, w_down)


def _combine_kernel(dest_ref, x_ref, route_ref, ys_ref, o_ref, buf, sem):
    s = pl.program_id(0)
    ns = pl.num_programs(0)
    slot = s % 2

    def chunk_copy(sl, c, step):
        src = ys_ref.at[pl.ds(pl.multiple_of(dest_ref[step, c] * MOE_CHUNK, MOE_CHUNK), MOE_CHUNK)]
        dst = buf.at[sl, pl.ds(pl.multiple_of(c * MOE_CHUNK, MOE_CHUNK), MOE_CHUNK)]
        return pltpu.make_async_copy(src, dst, sem.at[sl])

    def fetch(step, sl):
        def body(c, carry):
            chunk_copy(sl, c, step).start()
            return carry
        lax.fori_loop(0, MOE_LOCAL_CHUNKS, body, 0)

    @pl.when(s == 0)
    def _():
        fetch(0, 0)

    @pl.when(s + 1 < ns)
    def _():
        fetch(s + 1, 1 - slot)

    def wait(c, carry):
        chunk_copy(slot, c, s).wait()
        return carry
    lax.fori_loop(0, MOE_LOCAL_CHUNKS, wait, 0)

    rec = jnp.transpose(route_ref[0])
    pos = rec[:, 0:MOE_TOP_K].astype(jnp.int32)
    idx = lax.broadcasted_iota(jnp.int32, (MOE_SUB, MOE_LOCAL_CHUNKS * MOE_CHUNK), 1)
    pw = jnp.where(idx == pos[:, 0:1], rec[:, 2:3], 0.0) + jnp.where(idx == pos[:, 1:2], rec[:, 3:4], 0.0)
    ys = buf[slot]
    p_hi = pw.astype(BF16)
    p_lo = (pw - p_hi.astype(F32)).astype(BF16)
    y_hi = ys.astype(BF16)
    y_lo = (ys - y_hi.astype(F32)).astype(BF16)
    o_ref[...] = x_ref[...] + _dot(p_hi, y_hi) + _dot(p_hi, y_lo) + _dot(p_lo, y_hi)


def _combine(x2, ys, route, dest):
    n, d = x2.shape
    ns = n // MOE_SUB
    rows = MOE_LOCAL_CHUNKS * MOE_CHUNK
    return pl.pallas_call(
        _combine_kernel,
        grid_spec=pltpu.PrefetchScalarGridSpec(
            num_scalar_prefetch=1, grid=(ns,),
            in_specs=[pl.BlockSpec((MOE_SUB, d), lambda s, *_: (s, 0)),
                      pl.BlockSpec((1, ROUTE_ROWS, MOE_SUB), lambda s, *_: (s, 0, 0)),
                      pl.BlockSpec(memory_space=pl.ANY)],
            out_specs=pl.BlockSpec((MOE_SUB, d), lambda s, *_: (s, 0)),
            scratch_shapes=[pltpu.VMEM((2, rows, d), F32), pltpu.SemaphoreType.DMA((2,))]),
        out_shape=jax.ShapeDtypeStruct((n, d), F32),
        compiler_params=_cparams(("arbitrary",)),
    )(dest, x2, route, ys)


def _moe(x2, h2, route, nch, w_gate, w_up, w_down):
    dest, nloc, block_e, nvalid = _moe_plan(nch, x2.shape[0])
    xs = _dispatch(h2, route, dest, nloc)
    ys = _experts(xs, block_e, nvalid, w_gate, w_up, w_down)
    return _combine(x2, ys, route, dest)


def _tile(n, want):
    t = min(n, want)
    assert n % t == 0
    return t


def kernel(x, mem, attn_norm_w, w_in, na_q_norm_w, na_k_norm_w, na_rpb, rw_mu_prev, rw_mu_next, rw_w0, rw_w2, rw_a0, rw_a2, rw_g2, rw_k_k, rw_k_a, rw_r_k, rw_ln_w, rw_ln_b, mem_norm_w, w_mem_kv, mem_q_norm_w, mem_k_norm_w, w_out, ffn_norm_w, moe_w_group, moe_b_group, moe_w_expert, moe_b_expert, moe_w_gate, moe_w_up, moe_w_down):
    b, s, d = x.shape
    n = b * s
    depth = w_in.shape[0]
    assert s % (NA_BLOCK_ROWS * GRID_W) == 0 and s // GRID_W >= 2 * NA_KH
    tm = _tile(n, 512)
    ts = _tile(s, 512)
    for l in range(depth):
        q, k, v, rw, mq = _proj(
            x.reshape(n, d), _row(attn_norm_w[l]), w_in[l].astype(BF16),
            _row(jnp.tile(na_q_norm_w[l], NA_HEADS)), _row(jnp.tile(na_k_norm_w[l], NA_HEADS)),
            _row(jnp.tile(mem_q_norm_w[l], MEM_HEADS)), tm)
        y_na = _na(q.reshape(b, s, NA_WIDTH), k.reshape(b, s, NA_WIDTH), v.reshape(b, s, NA_WIDTH),
                   _na_bias_table(na_rpb[l]))
        r, vv, kk, kd, bb, lw, g, bonus = _rwprep(
            rw.reshape(b, s, RW_PROJ), rw_mu_prev[l], rw_mu_next[l], rw_w0[l], rw_w2[l], rw_a0[l],
            rw_a2[l], rw_g2[l], rw_k_k[l], rw_k_a[l], rw_r_k[l], ts)
        y_f, y_b = _wkv2(r, vv, kk, kd, bb, lw)
        mk, mv = _memkv(mem, _row(mem_norm_w[l]), w_mem_kv[l].astype(BF16),
                        _row(jnp.tile(mem_k_norm_w[l], MEM_HEADS)))
        x = _mixout(x, y_na, y_f, y_b, bonus, g, mq.reshape(b, s, MEM_WIDTH), mk, mv,
                    _row(rw_ln_w[l]), _row(rw_ln_b[l]), w_out[l].astype(BF16), ts)
        h2, route, nch = _router(x.reshape(n, d), _row(ffn_norm_w[l]), moe_w_group[l], moe_b_group[l],
                                 moe_w_expert[l], moe_b_expert[l], tm)
        x = _moe(x.reshape(n, d), h2, route, nch, moe_w_gate[l].astype(BF16),
                 moe_w_up[l].astype(BF16), moe_w_down[l].astype(BF16)).reshape(b, s, d)
    return x
```

```python
import functools

import jax
import jax.numpy as jnp
import numpy as np
from jax import lax
from jax.experimental import pallas as pl
from jax.experimental.pallas import tpu as pltpu

F32 = jnp.float32
BF16 = jnp.bfloat16

GRID_W = 64
HEAD_DIM = 64
NA_HEADS = 8
NA_WIDTH = NA_HEADS * HEAD_DIM
NA_KH = 8
NA_KW = 16
RW_HEADS = 4
RW_WIDTH = RW_HEADS * HEAD_DIM
RW_LORA_W = 64
RW_LORA_A = 64
RW_LORA_G = 128
RW_PROJ = 3 * RW_WIDTH + 2 * RW_LORA_W + 2 * RW_LORA_A + RW_LORA_G
MEM_HEADS = 4
MEM_WIDTH = MEM_HEADS * HEAD_DIM
MOE_GROUPS = 4
MOE_EXPERTS_PER_GROUP = 4
MOE_EXPERTS = MOE_GROUPS * MOE_EXPERTS_PER_GROUP
MOE_D_FF = 512
RMS_EPS = 1e-6
RW_GN_EPS = 64e-5

LANES = 128
WKV_CHUNK = 64
NEG_BIG = -1e30
VMEM_LIMIT = 56 * 1024 * 1024


def _cparams(sem):
    return pltpu.CompilerParams(dimension_semantics=sem, vmem_limit_bytes=VMEM_LIMIT)


def _dot(a, b):
    return jnp.dot(a, b, preferred_element_type=F32)


def _dot_nt(a, b):
    return lax.dot_general(a, b, (((1,), (1,)), ((), ())), preferred_element_type=F32)


def _dot_tn(a, b):
    return lax.dot_general(a, b, (((0,), (0,)), ((), ())), preferred_element_type=F32)


def _split_dot(t, m):
    hi = t.astype(BF16)
    lo = (t - hi.astype(F32)).astype(BF16)
    return _dot(hi, m) + _dot(lo, m)


def _head_ones(width):
    i = lax.broadcasted_iota(jnp.int32, (width, width), 0) // HEAD_DIM
    j = lax.broadcasted_iota(jnp.int32, (width, width), 1) // HEAD_DIM
    return (i == j).astype(BF16)


def _row(v):
    return v.reshape(1, -1).astype(F32)


def _proj_kernel(x_ref, nw_ref, w_ref, qw_ref, kw_ref, mw_ref, q_ref, k_ref, v_ref, rw_ref, mq_ref):
    x = x_ref[...]
    ms = jnp.mean(x * x, axis=-1, keepdims=True)
    h = (x * lax.rsqrt(ms + RMS_EPS) * nw_ref[...]).astype(BF16)
    ones_na = _head_ones(NA_WIDTH)

    def head_norm(t, w, ones):
        ms = _dot((t * t).astype(BF16), ones) * (1.0 / HEAD_DIM)
        return t * lax.rsqrt(ms + RMS_EPS) * w

    o = 0
    q = _dot(h, w_ref[:, o:o + NA_WIDTH])
    q_ref[...] = (head_norm(q, qw_ref[...], ones_na) * (HEAD_DIM ** -0.5)).astype(BF16)
    o += NA_WIDTH
    k = _dot(h, w_ref[:, o:o + NA_WIDTH])
    k_ref[...] = head_norm(k, kw_ref[...], ones_na).astype(BF16)
    o += NA_WIDTH
    v_ref[...] = _dot(h, w_ref[:, o:o + NA_WIDTH]).astype(BF16)
    o += NA_WIDTH
    rw_ref[...] = _dot(h, w_ref[:, o:o + RW_PROJ])
    o += RW_PROJ
    mq = _dot(h, w_ref[:, o:o + MEM_WIDTH])
    mq_ref[...] = (head_norm(mq, mw_ref[...], _head_ones(MEM_WIDTH)) * (HEAD_DIM ** -0.5)).astype(BF16)


def _proj(x2, nw, w_in, qw, kw, mw, tm):
    n, d = x2.shape
    p_in = w_in.shape[1]
    tok = lambda w: pl.BlockSpec((tm, w), lambda i: (i, 0))
    full = lambda a: pl.BlockSpec(a.shape, lambda i: (0,) * a.ndim)
    return pl.pallas_call(
        _proj_kernel,
        grid=(n // tm,),
        in_specs=[tok(d), full(nw), full(w_in), full(qw), full(kw), full(mw)],
        out_specs=[tok(NA_WIDTH), tok(NA_WIDTH), tok(NA_WIDTH), tok(RW_PROJ), tok(MEM_WIDTH)],
        out_shape=[jax.ShapeDtypeStruct((n, NA_WIDTH), BF16)] * 3
        + [jax.ShapeDtypeStruct((n, RW_PROJ), F32), jax.ShapeDtypeStruct((n, MEM_WIDTH), BF16)],
        compiler_params=_cparams(("parallel",)),
    )(x2, nw, w_in, qw, kw, mw)


NA_BLOCK_ROWS = 8


def _na_bias_table(rpb):
    col = np.arange(GRID_W)
    cs = np.clip(col - NA_KW // 2, 0, GRID_W - NA_KW)
    dc = col[None, :] - col[:, None]
    inside = (col[None, :] >= cs[:, None]) & (col[None, :] < cs[:, None] + NA_KW)
    dci = np.clip(dc + NA_KW - 1, 0, 2 * NA_KW - 2)
    off = np.arange(NA_KH) - (NA_KH - 1)
    dri = off[:, None] + np.arange(NA_KH)[None, :] + NA_KH - 1
    row_sel = (dri[:, :, None] == np.arange(2 * NA_KH - 1)).astype(np.float32)
    col_sel = (dci[:, :, None] == np.arange(2 * NA_KW - 1)).astype(np.float32)
    t = jnp.einsum("hrq,onr,cdq->ohcnd", rpb.astype(F32), row_sel, col_sel,
                   precision=lax.Precision.HIGHEST)
    t = jnp.where(inside[None, None, :, None, :], t, NEG_BIG)
    return t.reshape(NA_KH, NA_HEADS // 2, 2 * GRID_W, NA_KH * GRID_W).astype(F32)


def _na_kernel(q_ref, kp_ref, kc_ref, kn_ref, vp_ref, vc_ref, vn_ref, bias_ref, o_ref, kwin, vwin, *, rows):
    j = pl.program_id(1)
    blk = NA_BLOCK_ROWS * GRID_W
    kwin[0:blk] = kp_ref[0]
    kwin[blk:2 * blk] = kc_ref[0]
    kwin[2 * blk:3 * blk] = kn_ref[0]
    vwin[0:blk] = vp_ref[0]
    vwin[blk:2 * blk] = vc_ref[0]
    vwin[2 * blk:3 * blk] = vn_ref[0]
    lane = lax.broadcasted_iota(jnp.int32, (1, LANES), 1)
    first = lane < HEAD_DIM

    def row_body(i, carry):
        r = j * NA_BLOCK_ROWS + i
        rs = jnp.clip(r - NA_KH // 2, 0, rows - NA_KH)
        oi = rs - r + (NA_KH - 1)
        start = pl.multiple_of((rs - (j - 1) * NA_BLOCK_ROWS) * GRID_W, GRID_W)
        qs = pl.multiple_of(i * GRID_W, GRID_W)
        pairs = [slice(p * LANES, (p + 1) * LANES) for p in range(NA_HEADS // 2)]
        scores = []
        for p, ls in enumerate(pairs):
            qp = q_ref[0, pl.ds(qs, GRID_W), ls]
            zero = jnp.zeros_like(qp)
            qst = jnp.concatenate([jnp.where(first, qp, zero), jnp.where(first, zero, qp)], axis=0)
            scores.append(_dot_nt(qst, kwin[pl.ds(start, NA_KH * GRID_W), ls]) + bias_ref[oi, p])
        probs, norms = [], []
        for s in scores:
            e = jnp.exp(s - jnp.max(s, axis=-1, keepdims=True))
            norms.append(jnp.sum(e, axis=-1, keepdims=True))
            probs.append(e.astype(BF16))
        for ls, e, l in zip(pairs, probs, norms):
            o = _dot(e, vwin[pl.ds(start, NA_KH * GRID_W), ls]) / l
            o_ref[0, pl.ds(qs, GRID_W), ls] = jnp.where(first, o[:GRID_W], o[GRID_W:]).astype(o_ref.dtype)
        return carry

    lax.fori_loop(0, NA_BLOCK_ROWS, row_body, 0)


def _na(q, k, v, bias):
    b, s, w = q.shape
    rows = s // GRID_W
    nblk = rows // NA_BLOCK_ROWS
    blk = NA_BLOCK_ROWS * GRID_W
    cur = pl.BlockSpec((1, blk, w), lambda bi, j: (bi, j, 0))
    prv = pl.BlockSpec((1, blk, w), lambda bi, j: (bi, jnp.maximum(j - 1, 0), 0))
    nxt = pl.BlockSpec((1, blk, w), lambda bi, j: (bi, jnp.minimum(j + 1, nblk - 1), 0))
    return pl.pallas_call(
        functools.partial(_na_kernel, rows=rows),
        grid=(b, nblk),
        in_specs=[cur, prv, cur, nxt, prv, cur, nxt,
                  pl.BlockSpec(bias.shape, lambda bi, j: (0, 0, 0, 0))],
        out_specs=cur,
        out_shape=jax.ShapeDtypeStruct((b, s, w), BF16),
        scratch_shapes=[pltpu.VMEM((3 * blk, w), BF16), pltpu.VMEM((3 * blk, w), BF16)],
        compiler_params=_cparams(("parallel", "parallel")),
    )(q, k, k, k, v, v, v, bias)


def _rwprep_kernel(rw_ref, prev_ref, next_ref, mup_ref, mun_ref, w0_ref, w2_ref, a0_ref, a2_ref, g2_ref,
                   kk_w_ref, ka_ref, rk_ref,
                   r_ref, v_ref, kk_ref, kd_ref, b_ref, lw_ref, g_ref, bonus_ref, *, nblk):
    i = pl.program_id(1)
    s = rw_ref[0]
    ts = s.shape[0]
    rowi = lax.broadcasted_iota(jnp.int32, (ts, 1), 0)
    halo_p = jnp.where(i > 0, prev_ref[0, 7:8, :], 0.0)
    halo_n = jnp.where(i < nblk - 1, next_ref[0, 0:1, :], 0.0)
    prev = jnp.where(rowi == 0, halo_p, pltpu.roll(s, 1, 0))
    nxt = jnp.where(rowi == ts - 1, halo_n, pltpu.roll(s, ts - 1, 0))
    s = s + mup_ref[...] * (prev - s) + mun_ref[...] * (nxt - s)

    c = RW_WIDTH
    r = s[:, 0:c]
    k = s[:, c:2 * c]
    v = s[:, 2 * c:3 * c]
    lw = s[:, 3 * c:3 * c + 2 * RW_LORA_W]
    la = s[:, 3 * c + 2 * RW_LORA_W:3 * c + 2 * RW_LORA_W + 2 * RW_LORA_A]
    lg = s[:, 3 * c + 2 * RW_LORA_W + 2 * RW_LORA_A:]
    ones = _head_ones(c)

    g_ref[0] = _dot(jax.nn.sigmoid(lg).astype(BF16), g2_ref[...])
    kk = k * kk_w_ref[...]
    nrm = jnp.sqrt(_split_dot(kk * kk, ones))
    kk = kk / jnp.maximum(nrm, 1e-12)
    wl_pre = w0_ref[...] + _dot(jnp.tanh(lw).astype(BF16), w2_ref[...])
    a_all = jax.nn.sigmoid(a0_ref[...] + _dot(la.astype(BF16), a2_ref[...]))
    wl = -jax.nn.softplus(-wl_pre) - 0.5
    logdecay = -jnp.exp(wl)
    r_ref[0] = r
    v_ref[0] = v
    kk_ref[0] = kk
    kd_sum = jnp.zeros_like(k)
    for d in range(2):
        a = a_all[:, d * c:(d + 1) * c]
        kd = k * (1.0 + (a - 1.0) * ka_ref[...])
        kd_ref[d, 0] = kd
        b_ref[d, 0] = kk * a
        lw_ref[d, 0] = logdecay[:, d * c:(d + 1) * c]
        kd_sum = kd_sum + kd
    bonus_ref[0] = _split_dot(r * kd_sum * rk_ref[...], ones) * v


def _blockdiag2(m):
    z = jnp.zeros_like(m[0])
    return jnp.concatenate([jnp.concatenate([m[0], z], 1), jnp.concatenate([z, m[1]], 1)], 0)


def _rwprep(rw, mu_prev, mu_next, w0, w2, a0, a2, g2, k_k, k_a, r_k, ts):
    b, s, pw = rw.shape
    nblk = s // ts
    c = RW_WIDTH
    cur = pl.BlockSpec((1, ts, pw), lambda bi, i: (bi, i, 0))
    prv = pl.BlockSpec((1, 8, pw), lambda bi, i: (bi, jnp.maximum(i * (ts // 8) - 1, 0), 0))
    nxt = pl.BlockSpec((1, 8, pw), lambda bi, i: (bi, jnp.minimum((i + 1) * (ts // 8), s // 8 - 1), 0))
    params = [_row(mu_prev), _row(mu_next), _row(w0), _blockdiag2(w2).astype(BF16), _row(a0),
              _blockdiag2(a2).astype(BF16), g2.astype(BF16), _row(k_k), _row(k_a), _row(r_k)]
    full = lambda a: pl.BlockSpec(a.shape, lambda bi, i: (0,) * a.ndim)
    one = pl.BlockSpec((1, ts, c), lambda bi, i: (bi, i, 0))
    two = pl.BlockSpec((2, 1, ts, c), lambda bi, i: (0, bi, i, 0))
    s1 = jax.ShapeDtypeStruct((b, s, c), F32)
    s2 = jax.ShapeDtypeStruct((2, b, s, c), F32)
    return pl.pallas_call(
        functools.partial(_rwprep_kernel, nblk=nblk),
        grid=(b, nblk),
        in_specs=[cur, prv, nxt] + [full(p) for p in params],
        out_specs=[one, one, one, two, two, two, one, one],
        out_shape=[s1, s1, s1, s2, s2, s2, s1, s1],
        compiler_params=_cparams(("parallel", "parallel")),
    )(rw, rw, rw, *params)


WKV_STEP_CHUNKS = 4


def _wkv_kernel(r_ref, v_ref, kk_ref, kd_ref, b_ref, lw_ref, y_ref, s_ref, *, reverse):
    cc = WKV_CHUNK

    @pl.when(pl.program_id(1) == 0)
    def _():
        s_ref[...] = jnp.zeros_like(s_ref)

    row = lax.broadcasted_iota(jnp.int32, (cc, cc), 0)
    col = lax.broadcasted_iota(jnp.int32, (cc, cc), 1)
    strict = (col > row) if reverse else (col < row)
    incl = (col >= row) if reverse else (col <= row)
    eye = (col == row).astype(F32)
    rowc = lax.broadcasted_iota(jnp.int32, (cc, LANES), 0)
    first = lax.broadcasted_iota(jnp.int32, (1, LANES), 1) < HEAD_DIM
    same_head = ((lax.broadcasted_iota(jnp.int32, (LANES, LANES), 0) < HEAD_DIM)
                 == (lax.broadcasted_iota(jnp.int32, (LANES, LANES), 1) < HEAD_DIM))

    for step in range(WKV_STEP_CHUNKS):
        j = WKV_STEP_CHUNKS - 1 - step if reverse else step
        sl = slice(j * cc, (j + 1) * cc)
        for p in range(RW_HEADS // 2):
            ls = slice(p * LANES, (p + 1) * LANES)
            r = r_ref[0, sl, ls]
            v = v_ref[0, sl, ls]
            kk = kk_ref[0, sl, ls]
            kd = kd_ref[0, 0, sl, ls]
            b = b_ref[0, 0, sl, ls]
            lw = lw_ref[0, 0, sl, ls]
            cum = lw
            sh = 1
            while sh < cc:
                if reverse:
                    cum = cum + jnp.where(rowc < cc - sh, pltpu.roll(cum, cc - sh, 0), 0.0)
                else:
                    cum = cum + jnp.where(rowc >= sh, pltpu.roll(cum, sh, 0), 0.0)
                sh *= 2
            tot = cum[0:1] if reverse else cum[cc - 1:cc]
            at = -kk * jnp.exp(cum - lw)
            rt = r * jnp.exp(cum)
            einv = jnp.exp(-cum)
            kt = (kd * einv).astype(BF16)
            bt = (b * einv).astype(BF16)
            eh = jnp.exp(tot - cum)
            kh = kd * eh
            bh = b * eh
            vb = v.astype(BF16)
            qq = jnp.concatenate([at, rt], axis=0)
            atp, u0, y0, arb = [], [], [], []
            for hh in range(2):
                sel = first if hh == 0 else jnp.logical_not(first)
                qm = jnp.where(sel, qq, 0.0).astype(BF16)
                mb = _dot_nt(qm, bt)
                mk = _dot_nt(qm, kt)
                a_ab = jnp.where(strict, mb[:cc], 0.0)
                a_rb = jnp.where(incl, mb[cc:], 0.0)
                a_ak = jnp.where(strict, mk[:cc], 0.0)
                a_rk = jnp.where(incl, mk[cc:], 0.0)
                t = eye + a_ab
                pw = a_ab
                for _ in range(cc.bit_length() - 2):
                    pw = _dot(pw.astype(BF16), pw.astype(BF16))
                    t = t + _dot(t.astype(BF16), pw.astype(BF16))
                av = _dot(jnp.concatenate([a_ak, a_rk], axis=0).astype(BF16), vb)
                tx = _dot(t.astype(BF16), jnp.concatenate([at, av[:cc]], axis=1).astype(BF16))
                atp.append(tx[:, :LANES])
                u0.append(tx[:, LANES:])
                y0.append(av[cc:])
                arb.append(a_rb.astype(BF16))
            atp = jnp.where(first, atp[0], atp[1])
            u0 = jnp.where(first, u0[0], u0[1])
            y0 = jnp.where(first, y0[0], y0[1])
            st = s_ref[p]
            qs = _dot_nt(jnp.concatenate([atp, rt], axis=0).astype(BF16), st.astype(BF16))
            u = qs[:cc] + u0
            ub = u.astype(BF16)
            ru = jnp.where(first, _dot(arb[0], ub), _dot(arb[1], ub))
            y_ref[0, sl, ls] = qs[cc:] + y0 + ru
            upd = _dot_tn(jnp.concatenate([vb, ub], axis=0),
                          jnp.concatenate([kh, bh], axis=0).astype(BF16))
            s_ref[p] = st * jnp.exp(tot) + jnp.where(same_head, upd, 0.0)


def _wkv(r, v, kk, kd, bb, lw, d):
    b, s, c = r.shape
    tt = WKV_CHUNK * WKV_STEP_CHUNKS
    nblk = s // tt
    reverse = d == 1
    tix = (lambda i: nblk - 1 - i) if reverse else (lambda i: i)
    one = pl.BlockSpec((1, tt, c), lambda bi, i: (bi, tix(i), 0))
    two = pl.BlockSpec((1, 1, tt, c), lambda bi, i: (d, bi, tix(i), 0))
    return pl.pallas_call(
        functools.partial(_wkv_kernel, reverse=reverse),
        grid=(b, nblk),
        in_specs=[one, one, one, two, two, two],
        out_specs=one,
        out_shape=jax.ShapeDtypeStruct((b, s, c), F32),
        scratch_shapes=[pltpu.VMEM((RW_HEADS // 2, LANES, LANES), F32)],
        compiler_params=_cparams(("parallel", "arbitrary")),
    )(r, v, kk, kd, bb, lw)


WKV2_BATCH = 4


def _wkv2_kernel(rf, vf, kkf, rb, vb, kkb, kdf, bf, lwf, kdb, bb, lwb, yf_ref, yb_ref, s_ref):
    cc = WKV_CHUNK
    nb = rf.shape[0]
    nchunk = rf.shape[1] // cc

    @pl.when(pl.program_id(1) == 0)
    def _():
        s_ref[...] = jnp.zeros_like(s_ref)

    same_head = ((lax.broadcasted_iota(jnp.int32, (LANES, LANES), 0) < HEAD_DIM)
                 == (lax.broadcasted_iota(jnp.int32, (LANES, LANES), 1) < HEAD_DIM))
    rowc = lax.broadcasted_iota(jnp.int32, (cc, LANES), 0)
    colc = lax.broadcasted_iota(jnp.int32, (cc, LANES), 1) % cc
    eye = (colc == rowc).astype(F32)
    strict = (colc < rowc, colc > rowc)
    incl = (colc <= rowc, colc >= rowc)
    first = lax.broadcasted_iota(jnp.int32, (1, LANES), 1) < HEAD_DIM
    second = jnp.logical_not(first)

    def stack_heads(t):
        z = jnp.zeros_like(t)
        return jnp.concatenate([jnp.where(first, t, z), jnp.where(second, t, z)], axis=0)

    def make_chains(c):
        chains = []
        for bi in range(nb):
            for d in range(2):
                j = nchunk - 1 - c if d else c
                sl = slice(j * cc, (j + 1) * cc)
                src = (rb, vb, kkb, kdb, bb, lwb) if d else (rf, vf, kkf, kdf, bf, lwf)
                for p in range(RW_HEADS // 2):
                    ls = slice(p * LANES, (p + 1) * LANES)
                    chains.append(dict(
                        bi=bi, d=d, p=p, sl=sl, ls=ls,
                        r=src[0][bi, sl, ls], v=src[1][bi, sl, ls], kk=src[2][bi, sl, ls],
                        kd=src[3][0, bi, sl, ls], b=src[4][0, bi, sl, ls], lw=src[5][0, bi, sl, ls]))
        return chains

    def solve_chunks(chains):
        for ch in chains:
            cum = ch["lw"]
            sh = 1
            while sh < cc:
                if ch["d"]:
                    cum = cum + jnp.where(rowc < cc - sh, pltpu.roll(cum, cc - sh, 0), 0.0)
                else:
                    cum = cum + jnp.where(rowc >= sh, pltpu.roll(cum, sh, 0), 0.0)
                sh *= 2
            tot = cum[0:1] if ch["d"] else cum[cc - 1:cc]
            at = (-ch["kk"] * jnp.exp(cum - ch["lw"])).astype(BF16)
            einv = jnp.exp(-cum)
            eh = jnp.exp(tot - cum)
            ch["etot"] = jnp.exp(tot)
            ch["rt"] = (ch["r"] * jnp.exp(cum)).astype(BF16)
            ch["vb"] = ch["v"].astype(BF16)
            ch["v_st"] = stack_heads(ch["vb"])
            ch["at_st"] = stack_heads(at)
            ch["lhs"] = jnp.concatenate([at, ch["rt"]], axis=0)
            ch["rhs"] = jnp.concatenate([stack_heads((ch["b"] * einv).astype(BF16)),
                                         stack_heads((ch["kd"] * einv).astype(BF16))], axis=0)
            ch["kb_hat"] = jnp.concatenate([ch["kd"] * eh, ch["b"] * eh], axis=0).astype(BF16)
        for ch in chains:
            m1 = _dot_nt(ch["lhs"], ch["rhs"])
            d = ch["d"]
            ch["n"] = jnp.where(strict[d], m1[:cc, :LANES], 0.0)
            a_ak = jnp.where(strict[d], m1[:cc, LANES:], 0.0)
            ch["a_rb"] = jnp.where(incl[d], m1[cc:, :LANES], 0.0).astype(BF16)
            a_rk = jnp.where(incl[d], m1[cc:, LANES:], 0.0)
            ch["a_k"] = jnp.concatenate([a_ak, a_rk], axis=0).astype(BF16)
        for ch in chains:
            nbf = ch["n"].astype(BF16)
            ch["pw"] = _dot(nbf, stack_heads(nbf))
            ch["t"] = eye + ch["n"]
        for _ in range(cc.bit_length() - 3):
            for ch in chains:
                pwb = ch["pw"].astype(BF16)
                tp = _dot(jnp.concatenate([ch["t"].astype(BF16), pwb], axis=0), stack_heads(pwb))
                ch["t"] = ch["t"] + tp[:cc]
                ch["pw"] = tp[cc:]
        for ch in chains:
            ch["t"] = (ch["t"] + _dot(ch["t"].astype(BF16), stack_heads(ch["pw"].astype(BF16)))).astype(BF16)
        for ch in chains:
            ch["av"] = _dot(ch["a_k"], ch["v_st"])
        for ch in chains:
            akv = stack_heads(ch["av"][:cc].astype(BF16))
            tx = _dot(ch["t"], jnp.concatenate([ch["at_st"], akv], axis=1))
            ch["atp"] = tx[:, :LANES].astype(BF16)
            ch["u0"] = tx[:, LANES:]

    def advance_state(chains):
        for ch in chains:
            ch["st"] = s_ref[ch["bi"], ch["d"], ch["p"]]
            qs = _dot_nt(jnp.concatenate([ch["atp"], ch["rt"]], axis=0), ch["st"].astype(BF16))
            ch["u"] = (qs[:cc] + ch["u0"]).astype(BF16)
            ch["ys"] = qs[cc:] + ch["av"][cc:]
        for ch in chains:
            upd = _dot_tn(jnp.concatenate([ch["vb"], ch["u"]], axis=0), ch["kb_hat"])
            s_ref[ch["bi"], ch["d"], ch["p"]] = ch["st"] * ch["etot"] + jnp.where(same_head, upd, 0.0)
        for ch in chains:
            y_ref = yb_ref if ch["d"] else yf_ref
            y_ref[ch["bi"], ch["sl"], ch["ls"]] = ch["ys"] + _dot(ch["a_rb"], stack_heads(ch["u"]))

    per_chunk = [make_chains(c) for c in range(nchunk)]
    solve_chunks([ch for chains in per_chunk for ch in chains])
    for chains in per_chunk:
        advance_state(chains)


def _wkv2(r, v, kk, kd, bb, lw):
    b, s, c = r.shape
    tt = WKV_CHUNK * WKV_STEP_CHUNKS
    nblk = s // tt
    nb = WKV2_BATCH if b % WKV2_BATCH == 0 else 1
    fwd = pl.BlockSpec((nb, tt, c), lambda bi, i: (bi, i, 0))
    bwd = pl.BlockSpec((nb, tt, c), lambda bi, i: (bi, nblk - 1 - i, 0))
    fwd2 = pl.BlockSpec((1, nb, tt, c), lambda bi, i: (0, bi, i, 0))
    bwd2 = pl.BlockSpec((1, nb, tt, c), lambda bi, i: (1, bi, nblk - 1 - i, 0))
    return pl.pallas_call(
        _wkv2_kernel,
        grid=(b // nb, nblk),
        in_specs=[fwd, fwd, fwd, bwd, bwd, bwd, fwd2, fwd2, fwd2, bwd2, bwd2, bwd2],
        out_specs=[fwd, bwd],
        out_shape=[jax.ShapeDtypeStruct((b, s, c), F32)] * 2,
        scratch_shapes=[pltpu.VMEM((nb, 2, RW_HEADS // 2, LANES, LANES), F32)],
        compiler_params=_cparams(("parallel", "arbitrary")),
    )(r, v, kk, r, v, kk, kd, bb, lw, kd, bb, lw)


def _memkv_kernel(mem_ref, nw_ref, w_ref, kw_ref, k_ref, v_ref):
    x = mem_ref[0]
    ms = jnp.mean(x * x, axis=-1, keepdims=True)
    h = (x * lax.rsqrt(ms + RMS_EPS) * nw_ref[...]).astype(BF16)
    kv = _dot(h, w_ref[...])
    k = kv[:, :MEM_WIDTH]
    ms = _split_dot(k * k, _head_ones(MEM_WIDTH)) * (1.0 / HEAD_DIM)
    k_ref[0] = (k * lax.rsqrt(ms + RMS_EPS) * kw_ref[...]).astype(BF16)
    v_ref[0] = kv[:, MEM_WIDTH:].astype(BF16)


def _memkv(mem, nw, w_kv, kw):
    b, m, d = mem.shape
    full = lambda a: pl.BlockSpec(a.shape, lambda bi: (0,) * a.ndim)
    out = pl.BlockSpec((1, m, MEM_WIDTH), lambda bi: (bi, 0, 0))
    return pl.pallas_call(
        _memkv_kernel,
        grid=(b,),
        in_specs=[pl.BlockSpec((1, m, d), lambda bi: (bi, 0, 0)), full(nw), full(w_kv), full(kw)],
        out_specs=[out, out],
        out_shape=[jax.ShapeDtypeStruct((b, m, MEM_WIDTH), BF16)] * 2,
        compiler_params=_cparams(("parallel",)),
    )(mem, nw, w_kv, kw)


def _mixout_kernel(x_ref, na_ref, yf_ref, yb_ref, bonus_ref, g_ref, mq_ref, mk_ref, mv_ref,
                   lnw_ref, lnb_ref, wo_ref, o_ref):
    ones = _head_ones(RW_WIDTH)
    y = yf_ref[0] + yb_ref[0]
    mu = _split_dot(y, ones) * (1.0 / HEAD_DIM)
    yc = y - mu
    var = _split_dot(yc * yc, ones) * (1.0 / HEAD_DIM)
    yn = yc * lax.rsqrt(var + RW_GN_EPS) * lnw_ref[...] + lnb_ref[...]
    y_rw = ((yn + bonus_ref[0]) * g_ref[0]).astype(BF16)

    first = lax.broadcasted_iota(jnp.int32, (1, LANES), 1) < HEAD_DIM
    mems = []
    for p in range(MEM_HEADS // 2):
        ls = slice(p * LANES, (p + 1) * LANES)
        qp = mq_ref[0, :, ls]
        kp = mk_ref[0, :, ls]
        vp = mv_ref[0, :, ls]
        outs = []
        for hh in range(2):
            sel = first if hh == 0 else jnp.logical_not(first)
            s = _dot_nt(jnp.where(sel, qp, jnp.zeros_like(qp)), kp)
            m = jnp.max(s, axis=-1, keepdims=True)
            e = jnp.exp(s - m)
            l = jnp.sum(e, axis=-1, keepdims=True)
            outs.append(_dot(e.astype(BF16), vp) / l)
        mems.append(jnp.where(first, outs[0], outs[1]).astype(BF16))
    y_mem = jnp.concatenate(mems, axis=1)

    acc = _dot(na_ref[0], wo_ref[0:NA_WIDTH, :])
    acc = acc + _dot(y_rw, wo_ref[NA_WIDTH:NA_WIDTH + RW_WIDTH, :])
    acc = acc + _dot(y_mem, wo_ref[NA_WIDTH + RW_WIDTH:, :])
    o_ref[0] = x_ref[0] + acc


def _mixout(x, y_na, y_f, y_b, bonus, g, mq, mk, mv, ln_w, ln_b, w_out, tm):
    b, s, d = x.shape
    m = mk.shape[1]
    tok = lambda w: pl.BlockSpec((1, tm, w), lambda bi, i: (bi, i, 0))
    full = lambda a: pl.BlockSpec(a.shape, lambda bi, i: (0,) * a.ndim)
    memb = pl.BlockSpec((1, m, MEM_WIDTH), lambda bi, i: (bi, 0, 0))
    return pl.pallas_call(
        _mixout_kernel,
        grid=(b, s // tm),
        in_specs=[tok(d), tok(NA_WIDTH), tok(RW_WIDTH), tok(RW_WIDTH), tok(RW_WIDTH), tok(RW_WIDTH),
                  tok(MEM_WIDTH), memb, memb, full(ln_w), full(ln_b), full(w_out)],
        out_specs=tok(d),
        out_shape=jax.ShapeDtypeStruct((b, s, d), F32),
        compiler_params=_cparams(("parallel", "parallel")),
    )(x, y_na, y_f, y_b, bonus, g, mq, mk, mv, ln_w, ln_b, w_out)


ROUTER_LANES = 128


MOE_SUB = 256
MOE_CHUNK = 16
MOE_LOCAL_CHUNKS = 48
MOE_BLOCK_CHUNKS = 32
MOE_TOP_K = 2
ROUTE_ROWS = 8
assert MOE_LOCAL_CHUNKS >= MOE_TOP_K * MOE_SUB // MOE_CHUNK + MOE_EXPERTS - 1


def _route_subtile(lg):
    t = lg.shape[0]
    lane = lax.broadcasted_iota(jnp.int32, (1, ROUTER_LANES), 1)
    lanef = lane.astype(F32)
    big = float(ROUTER_LANES)

    def argmax_lane(v):
        m = jnp.max(v, axis=-1, keepdims=True)
        return m, jnp.min(jnp.where(v == m, lanef, big), axis=-1, keepdims=True)

    is_group = lane < MOE_GROUPS
    gmax, gidx = argmax_lane(jnp.where(is_group, lg, NEG_BIG))
    gsum = jnp.sum(jnp.where(is_group, jnp.exp(lg - gmax), 0.0), axis=-1, keepdims=True)
    lo = MOE_GROUPS + MOE_EXPERTS_PER_GROUP * gidx
    el = jnp.where((lanef >= lo) & (lanef < lo + MOE_EXPERTS_PER_GROUP), lg, NEG_BIG)
    m1, i1 = argmax_lane(el)
    m2, i2 = argmax_lane(jnp.where(lanef == i1, NEG_BIG, el))
    ratio = jnp.exp(m2 - m1)
    w0 = 1.0 / (gsum * (1.0 + ratio))
    w1 = w0 * ratio

    oh0 = (lanef == i1 - MOE_GROUPS).astype(F32)
    oh1 = (lanef == i2 - MOE_GROUPS).astype(F32)
    row = lax.broadcasted_iota(jnp.int32, (t, t), 0)
    col = lax.broadcasted_iota(jnp.int32, (t, t), 1)
    before = (col < row).astype(BF16)
    cnt0 = jnp.sum(oh0, axis=0, keepdims=True)
    cnt1 = jnp.sum(oh1, axis=0, keepdims=True)
    rank0 = _dot(before, oh0.astype(BF16))
    rank1 = _dot(before, oh1.astype(BF16)) + cnt0
    nch = jnp.floor((cnt0 + cnt1 + (MOE_CHUNK - 1.0)) * (1.0 / MOE_CHUNK))
    li = lax.broadcasted_iota(jnp.int32, (ROUTER_LANES, ROUTER_LANES), 0)
    lj = lax.broadcasted_iota(jnp.int32, (ROUTER_LANES, ROUTER_LANES), 1)
    start = _dot(jnp.broadcast_to(nch, (8, ROUTER_LANES)).astype(BF16), (li < lj).astype(BF16))[0:1]
    pos0 = jnp.sum((start * MOE_CHUNK + rank0) * oh0, axis=-1, keepdims=True)
    pos1 = jnp.sum((start * MOE_CHUNK + rank1) * oh1, axis=-1, keepdims=True)
    rec = (jnp.where(lane == 0, pos0, 0.0) + jnp.where(lane == 1, pos1, 0.0)
           + jnp.where(lane == 2, w0, 0.0) + jnp.where(lane == 3, w1, 0.0))
    return jnp.transpose(rec)[0:ROUTE_ROWS], nch


def _router_kernel(x_ref, nw_ref, whi_ref, wlo_ref, b_ref, h_ref, route_ref, nch_ref):
    x = x_ref[...]
    ms = jnp.mean(x * x, axis=-1, keepdims=True)
    h = x * lax.rsqrt(ms + RMS_EPS) * nw_ref[...]
    hi = h.astype(BF16)
    lo = (h - hi.astype(F32)).astype(BF16)
    h_ref[...] = hi
    lg = _dot(hi, whi_ref[...]) + _dot(hi, wlo_ref[...]) + _dot(lo, whi_ref[...]) + b_ref[...]
    for j in range(x.shape[0] // MOE_SUB):
        route_ref[j], nch_ref[j] = _route_subtile(lg[j * MOE_SUB:(j + 1) * MOE_SUB])


def _router(x2, nw, w_group, b_group, w_expert, b_expert, tm):
    n, d = x2.shape
    pad = ROUTER_LANES - MOE_GROUPS - MOE_EXPERTS
    w = jnp.concatenate([w_group, w_expert, jnp.zeros((d, pad), F32)], axis=1)
    whi = w.astype(BF16)
    wlo = (w - whi.astype(F32)).astype(BF16)
    bias = jnp.concatenate([b_group, b_expert, jnp.zeros((pad,), F32)]).reshape(1, -1)
    full = lambda a: pl.BlockSpec(a.shape, lambda i: (0,) * a.ndim)
    sub = tm // MOE_SUB
    return pl.pallas_call(
        _router_kernel,
        grid=(n // tm,),
        in_specs=[pl.BlockSpec((tm, d), lambda i: (i, 0)), full(nw), full(whi), full(wlo), full(bias)],
        out_specs=[pl.BlockSpec((tm, d), lambda i: (i, 0)),
                   pl.BlockSpec((sub, ROUTE_ROWS, MOE_SUB), lambda i: (i, 0, 0)),
                   pl.BlockSpec((sub, 1, ROUTER_LANES), lambda i: (i, 0, 0))],
        out_shape=[jax.ShapeDtypeStruct((n, d), BF16),
                   jax.ShapeDtypeStruct((n // MOE_SUB, ROUTE_ROWS, MOE_SUB), F32),
                   jax.ShapeDtypeStruct((n // MOE_SUB, 1, ROUTER_LANES), F32)],
        compiler_params=_cparams(("parallel",)),
    )(x2, nw, whi, wlo, bias)


def _moe_plan(nch, n):
    nch = nch[:, 0, :MOE_EXPERTS].astype(jnp.int32)
    lc_end = jnp.cumsum(nch, axis=1)
    lc_start = lc_end - nch
    nloc = lc_end[:, -1]
    nblk = (jnp.sum(nch, axis=0) + MOE_BLOCK_CHUNKS - 1) // MOE_BLOCK_CHUNKS
    bend = jnp.cumsum(nblk)
    gch = ((bend - nblk) * MOE_BLOCK_CHUNKS)[None, :] + jnp.cumsum(nch, axis=0) - nch
    c = jnp.arange(MOE_LOCAL_CHUNKS, dtype=jnp.int32)
    in_run = (c[None, :, None] >= lc_start[:, None, :]) & (c[None, :, None] < lc_end[:, None, :])
    dest = jnp.sum(jnp.where(in_run, (gch - lc_start)[:, None, :] + c[None, :, None], 0), axis=-1)

    nblocks = _moe_blocks(n)
    b = jnp.arange(nblocks, dtype=jnp.int32)
    block_e = jnp.minimum(jnp.sum(b[:, None] >= bend[None, :], axis=-1), MOE_EXPERTS - 1)
    tot = jnp.sum(nch, axis=0)
    tail_start = (bend - nblk) * MOE_BLOCK_CHUNKS + tot
    ntail = nblk * MOE_BLOCK_CHUNKS - tot
    i32 = lambda a: a.astype(jnp.int32)
    return i32(dest), i32(nloc), i32(block_e), i32(bend[-1:]), i32(tail_start), i32(ntail)


def _moe_blocks(n):
    ns = n // MOE_SUB
    chunks = ns * (MOE_TOP_K * MOE_SUB // MOE_CHUNK + MOE_EXPERTS - 1) + MOE_EXPERTS * (MOE_BLOCK_CHUNKS - 1)
    return -(-chunks // MOE_BLOCK_CHUNKS)


def _local_onehot(pos_rows, shape, row_axis):
    idx = lax.broadcasted_iota(jnp.int32, shape, row_axis)
    return (idx == pos_rows[0]) | (idx == pos_rows[1])


def _dispatch_kernel(dest_ref, nloc_ref, tail_ref, ntail_ref, nv_ref, h_ref, pos_ref, xs_ref,
                     buf, zbuf, sem, zsem):
    s = pl.program_id(0)
    ns = pl.num_programs(0)
    slot = s % 2
    bm = MOE_BLOCK_CHUNKS * MOE_CHUNK
    nblocks = xs_ref.shape[0] // bm

    def zero_fill(wait):
        def tail_copy(e, i):
            dst = xs_ref.at[pl.ds(pl.multiple_of((tail_ref[e] + i) * MOE_CHUNK, MOE_CHUNK), MOE_CHUNK)]
            return pltpu.make_async_copy(zbuf.at[pl.ds(0, MOE_CHUNK)], dst, zsem.at[0])

        def block_copy(b):
            return pltpu.make_async_copy(zbuf, xs_ref.at[pl.ds(pl.multiple_of(b * bm, bm), bm)], zsem.at[1])

        def run(cp):
            return cp.wait() if wait else cp.start()

        for e in range(MOE_EXPERTS):
            def tail_body(i, carry, e=e):
                run(tail_copy(e, i))
                return carry
            lax.fori_loop(0, ntail_ref[e], tail_body, 0)

        def block_body(b, carry):
            run(block_copy(b))
            return carry
        lax.fori_loop(nv_ref[0], nblocks, block_body, 0)

    @pl.when(s == 0)
    def _():
        zbuf[...] = jnp.zeros_like(zbuf)
        zero_fill(wait=False)

    def chunk_copy(sl, c, step):
        src = buf.at[sl, pl.ds(pl.multiple_of(c * MOE_CHUNK, MOE_CHUNK), MOE_CHUNK)]
        dst = xs_ref.at[pl.ds(pl.multiple_of(dest_ref[step, c] * MOE_CHUNK, MOE_CHUNK), MOE_CHUNK)]
        return pltpu.make_async_copy(src, dst, sem.at[sl])

    def wait_step(step, sl):
        def body(c, carry):
            chunk_copy(sl, c, step).wait()
            return carry
        lax.fori_loop(0, nloc_ref[step], body, 0)

    @pl.when(s >= 2)
    def _():
        wait_step(s - 2, slot)

    pos = pos_ref[0, 0:MOE_TOP_K].astype(jnp.int32)
    onehot = _local_onehot((pos[0:1], pos[1:2]), (MOE_LOCAL_CHUNKS * MOE_CHUNK, MOE_SUB), 0)
    buf[slot] = _dot(jnp.where(onehot, 1.0, 0.0).astype(BF16), h_ref[...]).astype(BF16)

    def issue(c, carry):
        chunk_copy(slot, c, s).start()
        return carry
    lax.fori_loop(0, nloc_ref[s], issue, 0)

    @pl.when(s == ns - 1)
    def _():
        @pl.when(s >= 1)
        def _():
            wait_step(s - 1, 1 - slot)
        wait_step(s, slot)
        zero_fill(wait=True)


def _dispatch(h2, pos, dest, nloc, tail_start, ntail, nvalid):
    n, d = h2.shape
    ns = n // MOE_SUB
    bm = MOE_BLOCK_CHUNKS * MOE_CHUNK
    rows = MOE_LOCAL_CHUNKS * MOE_CHUNK
    return pl.pallas_call(
        _dispatch_kernel,
        grid_spec=pltpu.PrefetchScalarGridSpec(
            num_scalar_prefetch=5, grid=(ns,),
            in_specs=[pl.BlockSpec((MOE_SUB, d), lambda s, *_: (s, 0)),
                      pl.BlockSpec((1, ROUTE_ROWS, MOE_SUB), lambda s, *_: (s, 0, 0))],
            out_specs=pl.BlockSpec(memory_space=pl.ANY),
            scratch_shapes=[pltpu.VMEM((2, rows, d), BF16), pltpu.VMEM((bm, d), BF16),
                            pltpu.SemaphoreType.DMA((2,)), pltpu.SemaphoreType.DMA((2,))]),
        out_shape=jax.ShapeDtypeStruct((_moe_blocks(n) * bm, d), BF16),
        compiler_params=_cparams(("arbitrary",)),
    )(dest, nloc, tail_start, ntail, nvalid, h2, pos)


def _experts_kernel(be_ref, nv_ref, x_ref, wg_ref, wu_ref, wd_ref, o_ref):
    del be_ref
    valid = pl.program_id(0) < nv_ref[0]

    @pl.when(valid)
    def _():
        x = x_ref[...]
        hid = jax.nn.silu(_dot(x, wg_ref[0])) * _dot(x, wu_ref[0])
        o_ref[...] = _dot(hid.astype(BF16), wd_ref[0])

    @pl.when(jnp.logical_not(valid))
    def _():
        o_ref[...] = jnp.zeros_like(o_ref)


def _experts(xs, block_e, nvalid, w_gate, w_up, w_down):
    cap, d = xs.shape
    bm = MOE_BLOCK_CHUNKS * MOE_CHUNK
    blk = lambda b, be, nv: (jnp.minimum(b, nv[0] - 1), 0)
    wsel = lambda b, be, nv: (be[jnp.minimum(b, nv[0] - 1)], 0, 0)
    return pl.pallas_call(
        _experts_kernel,
        grid_spec=pltpu.PrefetchScalarGridSpec(
            num_scalar_prefetch=2, grid=(cap // bm,),
            in_specs=[pl.BlockSpec((bm, d), blk),
                      pl.BlockSpec((1, d, MOE_D_FF), wsel),
                      pl.BlockSpec((1, d, MOE_D_FF), wsel),
                      pl.BlockSpec((1, MOE_D_FF, d), wsel)],
            out_specs=pl.BlockSpec((bm, d), lambda b, be, nv: (b, 0))),
        out_shape=jax.ShapeDtypeStruct((cap, d), F32),
        compiler_params=_cparams(("arbitrary",)),
    )(block_e, nvalid, xs, w_gate, w_up, w_down)


def _combine_kernel(dest_ref, x_ref, route_ref, ys_ref, o_ref, buf, sem):
    s = pl.program_id(0)
    ns = pl.num_programs(0)
    slot = s % 2

    def chunk_copy(sl, c, step):
        src = ys_ref.at[pl.ds(pl.multiple_of(dest_ref[step, c] * MOE_CHUNK, MOE_CHUNK), MOE_CHUNK)]
        dst = buf.at[sl, pl.ds(pl.multiple_of(c * MOE_CHUNK, MOE_CHUNK), MOE_CHUNK)]
        return pltpu.make_async_copy(src, dst, sem.at[sl])

    def fetch(step, sl):
        def body(c, carry):
            chunk_copy(sl, c, step).start()
            return carry
        lax.fori_loop(0, MOE_LOCAL_CHUNKS, body, 0)

    @pl.when(s == 0)
    def _():
        fetch(0, 0)

    @pl.when(s + 1 < ns)
    def _():
        fetch(s + 1, 1 - slot)

    def wait(c, carry):
        chunk_copy(slot, c, s).wait()
        return carry
    lax.fori_loop(0, MOE_LOCAL_CHUNKS, wait, 0)

    rec = jnp.transpose(route_ref[0])
    pos = rec[:, 0:MOE_TOP_K].astype(jnp.int32)
    idx = lax.broadcasted_iota(jnp.int32, (MOE_SUB, MOE_LOCAL_CHUNKS * MOE_CHUNK), 1)
    pw = jnp.where(idx == pos[:, 0:1], rec[:, 2:3], 0.0) + jnp.where(idx == pos[:, 1:2], rec[:, 3:4], 0.0)
    ys = buf[slot]
    p_hi = pw.astype(BF16)
    p_lo = (pw - p_hi.astype(F32)).astype(BF16)
    y_hi = ys.astype(BF16)
    y_lo = (ys - y_hi.astype(F32)).astype(BF16)
    o_ref[...] = x_ref[...] + _dot(p_hi, y_hi) + _dot(p_hi, y_lo) + _dot(p_lo, y_hi)


def _combine(x2, ys, route, dest):
    n, d = x2.shape
    ns = n // MOE_SUB
    rows = MOE_LOCAL_CHUNKS * MOE_CHUNK
    return pl.pallas_call(
        _combine_kernel,
        grid_spec=pltpu.PrefetchScalarGridSpec(
            num_scalar_prefetch=1, grid=(ns,),
            in_specs=[pl.BlockSpec((MOE_SUB, d), lambda s, *_: (s, 0)),
                      pl.BlockSpec((1, ROUTE_ROWS, MOE_SUB), lambda s, *_: (s, 0, 0)),
                      pl.BlockSpec(memory_space=pl.ANY)],
            out_specs=pl.BlockSpec((MOE_SUB, d), lambda s, *_: (s, 0)),
            scratch_shapes=[pltpu.VMEM((2, rows, d), F32), pltpu.SemaphoreType.DMA((2,))]),
        out_shape=jax.ShapeDtypeStruct((n, d), F32),
        compiler_params=_cparams(("arbitrary",)),
    )(dest, x2, route, ys)


def _moe(x2, h2, route, nch, w_gate, w_up, w_down):
    dest, nloc, block_e, nvalid, tail_start, ntail = _moe_plan(nch, x2.shape[0])
    xs = _dispatch(h2, route, dest, nloc, tail_start, ntail, nvalid)
    ys = _experts(xs, block_e, nvalid, w_gate, w_up, w_down)
    return _combine(x2, ys, route, dest)


def _tile(n, want):
    t = min(n, want)
    assert n % t == 0
    return t


def kernel(x, mem, attn_norm_w, w_in, na_q_norm_w, na_k_norm_w, na_rpb, rw_mu_prev, rw_mu_next, rw_w0, rw_w2, rw_a0, rw_a2, rw_g2, rw_k_k, rw_k_a, rw_r_k, rw_ln_w, rw_ln_b, mem_norm_w, w_mem_kv, mem_q_norm_w, mem_k_norm_w, w_out, ffn_norm_w, moe_w_group, moe_b_group, moe_w_expert, moe_b_expert, moe_w_gate, moe_w_up, moe_w_down):
    b, s, d = x.shape
    n = b * s
    depth = w_in.shape[0]
    assert s % (NA_BLOCK_ROWS * GRID_W) == 0 and s // GRID_W >= 2 * NA_KH
    tm = _tile(n, 512)
    ts = _tile(s, 512)
    for l in range(depth):
        q, k, v, rw, mq = _proj(
            x.reshape(n, d), _row(attn_norm_w[l]), w_in[l].astype(BF16),
            _row(jnp.tile(na_q_norm_w[l], NA_HEADS)), _row(jnp.tile(na_k_norm_w[l], NA_HEADS)),
            _row(jnp.tile(mem_q_norm_w[l], MEM_HEADS)), tm)
        y_na = _na(q.reshape(b, s, NA_WIDTH), k.reshape(b, s, NA_WIDTH), v.reshape(b, s, NA_WIDTH),
                   _na_bias_table(na_rpb[l]))
        r, vv, kk, kd, bb, lw, g, bonus = _rwprep(
            rw.reshape(b, s, RW_PROJ), rw_mu_prev[l], rw_mu_next[l], rw_w0[l], rw_w2[l], rw_a0[l],
            rw_a2[l], rw_g2[l], rw_k_k[l], rw_k_a[l], rw_r_k[l], ts)
        y_f, y_b = _wkv2(r, vv, kk, kd, bb, lw)
        mk, mv = _memkv(mem, _row(mem_norm_w[l]), w_mem_kv[l].astype(BF16),
                        _row(jnp.tile(mem_k_norm_w[l], MEM_HEADS)))
        x = _mixout(x, y_na, y_f, y_b, bonus, g, mq.reshape(b, s, MEM_WIDTH), mk, mv,
                    _row(rw_ln_w[l]), _row(rw_ln_b[l]), w_out[l].astype(BF16), ts)
        h2, route, nch = _router(x.reshape(n, d), _row(ffn_norm_w[l]), moe_w_group[l], moe_b_group[l],
                                 moe_w_expert[l], moe_b_expert[l], tm)
        x = _moe(x.reshape(n, d), h2, route, nch, moe_w_gate[l].astype(BF16),
                 moe_w_up[l].astype(BF16), moe_w_down[l].astype(BF16)).reshape(b, s, d)
    return x
```

```python
import functools

import jax
import jax.numpy as jnp
import numpy as np
from jax import lax
from jax.experimental import pallas as pl
from jax.experimental.pallas import tpu as pltpu

F32 = jnp.float32
BF16 = jnp.bfloat16

GRID_W = 64
HEAD_DIM = 64
NA_HEADS = 8
NA_WIDTH = NA_HEADS * HEAD_DIM
NA_KH = 8
NA_KW = 16
RW_HEADS = 4
RW_WIDTH = RW_HEADS * HEAD_DIM
RW_LORA_W = 64
RW_LORA_A = 64
RW_LORA_G = 128
RW_PROJ = 3 * RW_WIDTH + 2 * RW_LORA_W + 2 * RW_LORA_A + RW_LORA_G
MEM_HEADS = 4
MEM_WIDTH = MEM_HEADS * HEAD_DIM
MOE_GROUPS = 4
MOE_EXPERTS_PER_GROUP = 4
MOE_EXPERTS = MOE_GROUPS * MOE_EXPERTS_PER_GROUP
MOE_D_FF = 512
RMS_EPS = 1e-6
RW_GN_EPS = 64e-5

LANES = 128
WKV_CHUNK = 64
NEG_BIG = -1e30
VMEM_LIMIT = 56 * 1024 * 1024


def _cparams(sem):
    return pltpu.CompilerParams(dimension_semantics=sem, vmem_limit_bytes=VMEM_LIMIT)


def _dot(a, b):
    return jnp.dot(a, b, preferred_element_type=F32)


def _dot_nt(a, b):
    return lax.dot_general(a, b, (((1,), (1,)), ((), ())), preferred_element_type=F32)


def _dot_tn(a, b):
    return lax.dot_general(a, b, (((0,), (0,)), ((), ())), preferred_element_type=F32)


def _split_dot(t, m):
    hi = t.astype(BF16)
    lo = (t - hi.astype(F32)).astype(BF16)
    return _dot(hi, m) + _dot(lo, m)


def _head_ones(width):
    i = lax.broadcasted_iota(jnp.int32, (width, width), 0) // HEAD_DIM
    j = lax.broadcasted_iota(jnp.int32, (width, width), 1) // HEAD_DIM
    return (i == j).astype(BF16)


def _row(v):
    return v.reshape(1, -1).astype(F32)


def _proj_kernel(x_ref, nw_ref, w_ref, qw_ref, kw_ref, mw_ref, q_ref, k_ref, v_ref, rw_ref, mq_ref):
    x = x_ref[...]
    ms = jnp.mean(x * x, axis=-1, keepdims=True)
    h = (x * lax.rsqrt(ms + RMS_EPS) * nw_ref[...]).astype(BF16)
    ones_na = _head_ones(NA_WIDTH)

    def head_norm(t, w, ones):
        ms = _dot((t * t).astype(BF16), ones) * (1.0 / HEAD_DIM)
        return t * lax.rsqrt(ms + RMS_EPS) * w

    o = 0
    q = _dot(h, w_ref[:, o:o + NA_WIDTH])
    q_ref[...] = (head_norm(q, qw_ref[...], ones_na) * (HEAD_DIM ** -0.5)).astype(BF16)
    o += NA_WIDTH
    k = _dot(h, w_ref[:, o:o + NA_WIDTH])
    k_ref[...] = head_norm(k, kw_ref[...], ones_na).astype(BF16)
    o += NA_WIDTH
    v_ref[...] = _dot(h, w_ref[:, o:o + NA_WIDTH]).astype(BF16)
    o += NA_WIDTH
    rw_ref[...] = _dot(h, w_ref[:, o:o + RW_PROJ])
    o += RW_PROJ
    mq = _dot(h, w_ref[:, o:o + MEM_WIDTH])
    mq_ref[...] = (head_norm(mq, mw_ref[...], _head_ones(MEM_WIDTH)) * (HEAD_DIM ** -0.5)).astype(BF16)


def _proj(x2, nw, w_in, qw, kw, mw, tm):
    n, d = x2.shape
    p_in = w_in.shape[1]
    tok = lambda w: pl.BlockSpec((tm, w), lambda i: (i, 0))
    full = lambda a: pl.BlockSpec(a.shape, lambda i: (0,) * a.ndim)
    return pl.pallas_call(
        _proj_kernel,
        grid=(n // tm,),
        in_specs=[tok(d), full(nw), full(w_in), full(qw), full(kw), full(mw)],
        out_specs=[tok(NA_WIDTH), tok(NA_WIDTH), tok(NA_WIDTH), tok(RW_PROJ), tok(MEM_WIDTH)],
        out_shape=[jax.ShapeDtypeStruct((n, NA_WIDTH), BF16)] * 3
        + [jax.ShapeDtypeStruct((n, RW_PROJ), F32), jax.ShapeDtypeStruct((n, MEM_WIDTH), BF16)],
        compiler_params=_cparams(("parallel",)),
    )(x2, nw, w_in, qw, kw, mw)


NA_BLOCK_ROWS = 8


def _na_bias_table(rpb):
    col = np.arange(GRID_W)
    cs = np.clip(col - NA_KW // 2, 0, GRID_W - NA_KW)
    dc = col[None, :] - col[:, None]
    inside = (col[None, :] >= cs[:, None]) & (col[None, :] < cs[:, None] + NA_KW)
    dci = np.clip(dc + NA_KW - 1, 0, 2 * NA_KW - 2)
    off = np.arange(NA_KH) - (NA_KH - 1)
    dri = off[:, None] + np.arange(NA_KH)[None, :] + NA_KH - 1
    row_sel = (dri[:, :, None] == np.arange(2 * NA_KH - 1)).astype(np.float32)
    col_sel = (dci[:, :, None] == np.arange(2 * NA_KW - 1)).astype(np.float32)
    t = jnp.einsum("hrq,onr,cdq->ohcnd", rpb.astype(F32), row_sel, col_sel,
                   precision=lax.Precision.HIGHEST)
    t = jnp.where(inside[None, None, :, None, :], t, NEG_BIG)
    return t.reshape(NA_KH, NA_HEADS // 2, 2 * GRID_W, NA_KH * GRID_W).astype(F32)


def _na_kernel(q_ref, kp_ref, kc_ref, kn_ref, vp_ref, vc_ref, vn_ref, bias_ref, o_ref, kwin, vwin, *, rows):
    j = pl.program_id(1)
    blk = NA_BLOCK_ROWS * GRID_W
    kwin[0:blk] = kp_ref[0]
    kwin[blk:2 * blk] = kc_ref[0]
    kwin[2 * blk:3 * blk] = kn_ref[0]
    vwin[0:blk] = vp_ref[0]
    vwin[blk:2 * blk] = vc_ref[0]
    vwin[2 * blk:3 * blk] = vn_ref[0]
    lane = lax.broadcasted_iota(jnp.int32, (1, LANES), 1)
    first = lane < HEAD_DIM

    def row_body(i, carry):
        r = j * NA_BLOCK_ROWS + i
        rs = jnp.clip(r - NA_KH // 2, 0, rows - NA_KH)
        oi = rs - r + (NA_KH - 1)
        start = pl.multiple_of((rs - (j - 1) * NA_BLOCK_ROWS) * GRID_W, GRID_W)
        qs = pl.multiple_of(i * GRID_W, GRID_W)
        pairs = [slice(p * LANES, (p + 1) * LANES) for p in range(NA_HEADS // 2)]
        scores = []
        for p, ls in enumerate(pairs):
            qp = q_ref[0, pl.ds(qs, GRID_W), ls]
            zero = jnp.zeros_like(qp)
            qst = jnp.concatenate([jnp.where(first, qp, zero), jnp.where(first, zero, qp)], axis=0)
            scores.append(_dot_nt(qst, kwin[pl.ds(start, NA_KH * GRID_W), ls]) + bias_ref[oi, p])
        probs, norms = [], []
        for s in scores:
            e = jnp.exp(s - jnp.max(s, axis=-1, keepdims=True))
            norms.append(jnp.sum(e, axis=-1, keepdims=True))
            probs.append(e.astype(BF16))
        for ls, e, l in zip(pairs, probs, norms):
            o = _dot(e, vwin[pl.ds(start, NA_KH * GRID_W), ls]) / l
            o_ref[0, pl.ds(qs, GRID_W), ls] = jnp.where(first, o[:GRID_W], o[GRID_W:]).astype(o_ref.dtype)
        return carry

    lax.fori_loop(0, NA_BLOCK_ROWS, row_body, 0)


def _na(q, k, v, bias):
    b, s, w = q.shape
    rows = s // GRID_W
    nblk = rows // NA_BLOCK_ROWS
    blk = NA_BLOCK_ROWS * GRID_W
    cur = pl.BlockSpec((1, blk, w), lambda bi, j: (bi, j, 0))
    prv = pl.BlockSpec((1, blk, w), lambda bi, j: (bi, jnp.maximum(j - 1, 0), 0))
    nxt = pl.BlockSpec((1, blk, w), lambda bi, j: (bi, jnp.minimum(j + 1, nblk - 1), 0))
    return pl.pallas_call(
        functools.partial(_na_kernel, rows=rows),
        grid=(b, nblk),
        in_specs=[cur, prv, cur, nxt, prv, cur, nxt,
                  pl.BlockSpec(bias.shape, lambda bi, j: (0, 0, 0, 0))],
        out_specs=cur,
        out_shape=jax.ShapeDtypeStruct((b, s, w), BF16),
        scratch_shapes=[pltpu.VMEM((3 * blk, w), BF16), pltpu.VMEM((3 * blk, w), BF16)],
        compiler_params=_cparams(("parallel", "parallel")),
    )(q, k, k, k, v, v, v, bias)


def _rwprep_kernel(rw_ref, prev_ref, next_ref, mup_ref, mun_ref, w0_ref, w2_ref, a0_ref, a2_ref, g2_ref,
                   kk_w_ref, ka_ref, rk_ref,
                   r_ref, v_ref, kk_ref, kd_ref, b_ref, lw_ref, g_ref, bonus_ref, *, nblk):
    i = pl.program_id(1)
    s = rw_ref[0]
    ts = s.shape[0]
    rowi = lax.broadcasted_iota(jnp.int32, (ts, 1), 0)
    halo_p = jnp.where(i > 0, prev_ref[0, 7:8, :], 0.0)
    halo_n = jnp.where(i < nblk - 1, next_ref[0, 0:1, :], 0.0)
    prev = jnp.where(rowi == 0, halo_p, pltpu.roll(s, 1, 0))
    nxt = jnp.where(rowi == ts - 1, halo_n, pltpu.roll(s, ts - 1, 0))
    s = s + mup_ref[...] * (prev - s) + mun_ref[...] * (nxt - s)

    c = RW_WIDTH
    r = s[:, 0:c]
    k = s[:, c:2 * c]
    v = s[:, 2 * c:3 * c]
    lw = s[:, 3 * c:3 * c + 2 * RW_LORA_W]
    la = s[:, 3 * c + 2 * RW_LORA_W:3 * c + 2 * RW_LORA_W + 2 * RW_LORA_A]
    lg = s[:, 3 * c + 2 * RW_LORA_W + 2 * RW_LORA_A:]
    ones = _head_ones(c)

    g_ref[0] = _dot(jax.nn.sigmoid(lg).astype(BF16), g2_ref[...])
    kk = k * kk_w_ref[...]
    nrm = jnp.sqrt(_split_dot(kk * kk, ones))
    kk = kk / jnp.maximum(nrm, 1e-12)
    wl_pre = w0_ref[...] + _dot(jnp.tanh(lw).astype(BF16), w2_ref[...])
    a_all = jax.nn.sigmoid(a0_ref[...] + _dot(la.astype(BF16), a2_ref[...]))
    wl = -jax.nn.softplus(-wl_pre) - 0.5
    logdecay = -jnp.exp(wl)
    r_ref[0] = r
    v_ref[0] = v
    kk_ref[0] = kk
    kd_sum = jnp.zeros_like(k)
    for d in range(2):
        a = a_all[:, d * c:(d + 1) * c]
        kd = k * (1.0 + (a - 1.0) * ka_ref[...])
        kd_ref[d, 0] = kd
        b_ref[d, 0] = kk * a
        lw_ref[d, 0] = logdecay[:, d * c:(d + 1) * c]
        kd_sum = kd_sum + kd
    bonus_ref[0] = _split_dot(r * kd_sum * rk_ref[...], ones) * v


def _blockdiag2(m):
    z = jnp.zeros_like(m[0])
    return jnp.concatenate([jnp.concatenate([m[0], z], 1), jnp.concatenate([z, m[1]], 1)], 0)


def _rwprep(rw, mu_prev, mu_next, w0, w2, a0, a2, g2, k_k, k_a, r_k, ts):
    b, s, pw = rw.shape
    nblk = s // ts
    c = RW_WIDTH
    cur = pl.BlockSpec((1, ts, pw), lambda bi, i: (bi, i, 0))
    prv = pl.BlockSpec((1, 8, pw), lambda bi, i: (bi, jnp.maximum(i * (ts // 8) - 1, 0), 0))
    nxt = pl.BlockSpec((1, 8, pw), lambda bi, i: (bi, jnp.minimum((i + 1) * (ts // 8), s // 8 - 1), 0))
    params = [_row(mu_prev), _row(mu_next), _row(w0), _blockdiag2(w2).astype(BF16), _row(a0),
              _blockdiag2(a2).astype(BF16), g2.astype(BF16), _row(k_k), _row(k_a), _row(r_k)]
    full = lambda a: pl.BlockSpec(a.shape, lambda bi, i: (0,) * a.ndim)
    one = pl.BlockSpec((1, ts, c), lambda bi, i: (bi, i, 0))
    two = pl.BlockSpec((2, 1, ts, c), lambda bi, i: (0, bi, i, 0))
    s1 = jax.ShapeDtypeStruct((b, s, c), F32)
    s2 = jax.ShapeDtypeStruct((2, b, s, c), F32)
    return pl.pallas_call(
        functools.partial(_rwprep_kernel, nblk=nblk),
        grid=(b, nblk),
        in_specs=[cur, prv, nxt] + [full(p) for p in params],
        out_specs=[one, one, one, two, two, two, one, one],
        out_shape=[s1, s1, s1, s2, s2, s2, s1, s1],
        compiler_params=_cparams(("parallel", "parallel")),
    )(rw, rw, rw, *params)


WKV_STEP_CHUNKS = 4


WKV2_BATCH = 4


def _wkv2_kernel(rf, vf, kkf, rb, vb, kkb, kdf, bf, lwf, kdb, bb, lwb, yf_ref, yb_ref, s_ref):
    cc = WKV_CHUNK
    nb = rf.shape[0]
    nchunk = rf.shape[1] // cc

    @pl.when(pl.program_id(1) == 0)
    def _():
        s_ref[...] = jnp.zeros_like(s_ref)

    same_head = ((lax.broadcasted_iota(jnp.int32, (LANES, LANES), 0) < HEAD_DIM)
                 == (lax.broadcasted_iota(jnp.int32, (LANES, LANES), 1) < HEAD_DIM))
    rowc = lax.broadcasted_iota(jnp.int32, (cc, LANES), 0)
    colc = lax.broadcasted_iota(jnp.int32, (cc, LANES), 1) % cc
    eye = (colc == rowc).astype(F32)
    strict = (colc < rowc, colc > rowc)
    incl = (colc <= rowc, colc >= rowc)
    first = lax.broadcasted_iota(jnp.int32, (1, LANES), 1) < HEAD_DIM
    second = jnp.logical_not(first)

    def stack_heads(t):
        z = jnp.zeros_like(t)
        return jnp.concatenate([jnp.where(first, t, z), jnp.where(second, t, z)], axis=0)

    def make_chains(c):
        chains = []
        for bi in range(nb):
            for d in range(2):
                j = nchunk - 1 - c if d else c
                sl = slice(j * cc, (j + 1) * cc)
                src = (rb, vb, kkb, kdb, bb, lwb) if d else (rf, vf, kkf, kdf, bf, lwf)
                for p in range(RW_HEADS // 2):
                    ls = slice(p * LANES, (p + 1) * LANES)
                    chains.append(dict(
                        bi=bi, d=d, p=p, sl=sl, ls=ls,
                        r=src[0][bi, sl, ls], v=src[1][bi, sl, ls], kk=src[2][bi, sl, ls],
                        kd=src[3][0, bi, sl, ls], b=src[4][0, bi, sl, ls], lw=src[5][0, bi, sl, ls]))
        return chains

    def solve_chunks(chains):
        for ch in chains:
            cum = ch["lw"]
            sh = 1
            while sh < cc:
                if ch["d"]:
                    cum = cum + jnp.where(rowc < cc - sh, pltpu.roll(cum, cc - sh, 0), 0.0)
                else:
                    cum = cum + jnp.where(rowc >= sh, pltpu.roll(cum, sh, 0), 0.0)
                sh *= 2
            tot = cum[0:1] if ch["d"] else cum[cc - 1:cc]
            at = (-ch["kk"] * jnp.exp(cum - ch["lw"])).astype(BF16)
            einv = jnp.exp(-cum)
            eh = jnp.exp(tot - cum)
            ch["etot"] = jnp.exp(tot)
            ch["rt"] = (ch["r"] * jnp.exp(cum)).astype(BF16)
            ch["vb"] = ch["v"].astype(BF16)
            ch["v_st"] = stack_heads(ch["vb"])
            ch["at_st"] = stack_heads(at)
            ch["lhs"] = jnp.concatenate([at, ch["rt"]], axis=0)
            ch["rhs"] = jnp.concatenate([stack_heads((ch["b"] * einv).astype(BF16)),
                                         stack_heads((ch["kd"] * einv).astype(BF16))], axis=0)
            ch["kb_hat"] = jnp.concatenate([ch["kd"] * eh, ch["b"] * eh], axis=0).astype(BF16)
        for ch in chains:
            m1 = _dot_nt(ch["lhs"], ch["rhs"])
            d = ch["d"]
            ch["n"] = jnp.where(strict[d], m1[:cc, :LANES], 0.0)
            a_ak = jnp.where(strict[d], m1[:cc, LANES:], 0.0)
            ch["a_rb"] = jnp.where(incl[d], m1[cc:, :LANES], 0.0).astype(BF16)
            a_rk = jnp.where(incl[d], m1[cc:, LANES:], 0.0)
            ch["a_k"] = jnp.concatenate([a_ak, a_rk], axis=0).astype(BF16)
        for ch in chains:
            nbf = ch["n"].astype(BF16)
            ch["pw"] = _dot(nbf, stack_heads(nbf))
            ch["t"] = eye + ch["n"]
        for _ in range(cc.bit_length() - 3):
            for ch in chains:
                pwb = ch["pw"].astype(BF16)
                tp = _dot(jnp.concatenate([ch["t"].astype(BF16), pwb], axis=0), stack_heads(pwb))
                ch["t"] = ch["t"] + tp[:cc]
                ch["pw"] = tp[cc:]
        for ch in chains:
            ch["t"] = (ch["t"] + _dot(ch["t"].astype(BF16), stack_heads(ch["pw"].astype(BF16)))).astype(BF16)
        for ch in chains:
            ch["av"] = _dot(ch["a_k"], ch["v_st"])
        for ch in chains:
            akv = stack_heads(ch["av"][:cc].astype(BF16))
            tx = _dot(ch["t"], jnp.concatenate([ch["at_st"], akv], axis=1))
            ch["atp"] = tx[:, :LANES].astype(BF16)
            ch["u0"] = tx[:, LANES:]

    def advance_state(chains):
        for ch in chains:
            ch["st"] = s_ref[ch["bi"], ch["d"], ch["p"]]
            qs = _dot_nt(jnp.concatenate([ch["atp"], ch["rt"]], axis=0), ch["st"].astype(BF16))
            ch["u"] = (qs[:cc] + ch["u0"]).astype(BF16)
            ch["ys"] = qs[cc:] + ch["av"][cc:]
        for ch in chains:
            upd = _dot_tn(jnp.concatenate([ch["vb"], ch["u"]], axis=0), ch["kb_hat"])
            s_ref[ch["bi"], ch["d"], ch["p"]] = ch["st"] * ch["etot"] + jnp.where(same_head, upd, 0.0)
        for ch in chains:
            y_ref = yb_ref if ch["d"] else yf_ref
            y_ref[ch["bi"], ch["sl"], ch["ls"]] = ch["ys"] + _dot(ch["a_rb"], stack_heads(ch["u"]))

    per_chunk = [make_chains(c) for c in range(nchunk)]
    solve_chunks([ch for chains in per_chunk for ch in chains])
    for chains in per_chunk:
        advance_state(chains)


def _wkv2(r, v, kk, kd, bb, lw):
    b, s, c = r.shape
    tt = WKV_CHUNK * WKV_STEP_CHUNKS
    nblk = s // tt
    nb = WKV2_BATCH if b % WKV2_BATCH == 0 else 1
    fwd = pl.BlockSpec((nb, tt, c), lambda bi, i: (bi, i, 0))
    bwd = pl.BlockSpec((nb, tt, c), lambda bi, i: (bi, nblk - 1 - i, 0))
    fwd2 = pl.BlockSpec((1, nb, tt, c), lambda bi, i: (0, bi, i, 0))
    bwd2 = pl.BlockSpec((1, nb, tt, c), lambda bi, i: (1, bi, nblk - 1 - i, 0))
    return pl.pallas_call(
        _wkv2_kernel,
        grid=(b // nb, nblk),
        in_specs=[fwd, fwd, fwd, bwd, bwd, bwd, fwd2, fwd2, fwd2, bwd2, bwd2, bwd2],
        out_specs=[fwd, bwd],
        out_shape=[jax.ShapeDtypeStruct((b, s, c), F32)] * 2,
        scratch_shapes=[pltpu.VMEM((nb, 2, RW_HEADS // 2, LANES, LANES), F32)],
        compiler_params=_cparams(("parallel", "arbitrary")),
    )(r, v, kk, r, v, kk, kd, bb, lw, kd, bb, lw)


def _memkv_kernel(mem_ref, nw_ref, w_ref, kw_ref, k_ref, v_ref):
    x = mem_ref[0]
    ms = jnp.mean(x * x, axis=-1, keepdims=True)
    h = (x * lax.rsqrt(ms + RMS_EPS) * nw_ref[...]).astype(BF16)
    kv = _dot(h, w_ref[...])
    k = kv[:, :MEM_WIDTH]
    ms = _split_dot(k * k, _head_ones(MEM_WIDTH)) * (1.0 / HEAD_DIM)
    k_ref[0] = (k * lax.rsqrt(ms + RMS_EPS) * kw_ref[...]).astype(BF16)
    v_ref[0] = kv[:, MEM_WIDTH:].astype(BF16)


def _memkv(mem, nw, w_kv, kw):
    b, m, d = mem.shape
    full = lambda a: pl.BlockSpec(a.shape, lambda bi: (0,) * a.ndim)
    out = pl.BlockSpec((1, m, MEM_WIDTH), lambda bi: (bi, 0, 0))
    return pl.pallas_call(
        _memkv_kernel,
        grid=(b,),
        in_specs=[pl.BlockSpec((1, m, d), lambda bi: (bi, 0, 0)), full(nw), full(w_kv), full(kw)],
        out_specs=[out, out],
        out_shape=[jax.ShapeDtypeStruct((b, m, MEM_WIDTH), BF16)] * 2,
        compiler_params=_cparams(("parallel",)),
    )(mem, nw, w_kv, kw)


def _mixout_kernel(x_ref, na_ref, yf_ref, yb_ref, bonus_ref, g_ref, mq_ref, mk_ref, mv_ref,
                   lnw_ref, lnb_ref, wo_ref, fnw_ref, whi_ref, wlo_ref, rb_ref,
                   o_ref, h_ref, route_ref, nch_ref):
    ones = _head_ones(RW_WIDTH)
    y = yf_ref[0] + yb_ref[0]
    mu = _split_dot(y, ones) * (1.0 / HEAD_DIM)
    yc = y - mu
    var = _split_dot(yc * yc, ones) * (1.0 / HEAD_DIM)
    yn = yc * lax.rsqrt(var + RW_GN_EPS) * lnw_ref[...] + lnb_ref[...]
    y_rw = ((yn + bonus_ref[0]) * g_ref[0]).astype(BF16)

    first = lax.broadcasted_iota(jnp.int32, (1, LANES), 1) < HEAD_DIM
    mems = []
    for p in range(MEM_HEADS // 2):
        ls = slice(p * LANES, (p + 1) * LANES)
        qp = mq_ref[0, :, ls]
        kp = mk_ref[0, :, ls]
        vp = mv_ref[0, :, ls]
        outs = []
        for hh in range(2):
            sel = first if hh == 0 else jnp.logical_not(first)
            s = _dot_nt(jnp.where(sel, qp, jnp.zeros_like(qp)), kp)
            m = jnp.max(s, axis=-1, keepdims=True)
            e = jnp.exp(s - m)
            l = jnp.sum(e, axis=-1, keepdims=True)
            outs.append(_dot(e.astype(BF16), vp) / l)
        mems.append(jnp.where(first, outs[0], outs[1]).astype(BF16))
    y_mem = jnp.concatenate(mems, axis=1)

    acc = _dot(na_ref[0], wo_ref[0:NA_WIDTH, :])
    acc = acc + _dot(y_rw, wo_ref[NA_WIDTH:NA_WIDTH + RW_WIDTH, :])
    acc = acc + _dot(y_mem, wo_ref[NA_WIDTH + RW_WIDTH:, :])
    x_mid = x_ref[0] + acc
    o_ref[0] = x_mid
    _route_tile(x_mid, fnw_ref, whi_ref, wlo_ref, rb_ref, h_ref, route_ref, nch_ref)


def _mixout(x, y_na, y_f, y_b, bonus, g, mq, mk, mv, ln_w, ln_b, w_out, ffn_nw, router, tm):
    b, s, d = x.shape
    m = mk.shape[1]
    sub = tm // MOE_SUB
    tok = lambda w: pl.BlockSpec((1, tm, w), lambda bi, i: (bi, i, 0))
    full = lambda a: pl.BlockSpec(a.shape, lambda bi, i: (0,) * a.ndim)
    memb = pl.BlockSpec((1, m, MEM_WIDTH), lambda bi, i: (bi, 0, 0))
    x_mid, h2, route, nch = pl.pallas_call(
        _mixout_kernel,
        grid=(b, s // tm),
        in_specs=[tok(d), tok(NA_WIDTH), tok(RW_WIDTH), tok(RW_WIDTH), tok(RW_WIDTH), tok(RW_WIDTH),
                  tok(MEM_WIDTH), memb, memb, full(ln_w), full(ln_b), full(w_out), full(ffn_nw)]
        + [full(p) for p in router],
        out_specs=[tok(d), tok(d),
                   pl.BlockSpec((1, sub, ROUTE_ROWS, MOE_SUB), lambda bi, i: (bi, i, 0, 0)),
                   pl.BlockSpec((1, sub, 1, ROUTER_LANES), lambda bi, i: (bi, i, 0, 0))],
        out_shape=[jax.ShapeDtypeStruct((b, s, d), F32), jax.ShapeDtypeStruct((b, s, d), BF16),
                   jax.ShapeDtypeStruct((b, s // MOE_SUB, ROUTE_ROWS, MOE_SUB), F32),
                   jax.ShapeDtypeStruct((b, s // MOE_SUB, 1, ROUTER_LANES), F32)],
        compiler_params=_cparams(("parallel", "parallel")),
    )(x, y_na, y_f, y_b, bonus, g, mq, mk, mv, ln_w, ln_b, w_out, ffn_nw, *router)
    ns = b * s // MOE_SUB
    return (x_mid, h2.reshape(b * s, d), route.reshape(ns, ROUTE_ROWS, MOE_SUB),
            nch.reshape(ns, 1, ROUTER_LANES))


ROUTER_LANES = 128


MOE_SUB = 256
MOE_CHUNK = 16
MOE_LOCAL_CHUNKS = 48
MOE_BLOCK_CHUNKS = 32
MOE_TOP_K = 2
ROUTE_ROWS = 8
assert MOE_LOCAL_CHUNKS >= MOE_TOP_K * MOE_SUB // MOE_CHUNK + MOE_EXPERTS - 1


def _route_subtile(lg):
    t = lg.shape[0]
    lane = lax.broadcasted_iota(jnp.int32, (1, ROUTER_LANES), 1)
    lanef = lane.astype(F32)
    big = float(ROUTER_LANES)

    def argmax_lane(v):
        m = jnp.max(v, axis=-1, keepdims=True)
        return m, jnp.min(jnp.where(v == m, lanef, big), axis=-1, keepdims=True)

    is_group = lane < MOE_GROUPS
    gmax, gidx = argmax_lane(jnp.where(is_group, lg, NEG_BIG))
    gsum = jnp.sum(jnp.where(is_group, jnp.exp(lg - gmax), 0.0), axis=-1, keepdims=True)
    lo = MOE_GROUPS + MOE_EXPERTS_PER_GROUP * gidx
    el = jnp.where((lanef >= lo) & (lanef < lo + MOE_EXPERTS_PER_GROUP), lg, NEG_BIG)
    m1, i1 = argmax_lane(el)
    m2, i2 = argmax_lane(jnp.where(lanef == i1, NEG_BIG, el))
    ratio = jnp.exp(m2 - m1)
    w0 = 1.0 / (gsum * (1.0 + ratio))
    w1 = w0 * ratio

    oh0 = (lanef == i1 - MOE_GROUPS).astype(F32)
    oh1 = (lanef == i2 - MOE_GROUPS).astype(F32)
    row = lax.broadcasted_iota(jnp.int32, (t, t), 0)
    col = lax.broadcasted_iota(jnp.int32, (t, t), 1)
    before = (col < row).astype(BF16)
    cnt0 = jnp.sum(oh0, axis=0, keepdims=True)
    cnt1 = jnp.sum(oh1, axis=0, keepdims=True)
    rank0 = _dot(before, oh0.astype(BF16))
    rank1 = _dot(before, oh1.astype(BF16)) + cnt0
    nch = jnp.floor((cnt0 + cnt1 + (MOE_CHUNK - 1.0)) * (1.0 / MOE_CHUNK))
    li = lax.broadcasted_iota(jnp.int32, (ROUTER_LANES, ROUTER_LANES), 0)
    lj = lax.broadcasted_iota(jnp.int32, (ROUTER_LANES, ROUTER_LANES), 1)
    start = _dot(jnp.broadcast_to(nch, (8, ROUTER_LANES)).astype(BF16), (li < lj).astype(BF16))[0:1]
    pos0 = jnp.sum((start * MOE_CHUNK + rank0) * oh0, axis=-1, keepdims=True)
    pos1 = jnp.sum((start * MOE_CHUNK + rank1) * oh1, axis=-1, keepdims=True)
    rec = (jnp.where(lane == 0, pos0, 0.0) + jnp.where(lane == 1, pos1, 0.0)
           + jnp.where(lane == 2, w0, 0.0) + jnp.where(lane == 3, w1, 0.0))
    return jnp.transpose(rec)[0:ROUTE_ROWS], nch


def _route_tile(x, nw_ref, whi_ref, wlo_ref, b_ref, h_ref, route_ref, nch_ref):
    ms = jnp.mean(x * x, axis=-1, keepdims=True)
    h = x * lax.rsqrt(ms + RMS_EPS) * nw_ref[...]
    hi = h.astype(BF16)
    lo = (h - hi.astype(F32)).astype(BF16)
    h_ref[0] = hi
    lg = _dot(hi, whi_ref[...]) + _dot(hi, wlo_ref[...]) + _dot(lo, whi_ref[...]) + b_ref[...]
    for j in range(x.shape[0] // MOE_SUB):
        route_ref[0, j], nch_ref[0, j] = _route_subtile(lg[j * MOE_SUB:(j + 1) * MOE_SUB])


def _router_params(w_group, b_group, w_expert, b_expert):
    d = w_group.shape[0]
    pad = ROUTER_LANES - MOE_GROUPS - MOE_EXPERTS
    w = jnp.concatenate([w_group, w_expert, jnp.zeros((d, pad), F32)], axis=1)
    whi = w.astype(BF16)
    wlo = (w - whi.astype(F32)).astype(BF16)
    bias = jnp.concatenate([b_group, b_expert, jnp.zeros((pad,), F32)]).reshape(1, -1)
    return whi, wlo, bias


def _moe_plan(nch, n):
    nch = nch[:, 0, :MOE_EXPERTS].astype(jnp.int32)
    lc_end = jnp.cumsum(nch, axis=1)
    lc_start = lc_end - nch
    nloc = lc_end[:, -1]
    nblk = (jnp.sum(nch, axis=0) + MOE_BLOCK_CHUNKS - 1) // MOE_BLOCK_CHUNKS
    bend = jnp.cumsum(nblk)
    gch = ((bend - nblk) * MOE_BLOCK_CHUNKS)[None, :] + jnp.cumsum(nch, axis=0) - nch
    c = jnp.arange(MOE_LOCAL_CHUNKS, dtype=jnp.int32)
    in_run = (c[None, :, None] >= lc_start[:, None, :]) & (c[None, :, None] < lc_end[:, None, :])
    dest = jnp.sum(jnp.where(in_run, (gch - lc_start)[:, None, :] + c[None, :, None], 0), axis=-1)

    nblocks = _moe_blocks(n)
    b = jnp.arange(nblocks, dtype=jnp.int32)
    block_e = jnp.minimum(jnp.sum(b[:, None] >= bend[None, :], axis=-1), MOE_EXPERTS - 1)
    tot = jnp.sum(nch, axis=0)
    tail_start = (bend - nblk) * MOE_BLOCK_CHUNKS + tot
    ntail = nblk * MOE_BLOCK_CHUNKS - tot
    i32 = lambda a: a.astype(jnp.int32)
    return i32(dest), i32(nloc), i32(block_e), i32(bend[-1:]), i32(tail_start), i32(ntail)


def _moe_blocks(n):
    ns = n // MOE_SUB
    chunks = ns * (MOE_TOP_K * MOE_SUB // MOE_CHUNK + MOE_EXPERTS - 1) + MOE_EXPERTS * (MOE_BLOCK_CHUNKS - 1)
    return -(-chunks // MOE_BLOCK_CHUNKS)


def _local_onehot(pos_rows, shape, row_axis):
    idx = lax.broadcasted_iota(jnp.int32, shape, row_axis)
    return (idx == pos_rows[0]) | (idx == pos_rows[1])


def _dispatch_kernel(dest_ref, nloc_ref, tail_ref, ntail_ref, nv_ref, h_ref, pos_ref, xs_ref,
                     buf, zbuf, sem, zsem):
    s = pl.program_id(0)
    ns = pl.num_programs(0)
    slot = s % 2
    bm = MOE_BLOCK_CHUNKS * MOE_CHUNK
    nblocks = xs_ref.shape[0] // bm

    def zero_fill(wait):
        def tail_copy(e, i):
            dst = xs_ref.at[pl.ds(pl.multiple_of((tail_ref[e] + i) * MOE_CHUNK, MOE_CHUNK), MOE_CHUNK)]
            return pltpu.make_async_copy(zbuf.at[pl.ds(0, MOE_CHUNK)], dst, zsem.at[0])

        def block_copy(b):
            return pltpu.make_async_copy(zbuf, xs_ref.at[pl.ds(pl.multiple_of(b * bm, bm), bm)], zsem.at[1])

        def run(cp):
            return cp.wait() if wait else cp.start()

        for e in range(MOE_EXPERTS):
            def tail_body(i, carry, e=e):
                run(tail_copy(e, i))
                return carry
            lax.fori_loop(0, ntail_ref[e], tail_body, 0)

        def block_body(b, carry):
            run(block_copy(b))
            return carry
        lax.fori_loop(nv_ref[0], nblocks, block_body, 0)

    @pl.when(s == 0)
    def _():
        zbuf[...] = jnp.zeros_like(zbuf)
        zero_fill(wait=False)

    def chunk_copy(sl, c, step):
        src = buf.at[sl, pl.ds(pl.multiple_of(c * MOE_CHUNK, MOE_CHUNK), MOE_CHUNK)]
        dst = xs_ref.at[pl.ds(pl.multiple_of(dest_ref[step, c] * MOE_CHUNK, MOE_CHUNK), MOE_CHUNK)]
        return pltpu.make_async_copy(src, dst, sem.at[sl])

    def wait_step(step, sl):
        def body(c, carry):
            chunk_copy(sl, c, step).wait()
            return carry
        lax.fori_loop(0, nloc_ref[step], body, 0)

    @pl.when(s >= 2)
    def _():
        wait_step(s - 2, slot)

    pos = pos_ref[0, 0:MOE_TOP_K].astype(jnp.int32)
    onehot = _local_onehot((pos[0:1], pos[1:2]), (MOE_LOCAL_CHUNKS * MOE_CHUNK, MOE_SUB), 0)
    buf[slot] = _dot(jnp.where(onehot, 1.0, 0.0).astype(BF16), h_ref[...]).astype(BF16)

    def issue(c, carry):
        chunk_copy(slot, c, s).start()
        return carry
    lax.fori_loop(0, nloc_ref[s], issue, 0)

    @pl.when(s == ns - 1)
    def _():
        @pl.when(s >= 1)
        def _():
            wait_step(s - 1, 1 - slot)
        wait_step(s, slot)
        zero_fill(wait=True)


def _dispatch(h2, pos, dest, nloc, tail_start, ntail, nvalid):
    n, d = h2.shape
    ns = n // MOE_SUB
    bm = MOE_BLOCK_CHUNKS * MOE_CHUNK
    rows = MOE_LOCAL_CHUNKS * MOE_CHUNK
    return pl.pallas_call(
        _dispatch_kernel,
        grid_spec=pltpu.PrefetchScalarGridSpec(
            num_scalar_prefetch=5, grid=(ns,),
            in_specs=[pl.BlockSpec((MOE_SUB, d), lambda s, *_: (s, 0)),
                      pl.BlockSpec((1, ROUTE_ROWS, MOE_SUB), lambda s, *_: (s, 0, 0))],
            out_specs=pl.BlockSpec(memory_space=pl.ANY),
            scratch_shapes=[pltpu.VMEM((2, rows, d), BF16), pltpu.VMEM((bm, d), BF16),
                            pltpu.SemaphoreType.DMA((2,)), pltpu.SemaphoreType.DMA((2,))]),
        out_shape=jax.ShapeDtypeStruct((_moe_blocks(n) * bm, d), BF16),
        compiler_params=_cparams(("arbitrary",)),
    )(dest, nloc, tail_start, ntail, nvalid, h2, pos)


def _experts_kernel(be_ref, nv_ref, x_ref, wg_ref, wu_ref, wd_ref, o_ref):
    del be_ref
    valid = pl.program_id(0) < nv_ref[0]

    @pl.when(valid)
    def _():
        x = x_ref[...]
        hid = jax.nn.silu(_dot(x, wg_ref[0])) * _dot(x, wu_ref[0])
        o_ref[...] = _dot(hid.astype(BF16), wd_ref[0])

    @pl.when(jnp.logical_not(valid))
    def _():
        o_ref[...] = jnp.zeros_like(o_ref)


def _experts(xs, block_e, nvalid, w_gate, w_up, w_down):
    cap, d = xs.shape
    bm = MOE_BLOCK_CHUNKS * MOE_CHUNK
    blk = lambda b, be, nv: (jnp.minimum(b, nv[0] - 1), 0)
    wsel = lambda b, be, nv: (be[jnp.minimum(b, nv[0] - 1)], 0, 0)
    return pl.pallas_call(
        _experts_kernel,
        grid_spec=pltpu.PrefetchScalarGridSpec(
            num_scalar_prefetch=2, grid=(cap // bm,),
            in_specs=[pl.BlockSpec((bm, d), blk),
                      pl.BlockSpec((1, d, MOE_D_FF), wsel),
                      pl.BlockSpec((1, d, MOE_D_FF), wsel),
                      pl.BlockSpec((1, MOE_D_FF, d), wsel)],
            out_specs=pl.BlockSpec((bm, d), lambda b, be, nv: (b, 0))),
        out_shape=jax.ShapeDtypeStruct((cap, d), F32),
        compiler_params=_cparams(("arbitrary",)),
    )(block_e, nvalid, xs, w_gate, w_up, w_down)


def _combine_kernel(dest_ref, x_ref, route_ref, ys_ref, o_ref, buf, sem):
    s = pl.program_id(0)
    ns = pl.num_programs(0)
    slot = s % 2

    def chunk_copy(sl, c, step):
        src = ys_ref.at[pl.ds(pl.multiple_of(dest_ref[step, c] * MOE_CHUNK, MOE_CHUNK), MOE_CHUNK)]
        dst = buf.at[sl, pl.ds(pl.multiple_of(c * MOE_CHUNK, MOE_CHUNK), MOE_CHUNK)]
        return pltpu.make_async_copy(src, dst, sem.at[sl])

    def fetch(step, sl):
        def body(c, carry):
            chunk_copy(sl, c, step).start()
            return carry
        lax.fori_loop(0, MOE_LOCAL_CHUNKS, body, 0)

    @pl.when(s == 0)
    def _():
        fetch(0, 0)

    @pl.when(s + 1 < ns)
    def _():
        fetch(s + 1, 1 - slot)

    def wait(c, carry):
        chunk_copy(slot, c, s).wait()
        return carry
    lax.fori_loop(0, MOE_LOCAL_CHUNKS, wait, 0)

    rec = jnp.transpose(route_ref[0])
    pos = rec[:, 0:MOE_TOP_K].astype(jnp.int32)
    idx = lax.broadcasted_iota(jnp.int32, (MOE_SUB, MOE_LOCAL_CHUNKS * MOE_CHUNK), 1)
    pw = jnp.where(idx == pos[:, 0:1], rec[:, 2:3], 0.0) + jnp.where(idx == pos[:, 1:2], rec[:, 3:4], 0.0)
    ys = buf[slot]
    p_hi = pw.astype(BF16)
    p_lo = (pw - p_hi.astype(F32)).astype(BF16)
    y_hi = ys.astype(BF16)
    y_lo = (ys - y_hi.astype(F32)).astype(BF16)
    o_ref[...] = x_ref[...] + _dot(p_hi, y_hi) + _dot(p_hi, y_lo) + _dot(p_lo, y_hi)


def _combine(x2, ys, route, dest):
    n, d = x2.shape
    ns = n // MOE_SUB
    rows = MOE_LOCAL_CHUNKS * MOE_CHUNK
    return pl.pallas_call(
        _combine_kernel,
        grid_spec=pltpu.PrefetchScalarGridSpec(
            num_scalar_prefetch=1, grid=(ns,),
            in_specs=[pl.BlockSpec((MOE_SUB, d), lambda s, *_: (s, 0)),
                      pl.BlockSpec((1, ROUTE_ROWS, MOE_SUB), lambda s, *_: (s, 0, 0)),
                      pl.BlockSpec(memory_space=pl.ANY)],
            out_specs=pl.BlockSpec((MOE_SUB, d), lambda s, *_: (s, 0)),
            scratch_shapes=[pltpu.VMEM((2, rows, d), F32), pltpu.SemaphoreType.DMA((2,))]),
        out_shape=jax.ShapeDtypeStruct((n, d), F32),
        compiler_params=_cparams(("arbitrary",)),
    )(dest, x2, route, ys)


def _moe(x2, h2, route, nch, w_gate, w_up, w_down):
    dest, nloc, block_e, nvalid, tail_start, ntail = _moe_plan(nch, x2.shape[0])
    xs = _dispatch(h2, route, dest, nloc, tail_start, ntail, nvalid)
    ys = _experts(xs, block_e, nvalid, w_gate, w_up, w_down)
    return _combine(x2, ys, route, dest)


def _tile(n, want):
    t = min(n, want)
    assert n % t == 0
    return t


def kernel(x, mem, attn_norm_w, w_in, na_q_norm_w, na_k_norm_w, na_rpb, rw_mu_prev, rw_mu_next, rw_w0, rw_w2, rw_a0, rw_a2, rw_g2, rw_k_k, rw_k_a, rw_r_k, rw_ln_w, rw_ln_b, mem_norm_w, w_mem_kv, mem_q_norm_w, mem_k_norm_w, w_out, ffn_norm_w, moe_w_group, moe_b_group, moe_w_expert, moe_b_expert, moe_w_gate, moe_w_up, moe_w_down):
    b, s, d = x.shape
    n = b * s
    depth = w_in.shape[0]
    assert s % (NA_BLOCK_ROWS * GRID_W) == 0 and s // GRID_W >= 2 * NA_KH
    tm = _tile(n, 512)
    ts = _tile(s, 512)
    for l in range(depth):
        q, k, v, rw, mq = _proj(
            x.reshape(n, d), _row(attn_norm_w[l]), w_in[l].astype(BF16),
            _row(jnp.tile(na_q_norm_w[l], NA_HEADS)), _row(jnp.tile(na_k_norm_w[l], NA_HEADS)),
            _row(jnp.tile(mem_q_norm_w[l], MEM_HEADS)), tm)
        y_na = _na(q.reshape(b, s, NA_WIDTH), k.reshape(b, s, NA_WIDTH), v.reshape(b, s, NA_WIDTH),
                   _na_bias_table(na_rpb[l]))
        r, vv, kk, kd, bb, lw, g, bonus = _rwprep(
            rw.reshape(b, s, RW_PROJ), rw_mu_prev[l], rw_mu_next[l], rw_w0[l], rw_w2[l], rw_a0[l],
            rw_a2[l], rw_g2[l], rw_k_k[l], rw_k_a[l], rw_r_k[l], ts)
        y_f, y_b = _wkv2(r, vv, kk, kd, bb, lw)
        mk, mv = _memkv(mem, _row(mem_norm_w[l]), w_mem_kv[l].astype(BF16),
                        _row(jnp.tile(mem_k_norm_w[l], MEM_HEADS)))
        x, h2, route, nch = _mixout(
            x, y_na, y_f, y_b, bonus, g, mq.reshape(b, s, MEM_WIDTH), mk, mv,
            _row(rw_ln_w[l]), _row(rw_ln_b[l]), w_out[l].astype(BF16), _row(ffn_norm_w[l]),
            _router_params(moe_w_group[l], moe_b_group[l], moe_w_expert[l], moe_b_expert[l]), ts)
        x = _moe(x.reshape(n, d), h2, route, nch, moe_w_gate[l].astype(BF16),
                 moe_w_up[l].astype(BF16), moe_w_down[l].astype(BF16)).reshape(b, s, d)
    return x
```

```python
import functools

import jax
import jax.numpy as jnp
import numpy as np
from jax import lax
from jax.experimental import pallas as pl
from jax.experimental.pallas import tpu as pltpu

F32 = jnp.float32
BF16 = jnp.bfloat16

GRID_W = 64
HEAD_DIM = 64
NA_HEADS = 8
NA_WIDTH = NA_HEADS * HEAD_DIM
NA_KH = 8
NA_KW = 16
RW_HEADS = 4
RW_WIDTH = RW_HEADS * HEAD_DIM
RW_LORA_W = 64
RW_LORA_A = 64
RW_LORA_G = 128
RW_PROJ = 3 * RW_WIDTH + 2 * RW_LORA_W + 2 * RW_LORA_A + RW_LORA_G
MEM_HEADS = 4
MEM_WIDTH = MEM_HEADS * HEAD_DIM
MOE_GROUPS = 4
MOE_EXPERTS_PER_GROUP = 4
MOE_EXPERTS = MOE_GROUPS * MOE_EXPERTS_PER_GROUP
MOE_D_FF = 512
RMS_EPS = 1e-6
RW_GN_EPS = 64e-5

LANES = 128
WKV_CHUNK = 64
NEG_BIG = -1e30
VMEM_LIMIT = 56 * 1024 * 1024


def _cparams(sem):
    return pltpu.CompilerParams(dimension_semantics=sem, vmem_limit_bytes=VMEM_LIMIT)


def _dot(a, b):
    return jnp.dot(a, b, preferred_element_type=F32)


def _dot_nt(a, b):
    return lax.dot_general(a, b, (((1,), (1,)), ((), ())), preferred_element_type=F32)


def _dot_tn(a, b):
    return lax.dot_general(a, b, (((0,), (0,)), ((), ())), preferred_element_type=F32)


def _split_dot(t, m):
    hi = t.astype(BF16)
    lo = (t - hi.astype(F32)).astype(BF16)
    return _dot(hi, m) + _dot(lo, m)


def _head_ones(width):
    i = lax.broadcasted_iota(jnp.int32, (width, width), 0) // HEAD_DIM
    j = lax.broadcasted_iota(jnp.int32, (width, width), 1) // HEAD_DIM
    return (i == j).astype(BF16)


def _row(v):
    return v.reshape(1, -1).astype(F32)


def _proj_kernel(x_ref, nw_ref, w_ref, qw_ref, kw_ref, mw_ref, q_ref, k_ref, v_ref, rw_ref, mq_ref):
    x = x_ref[...]
    ms = jnp.mean(x * x, axis=-1, keepdims=True)
    h = (x * lax.rsqrt(ms + RMS_EPS) * nw_ref[...]).astype(BF16)
    ones_na = _head_ones(NA_WIDTH)

    def head_norm(t, w, ones):
        ms = _dot((t * t).astype(BF16), ones) * (1.0 / HEAD_DIM)
        return t * lax.rsqrt(ms + RMS_EPS) * w

    o = 0
    q = _dot(h, w_ref[:, o:o + NA_WIDTH])
    q_ref[...] = (head_norm(q, qw_ref[...], ones_na) * (HEAD_DIM ** -0.5)).astype(BF16)
    o += NA_WIDTH
    k = _dot(h, w_ref[:, o:o + NA_WIDTH])
    k_ref[...] = head_norm(k, kw_ref[...], ones_na).astype(BF16)
    o += NA_WIDTH
    v_ref[...] = _dot(h, w_ref[:, o:o + NA_WIDTH]).astype(BF16)
    o += NA_WIDTH
    rw_ref[...] = _dot(h, w_ref[:, o:o + RW_PROJ])
    o += RW_PROJ
    mq = _dot(h, w_ref[:, o:o + MEM_WIDTH])
    mq_ref[...] = (head_norm(mq, mw_ref[...], _head_ones(MEM_WIDTH)) * (HEAD_DIM ** -0.5)).astype(BF16)


def _proj(x2, nw, w_in, qw, kw, mw, tm):
    n, d = x2.shape
    p_in = w_in.shape[1]
    tok = lambda w: pl.BlockSpec((tm, w), lambda i: (i, 0))
    full = lambda a: pl.BlockSpec(a.shape, lambda i: (0,) * a.ndim)
    return pl.pallas_call(
        _proj_kernel,
        grid=(n // tm,),
        in_specs=[tok(d), full(nw), full(w_in), full(qw), full(kw), full(mw)],
        out_specs=[tok(NA_WIDTH), tok(NA_WIDTH), tok(NA_WIDTH), tok(RW_PROJ), tok(MEM_WIDTH)],
        out_shape=[jax.ShapeDtypeStruct((n, NA_WIDTH), BF16)] * 3
        + [jax.ShapeDtypeStruct((n, RW_PROJ), F32), jax.ShapeDtypeStruct((n, MEM_WIDTH), BF16)],
        compiler_params=_cparams(("parallel",)),
    )(x2, nw, w_in, qw, kw, mw)


NA_BLOCK_ROWS = 8


def _na_bias_table(rpb):
    col = np.arange(GRID_W)
    cs = np.clip(col - NA_KW // 2, 0, GRID_W - NA_KW)
    dc = col[None, :] - col[:, None]
    inside = (col[None, :] >= cs[:, None]) & (col[None, :] < cs[:, None] + NA_KW)
    dci = np.clip(dc + NA_KW - 1, 0, 2 * NA_KW - 2)
    off = np.arange(NA_KH) - (NA_KH - 1)
    dri = off[:, None] + np.arange(NA_KH)[None, :] + NA_KH - 1
    row_sel = (dri[:, :, None] == np.arange(2 * NA_KH - 1)).astype(np.float32)
    col_sel = (dci[:, :, None] == np.arange(2 * NA_KW - 1)).astype(np.float32)
    t = jnp.einsum("hrq,onr,cdq->ohcnd", rpb.astype(F32), row_sel, col_sel,
                   precision=lax.Precision.HIGHEST)
    t = jnp.where(inside[None, None, :, None, :], t, NEG_BIG)
    return t.reshape(NA_KH, NA_HEADS // 2, 2 * GRID_W, NA_KH * GRID_W).astype(F32)


def _na_kernel(q_ref, kp_ref, kc_ref, kn_ref, vp_ref, vc_ref, vn_ref, bias_ref, o_ref, kwin, vwin, *, rows):
    j = pl.program_id(1)
    blk = NA_BLOCK_ROWS * GRID_W
    kwin[0:blk] = kp_ref[0]
    kwin[blk:2 * blk] = kc_ref[0]
    kwin[2 * blk:3 * blk] = kn_ref[0]
    vwin[0:blk] = vp_ref[0]
    vwin[blk:2 * blk] = vc_ref[0]
    vwin[2 * blk:3 * blk] = vn_ref[0]
    lane = lax.broadcasted_iota(jnp.int32, (1, LANES), 1)
    first = lane < HEAD_DIM

    def row_body(i, carry):
        r = j * NA_BLOCK_ROWS + i
        rs = jnp.clip(r - NA_KH // 2, 0, rows - NA_KH)
        oi = rs - r + (NA_KH - 1)
        start = pl.multiple_of((rs - (j - 1) * NA_BLOCK_ROWS) * GRID_W, GRID_W)
        qs = pl.multiple_of(i * GRID_W, GRID_W)
        pairs = [slice(p * LANES, (p + 1) * LANES) for p in range(NA_HEADS // 2)]
        scores = []
        for p, ls in enumerate(pairs):
            qp = q_ref[0, pl.ds(qs, GRID_W), ls]
            zero = jnp.zeros_like(qp)
            qst = jnp.concatenate([jnp.where(first, qp, zero), jnp.where(first, zero, qp)], axis=0)
            scores.append(_dot_nt(qst, kwin[pl.ds(start, NA_KH * GRID_W), ls]) + bias_ref[oi, p])
        probs, norms = [], []
        for s in scores:
            e = jnp.exp(s - jnp.max(s, axis=-1, keepdims=True))
            norms.append(jnp.sum(e, axis=-1, keepdims=True))
            probs.append(e.astype(BF16))
        for ls, e, l in zip(pairs, probs, norms):
            o = _dot(e, vwin[pl.ds(start, NA_KH * GRID_W), ls]) / l
            o_ref[0, pl.ds(qs, GRID_W), ls] = jnp.where(first, o[:GRID_W], o[GRID_W:]).astype(o_ref.dtype)
        return carry

    lax.fori_loop(0, NA_BLOCK_ROWS, row_body, 0)


def _na(q, k, v, bias):
    b, s, w = q.shape
    rows = s // GRID_W
    nblk = rows // NA_BLOCK_ROWS
    blk = NA_BLOCK_ROWS * GRID_W
    cur = pl.BlockSpec((1, blk, w), lambda bi, j: (bi, j, 0))
    prv = pl.BlockSpec((1, blk, w), lambda bi, j: (bi, jnp.maximum(j - 1, 0), 0))
    nxt = pl.BlockSpec((1, blk, w), lambda bi, j: (bi, jnp.minimum(j + 1, nblk - 1), 0))
    return pl.pallas_call(
        functools.partial(_na_kernel, rows=rows),
        grid=(b, nblk),
        in_specs=[cur, prv, cur, nxt, prv, cur, nxt,
                  pl.BlockSpec(bias.shape, lambda bi, j: (0, 0, 0, 0))],
        out_specs=cur,
        out_shape=jax.ShapeDtypeStruct((b, s, w), BF16),
        scratch_shapes=[pltpu.VMEM((3 * blk, w), BF16), pltpu.VMEM((3 * blk, w), BF16)],
        compiler_params=_cparams(("parallel", "parallel")),
    )(q, k, k, k, v, v, v, bias)


def _rwprep_kernel(rw_ref, prev_ref, next_ref, mup_ref, mun_ref, w0_ref, w2_ref, a0_ref, a2_ref, g2_ref,
                   kk_w_ref, ka_ref, rk_ref,
                   r_ref, v_ref, kk_ref, kd_ref, b_ref, lw_ref, g_ref, bonus_ref, *, nblk):
    i = pl.program_id(1)
    s = rw_ref[0]
    ts = s.shape[0]
    rowi = lax.broadcasted_iota(jnp.int32, (ts, 1), 0)
    halo_p = jnp.where(i > 0, prev_ref[0, 7:8, :], 0.0)
    halo_n = jnp.where(i < nblk - 1, next_ref[0, 0:1, :], 0.0)
    prev = jnp.where(rowi == 0, halo_p, pltpu.roll(s, 1, 0))
    nxt = jnp.where(rowi == ts - 1, halo_n, pltpu.roll(s, ts - 1, 0))
    s = s + mup_ref[...] * (prev - s) + mun_ref[...] * (nxt - s)

    c = RW_WIDTH
    r = s[:, 0:c]
    k = s[:, c:2 * c]
    v = s[:, 2 * c:3 * c]
    lw = s[:, 3 * c:3 * c + 2 * RW_LORA_W]
    la = s[:, 3 * c + 2 * RW_LORA_W:3 * c + 2 * RW_LORA_W + 2 * RW_LORA_A]
    lg = s[:, 3 * c + 2 * RW_LORA_W + 2 * RW_LORA_A:]
    ones = _head_ones(c)

    g_ref[0] = _dot(jax.nn.sigmoid(lg).astype(BF16), g2_ref[...])
    kk = k * kk_w_ref[...]
    nrm = jnp.sqrt(_split_dot(kk * kk, ones))
    kk = kk / jnp.maximum(nrm, 1e-12)
    wl_pre = w0_ref[...] + _dot(jnp.tanh(lw).astype(BF16), w2_ref[...])
    a_all = jax.nn.sigmoid(a0_ref[...] + _dot(la.astype(BF16), a2_ref[...]))
    wl = -jax.nn.softplus(-wl_pre) - 0.5
    logdecay = -jnp.exp(wl)
    r_ref[0] = r
    v_ref[0] = v
    kk_ref[0] = kk
    kd_sum = jnp.zeros_like(k)
    for d in range(2):
        a = a_all[:, d * c:(d + 1) * c]
        kd = k * (1.0 + (a - 1.0) * ka_ref[...])
        kd_ref[d, 0] = kd
        b_ref[d, 0] = kk * a
        lw_ref[d, 0] = logdecay[:, d * c:(d + 1) * c]
        kd_sum = kd_sum + kd
    bonus_ref[0] = _split_dot(r * kd_sum * rk_ref[...], ones) * v


def _blockdiag2(m):
    z = jnp.zeros_like(m[0])
    return jnp.concatenate([jnp.concatenate([m[0], z], 1), jnp.concatenate([z, m[1]], 1)], 0)


def _rwprep(rw, mu_prev, mu_next, w0, w2, a0, a2, g2, k_k, k_a, r_k, ts):
    b, s, pw = rw.shape
    nblk = s // ts
    c = RW_WIDTH
    cur = pl.BlockSpec((1, ts, pw), lambda bi, i: (bi, i, 0))
    prv = pl.BlockSpec((1, 8, pw), lambda bi, i: (bi, jnp.maximum(i * (ts // 8) - 1, 0), 0))
    nxt = pl.BlockSpec((1, 8, pw), lambda bi, i: (bi, jnp.minimum((i + 1) * (ts // 8), s // 8 - 1), 0))
    params = [_row(mu_prev), _row(mu_next), _row(w0), _blockdiag2(w2).astype(BF16), _row(a0),
              _blockdiag2(a2).astype(BF16), g2.astype(BF16), _row(k_k), _row(k_a), _row(r_k)]
    full = lambda a: pl.BlockSpec(a.shape, lambda bi, i: (0,) * a.ndim)
    one = pl.BlockSpec((1, ts, c), lambda bi, i: (bi, i, 0))
    two = pl.BlockSpec((2, 1, ts, c), lambda bi, i: (0, bi, i, 0))
    s1 = jax.ShapeDtypeStruct((b, s, c), F32)
    s2 = jax.ShapeDtypeStruct((2, b, s, c), F32)
    return pl.pallas_call(
        functools.partial(_rwprep_kernel, nblk=nblk),
        grid=(b, nblk),
        in_specs=[cur, prv, nxt] + [full(p) for p in params],
        out_specs=[one, one, one, two, two, two, one, one],
        out_shape=[s1, s1, s1, s2, s2, s2, s1, s1],
        compiler_params=_cparams(("parallel", "parallel")),
    )(rw, rw, rw, *params)


WKV_STEP_CHUNKS = 4


WKV2_BATCH = 4


def _wkv2_kernel(rf, vf, kkf, rb, vb, kkb, kdf, bf, lwf, kdb, bb, lwb, yf_ref, yb_ref, s_ref):
    cc = WKV_CHUNK
    nb = rf.shape[0]
    nchunk = rf.shape[1] // cc

    @pl.when(pl.program_id(1) == 0)
    def _():
        s_ref[...] = jnp.zeros_like(s_ref)

    same_head = ((lax.broadcasted_iota(jnp.int32, (LANES, LANES), 0) < HEAD_DIM)
                 == (lax.broadcasted_iota(jnp.int32, (LANES, LANES), 1) < HEAD_DIM))
    rowc = lax.broadcasted_iota(jnp.int32, (cc, LANES), 0)
    colc = lax.broadcasted_iota(jnp.int32, (cc, LANES), 1) % cc
    eye = (colc == rowc).astype(F32)
    strict = (colc < rowc, colc > rowc)
    incl = (colc <= rowc, colc >= rowc)
    first = lax.broadcasted_iota(jnp.int32, (1, LANES), 1) < HEAD_DIM
    second = jnp.logical_not(first)

    def stack_heads(t):
        z = jnp.zeros_like(t)
        return jnp.concatenate([jnp.where(first, t, z), jnp.where(second, t, z)], axis=0)

    def make_chains(c):
        chains = []
        for bi in range(nb):
            for d in range(2):
                j = nchunk - 1 - c if d else c
                sl = slice(j * cc, (j + 1) * cc)
                src = (rb, vb, kkb, kdb, bb, lwb) if d else (rf, vf, kkf, kdf, bf, lwf)
                for p in range(RW_HEADS // 2):
                    ls = slice(p * LANES, (p + 1) * LANES)
                    chains.append(dict(
                        bi=bi, d=d, p=p, sl=sl, ls=ls,
                        r=src[0][bi, sl, ls], v=src[1][bi, sl, ls], kk=src[2][bi, sl, ls],
                        kd=src[3][0, bi, sl, ls], b=src[4][0, bi, sl, ls], lw=src[5][0, bi, sl, ls]))
        return chains

    def solve_chunks(chains):
        for ch in chains:
            cum = ch["lw"]
            sh = 1
            while sh < cc:
                if ch["d"]:
                    cum = cum + jnp.where(rowc < cc - sh, pltpu.roll(cum, cc - sh, 0), 0.0)
                else:
                    cum = cum + jnp.where(rowc >= sh, pltpu.roll(cum, sh, 0), 0.0)
                sh *= 2
            tot = cum[0:1] if ch["d"] else cum[cc - 1:cc]
            at = (-ch["kk"] * jnp.exp(cum - ch["lw"])).astype(BF16)
            einv = jnp.exp(-cum)
            eh = jnp.exp(tot - cum)
            ch["etot"] = jnp.exp(tot)
            ch["rt"] = (ch["r"] * jnp.exp(cum)).astype(BF16)
            ch["vb"] = ch["v"].astype(BF16)
            ch["v_st"] = stack_heads(ch["vb"])
            ch["at_st"] = stack_heads(at)
            ch["lhs"] = jnp.concatenate([at, ch["rt"]], axis=0)
            ch["rhs"] = jnp.concatenate([stack_heads((ch["b"] * einv).astype(BF16)),
                                         stack_heads((ch["kd"] * einv).astype(BF16))], axis=0)
            ch["kb_hat"] = jnp.concatenate([ch["kd"] * eh, ch["b"] * eh], axis=0).astype(BF16)
        for ch in chains:
            m1 = _dot_nt(ch["lhs"], ch["rhs"])
            d = ch["d"]
            ch["n"] = jnp.where(strict[d], m1[:cc, :LANES], 0.0)
            a_ak = jnp.where(strict[d], m1[:cc, LANES:], 0.0)
            ch["a_rb"] = jnp.where(incl[d], m1[cc:, :LANES], 0.0).astype(BF16)
            a_rk = jnp.where(incl[d], m1[cc:, LANES:], 0.0)
            ch["a_k"] = jnp.concatenate([a_ak, a_rk], axis=0).astype(BF16)
        for ch in chains:
            nbf = ch["n"].astype(BF16)
            ch["pw"] = _dot(nbf, stack_heads(nbf))
            ch["t"] = eye + ch["n"]
        for _ in range(cc.bit_length() - 3):
            for ch in chains:
                pwb = ch["pw"].astype(BF16)
                tp = _dot(jnp.concatenate([ch["t"].astype(BF16), pwb], axis=0), stack_heads(pwb))
                ch["t"] = ch["t"] + tp[:cc]
                ch["pw"] = tp[cc:]
        for ch in chains:
            ch["t"] = (ch["t"] + _dot(ch["t"].astype(BF16), stack_heads(ch["pw"].astype(BF16)))).astype(BF16)
        for ch in chains:
            ch["av"] = _dot(ch["a_k"], ch["v_st"])
        for ch in chains:
            akv = stack_heads(ch["av"][:cc].astype(BF16))
            tx = _dot(ch["t"], jnp.concatenate([ch["at_st"], akv], axis=1))
            ch["atp"] = tx[:, :LANES].astype(BF16)
            ch["u0"] = tx[:, LANES:]

    def advance_state(chains):
        for ch in chains:
            ch["st"] = s_ref[ch["bi"], ch["d"], ch["p"]]
            qs = _dot_nt(jnp.concatenate([ch["atp"], ch["rt"]], axis=0), ch["st"].astype(BF16))
            ch["u"] = (qs[:cc] + ch["u0"]).astype(BF16)
            ch["ys"] = qs[cc:] + ch["av"][cc:]
        for ch in chains:
            upd = _dot_tn(jnp.concatenate([ch["vb"], ch["u"]], axis=0), ch["kb_hat"])
            s_ref[ch["bi"], ch["d"], ch["p"]] = ch["st"] * ch["etot"] + jnp.where(same_head, upd, 0.0)
        for ch in chains:
            y_ref = yb_ref if ch["d"] else yf_ref
            y_ref[ch["bi"], ch["sl"], ch["ls"]] = ch["ys"] + _dot(ch["a_rb"], stack_heads(ch["u"]))

    per_chunk = [make_chains(c) for c in range(nchunk)]
    solve_chunks([ch for chains in per_chunk for ch in chains])
    for chains in per_chunk:
        advance_state(chains)


def _wkv2(r, v, kk, kd, bb, lw):
    b, s, c = r.shape
    tt = WKV_CHUNK * WKV_STEP_CHUNKS
    nblk = s // tt
    nb = WKV2_BATCH if b % WKV2_BATCH == 0 else 1
    fwd = pl.BlockSpec((nb, tt, c), lambda bi, i: (bi, i, 0))
    bwd = pl.BlockSpec((nb, tt, c), lambda bi, i: (bi, nblk - 1 - i, 0))
    fwd2 = pl.BlockSpec((1, nb, tt, c), lambda bi, i: (0, bi, i, 0))
    bwd2 = pl.BlockSpec((1, nb, tt, c), lambda bi, i: (1, bi, nblk - 1 - i, 0))
    return pl.pallas_call(
        _wkv2_kernel,
        grid=(b // nb, nblk),
        in_specs=[fwd, fwd, fwd, bwd, bwd, bwd, fwd2, fwd2, fwd2, bwd2, bwd2, bwd2],
        out_specs=[fwd, bwd],
        out_shape=[jax.ShapeDtypeStruct((b, s, c), F32)] * 2,
        scratch_shapes=[pltpu.VMEM((nb, 2, RW_HEADS // 2, LANES, LANES), F32)],
        compiler_params=_cparams(("parallel", "arbitrary")),
    )(r, v, kk, r, v, kk, kd, bb, lw, kd, bb, lw)


def _memkv_kernel(mem_ref, nw_ref, w_ref, kw_ref, k_ref, v_ref):
    x = mem_ref[0]
    ms = jnp.mean(x * x, axis=-1, keepdims=True)
    h = (x * lax.rsqrt(ms + RMS_EPS) * nw_ref[...]).astype(BF16)
    kv = _dot(h, w_ref[...])
    k = kv[:, :MEM_WIDTH]
    ms = _split_dot(k * k, _head_ones(MEM_WIDTH)) * (1.0 / HEAD_DIM)
    k_ref[0] = (k * lax.rsqrt(ms + RMS_EPS) * kw_ref[...]).astype(BF16)
    v_ref[0] = kv[:, MEM_WIDTH:].astype(BF16)


def _memkv(mem, nw, w_kv, kw):
    b, m, d = mem.shape
    full = lambda a: pl.BlockSpec(a.shape, lambda bi: (0,) * a.ndim)
    out = pl.BlockSpec((1, m, MEM_WIDTH), lambda bi: (bi, 0, 0))
    return pl.pallas_call(
        _memkv_kernel,
        grid=(b,),
        in_specs=[pl.BlockSpec((1, m, d), lambda bi: (bi, 0, 0)), full(nw), full(w_kv), full(kw)],
        out_specs=[out, out],
        out_shape=[jax.ShapeDtypeStruct((b, m, MEM_WIDTH), BF16)] * 2,
        compiler_params=_cparams(("parallel",)),
    )(mem, nw, w_kv, kw)


def _mixout_kernel(x_ref, na_ref, yf_ref, yb_ref, bonus_ref, g_ref, mq_ref, mk_ref, mv_ref,
                   lnw_ref, lnb_ref, wo_ref, fnw_ref, whi_ref, wlo_ref, rb_ref,
                   o_ref, h_ref, route_ref, nch_ref):
    ones = _head_ones(RW_WIDTH)
    y = yf_ref[0] + yb_ref[0]
    mu = _split_dot(y, ones) * (1.0 / HEAD_DIM)
    yc = y - mu
    var = _split_dot(yc * yc, ones) * (1.0 / HEAD_DIM)
    yn = yc * lax.rsqrt(var + RW_GN_EPS) * lnw_ref[...] + lnb_ref[...]
    y_rw = ((yn + bonus_ref[0]) * g_ref[0]).astype(BF16)

    first = lax.broadcasted_iota(jnp.int32, (1, LANES), 1) < HEAD_DIM
    mems = []
    for p in range(MEM_HEADS // 2):
        ls = slice(p * LANES, (p + 1) * LANES)
        qp = mq_ref[0, :, ls]
        kp = mk_ref[0, :, ls]
        vp = mv_ref[0, :, ls]
        outs = []
        for hh in range(2):
            sel = first if hh == 0 else jnp.logical_not(first)
            s = _dot_nt(jnp.where(sel, qp, jnp.zeros_like(qp)), kp)
            m = jnp.max(s, axis=-1, keepdims=True)
            e = jnp.exp(s - m)
            l = jnp.sum(e, axis=-1, keepdims=True)
            outs.append(_dot(e.astype(BF16), vp) / l)
        mems.append(jnp.where(first, outs[0], outs[1]).astype(BF16))
    y_mem = jnp.concatenate(mems, axis=1)

    acc = _dot(na_ref[0], wo_ref[0:NA_WIDTH, :])
    acc = acc + _dot(y_rw, wo_ref[NA_WIDTH:NA_WIDTH + RW_WIDTH, :])
    acc = acc + _dot(y_mem, wo_ref[NA_WIDTH + RW_WIDTH:, :])
    x_mid = x_ref[0] + acc
    o_ref[0] = x_mid
    _route_tile(x_mid, fnw_ref, whi_ref, wlo_ref, rb_ref, h_ref, route_ref, nch_ref)


def _mixout(x, y_na, y_f, y_b, bonus, g, mq, mk, mv, ln_w, ln_b, w_out, ffn_nw, router, tm):
    b, s, d = x.shape
    m = mk.shape[1]
    sub = tm // MOE_SUB
    tok = lambda w: pl.BlockSpec((1, tm, w), lambda bi, i: (bi, i, 0))
    full = lambda a: pl.BlockSpec(a.shape, lambda bi, i: (0,) * a.ndim)
    memb = pl.BlockSpec((1, m, MEM_WIDTH), lambda bi, i: (bi, 0, 0))
    x_mid, h2, route, nch = pl.pallas_call(
        _mixout_kernel,
        grid=(b, s // tm),
        in_specs=[tok(d), tok(NA_WIDTH), tok(RW_WIDTH), tok(RW_WIDTH), tok(RW_WIDTH), tok(RW_WIDTH),
                  tok(MEM_WIDTH), memb, memb, full(ln_w), full(ln_b), full(w_out), full(ffn_nw)]
        + [full(p) for p in router],
        out_specs=[tok(d), tok(d),
                   pl.BlockSpec((1, sub, ROUTE_ROWS, MOE_SUB), lambda bi, i: (bi, i, 0, 0)),
                   pl.BlockSpec((1, sub, 1, ROUTER_LANES), lambda bi, i: (bi, i, 0, 0))],
        out_shape=[jax.ShapeDtypeStruct((b, s, d), F32), jax.ShapeDtypeStruct((b, s, d), BF16),
                   jax.ShapeDtypeStruct((b, s // MOE_SUB, ROUTE_ROWS, MOE_SUB), F32),
                   jax.ShapeDtypeStruct((b, s // MOE_SUB, 1, ROUTER_LANES), F32)],
        compiler_params=_cparams(("parallel", "parallel")),
    )(x, y_na, y_f, y_b, bonus, g, mq, mk, mv, ln_w, ln_b, w_out, ffn_nw, *router)
    ns = b * s // MOE_SUB
    return (x_mid, h2.reshape(b * s, d), route.reshape(ns, ROUTE_ROWS, MOE_SUB),
            nch.reshape(ns, 1, ROUTER_LANES))


ROUTER_LANES = 128


MOE_SUB = 256
MOE_CHUNK = 16
MOE_LOCAL_CHUNKS = 48
MOE_BLOCK_CHUNKS = 32
MOE_TOP_K = 2
ROUTE_ROWS = 8
assert MOE_LOCAL_CHUNKS >= MOE_TOP_K * MOE_SUB // MOE_CHUNK + MOE_EXPERTS - 1


def _route_subtile(lg):
    t = lg.shape[0]
    lane = lax.broadcasted_iota(jnp.int32, (1, ROUTER_LANES), 1)
    lanef = lane.astype(F32)
    big = float(ROUTER_LANES)

    def argmax_lane(v):
        m = jnp.max(v, axis=-1, keepdims=True)
        return m, jnp.min(jnp.where(v == m, lanef, big), axis=-1, keepdims=True)

    is_group = lane < MOE_GROUPS
    gmax, gidx = argmax_lane(jnp.where(is_group, lg, NEG_BIG))
    gsum = jnp.sum(jnp.where(is_group, jnp.exp(lg - gmax), 0.0), axis=-1, keepdims=True)
    lo = MOE_GROUPS + MOE_EXPERTS_PER_GROUP * gidx
    el = jnp.where((lanef >= lo) & (lanef < lo + MOE_EXPERTS_PER_GROUP), lg, NEG_BIG)
    m1, i1 = argmax_lane(el)
    m2, i2 = argmax_lane(jnp.where(lanef == i1, NEG_BIG, el))
    ratio = jnp.exp(m2 - m1)
    w0 = 1.0 / (gsum * (1.0 + ratio))
    w1 = w0 * ratio

    oh0 = (lanef == i1 - MOE_GROUPS).astype(F32)
    oh1 = (lanef == i2 - MOE_GROUPS).astype(F32)
    row = lax.broadcasted_iota(jnp.int32, (t, t), 0)
    col = lax.broadcasted_iota(jnp.int32, (t, t), 1)
    before = (col < row).astype(BF16)
    cnt0 = jnp.sum(oh0, axis=0, keepdims=True)
    cnt1 = jnp.sum(oh1, axis=0, keepdims=True)
    rank0 = _dot(before, oh0.astype(BF16))
    rank1 = _dot(before, oh1.astype(BF16)) + cnt0
    nch = jnp.floor((cnt0 + cnt1 + (MOE_CHUNK - 1.0)) * (1.0 / MOE_CHUNK))
    li = lax.broadcasted_iota(jnp.int32, (ROUTER_LANES, ROUTER_LANES), 0)
    lj = lax.broadcasted_iota(jnp.int32, (ROUTER_LANES, ROUTER_LANES), 1)
    start = _dot(jnp.broadcast_to(nch, (8, ROUTER_LANES)).astype(BF16), (li < lj).astype(BF16))[0:1]
    pos0 = jnp.sum((start * MOE_CHUNK + rank0) * oh0, axis=-1, keepdims=True)
    pos1 = jnp.sum((start * MOE_CHUNK + rank1) * oh1, axis=-1, keepdims=True)
    rec = (jnp.where(lane == 0, pos0, 0.0) + jnp.where(lane == 1, pos1, 0.0)
           + jnp.where(lane == 2, w0, 0.0) + jnp.where(lane == 3, w1, 0.0))
    return jnp.transpose(rec)[0:ROUTE_ROWS], nch


def _route_tile(x, nw_ref, whi_ref, wlo_ref, b_ref, h_ref, route_ref, nch_ref):
    ms = jnp.mean(x * x, axis=-1, keepdims=True)
    h = x * lax.rsqrt(ms + RMS_EPS) * nw_ref[...]
    hi = h.astype(BF16)
    lo = (h - hi.astype(F32)).astype(BF16)
    h_ref[0] = hi
    lg = _dot(hi, whi_ref[...]) + _dot(hi, wlo_ref[...]) + _dot(lo, whi_ref[...]) + b_ref[...]
    for j in range(x.shape[0] // MOE_SUB):
        route_ref[0, j], nch_ref[0, j] = _route_subtile(lg[j * MOE_SUB:(j + 1) * MOE_SUB])


def _router_params(w_group, b_group, w_expert, b_expert):
    d = w_group.shape[0]
    pad = ROUTER_LANES - MOE_GROUPS - MOE_EXPERTS
    w = jnp.concatenate([w_group, w_expert, jnp.zeros((d, pad), F32)], axis=1)
    whi = w.astype(BF16)
    wlo = (w - whi.astype(F32)).astype(BF16)
    bias = jnp.concatenate([b_group, b_expert, jnp.zeros((pad,), F32)]).reshape(1, -1)
    return whi, wlo, bias


def _moe_plan(nch, n):
    nch = nch[:, 0, :MOE_EXPERTS].astype(jnp.int32)
    lc_end = jnp.cumsum(nch, axis=1)
    lc_start = lc_end - nch
    nloc = lc_end[:, -1]
    nblk = (jnp.sum(nch, axis=0) + MOE_BLOCK_CHUNKS - 1) // MOE_BLOCK_CHUNKS
    bend = jnp.cumsum(nblk)
    gch = ((bend - nblk) * MOE_BLOCK_CHUNKS)[None, :] + jnp.cumsum(nch, axis=0) - nch
    c = jnp.arange(MOE_LOCAL_CHUNKS, dtype=jnp.int32)
    in_run = (c[None, :, None] >= lc_start[:, None, :]) & (c[None, :, None] < lc_end[:, None, :])
    dest = jnp.sum(jnp.where(in_run, (gch - lc_start)[:, None, :] + c[None, :, None], 0), axis=-1)

    nblocks = _moe_blocks(n)
    b = jnp.arange(nblocks, dtype=jnp.int32)
    block_e = jnp.minimum(jnp.sum(b[:, None] >= bend[None, :], axis=-1), MOE_EXPERTS - 1)
    tot = jnp.sum(nch, axis=0)
    tail_start = (bend - nblk) * MOE_BLOCK_CHUNKS + tot
    ntail = nblk * MOE_BLOCK_CHUNKS - tot
    i32 = lambda a: a.astype(jnp.int32)
    return i32(dest), i32(nloc), i32(block_e), i32(bend[-1:]), i32(tail_start), i32(ntail)


def _moe_blocks(n):
    ns = n // MOE_SUB
    chunks = ns * (MOE_TOP_K * MOE_SUB // MOE_CHUNK + MOE_EXPERTS - 1) + MOE_EXPERTS * (MOE_BLOCK_CHUNKS - 1)
    return -(-chunks // MOE_BLOCK_CHUNKS)


def _local_onehot(pos_rows, shape, row_axis):
    idx = lax.broadcasted_iota(jnp.int32, shape, row_axis)
    return (idx == pos_rows[0]) | (idx == pos_rows[1])


def _dispatch_kernel(dest_ref, nloc_ref, tail_ref, ntail_ref, nv_ref, h_ref, pos_ref, xs_ref,
                     buf, zbuf, sem, zsem):
    s = pl.program_id(0)
    ns = pl.num_programs(0)
    slot = s % 2
    bm = MOE_BLOCK_CHUNKS * MOE_CHUNK
    nblocks = xs_ref.shape[0] // bm

    def zero_fill(wait):
        def tail_copy(e, i):
            dst = xs_ref.at[pl.ds(pl.multiple_of((tail_ref[e] + i) * MOE_CHUNK, MOE_CHUNK), MOE_CHUNK)]
            return pltpu.make_async_copy(zbuf.at[pl.ds(0, MOE_CHUNK)], dst, zsem.at[0])

        def block_copy(b):
            return pltpu.make_async_copy(zbuf, xs_ref.at[pl.ds(pl.multiple_of(b * bm, bm), bm)], zsem.at[1])

        def run(cp):
            return cp.wait() if wait else cp.start()

        for e in range(MOE_EXPERTS):
            def tail_body(i, carry, e=e):
                run(tail_copy(e, i))
                return carry
            lax.fori_loop(0, ntail_ref[e], tail_body, 0)

        def block_body(b, carry):
            run(block_copy(b))
            return carry
        lax.fori_loop(nv_ref[0], nblocks, block_body, 0)

    @pl.when(s == 0)
    def _():
        zbuf[...] = jnp.zeros_like(zbuf)
        zero_fill(wait=False)

    def chunk_copy(sl, c, step):
        src = buf.at[sl, pl.ds(pl.multiple_of(c * MOE_CHUNK, MOE_CHUNK), MOE_CHUNK)]
        dst = xs_ref.at[pl.ds(pl.multiple_of(dest_ref[step, c] * MOE_CHUNK, MOE_CHUNK), MOE_CHUNK)]
        return pltpu.make_async_copy(src, dst, sem.at[sl])

    def wait_step(step, sl):
        def body(c, carry):
            chunk_copy(sl, c, step).wait()
            return carry
        lax.fori_loop(0, nloc_ref[step], body, 0)

    @pl.when(s >= 2)
    def _():
        wait_step(s - 2, slot)

    pos = pos_ref[0, 0:MOE_TOP_K].astype(jnp.int32)
    onehot = _local_onehot((pos[0:1], pos[1:2]), (MOE_LOCAL_CHUNKS * MOE_CHUNK, MOE_SUB), 0)
    buf[slot] = _dot(jnp.where(onehot, 1.0, 0.0).astype(BF16), h_ref[...]).astype(BF16)

    def issue(c, carry):
        chunk_copy(slot, c, s).start()
        return carry
    lax.fori_loop(0, nloc_ref[s], issue, 0)

    @pl.when(s == ns - 1)
    def _():
        @pl.when(s >= 1)
        def _():
            wait_step(s - 1, 1 - slot)
        wait_step(s, slot)
        zero_fill(wait=True)


def _dispatch(h2, pos, dest, nloc, tail_start, ntail, nvalid):
    n, d = h2.shape
    ns = n // MOE_SUB
    bm = MOE_BLOCK_CHUNKS * MOE_CHUNK
    rows = MOE_LOCAL_CHUNKS * MOE_CHUNK
    return pl.pallas_call(
        _dispatch_kernel,
        grid_spec=pltpu.PrefetchScalarGridSpec(
            num_scalar_prefetch=5, grid=(ns,),
            in_specs=[pl.BlockSpec((MOE_SUB, d), lambda s, *_: (s, 0)),
                      pl.BlockSpec((1, ROUTE_ROWS, MOE_SUB), lambda s, *_: (s, 0, 0))],
            out_specs=pl.BlockSpec(memory_space=pl.ANY),
            scratch_shapes=[pltpu.VMEM((2, rows, d), BF16), pltpu.VMEM((bm, d), BF16),
                            pltpu.SemaphoreType.DMA((2,)), pltpu.SemaphoreType.DMA((2,))]),
        out_shape=jax.ShapeDtypeStruct((_moe_blocks(n) * bm, d), BF16),
        compiler_params=_cparams(("arbitrary",)),
    )(dest, nloc, tail_start, ntail, nvalid, h2, pos)


def _experts_kernel(be_ref, nv_ref, x_ref, wg_ref, wu_ref, wd_ref, o_ref):
    del be_ref
    valid = pl.program_id(0) < nv_ref[0]

    @pl.when(valid)
    def _():
        x = x_ref[...]
        hid = jax.nn.silu(_dot(x, wg_ref[0])) * _dot(x, wu_ref[0])
        o_ref[...] = _dot(hid.astype(BF16), wd_ref[0]).astype(o_ref.dtype)

    @pl.when(jnp.logical_not(valid))
    def _():
        o_ref[...] = jnp.zeros_like(o_ref)


def _experts(xs, block_e, nvalid, w_gate, w_up, w_down):
    cap, d = xs.shape
    bm = MOE_BLOCK_CHUNKS * MOE_CHUNK
    blk = lambda b, be, nv: (jnp.minimum(b, nv[0] - 1), 0)
    wsel = lambda b, be, nv: (be[jnp.minimum(b, nv[0] - 1)], 0, 0)
    return pl.pallas_call(
        _experts_kernel,
        grid_spec=pltpu.PrefetchScalarGridSpec(
            num_scalar_prefetch=2, grid=(cap // bm,),
            in_specs=[pl.BlockSpec((bm, d), blk),
                      pl.BlockSpec((1, d, MOE_D_FF), wsel),
                      pl.BlockSpec((1, d, MOE_D_FF), wsel),
                      pl.BlockSpec((1, MOE_D_FF, d), wsel)],
            out_specs=pl.BlockSpec((bm, d), lambda b, be, nv: (b, 0))),
        out_shape=jax.ShapeDtypeStruct((cap, d), BF16),
        compiler_params=_cparams(("arbitrary",)),
    )(block_e, nvalid, xs, w_gate, w_up, w_down)


def _combine_kernel(dest_ref, nloc_ref, x_ref, route_ref, ys_ref, o_ref, buf, sem):
    s = pl.program_id(0)
    ns = pl.num_programs(0)
    slot = s % 2

    def chunk_copy(sl, c, step):
        src = ys_ref.at[pl.ds(pl.multiple_of(dest_ref[step, c] * MOE_CHUNK, MOE_CHUNK), MOE_CHUNK)]
        dst = buf.at[sl, pl.ds(pl.multiple_of(c * MOE_CHUNK, MOE_CHUNK), MOE_CHUNK)]
        return pltpu.make_async_copy(src, dst, sem.at[sl])

    def fetch(step, sl):
        def body(c, carry):
            chunk_copy(sl, c, step).start()
            return carry
        lax.fori_loop(0, nloc_ref[step], body, 0)

    @pl.when(s == 0)
    def _():
        buf[...] = jnp.zeros_like(buf)
        fetch(0, 0)

    @pl.when(s + 1 < ns)
    def _():
        fetch(s + 1, 1 - slot)

    def wait(c, carry):
        chunk_copy(slot, c, s).wait()
        return carry
    lax.fori_loop(0, nloc_ref[s], wait, 0)

    rec = jnp.transpose(route_ref[0])
    pos = rec[:, 0:MOE_TOP_K].astype(jnp.int32)
    idx = lax.broadcasted_iota(jnp.int32, (MOE_SUB, MOE_LOCAL_CHUNKS * MOE_CHUNK), 1)
    pw = jnp.where(idx == pos[:, 0:1], rec[:, 2:3], 0.0) + jnp.where(idx == pos[:, 1:2], rec[:, 3:4], 0.0)
    ys = buf[slot]
    p_hi = pw.astype(BF16)
    p_lo = (pw - p_hi.astype(F32)).astype(BF16)
    o_ref[...] = x_ref[...] + _dot(p_hi, ys) + _dot(p_lo, ys)


def _combine(x2, ys, route, dest, nloc):
    n, d = x2.shape
    ns = n // MOE_SUB
    rows = MOE_LOCAL_CHUNKS * MOE_CHUNK
    return pl.pallas_call(
        _combine_kernel,
        grid_spec=pltpu.PrefetchScalarGridSpec(
            num_scalar_prefetch=2, grid=(ns,),
            in_specs=[pl.BlockSpec((MOE_SUB, d), lambda s, *_: (s, 0)),
                      pl.BlockSpec((1, ROUTE_ROWS, MOE_SUB), lambda s, *_: (s, 0, 0)),
                      pl.BlockSpec(memory_space=pl.ANY)],
            out_specs=pl.BlockSpec((MOE_SUB, d), lambda s, *_: (s, 0)),
            scratch_shapes=[pltpu.VMEM((2, rows, d), BF16), pltpu.SemaphoreType.DMA((2,))]),
        out_shape=jax.ShapeDtypeStruct((n, d), F32),
        compiler_params=_cparams(("arbitrary",)),
    )(dest, nloc, x2, route, ys)


def _moe(x2, h2, route, nch, w_gate, w_up, w_down):
    dest, nloc, block_e, nvalid, tail_start, ntail = _moe_plan(nch, x2.shape[0])
    xs = _dispatch(h2, route, dest, nloc, tail_start, ntail, nvalid)
    ys = _experts(xs, block_e, nvalid, w_gate, w_up, w_down)
    return _combine(x2, ys, route, dest, nloc)


def _tile(n, want):
    t = min(n, want)
    assert n % t == 0
    return t


def kernel(x, mem, attn_norm_w, w_in, na_q_norm_w, na_k_norm_w, na_rpb, rw_mu_prev, rw_mu_next, rw_w0, rw_w2, rw_a0, rw_a2, rw_g2, rw_k_k, rw_k_a, rw_r_k, rw_ln_w, rw_ln_b, mem_norm_w, w_mem_kv, mem_q_norm_w, mem_k_norm_w, w_out, ffn_norm_w, moe_w_group, moe_b_group, moe_w_expert, moe_b_expert, moe_w_gate, moe_w_up, moe_w_down):
    b, s, d = x.shape
    n = b * s
    depth = w_in.shape[0]
    assert s % (NA_BLOCK_ROWS * GRID_W) == 0 and s // GRID_W >= 2 * NA_KH
    tm = _tile(n, 512)
    ts = _tile(s, 512)
    for l in range(depth):
        q, k, v, rw, mq = _proj(
            x.reshape(n, d), _row(attn_norm_w[l]), w_in[l].astype(BF16),
            _row(jnp.tile(na_q_norm_w[l], NA_HEADS)), _row(jnp.tile(na_k_norm_w[l], NA_HEADS)),
            _row(jnp.tile(mem_q_norm_w[l], MEM_HEADS)), tm)
        y_na = _na(q.reshape(b, s, NA_WIDTH), k.reshape(b, s, NA_WIDTH), v.reshape(b, s, NA_WIDTH),
                   _na_bias_table(na_rpb[l]))
        r, vv, kk, kd, bb, lw, g, bonus = _rwprep(
            rw.reshape(b, s, RW_PROJ), rw_mu_prev[l], rw_mu_next[l], rw_w0[l], rw_w2[l], rw_a0[l],
            rw_a2[l], rw_g2[l], rw_k_k[l], rw_k_a[l], rw_r_k[l], ts)
        y_f, y_b = _wkv2(r, vv, kk, kd, bb, lw)
        mk, mv = _memkv(mem, _row(mem_norm_w[l]), w_mem_kv[l].astype(BF16),
                        _row(jnp.tile(mem_k_norm_w[l], MEM_HEADS)))
        x, h2, route, nch = _mixout(
            x, y_na, y_f, y_b, bonus, g, mq.reshape(b, s, MEM_WIDTH), mk, mv,
            _row(rw_ln_w[l]), _row(rw_ln_b[l]), w_out[l].astype(BF16), _row(ffn_norm_w[l]),
            _router_params(moe_w_group[l], moe_b_group[l], moe_w_expert[l], moe_b_expert[l]), ts)
        x = _moe(x.reshape(n, d), h2, route, nch, moe_w_gate[l].astype(BF16),
                 moe_w_up[l].astype(BF16), moe_w_down[l].astype(BF16)).reshape(b, s, d)
    return x
```

```python
import functools

import jax
import jax.numpy as jnp
import numpy as np
from jax import lax
from jax.experimental import pallas as pl
from jax.experimental.pallas import tpu as pltpu

F32 = jnp.float32
BF16 = jnp.bfloat16

GRID_W = 64
HEAD_DIM = 64
NA_HEADS = 8
NA_WIDTH = NA_HEADS * HEAD_DIM
NA_KH = 8
NA_KW = 16
RW_HEADS = 4
RW_WIDTH = RW_HEADS * HEAD_DIM
RW_LORA_W = 64
RW_LORA_A = 64
RW_LORA_G = 128
RW_PROJ = 3 * RW_WIDTH + 2 * RW_LORA_W + 2 * RW_LORA_A + RW_LORA_G
MEM_HEADS = 4
MEM_WIDTH = MEM_HEADS * HEAD_DIM
MOE_GROUPS = 4
MOE_EXPERTS_PER_GROUP = 4
MOE_EXPERTS = MOE_GROUPS * MOE_EXPERTS_PER_GROUP
MOE_D_FF = 512
RMS_EPS = 1e-6
RW_GN_EPS = 64e-5

LANES = 128
WKV_CHUNK = 64
NEG_BIG = -1e30
VMEM_LIMIT = 56 * 1024 * 1024


def _cparams(sem):
    return pltpu.CompilerParams(dimension_semantics=sem, vmem_limit_bytes=VMEM_LIMIT)


def _dot(a, b):
    return jnp.dot(a, b, preferred_element_type=F32)


def _dot_nt(a, b):
    return lax.dot_general(a, b, (((1,), (1,)), ((), ())), preferred_element_type=F32)


def _dot_tn(a, b):
    return lax.dot_general(a, b, (((0,), (0,)), ((), ())), preferred_element_type=F32)


def _split_dot(t, m):
    hi = t.astype(BF16)
    lo = (t - hi.astype(F32)).astype(BF16)
    return _dot(hi, m) + _dot(lo, m)


def _head_ones(width):
    i = lax.broadcasted_iota(jnp.int32, (width, width), 0) // HEAD_DIM
    j = lax.broadcasted_iota(jnp.int32, (width, width), 1) // HEAD_DIM
    return (i == j).astype(BF16)


def _row(v):
    return v.reshape(1, -1).astype(F32)


def _proj_kernel(x_ref, nw_ref, w_ref, qw_ref, kw_ref, mw_ref, q_ref, k_ref, v_ref, rw_ref, mq_ref):
    x = x_ref[...]
    ms = jnp.mean(x * x, axis=-1, keepdims=True)
    h = (x * lax.rsqrt(ms + RMS_EPS) * nw_ref[...]).astype(BF16)
    ones_na = _head_ones(NA_WIDTH)

    def head_norm(t, w, ones):
        ms = _dot((t * t).astype(BF16), ones) * (1.0 / HEAD_DIM)
        return t * lax.rsqrt(ms + RMS_EPS) * w

    o = 0
    q = _dot(h, w_ref[:, o:o + NA_WIDTH])
    q_ref[...] = (head_norm(q, qw_ref[...], ones_na) * (HEAD_DIM ** -0.5)).astype(BF16)
    o += NA_WIDTH
    k = _dot(h, w_ref[:, o:o + NA_WIDTH])
    k_ref[...] = head_norm(k, kw_ref[...], ones_na).astype(BF16)
    o += NA_WIDTH
    v_ref[...] = _dot(h, w_ref[:, o:o + NA_WIDTH]).astype(BF16)
    o += NA_WIDTH
    rw_ref[...] = _dot(h, w_ref[:, o:o + RW_PROJ])
    o += RW_PROJ
    mq = _dot(h, w_ref[:, o:o + MEM_WIDTH])
    mq_ref[...] = (head_norm(mq, mw_ref[...], _head_ones(MEM_WIDTH)) * (HEAD_DIM ** -0.5)).astype(BF16)


def _proj(x2, nw, w_in, qw, kw, mw, tm):
    n, d = x2.shape
    p_in = w_in.shape[1]
    tok = lambda w: pl.BlockSpec((tm, w), lambda i: (i, 0))
    full = lambda a: pl.BlockSpec(a.shape, lambda i: (0,) * a.ndim)
    return pl.pallas_call(
        _proj_kernel,
        grid=(n // tm,),
        in_specs=[tok(d), full(nw), full(w_in), full(qw), full(kw), full(mw)],
        out_specs=[tok(NA_WIDTH), tok(NA_WIDTH), tok(NA_WIDTH), tok(RW_PROJ), tok(MEM_WIDTH)],
        out_shape=[jax.ShapeDtypeStruct((n, NA_WIDTH), BF16)] * 3
        + [jax.ShapeDtypeStruct((n, RW_PROJ), F32), jax.ShapeDtypeStruct((n, MEM_WIDTH), BF16)],
        compiler_params=_cparams(("parallel",)),
    )(x2, nw, w_in, qw, kw, mw)


NA_BLOCK_ROWS = 8


def _na_bias_table(rpb):
    col = np.arange(GRID_W)
    cs = np.clip(col - NA_KW // 2, 0, GRID_W - NA_KW)
    dc = col[None, :] - col[:, None]
    inside = (col[None, :] >= cs[:, None]) & (col[None, :] < cs[:, None] + NA_KW)
    dci = np.clip(dc + NA_KW - 1, 0, 2 * NA_KW - 2)
    off = np.arange(NA_KH) - (NA_KH - 1)
    dri = off[:, None] + np.arange(NA_KH)[None, :] + NA_KH - 1
    row_sel = (dri[:, :, None] == np.arange(2 * NA_KH - 1)).astype(np.float32)
    col_sel = (dci[:, :, None] == np.arange(2 * NA_KW - 1)).astype(np.float32)
    t = jnp.einsum("hrq,onr,cdq->ohcnd", rpb.astype(F32), row_sel, col_sel,
                   precision=lax.Precision.HIGHEST)
    t = jnp.where(inside[None, None, :, None, :], t, NEG_BIG)
    return t.reshape(NA_KH, NA_HEADS // 2, 2 * GRID_W, NA_KH * GRID_W).astype(F32)


def _na_kernel(q_ref, kp_ref, kc_ref, kn_ref, vp_ref, vc_ref, vn_ref, bias_ref, o_ref, kwin, vwin, *, rows):
    j = pl.program_id(1)
    blk = NA_BLOCK_ROWS * GRID_W
    kwin[0:blk] = kp_ref[0]
    kwin[blk:2 * blk] = kc_ref[0]
    kwin[2 * blk:3 * blk] = kn_ref[0]
    vwin[0:blk] = vp_ref[0]
    vwin[blk:2 * blk] = vc_ref[0]
    vwin[2 * blk:3 * blk] = vn_ref[0]
    lane = lax.broadcasted_iota(jnp.int32, (1, LANES), 1)
    first = lane < HEAD_DIM

    def row_body(i, carry):
        r = j * NA_BLOCK_ROWS + i
        rs = jnp.clip(r - NA_KH // 2, 0, rows - NA_KH)
        oi = rs - r + (NA_KH - 1)
        start = pl.multiple_of((rs - (j - 1) * NA_BLOCK_ROWS) * GRID_W, GRID_W)
        qs = pl.multiple_of(i * GRID_W, GRID_W)
        pairs = [slice(p * LANES, (p + 1) * LANES) for p in range(NA_HEADS // 2)]
        scores = []
        for p, ls in enumerate(pairs):
            qp = q_ref[0, pl.ds(qs, GRID_W), ls]
            zero = jnp.zeros_like(qp)
            qst = jnp.concatenate([jnp.where(first, qp, zero), jnp.where(first, zero, qp)], axis=0)
            scores.append(_dot_nt(qst, kwin[pl.ds(start, NA_KH * GRID_W), ls]) + bias_ref[oi, p])
        probs, norms = [], []
        for s in scores:
            e = jnp.exp(s - jnp.max(s, axis=-1, keepdims=True))
            norms.append(jnp.sum(e, axis=-1, keepdims=True))
            probs.append(e.astype(BF16))
        for ls, e, l in zip(pairs, probs, norms):
            o = _dot(e, vwin[pl.ds(start, NA_KH * GRID_W), ls]) / l
            o_ref[0, pl.ds(qs, GRID_W), ls] = jnp.where(first, o[:GRID_W], o[GRID_W:]).astype(o_ref.dtype)
        return carry

    lax.fori_loop(0, NA_BLOCK_ROWS, row_body, 0, unroll=True)


def _na(q, k, v, bias):
    b, s, w = q.shape
    rows = s // GRID_W
    nblk = rows // NA_BLOCK_ROWS
    blk = NA_BLOCK_ROWS * GRID_W
    cur = pl.BlockSpec((1, blk, w), lambda bi, j: (bi, j, 0))
    prv = pl.BlockSpec((1, blk, w), lambda bi, j: (bi, jnp.maximum(j - 1, 0), 0))
    nxt = pl.BlockSpec((1, blk, w), lambda bi, j: (bi, jnp.minimum(j + 1, nblk - 1), 0))
    return pl.pallas_call(
        functools.partial(_na_kernel, rows=rows),
        grid=(b, nblk),
        in_specs=[cur, prv, cur, nxt, prv, cur, nxt,
                  pl.BlockSpec(bias.shape, lambda bi, j: (0, 0, 0, 0))],
        out_specs=cur,
        out_shape=jax.ShapeDtypeStruct((b, s, w), BF16),
        scratch_shapes=[pltpu.VMEM((3 * blk, w), BF16), pltpu.VMEM((3 * blk, w), BF16)],
        compiler_params=_cparams(("parallel", "parallel")),
    )(q, k, k, k, v, v, v, bias)


def _rwprep_kernel(rw_ref, prev_ref, next_ref, mup_ref, mun_ref, w0_ref, w2_ref, a0_ref, a2_ref, g2_ref,
                   kk_w_ref, ka_ref, rk_ref,
                   r_ref, v_ref, kk_ref, kd_ref, b_ref, lw_ref, g_ref, bonus_ref, *, nblk):
    i = pl.program_id(1)
    s = rw_ref[0]
    ts = s.shape[0]
    rowi = lax.broadcasted_iota(jnp.int32, (ts, 1), 0)
    halo_p = jnp.where(i > 0, prev_ref[0, 7:8, :], 0.0)
    halo_n = jnp.where(i < nblk - 1, next_ref[0, 0:1, :], 0.0)
    prev = jnp.where(rowi == 0, halo_p, pltpu.roll(s, 1, 0))
    nxt = jnp.where(rowi == ts - 1, halo_n, pltpu.roll(s, ts - 1, 0))
    s = s + mup_ref[...] * (prev - s) + mun_ref[...] * (nxt - s)

    c = RW_WIDTH
    r = s[:, 0:c]
    k = s[:, c:2 * c]
    v = s[:, 2 * c:3 * c]
    lw = s[:, 3 * c:3 * c + 2 * RW_LORA_W]
    la = s[:, 3 * c + 2 * RW_LORA_W:3 * c + 2 * RW_LORA_W + 2 * RW_LORA_A]
    lg = s[:, 3 * c + 2 * RW_LORA_W + 2 * RW_LORA_A:]
    ones = _head_ones(c)

    g_ref[0] = _dot(jax.nn.sigmoid(lg).astype(BF16), g2_ref[...])
    kk = k * kk_w_ref[...]
    nrm = jnp.sqrt(_split_dot(kk * kk, ones))
    kk = kk / jnp.maximum(nrm, 1e-12)
    wl_pre = w0_ref[...] + _dot(jnp.tanh(lw).astype(BF16), w2_ref[...])
    a_all = jax.nn.sigmoid(a0_ref[...] + _dot(la.astype(BF16), a2_ref[...]))
    wl = -jax.nn.softplus(-wl_pre) - 0.5
    logdecay = -jnp.exp(wl)
    r_ref[0] = r
    v_ref[0] = v
    kk_ref[0] = kk
    kd_sum = jnp.zeros_like(k)
    for d in range(2):
        a = a_all[:, d * c:(d + 1) * c]
        kd = k * (1.0 + (a - 1.0) * ka_ref[...])
        kd_ref[d, 0] = kd
        b_ref[d, 0] = kk * a
        lw_ref[d, 0] = logdecay[:, d * c:(d + 1) * c]
        kd_sum = kd_sum + kd
    bonus_ref[0] = _split_dot(r * kd_sum * rk_ref[...], ones) * v


def _blockdiag2(m):
    z = jnp.zeros_like(m[0])
    return jnp.concatenate([jnp.concatenate([m[0], z], 1), jnp.concatenate([z, m[1]], 1)], 0)


def _rwprep(rw, mu_prev, mu_next, w0, w2, a0, a2, g2, k_k, k_a, r_k, ts):
    b, s, pw = rw.shape
    nblk = s // ts
    c = RW_WIDTH
    cur = pl.BlockSpec((1, ts, pw), lambda bi, i: (bi, i, 0))
    prv = pl.BlockSpec((1, 8, pw), lambda bi, i: (bi, jnp.maximum(i * (ts // 8) - 1, 0), 0))
    nxt = pl.BlockSpec((1, 8, pw), lambda bi, i: (bi, jnp.minimum((i + 1) * (ts // 8), s // 8 - 1), 0))
    params = [_row(mu_prev), _row(mu_next), _row(w0), _blockdiag2(w2).astype(BF16), _row(a0),
              _blockdiag2(a2).astype(BF16), g2.astype(BF16), _row(k_k), _row(k_a), _row(r_k)]
    full = lambda a: pl.BlockSpec(a.shape, lambda bi, i: (0,) * a.ndim)
    one = pl.BlockSpec((1, ts, c), lambda bi, i: (bi, i, 0))
    two = pl.BlockSpec((2, 1, ts, c), lambda bi, i: (0, bi, i, 0))
    s1 = jax.ShapeDtypeStruct((b, s, c), F32)
    s2 = jax.ShapeDtypeStruct((2, b, s, c), F32)
    return pl.pallas_call(
        functools.partial(_rwprep_kernel, nblk=nblk),
        grid=(b, nblk),
        in_specs=[cur, prv, nxt] + [full(p) for p in params],
        out_specs=[one, one, one, two, two, two, one, one],
        out_shape=[s1, s1, s1, s2, s2, s2, s1, s1],
        compiler_params=_cparams(("parallel", "parallel")),
    )(rw, rw, rw, *params)


WKV_STEP_CHUNKS = 4


WKV2_BATCH = 4


def _wkv2_kernel(rf, vf, kkf, rb, vb, kkb, kdf, bf, lwf, kdb, bb, lwb, yf_ref, yb_ref, s_ref):
    cc = WKV_CHUNK
    nb = rf.shape[0]
    nchunk = rf.shape[1] // cc

    @pl.when(pl.program_id(1) == 0)
    def _():
        s_ref[...] = jnp.zeros_like(s_ref)

    same_head = ((lax.broadcasted_iota(jnp.int32, (LANES, LANES), 0) < HEAD_DIM)
                 == (lax.broadcasted_iota(jnp.int32, (LANES, LANES), 1) < HEAD_DIM))
    rowc = lax.broadcasted_iota(jnp.int32, (cc, LANES), 0)
    colc = lax.broadcasted_iota(jnp.int32, (cc, LANES), 1) % cc
    eye = (colc == rowc).astype(F32)
    strict = (colc < rowc, colc > rowc)
    incl = (colc <= rowc, colc >= rowc)
    first = lax.broadcasted_iota(jnp.int32, (1, LANES), 1) < HEAD_DIM
    second = jnp.logical_not(first)

    def stack_heads(t):
        z = jnp.zeros_like(t)
        return jnp.concatenate([jnp.where(first, t, z), jnp.where(second, t, z)], axis=0)

    def make_chains(c):
        chains = []
        for bi in range(nb):
            for d in range(2):
                j = nchunk - 1 - c if d else c
                sl = slice(j * cc, (j + 1) * cc)
                src = (rb, vb, kkb, kdb, bb, lwb) if d else (rf, vf, kkf, kdf, bf, lwf)
                for p in range(RW_HEADS // 2):
                    ls = slice(p * LANES, (p + 1) * LANES)
                    chains.append(dict(
                        bi=bi, d=d, p=p, sl=sl, ls=ls,
                        r=src[0][bi, sl, ls], v=src[1][bi, sl, ls], kk=src[2][bi, sl, ls],
                        kd=src[3][0, bi, sl, ls], b=src[4][0, bi, sl, ls], lw=src[5][0, bi, sl, ls]))
        return chains

    def solve_chunks(chains):
        for ch in chains:
            cum = ch["lw"]
            sh = 1
            while sh < cc:
                if ch["d"]:
                    cum = cum + jnp.where(rowc < cc - sh, pltpu.roll(cum, cc - sh, 0), 0.0)
                else:
                    cum = cum + jnp.where(rowc >= sh, pltpu.roll(cum, sh, 0), 0.0)
                sh *= 2
            tot = cum[0:1] if ch["d"] else cum[cc - 1:cc]
            at = (-ch["kk"] * jnp.exp(cum - ch["lw"])).astype(BF16)
            einv = jnp.exp(-cum)
            eh = jnp.exp(tot - cum)
            ch["etot"] = jnp.exp(tot)
            ch["rt"] = (ch["r"] * jnp.exp(cum)).astype(BF16)
            ch["vb"] = ch["v"].astype(BF16)
            ch["v_st"] = stack_heads(ch["vb"])
            ch["at_st"] = stack_heads(at)
            ch["lhs"] = jnp.concatenate([at, ch["rt"]], axis=0)
            ch["rhs"] = jnp.concatenate([stack_heads((ch["b"] * einv).astype(BF16)),
                                         stack_heads((ch["kd"] * einv).astype(BF16))], axis=0)
            ch["kb_hat"] = jnp.concatenate([ch["kd"] * eh, ch["b"] * eh], axis=0).astype(BF16)
        for ch in chains:
            m1 = _dot_nt(ch["lhs"], ch["rhs"])
            d = ch["d"]
            ch["n"] = jnp.where(strict[d], m1[:cc, :LANES], 0.0)
            a_ak = jnp.where(strict[d], m1[:cc, LANES:], 0.0)
            ch["a_rb"] = jnp.where(incl[d], m1[cc:, :LANES], 0.0).astype(BF16)
            a_rk = jnp.where(incl[d], m1[cc:, LANES:], 0.0)
            ch["a_k"] = jnp.concatenate([a_ak, a_rk], axis=0).astype(BF16)
        for ch in chains:
            nbf = ch["n"].astype(BF16)
            ch["pw"] = _dot(nbf, stack_heads(nbf))
            ch["t"] = eye + ch["n"]
        for _ in range(cc.bit_length() - 3):
            for ch in chains:
                pwb = ch["pw"].astype(BF16)
                tp = _dot(jnp.concatenate([ch["t"].astype(BF16), pwb], axis=0), stack_heads(pwb))
                ch["t"] = ch["t"] + tp[:cc]
                ch["pw"] = tp[cc:]
        for ch in chains:
            ch["t"] = (ch["t"] + _dot(ch["t"].astype(BF16), stack_heads(ch["pw"].astype(BF16)))).astype(BF16)
        for ch in chains:
            ch["av"] = _dot(ch["a_k"], ch["v_st"])
        for ch in chains:
            akv = stack_heads(ch["av"][:cc].astype(BF16))
            tx = _dot(ch["t"], jnp.concatenate([ch["at_st"], akv], axis=1))
            ch["atp"] = tx[:, :LANES].astype(BF16)
            ch["u0"] = tx[:, LANES:]

    def advance_state(chains):
        for ch in chains:
            ch["st"] = s_ref[ch["bi"], ch["d"], ch["p"]]
            qs = _dot_nt(jnp.concatenate([ch["atp"], ch["rt"]], axis=0), ch["st"].astype(BF16))
            ch["u"] = (qs[:cc] + ch["u0"]).astype(BF16)
            ch["ys"] = qs[cc:] + ch["av"][cc:]
        for ch in chains:
            upd = _dot_tn(jnp.concatenate([ch["vb"], ch["u"]], axis=0), ch["kb_hat"])
            s_ref[ch["bi"], ch["d"], ch["p"]] = ch["st"] * ch["etot"] + jnp.where(same_head, upd, 0.0)
        for ch in chains:
            y_ref = yb_ref if ch["d"] else yf_ref
            y_ref[ch["bi"], ch["sl"], ch["ls"]] = ch["ys"] + _dot(ch["a_rb"], stack_heads(ch["u"]))

    per_chunk = [make_chains(c) for c in range(nchunk)]
    solve_chunks([ch for chains in per_chunk for ch in chains])
    for chains in per_chunk:
        advance_state(chains)


def _wkv2(r, v, kk, kd, bb, lw):
    b, s, c = r.shape
    tt = WKV_CHUNK * WKV_STEP_CHUNKS
    nblk = s // tt
    nb = WKV2_BATCH if b % WKV2_BATCH == 0 else 1
    fwd = pl.BlockSpec((nb, tt, c), lambda bi, i: (bi, i, 0))
    bwd = pl.BlockSpec((nb, tt, c), lambda bi, i: (bi, nblk - 1 - i, 0))
    fwd2 = pl.BlockSpec((1, nb, tt, c), lambda bi, i: (0, bi, i, 0))
    bwd2 = pl.BlockSpec((1, nb, tt, c), lambda bi, i: (1, bi, nblk - 1 - i, 0))
    return pl.pallas_call(
        _wkv2_kernel,
        grid=(b // nb, nblk),
        in_specs=[fwd, fwd, fwd, bwd, bwd, bwd, fwd2, fwd2, fwd2, bwd2, bwd2, bwd2],
        out_specs=[fwd, bwd],
        out_shape=[jax.ShapeDtypeStruct((b, s, c), F32)] * 2,
        scratch_shapes=[pltpu.VMEM((nb, 2, RW_HEADS // 2, LANES, LANES), F32)],
        compiler_params=_cparams(("parallel", "arbitrary")),
    )(r, v, kk, r, v, kk, kd, bb, lw, kd, bb, lw)


def _memkv_kernel(mem_ref, nw_ref, w_ref, kw_ref, k_ref, v_ref):
    x = mem_ref[0]
    ms = jnp.mean(x * x, axis=-1, keepdims=True)
    h = (x * lax.rsqrt(ms + RMS_EPS) * nw_ref[...]).astype(BF16)
    kv = _dot(h, w_ref[...])
    k = kv[:, :MEM_WIDTH]
    ms = _split_dot(k * k, _head_ones(MEM_WIDTH)) * (1.0 / HEAD_DIM)
    k_ref[0] = (k * lax.rsqrt(ms + RMS_EPS) * kw_ref[...]).astype(BF16)
    v_ref[0] = kv[:, MEM_WIDTH:].astype(BF16)


def _memkv(mem, nw, w_kv, kw):
    b, m, d = mem.shape
    full = lambda a: pl.BlockSpec(a.shape, lambda bi: (0,) * a.ndim)
    out = pl.BlockSpec((1, m, MEM_WIDTH), lambda bi: (bi, 0, 0))
    return pl.pallas_call(
        _memkv_kernel,
        grid=(b,),
        in_specs=[pl.BlockSpec((1, m, d), lambda bi: (bi, 0, 0)), full(nw), full(w_kv), full(kw)],
        out_specs=[out, out],
        out_shape=[jax.ShapeDtypeStruct((b, m, MEM_WIDTH), BF16)] * 2,
        compiler_params=_cparams(("parallel",)),
    )(mem, nw, w_kv, kw)


def _mixout_kernel(x_ref, na_ref, yf_ref, yb_ref, bonus_ref, g_ref, mq_ref, mk_ref, mv_ref,
                   lnw_ref, lnb_ref, wo_ref, fnw_ref, whi_ref, wlo_ref, rb_ref,
                   o_ref, h_ref, route_ref, nch_ref):
    ones = _head_ones(RW_WIDTH)
    y = yf_ref[0] + yb_ref[0]
    mu = _split_dot(y, ones) * (1.0 / HEAD_DIM)
    yc = y - mu
    var = _split_dot(yc * yc, ones) * (1.0 / HEAD_DIM)
    yn = yc * lax.rsqrt(var + RW_GN_EPS) * lnw_ref[...] + lnb_ref[...]
    y_rw = ((yn + bonus_ref[0]) * g_ref[0]).astype(BF16)

    first = lax.broadcasted_iota(jnp.int32, (1, LANES), 1) < HEAD_DIM
    mems = []
    for p in range(MEM_HEADS // 2):
        ls = slice(p * LANES, (p + 1) * LANES)
        qp = mq_ref[0, :, ls]
        kp = mk_ref[0, :, ls]
        vp = mv_ref[0, :, ls]
        outs = []
        for hh in range(2):
            sel = first if hh == 0 else jnp.logical_not(first)
            s = _dot_nt(jnp.where(sel, qp, jnp.zeros_like(qp)), kp)
            m = jnp.max(s, axis=-1, keepdims=True)
            e = jnp.exp(s - m)
            l = jnp.sum(e, axis=-1, keepdims=True)
            outs.append(_dot(e.astype(BF16), vp) / l)
        mems.append(jnp.where(first, outs[0], outs[1]).astype(BF16))
    y_mem = jnp.concatenate(mems, axis=1)

    acc = _dot(na_ref[0], wo_ref[0:NA_WIDTH, :])
    acc = acc + _dot(y_rw, wo_ref[NA_WIDTH:NA_WIDTH + RW_WIDTH, :])
    acc = acc + _dot(y_mem, wo_ref[NA_WIDTH + RW_WIDTH:, :])
    x_mid = x_ref[0] + acc
    o_ref[0] = x_mid
    _route_tile(x_mid, fnw_ref, whi_ref, wlo_ref, rb_ref, h_ref, route_ref, nch_ref)


def _mixout(x, y_na, y_f, y_b, bonus, g, mq, mk, mv, ln_w, ln_b, w_out, ffn_nw, router, tm):
    b, s, d = x.shape
    m = mk.shape[1]
    sub = tm // MOE_SUB
    tok = lambda w: pl.BlockSpec((1, tm, w), lambda bi, i: (bi, i, 0))
    full = lambda a: pl.BlockSpec(a.shape, lambda bi, i: (0,) * a.ndim)
    memb = pl.BlockSpec((1, m, MEM_WIDTH), lambda bi, i: (bi, 0, 0))
    x_mid, h2, route, nch = pl.pallas_call(
        _mixout_kernel,
        grid=(b, s // tm),
        in_specs=[tok(d), tok(NA_WIDTH), tok(RW_WIDTH), tok(RW_WIDTH), tok(RW_WIDTH), tok(RW_WIDTH),
                  tok(MEM_WIDTH), memb, memb, full(ln_w), full(ln_b), full(w_out), full(ffn_nw)]
        + [full(p) for p in router],
        out_specs=[tok(d), tok(d),
                   pl.BlockSpec((1, sub, ROUTE_ROWS, MOE_SUB), lambda bi, i: (bi, i, 0, 0)),
                   pl.BlockSpec((1, sub, 1, ROUTER_LANES), lambda bi, i: (bi, i, 0, 0))],
        out_shape=[jax.ShapeDtypeStruct((b, s, d), F32), jax.ShapeDtypeStruct((b, s, d), BF16),
                   jax.ShapeDtypeStruct((b, s // MOE_SUB, ROUTE_ROWS, MOE_SUB), F32),
                   jax.ShapeDtypeStruct((b, s // MOE_SUB, 1, ROUTER_LANES), F32)],
        compiler_params=_cparams(("parallel", "parallel")),
    )(x, y_na, y_f, y_b, bonus, g, mq, mk, mv, ln_w, ln_b, w_out, ffn_nw, *router)
    ns = b * s // MOE_SUB
    return (x_mid, h2.reshape(b * s, d), route.reshape(ns, ROUTE_ROWS, MOE_SUB),
            nch.reshape(ns, 1, ROUTER_LANES))


ROUTER_LANES = 128


MOE_SUB = 256
MOE_CHUNK = 16
MOE_LOCAL_CHUNKS = 48
MOE_BLOCK_CHUNKS = 32
MOE_TOP_K = 2
ROUTE_ROWS = 8
assert MOE_LOCAL_CHUNKS >= MOE_TOP_K * MOE_SUB // MOE_CHUNK + MOE_EXPERTS - 1


def _route_subtile(lg):
    t = lg.shape[0]
    lane = lax.broadcasted_iota(jnp.int32, (1, ROUTER_LANES), 1)
    lanef = lane.astype(F32)
    big = float(ROUTER_LANES)

    def argmax_lane(v):
        m = jnp.max(v, axis=-1, keepdims=True)
        return m, jnp.min(jnp.where(v == m, lanef, big), axis=-1, keepdims=True)

    is_group = lane < MOE_GROUPS
    gmax, gidx = argmax_lane(jnp.where(is_group, lg, NEG_BIG))
    gsum = jnp.sum(jnp.where(is_group, jnp.exp(lg - gmax), 0.0), axis=-1, keepdims=True)
    lo = MOE_GROUPS + MOE_EXPERTS_PER_GROUP * gidx
    el = jnp.where((lanef >= lo) & (lanef < lo + MOE_EXPERTS_PER_GROUP), lg, NEG_BIG)
    m1, i1 = argmax_lane(el)
    m2, i2 = argmax_lane(jnp.where(lanef == i1, NEG_BIG, el))
    ratio = jnp.exp(m2 - m1)
    w0 = 1.0 / (gsum * (1.0 + ratio))
    w1 = w0 * ratio

    oh0 = (lanef == i1 - MOE_GROUPS).astype(F32)
    oh1 = (lanef == i2 - MOE_GROUPS).astype(F32)
    row = lax.broadcasted_iota(jnp.int32, (t, t), 0)
    col = lax.broadcasted_iota(jnp.int32, (t, t), 1)
    before = (col < row).astype(BF16)
    cnt0 = jnp.sum(oh0, axis=0, keepdims=True)
    cnt1 = jnp.sum(oh1, axis=0, keepdims=True)
    rank0 = _dot(before, oh0.astype(BF16))
    rank1 = _dot(before, oh1.astype(BF16)) + cnt0
    nch = jnp.floor((cnt0 + cnt1 + (MOE_CHUNK - 1.0)) * (1.0 / MOE_CHUNK))
    li = lax.broadcasted_iota(jnp.int32, (ROUTER_LANES, ROUTER_LANES), 0)
    lj = lax.broadcasted_iota(jnp.int32, (ROUTER_LANES, ROUTER_LANES), 1)
    start = _dot(jnp.broadcast_to(nch, (8, ROUTER_LANES)).astype(BF16), (li < lj).astype(BF16))[0:1]
    pos0 = jnp.sum((start * MOE_CHUNK + rank0) * oh0, axis=-1, keepdims=True)
    pos1 = jnp.sum((start * MOE_CHUNK + rank1) * oh1, axis=-1, keepdims=True)
    rec = (jnp.where(lane == 0, pos0, 0.0) + jnp.where(lane == 1, pos1, 0.0)
           + jnp.where(lane == 2, w0, 0.0) + jnp.where(lane == 3, w1, 0.0))
    return jnp.transpose(rec)[0:ROUTE_ROWS], nch


def _route_tile(x, nw_ref, whi_ref, wlo_ref, b_ref, h_ref, route_ref, nch_ref):
    ms = jnp.mean(x * x, axis=-1, keepdims=True)
    h = x * lax.rsqrt(ms + RMS_EPS) * nw_ref[...]
    hi = h.astype(BF16)
    lo = (h - hi.astype(F32)).astype(BF16)
    h_ref[0] = hi
    lg = _dot(hi, whi_ref[...]) + _dot(hi, wlo_ref[...]) + _dot(lo, whi_ref[...]) + b_ref[...]
    for j in range(x.shape[0] // MOE_SUB):
        route_ref[0, j], nch_ref[0, j] = _route_subtile(lg[j * MOE_SUB:(j + 1) * MOE_SUB])


def _router_params(w_group, b_group, w_expert, b_expert):
    d = w_group.shape[0]
    pad = ROUTER_LANES - MOE_GROUPS - MOE_EXPERTS
    w = jnp.concatenate([w_group, w_expert, jnp.zeros((d, pad), F32)], axis=1)
    whi = w.astype(BF16)
    wlo = (w - whi.astype(F32)).astype(BF16)
    bias = jnp.concatenate([b_group, b_expert, jnp.zeros((pad,), F32)]).reshape(1, -1)
    return whi, wlo, bias


def _moe_plan(nch, n):
    nch = nch[:, 0, :MOE_EXPERTS].astype(jnp.int32)
    lc_end = jnp.cumsum(nch, axis=1)
    lc_start = lc_end - nch
    nloc = lc_end[:, -1]
    nblk = (jnp.sum(nch, axis=0) + MOE_BLOCK_CHUNKS - 1) // MOE_BLOCK_CHUNKS
    bend = jnp.cumsum(nblk)
    gch = ((bend - nblk) * MOE_BLOCK_CHUNKS)[None, :] + jnp.cumsum(nch, axis=0) - nch
    c = jnp.arange(MOE_LOCAL_CHUNKS, dtype=jnp.int32)
    in_run = (c[None, :, None] >= lc_start[:, None, :]) & (c[None, :, None] < lc_end[:, None, :])
    dest = jnp.sum(jnp.where(in_run, (gch - lc_start)[:, None, :] + c[None, :, None], 0), axis=-1)

    nblocks = _moe_blocks(n)
    b = jnp.arange(nblocks, dtype=jnp.int32)
    block_e = jnp.minimum(jnp.sum(b[:, None] >= bend[None, :], axis=-1), MOE_EXPERTS - 1)
    tot = jnp.sum(nch, axis=0)
    tail_start = (bend - nblk) * MOE_BLOCK_CHUNKS + tot
    ntail = nblk * MOE_BLOCK_CHUNKS - tot
    i32 = lambda a: a.astype(jnp.int32)
    return i32(dest), i32(nloc), i32(block_e), i32(bend[-1:]), i32(tail_start), i32(ntail)


def _moe_blocks(n):
    ns = n // MOE_SUB
    chunks = ns * (MOE_TOP_K * MOE_SUB // MOE_CHUNK + MOE_EXPERTS - 1) + MOE_EXPERTS * (MOE_BLOCK_CHUNKS - 1)
    return -(-chunks // MOE_BLOCK_CHUNKS)


def _local_onehot(pos_rows, shape, row_axis):
    idx = lax.broadcasted_iota(jnp.int32, shape, row_axis)
    return (idx == pos_rows[0]) | (idx == pos_rows[1])


def _dispatch_kernel(dest_ref, nloc_ref, tail_ref, ntail_ref, nv_ref, h_ref, pos_ref, xs_ref,
                     buf, zbuf, sem, zsem):
    s = pl.program_id(0)
    ns = pl.num_programs(0)
    slot = s % 2
    bm = MOE_BLOCK_CHUNKS * MOE_CHUNK
    nblocks = xs_ref.shape[0] // bm

    def zero_fill(wait):
        def tail_copy(e, i):
            dst = xs_ref.at[pl.ds(pl.multiple_of((tail_ref[e] + i) * MOE_CHUNK, MOE_CHUNK), MOE_CHUNK)]
            return pltpu.make_async_copy(zbuf.at[pl.ds(0, MOE_CHUNK)], dst, zsem.at[0])

        def block_copy(b):
            return pltpu.make_async_copy(zbuf, xs_ref.at[pl.ds(pl.multiple_of(b * bm, bm), bm)], zsem.at[1])

        def run(cp):
            return cp.wait() if wait else cp.start()

        for e in range(MOE_EXPERTS):
            def tail_body(i, carry, e=e):
                run(tail_copy(e, i))
                return carry
            lax.fori_loop(0, ntail_ref[e], tail_body, 0)

        def block_body(b, carry):
            run(block_copy(b))
            return carry
        lax.fori_loop(nv_ref[0], nblocks, block_body, 0)

    @pl.when(s == 0)
    def _():
        zbuf[...] = jnp.zeros_like(zbuf)
        zero_fill(wait=False)

    def chunk_copy(sl, c, step):
        src = buf.at[sl, pl.ds(pl.multiple_of(c * MOE_CHUNK, MOE_CHUNK), MOE_CHUNK)]
        dst = xs_ref.at[pl.ds(pl.multiple_of(dest_ref[step, c] * MOE_CHUNK, MOE_CHUNK), MOE_CHUNK)]
        return pltpu.make_async_copy(src, dst, sem.at[sl])

    def wait_step(step, sl):
        def body(c, carry):
            chunk_copy(sl, c, step).wait()
            return carry
        lax.fori_loop(0, nloc_ref[step], body, 0)

    @pl.when(s >= 2)
    def _():
        wait_step(s - 2, slot)

    pos = pos_ref[0, 0:MOE_TOP_K].astype(jnp.int32)
    onehot = _local_onehot((pos[0:1], pos[1:2]), (MOE_LOCAL_CHUNKS * MOE_CHUNK, MOE_SUB), 0)
    buf[slot] = _dot(jnp.where(onehot, 1.0, 0.0).astype(BF16), h_ref[...]).astype(BF16)

    def issue(c, carry):
        chunk_copy(slot, c, s).start()
        return carry
    lax.fori_loop(0, nloc_ref[s], issue, 0)

    @pl.when(s == ns - 1)
    def _():
        @pl.when(s >= 1)
        def _():
            wait_step(s - 1, 1 - slot)
        wait_step(s, slot)
        zero_fill(wait=True)


def _dispatch(h2, pos, dest, nloc, tail_start, ntail, nvalid):
    n, d = h2.shape
    ns = n // MOE_SUB
    bm = MOE_BLOCK_CHUNKS * MOE_CHUNK
    rows = MOE_LOCAL_CHUNKS * MOE_CHUNK
    return pl.pallas_call(
        _dispatch_kernel,
        grid_spec=pltpu.PrefetchScalarGridSpec(
            num_scalar_prefetch=5, grid=(ns,),
            in_specs=[pl.BlockSpec((MOE_SUB, d), lambda s, *_: (s, 0)),
                      pl.BlockSpec((1, ROUTE_ROWS, MOE_SUB), lambda s, *_: (s, 0, 0))],
            out_specs=pl.BlockSpec(memory_space=pl.ANY),
            scratch_shapes=[pltpu.VMEM((2, rows, d), BF16), pltpu.VMEM((bm, d), BF16),
                            pltpu.SemaphoreType.DMA((2,)), pltpu.SemaphoreType.DMA((2,))]),
        out_shape=jax.ShapeDtypeStruct((_moe_blocks(n) * bm, d), BF16),
        compiler_params=_cparams(("arbitrary",)),
    )(dest, nloc, tail_start, ntail, nvalid, h2, pos)


def _experts_kernel(be_ref, nv_ref, x_ref, wg_ref, wu_ref, wd_ref, o_ref):
    del be_ref
    valid = pl.program_id(0) < nv_ref[0]

    @pl.when(valid)
    def _():
        x = x_ref[...]
        hid = jax.nn.silu(_dot(x, wg_ref[0])) * _dot(x, wu_ref[0])
        o_ref[...] = _dot(hid.astype(BF16), wd_ref[0]).astype(o_ref.dtype)

    @pl.when(jnp.logical_not(valid))
    def _():
        o_ref[...] = jnp.zeros_like(o_ref)


def _experts(xs, block_e, nvalid, w_gate, w_up, w_down):
    cap, d = xs.shape
    bm = MOE_BLOCK_CHUNKS * MOE_CHUNK
    blk = lambda b, be, nv: (jnp.minimum(b, nv[0] - 1), 0)
    wsel = lambda b, be, nv: (be[jnp.minimum(b, nv[0] - 1)], 0, 0)
    return pl.pallas_call(
        _experts_kernel,
        grid_spec=pltpu.PrefetchScalarGridSpec(
            num_scalar_prefetch=2, grid=(cap // bm,),
            in_specs=[pl.BlockSpec((bm, d), blk),
                      pl.BlockSpec((1, d, MOE_D_FF), wsel),
                      pl.BlockSpec((1, d, MOE_D_FF), wsel),
                      pl.BlockSpec((1, MOE_D_FF, d), wsel)],
            out_specs=pl.BlockSpec((bm, d), lambda b, be, nv: (b, 0))),
        out_shape=jax.ShapeDtypeStruct((cap, d), BF16),
        compiler_params=_cparams(("arbitrary",)),
    )(block_e, nvalid, xs, w_gate, w_up, w_down)


def _combine_kernel(dest_ref, nloc_ref, x_ref, route_ref, ys_ref, o_ref, buf, sem):
    s = pl.program_id(0)
    ns = pl.num_programs(0)
    slot = s % 2

    def chunk_copy(sl, c, step):
        src = ys_ref.at[pl.ds(pl.multiple_of(dest_ref[step, c] * MOE_CHUNK, MOE_CHUNK), MOE_CHUNK)]
        dst = buf.at[sl, pl.ds(pl.multiple_of(c * MOE_CHUNK, MOE_CHUNK), MOE_CHUNK)]
        return pltpu.make_async_copy(src, dst, sem.at[sl])

    def fetch(step, sl):
        def body(c, carry):
            chunk_copy(sl, c, step).start()
            return carry
        lax.fori_loop(0, nloc_ref[step], body, 0)

    @pl.when(s == 0)
    def _():
        buf[...] = jnp.zeros_like(buf)
        fetch(0, 0)

    @pl.when(s + 1 < ns)
    def _():
        fetch(s + 1, 1 - slot)

    def wait(c, carry):
        chunk_copy(slot, c, s).wait()
        return carry
    lax.fori_loop(0, nloc_ref[s], wait, 0)

    rec = jnp.transpose(route_ref[0])
    pos = rec[:, 0:MOE_TOP_K].astype(jnp.int32)
    idx = lax.broadcasted_iota(jnp.int32, (MOE_SUB, MOE_LOCAL_CHUNKS * MOE_CHUNK), 1)
    pw = jnp.where(idx == pos[:, 0:1], rec[:, 2:3], 0.0) + jnp.where(idx == pos[:, 1:2], rec[:, 3:4], 0.0)
    ys = buf[slot]
    p_hi = pw.astype(BF16)
    p_lo = (pw - p_hi.astype(F32)).astype(BF16)
    o_ref[...] = x_ref[...] + _dot(p_hi, ys) + _dot(p_lo, ys)


def _combine(x2, ys, route, dest, nloc):
    n, d = x2.shape
    ns = n // MOE_SUB
    rows = MOE_LOCAL_CHUNKS * MOE_CHUNK
    return pl.pallas_call(
        _combine_kernel,
        grid_spec=pltpu.PrefetchScalarGridSpec(
            num_scalar_prefetch=2, grid=(ns,),
            in_specs=[pl.BlockSpec((MOE_SUB, d), lambda s, *_: (s, 0)),
                      pl.BlockSpec((1, ROUTE_ROWS, MOE_SUB), lambda s, *_: (s, 0, 0)),
                      pl.BlockSpec(memory_space=pl.ANY)],
            out_specs=pl.BlockSpec((MOE_SUB, d), lambda s, *_: (s, 0)),
            scratch_shapes=[pltpu.VMEM((2, rows, d), BF16), pltpu.SemaphoreType.DMA((2,))]),
        out_shape=jax.ShapeDtypeStruct((n, d), F32),
        compiler_params=_cparams(("arbitrary",)),
    )(dest, nloc, x2, route, ys)


def _moe(x2, h2, route, nch, w_gate, w_up, w_down):
    dest, nloc, block_e, nvalid, tail_start, ntail = _moe_plan(nch, x2.shape[0])
    xs = _dispatch(h2, route, dest, nloc, tail_start, ntail, nvalid)
    ys = _experts(xs, block_e, nvalid, w_gate, w_up, w_down)
    return _combine(x2, ys, route, dest, nloc)


def _tile(n, want):
    t = min(n, want)
    assert n % t == 0
    return t


def kernel(x, mem, attn_norm_w, w_in, na_q_norm_w, na_k_norm_w, na_rpb, rw_mu_prev, rw_mu_next, rw_w0, rw_w2, rw_a0, rw_a2, rw_g2, rw_k_k, rw_k_a, rw_r_k, rw_ln_w, rw_ln_b, mem_norm_w, w_mem_kv, mem_q_norm_w, mem_k_norm_w, w_out, ffn_norm_w, moe_w_group, moe_b_group, moe_w_expert, moe_b_expert, moe_w_gate, moe_w_up, moe_w_down):
    b, s, d = x.shape
    n = b * s
    depth = w_in.shape[0]
    assert s % (NA_BLOCK_ROWS * GRID_W) == 0 and s // GRID_W >= 2 * NA_KH
    tm = _tile(n, 512)
    ts = _tile(s, 512)
    for l in range(depth):
        q, k, v, rw, mq = _proj(
            x.reshape(n, d), _row(attn_norm_w[l]), w_in[l].astype(BF16),
            _row(jnp.tile(na_q_norm_w[l], NA_HEADS)), _row(jnp.tile(na_k_norm_w[l], NA_HEADS)),
            _row(jnp.tile(mem_q_norm_w[l], MEM_HEADS)), tm)
        y_na = _na(q.reshape(b, s, NA_WIDTH), k.reshape(b, s, NA_WIDTH), v.reshape(b, s, NA_WIDTH),
                   _na_bias_table(na_rpb[l]))
        r, vv, kk, kd, bb, lw, g, bonus = _rwprep(
            rw.reshape(b, s, RW_PROJ), rw_mu_prev[l], rw_mu_next[l], rw_w0[l], rw_w2[l], rw_a0[l],
            rw_a2[l], rw_g2[l], rw_k_k[l], rw_k_a[l], rw_r_k[l], ts)
        y_f, y_b = _wkv2(r, vv, kk, kd, bb, lw)
        mk, mv = _memkv(mem, _row(mem_norm_w[l]), w_mem_kv[l].astype(BF16),
                        _row(jnp.tile(mem_k_norm_w[l], MEM_HEADS)))
        x, h2, route, nch = _mixout(
            x, y_na, y_f, y_b, bonus, g, mq.reshape(b, s, MEM_WIDTH), mk, mv,
            _row(rw_ln_w[l]), _row(rw_ln_b[l]), w_out[l].astype(BF16), _row(ffn_norm_w[l]),
            _router_params(moe_w_group[l], moe_b_group[l], moe_w_expert[l], moe_b_expert[l]), ts)
        x = _moe(x.reshape(n, d), h2, route, nch, moe_w_gate[l].astype(BF16),
                 moe_w_up[l].astype(BF16), moe_w_down[l].astype(BF16)).reshape(b, s, d)
    return x
```

```python
import functools

import jax
import jax.numpy as jnp
import numpy as np
from jax import lax
from jax.experimental import pallas as pl
from jax.experimental.pallas import tpu as pltpu

F32 = jnp.float32
BF16 = jnp.bfloat16

GRID_W = 64
HEAD_DIM = 64
NA_HEADS = 8
NA_WIDTH = NA_HEADS * HEAD_DIM
NA_KH = 8
NA_KW = 16
RW_HEADS = 4
RW_WIDTH = RW_HEADS * HEAD_DIM
RW_LORA_W = 64
RW_LORA_A = 64
RW_LORA_G = 128
RW_PROJ = 3 * RW_WIDTH + 2 * RW_LORA_W + 2 * RW_LORA_A + RW_LORA_G
MEM_HEADS = 4
MEM_WIDTH = MEM_HEADS * HEAD_DIM
MOE_GROUPS = 4
MOE_EXPERTS_PER_GROUP = 4
MOE_EXPERTS = MOE_GROUPS * MOE_EXPERTS_PER_GROUP
MOE_D_FF = 512
RMS_EPS = 1e-6
RW_GN_EPS = 64e-5

LANES = 128
WKV_CHUNK = 64
NEG_BIG = -1e30
VMEM_LIMIT = 56 * 1024 * 1024


def _cparams(sem):
    return pltpu.CompilerParams(dimension_semantics=sem, vmem_limit_bytes=VMEM_LIMIT)


def _dot(a, b):
    return jnp.dot(a, b, preferred_element_type=F32)


def _dot_nt(a, b):
    return lax.dot_general(a, b, (((1,), (1,)), ((), ())), preferred_element_type=F32)


def _dot_tn(a, b):
    return lax.dot_general(a, b, (((0,), (0,)), ((), ())), preferred_element_type=F32)


def _split_dot(t, m):
    hi = t.astype(BF16)
    lo = (t - hi.astype(F32)).astype(BF16)
    return _dot(hi, m) + _dot(lo, m)


def _head_ones(width):
    i = lax.broadcasted_iota(jnp.int32, (width, width), 0) // HEAD_DIM
    j = lax.broadcasted_iota(jnp.int32, (width, width), 1) // HEAD_DIM
    return (i == j).astype(BF16)


def _row(v):
    return v.reshape(1, -1).astype(F32)


def _proj_kernel(x_ref, nw_ref, w_ref, qw_ref, kw_ref, mw_ref, q_ref, k_ref, v_ref, rw_ref, mq_ref):
    x = x_ref[...]
    ms = jnp.mean(x * x, axis=-1, keepdims=True)
    h = (x * lax.rsqrt(ms + RMS_EPS) * nw_ref[...]).astype(BF16)
    ones_na = _head_ones(NA_WIDTH)

    def head_norm(t, w, ones):
        ms = _dot((t * t).astype(BF16), ones) * (1.0 / HEAD_DIM)
        return t * lax.rsqrt(ms + RMS_EPS) * w

    o = 0
    q = _dot(h, w_ref[:, o:o + NA_WIDTH])
    q_ref[...] = (head_norm(q, qw_ref[...], ones_na) * (HEAD_DIM ** -0.5)).astype(BF16)
    o += NA_WIDTH
    k = _dot(h, w_ref[:, o:o + NA_WIDTH])
    k_ref[...] = head_norm(k, kw_ref[...], ones_na).astype(BF16)
    o += NA_WIDTH
    v_ref[...] = _dot(h, w_ref[:, o:o + NA_WIDTH]).astype(BF16)
    o += NA_WIDTH
    rw_ref[...] = _dot(h, w_ref[:, o:o + RW_PROJ])
    o += RW_PROJ
    mq = _dot(h, w_ref[:, o:o + MEM_WIDTH])
    mq_ref[...] = (head_norm(mq, mw_ref[...], _head_ones(MEM_WIDTH)) * (HEAD_DIM ** -0.5)).astype(BF16)


def _proj(x2, nw, w_in, qw, kw, mw, tm):
    n, d = x2.shape
    p_in = w_in.shape[1]
    tok = lambda w: pl.BlockSpec((tm, w), lambda i: (i, 0))
    full = lambda a: pl.BlockSpec(a.shape, lambda i: (0,) * a.ndim)
    return pl.pallas_call(
        _proj_kernel,
        grid=(n // tm,),
        in_specs=[tok(d), full(nw), full(w_in), full(qw), full(kw), full(mw)],
        out_specs=[tok(NA_WIDTH), tok(NA_WIDTH), tok(NA_WIDTH), tok(RW_PROJ), tok(MEM_WIDTH)],
        out_shape=[jax.ShapeDtypeStruct((n, NA_WIDTH), BF16)] * 3
        + [jax.ShapeDtypeStruct((n, RW_PROJ), F32), jax.ShapeDtypeStruct((n, MEM_WIDTH), BF16)],
        compiler_params=_cparams(("parallel",)),
    )(x2, nw, w_in, qw, kw, mw)


NA_BLOCK_ROWS = 8


def _na_bias_table(rpb):
    col = np.arange(GRID_W)
    cs = np.clip(col - NA_KW // 2, 0, GRID_W - NA_KW)
    dc = col[None, :] - col[:, None]
    inside = (col[None, :] >= cs[:, None]) & (col[None, :] < cs[:, None] + NA_KW)
    dci = np.clip(dc + NA_KW - 1, 0, 2 * NA_KW - 2)
    off = np.arange(NA_KH) - (NA_KH - 1)
    dri = off[:, None] + np.arange(NA_KH)[None, :] + NA_KH - 1
    row_sel = (dri[:, :, None] == np.arange(2 * NA_KH - 1)).astype(np.float32)
    col_sel = (dci[:, :, None] == np.arange(2 * NA_KW - 1)).astype(np.float32)
    t = jnp.einsum("hrq,onr,cdq->ohcnd", rpb.astype(F32), row_sel, col_sel,
                   precision=lax.Precision.HIGHEST)
    t = jnp.where(inside[None, None, :, None, :], t, NEG_BIG)
    return t.reshape(NA_KH, NA_HEADS // 2, 2 * GRID_W, NA_KH * GRID_W).astype(F32)


def _na_kernel(q_ref, kp_ref, kc_ref, kn_ref, vp_ref, vc_ref, vn_ref, bias_ref, o_ref, kwin, vwin, *, rows):
    j = pl.program_id(1)
    blk = NA_BLOCK_ROWS * GRID_W
    kwin[0:blk] = kp_ref[0]
    kwin[blk:2 * blk] = kc_ref[0]
    kwin[2 * blk:3 * blk] = kn_ref[0]
    vwin[0:blk] = vp_ref[0]
    vwin[blk:2 * blk] = vc_ref[0]
    vwin[2 * blk:3 * blk] = vn_ref[0]
    lane = lax.broadcasted_iota(jnp.int32, (1, LANES), 1)
    first = lane < HEAD_DIM

    def row_body(i, carry):
        r = j * NA_BLOCK_ROWS + i
        rs = jnp.clip(r - NA_KH // 2, 0, rows - NA_KH)
        oi = rs - r + (NA_KH - 1)
        start = pl.multiple_of((rs - (j - 1) * NA_BLOCK_ROWS) * GRID_W, GRID_W)
        qs = pl.multiple_of(i * GRID_W, GRID_W)
        pairs = [slice(p * LANES, (p + 1) * LANES) for p in range(NA_HEADS // 2)]
        scores = []
        for p, ls in enumerate(pairs):
            qp = q_ref[0, pl.ds(qs, GRID_W), ls]
            zero = jnp.zeros_like(qp)
            qst = jnp.concatenate([jnp.where(first, qp, zero), jnp.where(first, zero, qp)], axis=0)
            scores.append(_dot_nt(qst, kwin[pl.ds(start, NA_KH * GRID_W), ls]) + bias_ref[oi, p])
        probs, norms = [], []
        for s in scores:
            e = jnp.exp(s - jnp.max(s, axis=-1, keepdims=True))
            norms.append(jnp.sum(e, axis=-1, keepdims=True))
            probs.append(e.astype(BF16))
        for ls, e, l in zip(pairs, probs, norms):
            o = _dot(e, vwin[pl.ds(start, NA_KH * GRID_W), ls]) / l
            o_ref[0, pl.ds(qs, GRID_W), ls] = jnp.where(first, o[:GRID_W], o[GRID_W:]).astype(o_ref.dtype)
        return carry

    lax.fori_loop(0, NA_BLOCK_ROWS, row_body, 0, unroll=True)


def _na(q, k, v, bias):
    b, s, w = q.shape
    rows = s // GRID_W
    nblk = rows // NA_BLOCK_ROWS
    blk = NA_BLOCK_ROWS * GRID_W
    cur = pl.BlockSpec((1, blk, w), lambda bi, j: (bi, j, 0))
    prv = pl.BlockSpec((1, blk, w), lambda bi, j: (bi, jnp.maximum(j - 1, 0), 0))
    nxt = pl.BlockSpec((1, blk, w), lambda bi, j: (bi, jnp.minimum(j + 1, nblk - 1), 0))
    return pl.pallas_call(
        functools.partial(_na_kernel, rows=rows),
        grid=(b, nblk),
        in_specs=[cur, prv, cur, nxt, prv, cur, nxt,
                  pl.BlockSpec(bias.shape, lambda bi, j: (0, 0, 0, 0))],
        out_specs=cur,
        out_shape=jax.ShapeDtypeStruct((b, s, w), BF16),
        scratch_shapes=[pltpu.VMEM((3 * blk, w), BF16), pltpu.VMEM((3 * blk, w), BF16)],
        compiler_params=_cparams(("parallel", "parallel")),
    )(q, k, k, k, v, v, v, bias)


def _rwprep_kernel(rw_ref, prev_ref, next_ref, mup_ref, mun_ref, w0_ref, w2_ref, a0_ref, a2_ref, g2_ref,
                   kk_w_ref, ka_ref, rk_ref,
                   r_ref, v_ref, kk_ref, kd_ref, b_ref, lw_ref, g_ref, bonus_ref, *, nblk):
    i = pl.program_id(1)
    s = rw_ref[0]
    ts = s.shape[0]
    rowi = lax.broadcasted_iota(jnp.int32, (ts, 1), 0)
    halo_p = jnp.where(i > 0, prev_ref[0, 7:8, :], 0.0)
    halo_n = jnp.where(i < nblk - 1, next_ref[0, 0:1, :], 0.0)
    prev = jnp.where(rowi == 0, halo_p, pltpu.roll(s, 1, 0))
    nxt = jnp.where(rowi == ts - 1, halo_n, pltpu.roll(s, ts - 1, 0))
    s = s + mup_ref[...] * (prev - s) + mun_ref[...] * (nxt - s)

    c = RW_WIDTH
    r = s[:, 0:c]
    k = s[:, c:2 * c]
    v = s[:, 2 * c:3 * c]
    lw = s[:, 3 * c:3 * c + 2 * RW_LORA_W]
    la = s[:, 3 * c + 2 * RW_LORA_W:3 * c + 2 * RW_LORA_W + 2 * RW_LORA_A]
    lg = s[:, 3 * c + 2 * RW_LORA_W + 2 * RW_LORA_A:]
    ones = _head_ones(c)

    g_ref[0] = _dot(jax.nn.sigmoid(lg).astype(BF16), g2_ref[...])
    kk = k * kk_w_ref[...]
    nrm = jnp.sqrt(_split_dot(kk * kk, ones))
    kk = kk / jnp.maximum(nrm, 1e-12)
    wl_pre = w0_ref[...] + _dot(jnp.tanh(lw).astype(BF16), w2_ref[...])
    a_all = jax.nn.sigmoid(a0_ref[...] + _dot(la.astype(BF16), a2_ref[...]))
    wl = -jax.nn.softplus(-wl_pre) - 0.5
    logdecay = -jnp.exp(wl)
    r_ref[0] = r
    v_ref[0] = v
    kk_ref[0] = kk
    kd_sum = jnp.zeros_like(k)
    for d in range(2):
        a = a_all[:, d * c:(d + 1) * c]
        kd = k * (1.0 + (a - 1.0) * ka_ref[...])
        kd_ref[d, 0] = kd
        b_ref[d, 0] = kk * a
        lw_ref[d, 0] = logdecay[:, d * c:(d + 1) * c]
        kd_sum = kd_sum + kd
    bonus_ref[0] = _split_dot(r * kd_sum * rk_ref[...], ones) * v


def _blockdiag2(m):
    z = jnp.zeros_like(m[0])
    return jnp.concatenate([jnp.concatenate([m[0], z], 1), jnp.concatenate([z, m[1]], 1)], 0)


def _rwprep(rw, mu_prev, mu_next, w0, w2, a0, a2, g2, k_k, k_a, r_k, ts):
    b, s, pw = rw.shape
    nblk = s // ts
    c = RW_WIDTH
    cur = pl.BlockSpec((1, ts, pw), lambda bi, i: (bi, i, 0))
    prv = pl.BlockSpec((1, 8, pw), lambda bi, i: (bi, jnp.maximum(i * (ts // 8) - 1, 0), 0))
    nxt = pl.BlockSpec((1, 8, pw), lambda bi, i: (bi, jnp.minimum((i + 1) * (ts // 8), s // 8 - 1), 0))
    params = [_row(mu_prev), _row(mu_next), _row(w0), _blockdiag2(w2).astype(BF16), _row(a0),
              _blockdiag2(a2).astype(BF16), g2.astype(BF16), _row(k_k), _row(k_a), _row(r_k)]
    full = lambda a: pl.BlockSpec(a.shape, lambda bi, i: (0,) * a.ndim)
    one = pl.BlockSpec((1, ts, c), lambda bi, i: (bi, i, 0))
    two = pl.BlockSpec((2, 1, ts, c), lambda bi, i: (0, bi, i, 0))
    s1 = jax.ShapeDtypeStruct((b, s, c), F32)
    s2 = jax.ShapeDtypeStruct((2, b, s, c), F32)
    return pl.pallas_call(
        functools.partial(_rwprep_kernel, nblk=nblk),
        grid=(b, nblk),
        in_specs=[cur, prv, nxt] + [full(p) for p in params],
        out_specs=[one, one, one, two, two, two, one, one],
        out_shape=[s1, s1, s1, s2, s2, s2, s1, s1],
        compiler_params=_cparams(("parallel", "parallel")),
    )(rw, rw, rw, *params)


WKV_STEP_CHUNKS = 4


WKV2_BATCH = 4


def _wkv2_kernel(rf, vf, kkf, rb, vb, kkb, kdf, bf, lwf, kdb, bb, lwb, yf_ref, yb_ref, s_ref):
    cc = WKV_CHUNK
    nb = rf.shape[0]
    nchunk = rf.shape[1] // cc

    @pl.when(pl.program_id(1) == 0)
    def _():
        s_ref[...] = jnp.zeros_like(s_ref)

    same_head = ((lax.broadcasted_iota(jnp.int32, (LANES, LANES), 0) < HEAD_DIM)
                 == (lax.broadcasted_iota(jnp.int32, (LANES, LANES), 1) < HEAD_DIM))
    rowc = lax.broadcasted_iota(jnp.int32, (cc, LANES), 0)
    colc = lax.broadcasted_iota(jnp.int32, (cc, LANES), 1) % cc
    eye = (colc == rowc).astype(F32)
    strict = (colc < rowc, colc > rowc)
    incl = (colc <= rowc, colc >= rowc)
    first = lax.broadcasted_iota(jnp.int32, (1, LANES), 1) < HEAD_DIM
    second = jnp.logical_not(first)

    def stack_heads(t):
        z = jnp.zeros_like(t)
        return jnp.concatenate([jnp.where(first, t, z), jnp.where(second, t, z)], axis=0)

    def make_chains(c):
        chains = []
        for bi in range(nb):
            for d in range(2):
                j = nchunk - 1 - c if d else c
                sl = slice(j * cc, (j + 1) * cc)
                src = (rb, vb, kkb, kdb, bb, lwb) if d else (rf, vf, kkf, kdf, bf, lwf)
                for p in range(RW_HEADS // 2):
                    ls = slice(p * LANES, (p + 1) * LANES)
                    chains.append(dict(
                        bi=bi, d=d, p=p, sl=sl, ls=ls,
                        r=src[0][bi, sl, ls], v=src[1][bi, sl, ls], kk=src[2][bi, sl, ls],
                        kd=src[3][0, bi, sl, ls], b=src[4][0, bi, sl, ls], lw=src[5][0, bi, sl, ls]))
        return chains

    def solve_chunks(chains):
        for ch in chains:
            cum = ch["lw"]
            sh = 1
            while sh < cc:
                if ch["d"]:
                    cum = cum + jnp.where(rowc < cc - sh, pltpu.roll(cum, cc - sh, 0), 0.0)
                else:
                    cum = cum + jnp.where(rowc >= sh, pltpu.roll(cum, sh, 0), 0.0)
                sh *= 2
            tot = cum[0:1] if ch["d"] else cum[cc - 1:cc]
            at = (-ch["kk"] * jnp.exp(cum - ch["lw"])).astype(BF16)
            einv = jnp.exp(-cum)
            eh = jnp.exp(tot - cum)
            ch["etot"] = jnp.exp(tot)
            ch["rt"] = (ch["r"] * jnp.exp(cum)).astype(BF16)
            ch["vb"] = ch["v"].astype(BF16)
            ch["v_st"] = stack_heads(ch["vb"])
            ch["at_st"] = stack_heads(at)
            ch["lhs"] = jnp.concatenate([at, ch["rt"]], axis=0)
            ch["rhs"] = jnp.concatenate([stack_heads((ch["b"] * einv).astype(BF16)),
                                         stack_heads((ch["kd"] * einv).astype(BF16))], axis=0)
            ch["kb_hat"] = jnp.concatenate([ch["kd"] * eh, ch["b"] * eh], axis=0).astype(BF16)
        for ch in chains:
            m1 = _dot_nt(ch["lhs"], ch["rhs"])
            d = ch["d"]
            ch["n"] = jnp.where(strict[d], m1[:cc, :LANES], 0.0)
            a_ak = jnp.where(strict[d], m1[:cc, LANES:], 0.0)
            ch["a_rb"] = jnp.where(incl[d], m1[cc:, :LANES], 0.0).astype(BF16)
            a_rk = jnp.where(incl[d], m1[cc:, LANES:], 0.0)
            ch["a_k"] = jnp.concatenate([a_ak, a_rk], axis=0).astype(BF16)
        for ch in chains:
            nbf = ch["n"].astype(BF16)
            ch["pw"] = _dot(nbf, stack_heads(nbf))
            ch["t"] = eye + ch["n"]
        for _ in range(cc.bit_length() - 3):
            for ch in chains:
                pwb = ch["pw"].astype(BF16)
                tp = _dot(jnp.concatenate([ch["t"].astype(BF16), pwb], axis=0), stack_heads(pwb))
                ch["t"] = ch["t"] + tp[:cc]
                ch["pw"] = tp[cc:]
        for ch in chains:
            ch["t"] = (ch["t"] + _dot(ch["t"].astype(BF16), stack_heads(ch["pw"].astype(BF16)))).astype(BF16)
        for ch in chains:
            ch["av"] = _dot(ch["a_k"], ch["v_st"])
        for ch in chains:
            akv = stack_heads(ch["av"][:cc].astype(BF16))
            tx = _dot(ch["t"], jnp.concatenate([ch["at_st"], akv], axis=1))
            ch["atp"] = tx[:, :LANES].astype(BF16)
            ch["u0"] = tx[:, LANES:]

    def advance_state(chains):
        for ch in chains:
            ch["st"] = s_ref[ch["bi"], ch["d"], ch["p"]]
            qs = _dot_nt(jnp.concatenate([ch["atp"], ch["rt"]], axis=0), ch["st"].astype(BF16))
            ch["u"] = (qs[:cc] + ch["u0"]).astype(BF16)
            ch["ys"] = qs[cc:] + ch["av"][cc:]
        for ch in chains:
            upd = _dot_tn(jnp.concatenate([ch["vb"], ch["u"]], axis=0), ch["kb_hat"])
            s_ref[ch["bi"], ch["d"], ch["p"]] = ch["st"] * ch["etot"] + jnp.where(same_head, upd, 0.0)
        for ch in chains:
            y_ref = yb_ref if ch["d"] else yf_ref
            y_ref[ch["bi"], ch["sl"], ch["ls"]] = ch["ys"] + _dot(ch["a_rb"], stack_heads(ch["u"]))

    per_chunk = [make_chains(c) for c in range(nchunk)]
    solve_chunks([ch for chains in per_chunk for ch in chains])
    for chains in per_chunk:
        advance_state(chains)


def _wkv2(r, v, kk, kd, bb, lw):
    b, s, c = r.shape
    tt = WKV_CHUNK * WKV_STEP_CHUNKS
    nblk = s // tt
    nb = WKV2_BATCH if b % WKV2_BATCH == 0 else 1
    fwd = pl.BlockSpec((nb, tt, c), lambda bi, i: (bi, i, 0))
    bwd = pl.BlockSpec((nb, tt, c), lambda bi, i: (bi, nblk - 1 - i, 0))
    fwd2 = pl.BlockSpec((1, nb, tt, c), lambda bi, i: (0, bi, i, 0))
    bwd2 = pl.BlockSpec((1, nb, tt, c), lambda bi, i: (1, bi, nblk - 1 - i, 0))
    return pl.pallas_call(
        _wkv2_kernel,
        grid=(b // nb, nblk),
        in_specs=[fwd, fwd, fwd, bwd, bwd, bwd, fwd2, fwd2, fwd2, bwd2, bwd2, bwd2],
        out_specs=[fwd, bwd],
        out_shape=[jax.ShapeDtypeStruct((b, s, c), F32)] * 2,
        scratch_shapes=[pltpu.VMEM((nb, 2, RW_HEADS // 2, LANES, LANES), F32)],
        compiler_params=_cparams(("parallel", "arbitrary")),
    )(r, v, kk, r, v, kk, kd, bb, lw, kd, bb, lw)


def _memkv_kernel(mem_ref, nw_ref, w_ref, kw_ref, k_ref, v_ref):
    x = mem_ref[0]
    ms = jnp.mean(x * x, axis=-1, keepdims=True)
    h = (x * lax.rsqrt(ms + RMS_EPS) * nw_ref[...]).astype(BF16)
    kv = _dot(h, w_ref[...])
    k = kv[:, :MEM_WIDTH]
    ms = _split_dot(k * k, _head_ones(MEM_WIDTH)) * (1.0 / HEAD_DIM)
    k_ref[0] = (k * lax.rsqrt(ms + RMS_EPS) * kw_ref[...]).astype(BF16)
    v_ref[0] = kv[:, MEM_WIDTH:].astype(BF16)


def _memkv(mem, nw, w_kv, kw):
    b, m, d = mem.shape
    full = lambda a: pl.BlockSpec(a.shape, lambda bi: (0,) * a.ndim)
    out = pl.BlockSpec((1, m, MEM_WIDTH), lambda bi: (bi, 0, 0))
    return pl.pallas_call(
        _memkv_kernel,
        grid=(b,),
        in_specs=[pl.BlockSpec((1, m, d), lambda bi: (bi, 0, 0)), full(nw), full(w_kv), full(kw)],
        out_specs=[out, out],
        out_shape=[jax.ShapeDtypeStruct((b, m, MEM_WIDTH), BF16)] * 2,
        compiler_params=_cparams(("parallel",)),
    )(mem, nw, w_kv, kw)


def _mixout_kernel(x_ref, na_ref, yf_ref, yb_ref, bonus_ref, g_ref, mq_ref, mk_ref, mv_ref,
                   lnw_ref, lnb_ref, wo_ref, fnw_ref, whi_ref, wlo_ref, rb_ref,
                   o_ref, h_ref, route_ref, nch_ref):
    ones = _head_ones(RW_WIDTH)
    y = yf_ref[0] + yb_ref[0]
    mu = _split_dot(y, ones) * (1.0 / HEAD_DIM)
    yc = y - mu
    var = _split_dot(yc * yc, ones) * (1.0 / HEAD_DIM)
    yn = yc * lax.rsqrt(var + RW_GN_EPS) * lnw_ref[...] + lnb_ref[...]
    y_rw = ((yn + bonus_ref[0]) * g_ref[0]).astype(BF16)

    first = lax.broadcasted_iota(jnp.int32, (1, LANES), 1) < HEAD_DIM
    mems = []
    for p in range(MEM_HEADS // 2):
        ls = slice(p * LANES, (p + 1) * LANES)
        qp = mq_ref[0, :, ls]
        kp = mk_ref[0, :, ls]
        vp = mv_ref[0, :, ls]
        outs = []
        for hh in range(2):
            sel = first if hh == 0 else jnp.logical_not(first)
            s = _dot_nt(jnp.where(sel, qp, jnp.zeros_like(qp)), kp)
            m = jnp.max(s, axis=-1, keepdims=True)
            e = jnp.exp(s - m)
            l = jnp.sum(e, axis=-1, keepdims=True)
            outs.append(_dot(e.astype(BF16), vp) / l)
        mems.append(jnp.where(first, outs[0], outs[1]).astype(BF16))
    y_mem = jnp.concatenate(mems, axis=1)

    acc = _dot(na_ref[0], wo_ref[0:NA_WIDTH, :])
    acc = acc + _dot(y_rw, wo_ref[NA_WIDTH:NA_WIDTH + RW_WIDTH, :])
    acc = acc + _dot(y_mem, wo_ref[NA_WIDTH + RW_WIDTH:, :])
    x_mid = x_ref[0] + acc
    o_ref[0] = x_mid
    _route_tile(x_mid, fnw_ref, whi_ref, wlo_ref, rb_ref, h_ref, route_ref, nch_ref)


def _mixout(x, y_na, y_f, y_b, bonus, g, mq, mk, mv, ln_w, ln_b, w_out, ffn_nw, router, tm):
    b, s, d = x.shape
    m = mk.shape[1]
    sub = tm // MOE_SUB
    tok = lambda w: pl.BlockSpec((1, tm, w), lambda bi, i: (bi, i, 0))
    full = lambda a: pl.BlockSpec(a.shape, lambda bi, i: (0,) * a.ndim)
    memb = pl.BlockSpec((1, m, MEM_WIDTH), lambda bi, i: (bi, 0, 0))
    x_mid, h2, route, nch = pl.pallas_call(
        _mixout_kernel,
        grid=(b, s // tm),
        in_specs=[tok(d), tok(NA_WIDTH), tok(RW_WIDTH), tok(RW_WIDTH), tok(RW_WIDTH), tok(RW_WIDTH),
                  tok(MEM_WIDTH), memb, memb, full(ln_w), full(ln_b), full(w_out), full(ffn_nw)]
        + [full(p) for p in router],
        out_specs=[tok(d), tok(d),
                   pl.BlockSpec((1, sub, ROUTE_ROWS, MOE_SUB), lambda bi, i: (bi, i, 0, 0)),
                   pl.BlockSpec((1, sub, 1, ROUTER_LANES), lambda bi, i: (bi, i, 0, 0))],
        out_shape=[jax.ShapeDtypeStruct((b, s, d), F32), jax.ShapeDtypeStruct((b, s, d), BF16),
                   jax.ShapeDtypeStruct((b, s // MOE_SUB, ROUTE_ROWS, MOE_SUB), F32),
                   jax.ShapeDtypeStruct((b, s // MOE_SUB, 1, ROUTER_LANES), F32)],
        compiler_params=_cparams(("parallel", "parallel")),
    )(x, y_na, y_f, y_b, bonus, g, mq, mk, mv, ln_w, ln_b, w_out, ffn_nw, *router)
    ns = b * s // MOE_SUB
    return (x_mid, h2.reshape(b * s, d), route.reshape(ns, ROUTE_ROWS, MOE_SUB),
            nch.reshape(ns, 1, ROUTER_LANES))


ROUTER_LANES = 128


MOE_SUB = 256
MOE_CHUNK = 16
MOE_LOCAL_CHUNKS = 48
MOE_BLOCK_CHUNKS = 32
MOE_STEP_BLOCKS = 2
MOE_STEP_SUBS = 4
MOE_TOP_K = 2
ROUTE_ROWS = 8
assert MOE_LOCAL_CHUNKS >= MOE_TOP_K * MOE_SUB // MOE_CHUNK + MOE_EXPERTS - 1


def _route_subtile(lg):
    t = lg.shape[0]
    lane = lax.broadcasted_iota(jnp.int32, (1, ROUTER_LANES), 1)
    lanef = lane.astype(F32)
    big = float(ROUTER_LANES)

    def argmax_lane(v):
        m = jnp.max(v, axis=-1, keepdims=True)
        return m, jnp.min(jnp.where(v == m, lanef, big), axis=-1, keepdims=True)

    is_group = lane < MOE_GROUPS
    gmax, gidx = argmax_lane(jnp.where(is_group, lg, NEG_BIG))
    gsum = jnp.sum(jnp.where(is_group, jnp.exp(lg - gmax), 0.0), axis=-1, keepdims=True)
    lo = MOE_GROUPS + MOE_EXPERTS_PER_GROUP * gidx
    el = jnp.where((lanef >= lo) & (lanef < lo + MOE_EXPERTS_PER_GROUP), lg, NEG_BIG)
    m1, i1 = argmax_lane(el)
    m2, i2 = argmax_lane(jnp.where(lanef == i1, NEG_BIG, el))
    ratio = jnp.exp(m2 - m1)
    w0 = 1.0 / (gsum * (1.0 + ratio))
    w1 = w0 * ratio

    oh0 = (lanef == i1 - MOE_GROUPS).astype(F32)
    oh1 = (lanef == i2 - MOE_GROUPS).astype(F32)
    row = lax.broadcasted_iota(jnp.int32, (t, t), 0)
    col = lax.broadcasted_iota(jnp.int32, (t, t), 1)
    before = (col < row).astype(BF16)
    cnt0 = jnp.sum(oh0, axis=0, keepdims=True)
    cnt1 = jnp.sum(oh1, axis=0, keepdims=True)
    rank0 = _dot(before, oh0.astype(BF16))
    rank1 = _dot(before, oh1.astype(BF16)) + cnt0
    nch = jnp.floor((cnt0 + cnt1 + (MOE_CHUNK - 1.0)) * (1.0 / MOE_CHUNK))
    li = lax.broadcasted_iota(jnp.int32, (ROUTER_LANES, ROUTER_LANES), 0)
    lj = lax.broadcasted_iota(jnp.int32, (ROUTER_LANES, ROUTER_LANES), 1)
    start = _dot(jnp.broadcast_to(nch, (8, ROUTER_LANES)).astype(BF16), (li < lj).astype(BF16))[0:1]
    pos0 = jnp.sum((start * MOE_CHUNK + rank0) * oh0, axis=-1, keepdims=True)
    pos1 = jnp.sum((start * MOE_CHUNK + rank1) * oh1, axis=-1, keepdims=True)
    rec = (jnp.where(lane == 0, pos0, 0.0) + jnp.where(lane == 1, pos1, 0.0)
           + jnp.where(lane == 2, w0, 0.0) + jnp.where(lane == 3, w1, 0.0))
    return jnp.transpose(rec)[0:ROUTE_ROWS], nch


def _route_tile(x, nw_ref, whi_ref, wlo_ref, b_ref, h_ref, route_ref, nch_ref):
    ms = jnp.mean(x * x, axis=-1, keepdims=True)
    h = x * lax.rsqrt(ms + RMS_EPS) * nw_ref[...]
    hi = h.astype(BF16)
    lo = (h - hi.astype(F32)).astype(BF16)
    h_ref[0] = hi
    lg = _dot(hi, whi_ref[...]) + _dot(hi, wlo_ref[...]) + _dot(lo, whi_ref[...]) + b_ref[...]
    for j in range(x.shape[0] // MOE_SUB):
        route_ref[0, j], nch_ref[0, j] = _route_subtile(lg[j * MOE_SUB:(j + 1) * MOE_SUB])


def _router_params(w_group, b_group, w_expert, b_expert):
    d = w_group.shape[0]
    pad = ROUTER_LANES - MOE_GROUPS - MOE_EXPERTS
    w = jnp.concatenate([w_group, w_expert, jnp.zeros((d, pad), F32)], axis=1)
    whi = w.astype(BF16)
    wlo = (w - whi.astype(F32)).astype(BF16)
    bias = jnp.concatenate([b_group, b_expert, jnp.zeros((pad,), F32)]).reshape(1, -1)
    return whi, wlo, bias


def _moe_plan(nch, n):
    nch = nch[:, 0, :MOE_EXPERTS].astype(jnp.int32)
    lc_end = jnp.cumsum(nch, axis=1)
    lc_start = lc_end - nch
    nloc = lc_end[:, -1]
    nblk = (jnp.sum(nch, axis=0) + MOE_BLOCK_CHUNKS - 1) // MOE_BLOCK_CHUNKS
    bend = jnp.cumsum(nblk)
    gch = ((bend - nblk) * MOE_BLOCK_CHUNKS)[None, :] + jnp.cumsum(nch, axis=0) - nch
    c = jnp.arange(MOE_LOCAL_CHUNKS, dtype=jnp.int32)
    in_run = (c[None, :, None] >= lc_start[:, None, :]) & (c[None, :, None] < lc_end[:, None, :])
    dest = jnp.sum(jnp.where(in_run, (gch - lc_start)[:, None, :] + c[None, :, None], 0), axis=-1)

    nblocks = _moe_blocks(n)
    b = jnp.arange(nblocks, dtype=jnp.int32)
    block_e = jnp.minimum(jnp.sum(b[:, None] >= bend[None, :], axis=-1), MOE_EXPERTS - 1)
    tot = jnp.sum(nch, axis=0)
    tail_start = (bend - nblk) * MOE_BLOCK_CHUNKS + tot
    ntail = nblk * MOE_BLOCK_CHUNKS - tot
    i32 = lambda a: a.astype(jnp.int32)
    return i32(dest), i32(nloc), i32(block_e), i32(bend[-1:]), i32(tail_start), i32(ntail)


def _moe_blocks(n):
    ns = n // MOE_SUB
    chunks = ns * (MOE_TOP_K * MOE_SUB // MOE_CHUNK + MOE_EXPERTS - 1) + MOE_EXPERTS * (MOE_BLOCK_CHUNKS - 1)
    per_step = MOE_BLOCK_CHUNKS * MOE_STEP_BLOCKS
    return -(-chunks // per_step) * MOE_STEP_BLOCKS


def _local_onehot(pos_rows, shape, row_axis):
    idx = lax.broadcasted_iota(jnp.int32, shape, row_axis)
    return (idx == pos_rows[0]) | (idx == pos_rows[1])


def _dispatch_kernel(dest_ref, nloc_ref, tail_ref, ntail_ref, nv_ref, h_ref, pos_ref, xs_ref,
                     buf, zbuf, sem, zsem):
    s = pl.program_id(0)
    ns = pl.num_programs(0)
    slot = s % 2
    bm = MOE_BLOCK_CHUNKS * MOE_CHUNK
    nblocks = xs_ref.shape[0] // bm

    def zero_fill(wait):
        def tail_copy(e, i):
            dst = xs_ref.at[pl.ds(pl.multiple_of((tail_ref[e] + i) * MOE_CHUNK, MOE_CHUNK), MOE_CHUNK)]
            return pltpu.make_async_copy(zbuf.at[pl.ds(0, MOE_CHUNK)], dst, zsem.at[0])

        def block_copy(b):
            return pltpu.make_async_copy(zbuf, xs_ref.at[pl.ds(pl.multiple_of(b * bm, bm), bm)], zsem.at[1])

        def run(cp):
            return cp.wait() if wait else cp.start()

        for e in range(MOE_EXPERTS):
            def tail_body(i, carry, e=e):
                run(tail_copy(e, i))
                return carry
            lax.fori_loop(0, ntail_ref[e], tail_body, 0)

        def block_body(b, carry):
            run(block_copy(b))
            return carry
        lax.fori_loop(nv_ref[0], nblocks, block_body, 0)

    @pl.when(s == 0)
    def _():
        zbuf[...] = jnp.zeros_like(zbuf)
        zero_fill(wait=False)

    def chunk_copy(sl, k, c, sub):
        src = buf.at[sl, k, pl.ds(pl.multiple_of(c * MOE_CHUNK, MOE_CHUNK), MOE_CHUNK)]
        dst = xs_ref.at[pl.ds(pl.multiple_of(dest_ref[sub, c] * MOE_CHUNK, MOE_CHUNK), MOE_CHUNK)]
        return pltpu.make_async_copy(src, dst, sem.at[sl])

    def for_chunks(step, sl, wait):
        for k in range(MOE_STEP_SUBS):
            sub = step * MOE_STEP_SUBS + k

            def body(c, carry, k=k, sub=sub):
                cp = chunk_copy(sl, k, c, sub)
                cp.wait() if wait else cp.start()
                return carry
            lax.fori_loop(0, nloc_ref[sub], body, 0)

    def wait_step(step, sl):
        for_chunks(step, sl, wait=True)

    @pl.when(s >= 2)
    def _():
        wait_step(s - 2, slot)

    onehots = []
    for k in range(MOE_STEP_SUBS):
        pos = pos_ref[k, 0:MOE_TOP_K].astype(jnp.int32)
        onehot = _local_onehot((pos[0:1], pos[1:2]), (MOE_LOCAL_CHUNKS * MOE_CHUNK, MOE_SUB), 0)
        onehots.append(jnp.where(onehot, 1.0, 0.0).astype(BF16))
    for k in range(MOE_STEP_SUBS):
        buf[slot, k] = _dot(onehots[k], h_ref[k * MOE_SUB:(k + 1) * MOE_SUB, :]).astype(BF16)
    for_chunks(s, slot, wait=False)

    @pl.when(s == ns - 1)
    def _():
        @pl.when(s >= 1)
        def _():
            wait_step(s - 1, 1 - slot)
        wait_step(s, slot)
        zero_fill(wait=True)


def _dispatch(h2, pos, dest, nloc, tail_start, ntail, nvalid):
    n, d = h2.shape
    ns = n // MOE_SUB
    bm = MOE_BLOCK_CHUNKS * MOE_CHUNK
    rows = MOE_LOCAL_CHUNKS * MOE_CHUNK
    return pl.pallas_call(
        _dispatch_kernel,
        grid_spec=pltpu.PrefetchScalarGridSpec(
            num_scalar_prefetch=5, grid=(ns // MOE_STEP_SUBS,),
            in_specs=[pl.BlockSpec((MOE_STEP_SUBS * MOE_SUB, d), lambda s, *_: (s, 0)),
                      pl.BlockSpec((MOE_STEP_SUBS, ROUTE_ROWS, MOE_SUB), lambda s, *_: (s, 0, 0))],
            out_specs=pl.BlockSpec(memory_space=pl.ANY),
            scratch_shapes=[pltpu.VMEM((2, MOE_STEP_SUBS, rows, d), BF16), pltpu.VMEM((bm, d), BF16),
                            pltpu.SemaphoreType.DMA((2,)), pltpu.SemaphoreType.DMA((2,))]),
        out_shape=jax.ShapeDtypeStruct((_moe_blocks(n) * bm, d), BF16),
        compiler_params=_cparams(("arbitrary",)),
    )(dest, nloc, tail_start, ntail, nvalid, h2, pos)


def _experts_kernel(be_ref, nv_ref, x_ref, *refs):
    del be_ref
    w_refs, o_ref = refs[:-1], refs[-1]
    bm = MOE_BLOCK_CHUNKS * MOE_CHUNK
    valid = pl.program_id(0) * MOE_STEP_BLOCKS < nv_ref[0]

    @pl.when(valid)
    def _():
        rows = [slice(j * bm, (j + 1) * bm) for j in range(MOE_STEP_BLOCKS)]
        gated = []
        for j, r in enumerate(rows):
            x = x_ref[r, :]
            gated.append((_dot(x, w_refs[3 * j][0]), _dot(x, w_refs[3 * j + 1][0])))
        hids = [(jax.nn.silu(gate) * up).astype(BF16) for gate, up in gated]
        for j, r in enumerate(rows):
            o_ref[r, :] = _dot(hids[j], w_refs[3 * j + 2][0]).astype(o_ref.dtype)

    @pl.when(jnp.logical_not(valid))
    def _():
        o_ref[...] = jnp.zeros_like(o_ref)


def _experts(xs, block_e, nvalid, w_gate, w_up, w_down):
    cap, d = xs.shape
    rows = MOE_STEP_BLOCKS * MOE_BLOCK_CHUNKS * MOE_CHUNK
    step = lambda p, be, nv: (jnp.minimum(p, (nv[0] - 1) // MOE_STEP_BLOCKS), 0)
    w_specs, w_args = [], []
    for j in range(MOE_STEP_BLOCKS):
        wsel = lambda p, be, nv, j=j: (be[jnp.minimum(p * MOE_STEP_BLOCKS + j, nv[0] - 1)], 0, 0)
        w_specs += [pl.BlockSpec((1, d, MOE_D_FF), wsel), pl.BlockSpec((1, d, MOE_D_FF), wsel),
                    pl.BlockSpec((1, MOE_D_FF, d), wsel)]
        w_args += [w_gate, w_up, w_down]
    return pl.pallas_call(
        _experts_kernel,
        grid_spec=pltpu.PrefetchScalarGridSpec(
            num_scalar_prefetch=2, grid=(cap // rows,),
            in_specs=[pl.BlockSpec((rows, d), step)] + w_specs,
            out_specs=pl.BlockSpec((rows, d), lambda p, be, nv: (p, 0))),
        out_shape=jax.ShapeDtypeStruct((cap, d), BF16),
        compiler_params=_cparams(("arbitrary",)),
    )(block_e, nvalid, xs, *w_args)


def _combine_kernel(dest_ref, nloc_ref, x_ref, route_ref, ys_ref, o_ref, buf, sem):
    s = pl.program_id(0)
    ns = pl.num_programs(0)
    slot = s % 2

    def chunk_copy(sl, k, c, sub):
        src = ys_ref.at[pl.ds(pl.multiple_of(dest_ref[sub, c] * MOE_CHUNK, MOE_CHUNK), MOE_CHUNK)]
        dst = buf.at[sl, k, pl.ds(pl.multiple_of(c * MOE_CHUNK, MOE_CHUNK), MOE_CHUNK)]
        return pltpu.make_async_copy(src, dst, sem.at[sl])

    def for_chunks(step, sl, wait):
        for k in range(MOE_STEP_SUBS):
            sub = step * MOE_STEP_SUBS + k

            def body(c, carry, k=k, sub=sub):
                cp = chunk_copy(sl, k, c, sub)
                cp.wait() if wait else cp.start()
                return carry
            lax.fori_loop(0, nloc_ref[sub], body, 0)

    @pl.when(s == 0)
    def _():
        buf[...] = jnp.zeros_like(buf)
        for_chunks(0, 0, wait=False)

    @pl.when(s + 1 < ns)
    def _():
        for_chunks(s + 1, 1 - slot, wait=False)

    for_chunks(s, slot, wait=True)

    idx = lax.broadcasted_iota(jnp.int32, (MOE_SUB, MOE_LOCAL_CHUNKS * MOE_CHUNK), 1)
    splits = []
    for k in range(MOE_STEP_SUBS):
        rec = jnp.transpose(route_ref[k])
        pos = rec[:, 0:MOE_TOP_K].astype(jnp.int32)
        pw = (jnp.where(idx == pos[:, 0:1], rec[:, 2:3], 0.0)
              + jnp.where(idx == pos[:, 1:2], rec[:, 3:4], 0.0))
        p_hi = pw.astype(BF16)
        splits.append((p_hi, (pw - p_hi.astype(F32)).astype(BF16)))
    for k, (p_hi, p_lo) in enumerate(splits):
        rows = slice(k * MOE_SUB, (k + 1) * MOE_SUB)
        o_ref[rows, :] = x_ref[rows, :] + _dot(p_hi, buf[slot, k]) + _dot(p_lo, buf[slot, k])


def _combine(x2, ys, route, dest, nloc):
    n, d = x2.shape
    ns = n // MOE_SUB
    rows = MOE_LOCAL_CHUNKS * MOE_CHUNK
    return pl.pallas_call(
        _combine_kernel,
        grid_spec=pltpu.PrefetchScalarGridSpec(
            num_scalar_prefetch=2, grid=(ns // MOE_STEP_SUBS,),
            in_specs=[pl.BlockSpec((MOE_STEP_SUBS * MOE_SUB, d), lambda s, *_: (s, 0)),
                      pl.BlockSpec((MOE_STEP_SUBS, ROUTE_ROWS, MOE_SUB), lambda s, *_: (s, 0, 0)),
                      pl.BlockSpec(memory_space=pl.ANY)],
            out_specs=pl.BlockSpec((MOE_STEP_SUBS * MOE_SUB, d), lambda s, *_: (s, 0)),
            scratch_shapes=[pltpu.VMEM((2, MOE_STEP_SUBS, rows, d), BF16), pltpu.SemaphoreType.DMA((2,))]),
        out_shape=jax.ShapeDtypeStruct((n, d), F32),
        compiler_params=_cparams(("arbitrary",)),
    )(dest, nloc, x2, route, ys)


def _moe(x2, h2, route, nch, w_gate, w_up, w_down):
    dest, nloc, block_e, nvalid, tail_start, ntail = _moe_plan(nch, x2.shape[0])
    xs = _dispatch(h2, route, dest, nloc, tail_start, ntail, nvalid)
    ys = _experts(xs, block_e, nvalid, w_gate, w_up, w_down)
    return _combine(x2, ys, route, dest, nloc)


def _tile(n, want):
    t = min(n, want)
    assert n % t == 0
    return t


def kernel(x, mem, attn_norm_w, w_in, na_q_norm_w, na_k_norm_w, na_rpb, rw_mu_prev, rw_mu_next, rw_w0, rw_w2, rw_a0, rw_a2, rw_g2, rw_k_k, rw_k_a, rw_r_k, rw_ln_w, rw_ln_b, mem_norm_w, w_mem_kv, mem_q_norm_w, mem_k_norm_w, w_out, ffn_norm_w, moe_w_group, moe_b_group, moe_w_expert, moe_b_expert, moe_w_gate, moe_w_up, moe_w_down):
    b, s, d = x.shape
    n = b * s
    depth = w_in.shape[0]
    assert s % (NA_BLOCK_ROWS * GRID_W) == 0 and s // GRID_W >= 2 * NA_KH
    tm = _tile(n, 512)
    ts = _tile(s, 512)
    for l in range(depth):
        q, k, v, rw, mq = _proj(
            x.reshape(n, d), _row(attn_norm_w[l]), w_in[l].astype(BF16),
            _row(jnp.tile(na_q_norm_w[l], NA_HEADS)), _row(jnp.tile(na_k_norm_w[l], NA_HEADS)),
            _row(jnp.tile(mem_q_norm_w[l], MEM_HEADS)), tm)
        y_na = _na(q.reshape(b, s, NA_WIDTH), k.reshape(b, s, NA_WIDTH), v.reshape(b, s, NA_WIDTH),
                   _na_bias_table(na_rpb[l]))
        r, vv, kk, kd, bb, lw, g, bonus = _rwprep(
            rw.reshape(b, s, RW_PROJ), rw_mu_prev[l], rw_mu_next[l], rw_w0[l], rw_w2[l], rw_a0[l],
            rw_a2[l], rw_g2[l], rw_k_k[l], rw_k_a[l], rw_r_k[l], ts)
        y_f, y_b = _wkv2(r, vv, kk, kd, bb, lw)
        mk, mv = _memkv(mem, _row(mem_norm_w[l]), w_mem_kv[l].astype(BF16),
                        _row(jnp.tile(mem_k_norm_w[l], MEM_HEADS)))
        x, h2, route, nch = _mixout(
            x, y_na, y_f, y_b, bonus, g, mq.reshape(b, s, MEM_WIDTH), mk, mv,
            _row(rw_ln_w[l]), _row(rw_ln_b[l]), w_out[l].astype(BF16), _row(ffn_norm_w[l]),
            _router_params(moe_w_group[l], moe_b_group[l], moe_w_expert[l], moe_b_expert[l]), ts)
        x = _moe(x.reshape(n, d), h2, route, nch, moe_w_gate[l].astype(BF16),
                 moe_w_up[l].astype(BF16), moe_w_down[l].astype(BF16)).reshape(b, s, d)
    return x
```

```python
import functools

import jax
import jax.numpy as jnp
import numpy as np
from jax import lax
from jax.experimental import pallas as pl
from jax.experimental.pallas import tpu as pltpu

F32 = jnp.float32
BF16 = jnp.bfloat16

GRID_W = 64
HEAD_DIM = 64
NA_HEADS = 8
NA_WIDTH = NA_HEADS * HEAD_DIM
NA_KH = 8
NA_KW = 16
RW_HEADS = 4
RW_WIDTH = RW_HEADS * HEAD_DIM
RW_LORA_W = 64
RW_LORA_A = 64
RW_LORA_G = 128
RW_PROJ = 3 * RW_WIDTH + 2 * RW_LORA_W + 2 * RW_LORA_A + RW_LORA_G
MEM_HEADS = 4
MEM_WIDTH = MEM_HEADS * HEAD_DIM
MOE_GROUPS = 4
MOE_EXPERTS_PER_GROUP = 4
MOE_EXPERTS = MOE_GROUPS * MOE_EXPERTS_PER_GROUP
MOE_D_FF = 512
RMS_EPS = 1e-6
RW_GN_EPS = 64e-5

LANES = 128
WKV_CHUNK = 64
NEG_BIG = -1e30
VMEM_LIMIT = 56 * 1024 * 1024


def _cparams(sem):
    return pltpu.CompilerParams(dimension_semantics=sem, vmem_limit_bytes=VMEM_LIMIT)


def _dot(a, b):
    return jnp.dot(a, b, preferred_element_type=F32)


def _dot_nt(a, b):
    return lax.dot_general(a, b, (((1,), (1,)), ((), ())), preferred_element_type=F32)


def _dot_tn(a, b):
    return lax.dot_general(a, b, (((0,), (0,)), ((), ())), preferred_element_type=F32)


def _split_dot(t, m):
    hi = t.astype(BF16)
    lo = (t - hi.astype(F32)).astype(BF16)
    return _dot(hi, m) + _dot(lo, m)


def _head_ones(width):
    i = lax.broadcasted_iota(jnp.int32, (width, width), 0) // HEAD_DIM
    j = lax.broadcasted_iota(jnp.int32, (width, width), 1) // HEAD_DIM
    return (i == j).astype(BF16)


def _row(v):
    return v.reshape(1, -1).astype(F32)


def _proj_kernel(x_ref, nw_ref, w_ref, qw_ref, kw_ref, mw_ref, q_ref, k_ref, v_ref, rw_ref, mq_ref):
    x = x_ref[...]
    ms = jnp.mean(x * x, axis=-1, keepdims=True)
    h = (x * lax.rsqrt(ms + RMS_EPS) * nw_ref[...]).astype(BF16)
    ones_na = _head_ones(NA_WIDTH)

    def head_norm(t, w, ones):
        ms = _dot((t * t).astype(BF16), ones) * (1.0 / HEAD_DIM)
        return t * lax.rsqrt(ms + RMS_EPS) * w

    o = 0
    q = _dot(h, w_ref[:, o:o + NA_WIDTH])
    q_ref[...] = (head_norm(q, qw_ref[...], ones_na) * (HEAD_DIM ** -0.5)).astype(BF16)
    o += NA_WIDTH
    k = _dot(h, w_ref[:, o:o + NA_WIDTH])
    k_ref[...] = head_norm(k, kw_ref[...], ones_na).astype(BF16)
    o += NA_WIDTH
    v_ref[...] = _dot(h, w_ref[:, o:o + NA_WIDTH]).astype(BF16)
    o += NA_WIDTH
    rw_ref[...] = _dot(h, w_ref[:, o:o + RW_PROJ])
    o += RW_PROJ
    mq = _dot(h, w_ref[:, o:o + MEM_WIDTH])
    mq_ref[...] = (head_norm(mq, mw_ref[...], _head_ones(MEM_WIDTH)) * (HEAD_DIM ** -0.5)).astype(BF16)


def _proj(x2, nw, w_in, qw, kw, mw, tm):
    n, d = x2.shape
    p_in = w_in.shape[1]
    tok = lambda w: pl.BlockSpec((tm, w), lambda i: (i, 0))
    full = lambda a: pl.BlockSpec(a.shape, lambda i: (0,) * a.ndim)
    return pl.pallas_call(
        _proj_kernel,
        grid=(n // tm,),
        in_specs=[tok(d), full(nw), full(w_in), full(qw), full(kw), full(mw)],
        out_specs=[tok(NA_WIDTH), tok(NA_WIDTH), tok(NA_WIDTH), tok(RW_PROJ), tok(MEM_WIDTH)],
        out_shape=[jax.ShapeDtypeStruct((n, NA_WIDTH), BF16)] * 3
        + [jax.ShapeDtypeStruct((n, RW_PROJ), F32), jax.ShapeDtypeStruct((n, MEM_WIDTH), BF16)],
        compiler_params=_cparams(("parallel",)),
    )(x2, nw, w_in, qw, kw, mw)


NA_BLOCK_ROWS = 8


def _na_bias_table(rpb):
    col = np.arange(GRID_W)
    cs = np.clip(col - NA_KW // 2, 0, GRID_W - NA_KW)
    dc = col[None, :] - col[:, None]
    inside = (col[None, :] >= cs[:, None]) & (col[None, :] < cs[:, None] + NA_KW)
    dci = np.clip(dc + NA_KW - 1, 0, 2 * NA_KW - 2)
    off = np.arange(NA_KH) - (NA_KH - 1)
    dri = off[:, None] + np.arange(NA_KH)[None, :] + NA_KH - 1
    row_sel = (dri[:, :, None] == np.arange(2 * NA_KH - 1)).astype(np.float32)
    col_sel = (dci[:, :, None] == np.arange(2 * NA_KW - 1)).astype(np.float32)
    t = jnp.einsum("hrq,onr,cdq->ohcnd", rpb.astype(F32), row_sel, col_sel,
                   precision=lax.Precision.HIGHEST)
    t = jnp.where(inside[None, None, :, None, :], t, NEG_BIG)
    return t.reshape(NA_KH, NA_HEADS // 2, 2 * GRID_W, NA_KH * GRID_W).astype(F32)


def _na_kernel(q_ref, kp_ref, kc_ref, kn_ref, vp_ref, vc_ref, vn_ref, bias_ref, o_ref, kwin, vwin, *, rows):
    j = pl.program_id(1)
    blk = NA_BLOCK_ROWS * GRID_W
    kwin[0:blk] = kp_ref[0]
    kwin[blk:2 * blk] = kc_ref[0]
    kwin[2 * blk:3 * blk] = kn_ref[0]
    vwin[0:blk] = vp_ref[0]
    vwin[blk:2 * blk] = vc_ref[0]
    vwin[2 * blk:3 * blk] = vn_ref[0]
    lane = lax.broadcasted_iota(jnp.int32, (1, LANES), 1)
    first = lane < HEAD_DIM

    def row_body(i, carry):
        r = j * NA_BLOCK_ROWS + i
        rs = jnp.clip(r - NA_KH // 2, 0, rows - NA_KH)
        oi = rs - r + (NA_KH - 1)
        start = pl.multiple_of((rs - (j - 1) * NA_BLOCK_ROWS) * GRID_W, GRID_W)
        qs = pl.multiple_of(i * GRID_W, GRID_W)
        pairs = [slice(p * LANES, (p + 1) * LANES) for p in range(NA_HEADS // 2)]
        scores = []
        for p, ls in enumerate(pairs):
            qp = q_ref[0, pl.ds(qs, GRID_W), ls]
            zero = jnp.zeros_like(qp)
            qst = jnp.concatenate([jnp.where(first, qp, zero), jnp.where(first, zero, qp)], axis=0)
            scores.append(_dot_nt(qst, kwin[pl.ds(start, NA_KH * GRID_W), ls]) + bias_ref[oi, p])
        probs, norms = [], []
        for s in scores:
            e = jnp.exp(s - jnp.max(s, axis=-1, keepdims=True))
            norms.append(jnp.sum(e, axis=-1, keepdims=True))
            probs.append(e.astype(BF16))
        for ls, e, l in zip(pairs, probs, norms):
            o = _dot(e, vwin[pl.ds(start, NA_KH * GRID_W), ls]) / l
            o_ref[0, pl.ds(qs, GRID_W), ls] = jnp.where(first, o[:GRID_W], o[GRID_W:]).astype(o_ref.dtype)
        return carry

    lax.fori_loop(0, NA_BLOCK_ROWS, row_body, 0, unroll=True)


def _na(q, k, v, bias):
    b, s, w = q.shape
    rows = s // GRID_W
    nblk = rows // NA_BLOCK_ROWS
    blk = NA_BLOCK_ROWS * GRID_W
    cur = pl.BlockSpec((1, blk, w), lambda bi, j: (bi, j, 0))
    prv = pl.BlockSpec((1, blk, w), lambda bi, j: (bi, jnp.maximum(j - 1, 0), 0))
    nxt = pl.BlockSpec((1, blk, w), lambda bi, j: (bi, jnp.minimum(j + 1, nblk - 1), 0))
    return pl.pallas_call(
        functools.partial(_na_kernel, rows=rows),
        grid=(b, nblk),
        in_specs=[cur, prv, cur, nxt, prv, cur, nxt,
                  pl.BlockSpec(bias.shape, lambda bi, j: (0, 0, 0, 0))],
        out_specs=cur,
        out_shape=jax.ShapeDtypeStruct((b, s, w), BF16),
        scratch_shapes=[pltpu.VMEM((3 * blk, w), BF16), pltpu.VMEM((3 * blk, w), BF16)],
        compiler_params=_cparams(("parallel", "parallel")),
    )(q, k, k, k, v, v, v, bias)


def _rwprep_kernel(rw_ref, prev_ref, next_ref, mup_ref, mun_ref, w0_ref, w2_ref, a0_ref, a2_ref, g2_ref,
                   kk_w_ref, ka_ref, rk_ref,
                   r_ref, v_ref, kk_ref, kd_ref, b_ref, lw_ref, g_ref, bonus_ref, *, nblk):
    i = pl.program_id(1)
    s = rw_ref[0]
    ts = s.shape[0]
    rowi = lax.broadcasted_iota(jnp.int32, (ts, 1), 0)
    halo_p = jnp.where(i > 0, prev_ref[0, 7:8, :], 0.0)
    halo_n = jnp.where(i < nblk - 1, next_ref[0, 0:1, :], 0.0)
    prev = jnp.where(rowi == 0, halo_p, pltpu.roll(s, 1, 0))
    nxt = jnp.where(rowi == ts - 1, halo_n, pltpu.roll(s, ts - 1, 0))
    s = s + mup_ref[...] * (prev - s) + mun_ref[...] * (nxt - s)

    c = RW_WIDTH
    r = s[:, 0:c]
    k = s[:, c:2 * c]
    v = s[:, 2 * c:3 * c]
    lw = s[:, 3 * c:3 * c + 2 * RW_LORA_W]
    la = s[:, 3 * c + 2 * RW_LORA_W:3 * c + 2 * RW_LORA_W + 2 * RW_LORA_A]
    lg = s[:, 3 * c + 2 * RW_LORA_W + 2 * RW_LORA_A:]
    ones = _head_ones(c)

    g_ref[0] = _dot(jax.nn.sigmoid(lg).astype(BF16), g2_ref[...])
    kk = k * kk_w_ref[...]
    nrm = jnp.sqrt(_split_dot(kk * kk, ones))
    kk = kk / jnp.maximum(nrm, 1e-12)
    wl_pre = w0_ref[...] + _dot(jnp.tanh(lw).astype(BF16), w2_ref[...])
    a_all = jax.nn.sigmoid(a0_ref[...] + _dot(la.astype(BF16), a2_ref[...]))
    wl = -jax.nn.softplus(-wl_pre) - 0.5
    logdecay = -jnp.exp(wl)
    r_ref[0] = r
    v_ref[0] = v
    kk_ref[0] = kk
    kd_sum = jnp.zeros_like(k)
    for d in range(2):
        a = a_all[:, d * c:(d + 1) * c]
        kd = k * (1.0 + (a - 1.0) * ka_ref[...])
        kd_ref[d, 0] = kd
        b_ref[d, 0] = kk * a
        lw_ref[d, 0] = logdecay[:, d * c:(d + 1) * c]
        kd_sum = kd_sum + kd
    bonus_ref[0] = _split_dot(r * kd_sum * rk_ref[...], ones) * v


def _blockdiag2(m):
    z = jnp.zeros_like(m[0])
    return jnp.concatenate([jnp.concatenate([m[0], z], 1), jnp.concatenate([z, m[1]], 1)], 0)


def _rwprep(rw, mu_prev, mu_next, w0, w2, a0, a2, g2, k_k, k_a, r_k, ts):
    b, s, pw = rw.shape
    nblk = s // ts
    c = RW_WIDTH
    cur = pl.BlockSpec((1, ts, pw), lambda bi, i: (bi, i, 0))
    prv = pl.BlockSpec((1, 8, pw), lambda bi, i: (bi, jnp.maximum(i * (ts // 8) - 1, 0), 0))
    nxt = pl.BlockSpec((1, 8, pw), lambda bi, i: (bi, jnp.minimum((i + 1) * (ts // 8), s // 8 - 1), 0))
    params = [_row(mu_prev), _row(mu_next), _row(w0), _blockdiag2(w2).astype(BF16), _row(a0),
              _blockdiag2(a2).astype(BF16), g2.astype(BF16), _row(k_k), _row(k_a), _row(r_k)]
    full = lambda a: pl.BlockSpec(a.shape, lambda bi, i: (0,) * a.ndim)
    one = pl.BlockSpec((1, ts, c), lambda bi, i: (bi, i, 0))
    two = pl.BlockSpec((2, 1, ts, c), lambda bi, i: (0, bi, i, 0))
    s1 = jax.ShapeDtypeStruct((b, s, c), F32)
    s2 = jax.ShapeDtypeStruct((2, b, s, c), F32)
    return pl.pallas_call(
        functools.partial(_rwprep_kernel, nblk=nblk),
        grid=(b, nblk),
        in_specs=[cur, prv, nxt] + [full(p) for p in params],
        out_specs=[one, one, one, two, two, two, one, one],
        out_shape=[s1, s1, s1, s2, s2, s2, s1, s1],
        compiler_params=_cparams(("parallel", "parallel")),
    )(rw, rw, rw, *params)


WKV_STEP_CHUNKS = 4


WKV2_BATCH = 4


def _wkv2_kernel(rf, vf, kkf, rb, vb, kkb, kdf, bf, lwf, kdb, bb, lwb, yf_ref, yb_ref, s_ref):
    cc = WKV_CHUNK
    nb = rf.shape[0]
    nchunk = rf.shape[1] // cc

    @pl.when(pl.program_id(1) == 0)
    def _():
        s_ref[...] = jnp.zeros_like(s_ref)

    same_head = ((lax.broadcasted_iota(jnp.int32, (LANES, LANES), 0) < HEAD_DIM)
                 == (lax.broadcasted_iota(jnp.int32, (LANES, LANES), 1) < HEAD_DIM))
    rowc = lax.broadcasted_iota(jnp.int32, (cc, LANES), 0)
    colc = lax.broadcasted_iota(jnp.int32, (cc, LANES), 1) % cc
    eye = (colc == rowc).astype(F32)
    strict = (colc < rowc, colc > rowc)
    incl = (colc <= rowc, colc >= rowc)
    first = lax.broadcasted_iota(jnp.int32, (1, LANES), 1) < HEAD_DIM
    second = jnp.logical_not(first)

    def stack_heads(t):
        z = jnp.zeros_like(t)
        return jnp.concatenate([jnp.where(first, t, z), jnp.where(second, t, z)], axis=0)

    def make_chains(c):
        chains = []
        for bi in range(nb):
            for d in range(2):
                j = nchunk - 1 - c if d else c
                sl = slice(j * cc, (j + 1) * cc)
                src = (rb, vb, kkb, kdb, bb, lwb) if d else (rf, vf, kkf, kdf, bf, lwf)
                for p in range(RW_HEADS // 2):
                    ls = slice(p * LANES, (p + 1) * LANES)
                    chains.append(dict(
                        bi=bi, d=d, p=p, sl=sl, ls=ls,
                        r=src[0][bi, sl, ls], v=src[1][bi, sl, ls], kk=src[2][bi, sl, ls],
                        kd=src[3][0, bi, sl, ls], b=src[4][0, bi, sl, ls], lw=src[5][0, bi, sl, ls]))
        return chains

    def solve_chunks(chains):
        for ch in chains:
            cum = ch["lw"]
            sh = 1
            while sh < cc:
                if ch["d"]:
                    cum = cum + jnp.where(rowc < cc - sh, pltpu.roll(cum, cc - sh, 0), 0.0)
                else:
                    cum = cum + jnp.where(rowc >= sh, pltpu.roll(cum, sh, 0), 0.0)
                sh *= 2
            tot = cum[0:1] if ch["d"] else cum[cc - 1:cc]
            at = (-ch["kk"] * jnp.exp(cum - ch["lw"])).astype(BF16)
            einv = jnp.exp(-cum)
            eh = jnp.exp(tot - cum)
            ch["etot"] = jnp.exp(tot)
            ch["rt"] = (ch["r"] * jnp.exp(cum)).astype(BF16)
            ch["vb"] = ch["v"].astype(BF16)
            ch["v_st"] = stack_heads(ch["vb"])
            ch["at_st"] = stack_heads(at)
            ch["lhs"] = jnp.concatenate([at, ch["rt"]], axis=0)
            ch["rhs"] = jnp.concatenate([stack_heads((ch["b"] * einv).astype(BF16)),
                                         stack_heads((ch["kd"] * einv).astype(BF16))], axis=0)
            ch["kb_hat"] = jnp.concatenate([ch["kd"] * eh, ch["b"] * eh], axis=0).astype(BF16)
        for ch in chains:
            m1 = _dot_nt(ch["lhs"], ch["rhs"])
            d = ch["d"]
            ch["n"] = jnp.where(strict[d], m1[:cc, :LANES], 0.0)
            a_ak = jnp.where(strict[d], m1[:cc, LANES:], 0.0)
            ch["a_rb"] = jnp.where(incl[d], m1[cc:, :LANES], 0.0).astype(BF16)
            a_rk = jnp.where(incl[d], m1[cc:, LANES:], 0.0)
            ch["a_k"] = jnp.concatenate([a_ak, a_rk], axis=0).astype(BF16)
        for ch in chains:
            nbf = ch["n"].astype(BF16)
            ch["pw"] = _dot(nbf, stack_heads(nbf))
            ch["t"] = eye + ch["n"]
        for _ in range(cc.bit_length() - 3):
            for ch in chains:
                pwb = ch["pw"].astype(BF16)
                tp = _dot(jnp.concatenate([ch["t"].astype(BF16), pwb], axis=0), stack_heads(pwb))
                ch["t"] = ch["t"] + tp[:cc]
                ch["pw"] = tp[cc:]
        for ch in chains:
            ch["t"] = (ch["t"] + _dot(ch["t"].astype(BF16), stack_heads(ch["pw"].astype(BF16)))).astype(BF16)
        for ch in chains:
            ch["av"] = _dot(ch["a_k"], ch["v_st"])
        for ch in chains:
            akv = stack_heads(ch["av"][:cc].astype(BF16))
            tx = _dot(ch["t"], jnp.concatenate([ch["at_st"], akv], axis=1))
            ch["atp"] = tx[:, :LANES].astype(BF16)
            ch["u0"] = tx[:, LANES:]

    def advance_state(chains):
        for ch in chains:
            ch["st"] = s_ref[ch["bi"], ch["d"], ch["p"]]
            qs = _dot_nt(jnp.concatenate([ch["atp"], ch["rt"]], axis=0), ch["st"].astype(BF16))
            ch["u"] = (qs[:cc] + ch["u0"]).astype(BF16)
            ch["ys"] = qs[cc:] + ch["av"][cc:]
        for ch in chains:
            upd = _dot_tn(jnp.concatenate([ch["vb"], ch["u"]], axis=0), ch["kb_hat"])
            s_ref[ch["bi"], ch["d"], ch["p"]] = ch["st"] * ch["etot"] + jnp.where(same_head, upd, 0.0)
        for ch in chains:
            y_ref = yb_ref if ch["d"] else yf_ref
            y_ref[ch["bi"], ch["sl"], ch["ls"]] = ch["ys"] + _dot(ch["a_rb"], stack_heads(ch["u"]))

    per_chunk = [make_chains(c) for c in range(nchunk)]
    solve_chunks([ch for chains in per_chunk for ch in chains])
    for chains in per_chunk:
        advance_state(chains)


def _wkv2(r, v, kk, kd, bb, lw):
    b, s, c = r.shape
    tt = WKV_CHUNK * WKV_STEP_CHUNKS
    nblk = s // tt
    nb = WKV2_BATCH if b % WKV2_BATCH == 0 else 1
    fwd = pl.BlockSpec((nb, tt, c), lambda bi, i: (bi, i, 0))
    bwd = pl.BlockSpec((nb, tt, c), lambda bi, i: (bi, nblk - 1 - i, 0))
    fwd2 = pl.BlockSpec((1, nb, tt, c), lambda bi, i: (0, bi, i, 0))
    bwd2 = pl.BlockSpec((1, nb, tt, c), lambda bi, i: (1, bi, nblk - 1 - i, 0))
    return pl.pallas_call(
        _wkv2_kernel,
        grid=(b // nb, nblk),
        in_specs=[fwd, fwd, fwd, bwd, bwd, bwd, fwd2, fwd2, fwd2, bwd2, bwd2, bwd2],
        out_specs=[fwd, bwd],
        out_shape=[jax.ShapeDtypeStruct((b, s, c), F32)] * 2,
        scratch_shapes=[pltpu.VMEM((nb, 2, RW_HEADS // 2, LANES, LANES), F32)],
        compiler_params=_cparams(("parallel", "arbitrary")),
    )(r, v, kk, r, v, kk, kd, bb, lw, kd, bb, lw)


def _memkv_kernel(mem_ref, nw_ref, w_ref, kw_ref, k_ref, v_ref):
    x = mem_ref[0]
    ms = jnp.mean(x * x, axis=-1, keepdims=True)
    h = (x * lax.rsqrt(ms + RMS_EPS) * nw_ref[...]).astype(BF16)
    kv = _dot(h, w_ref[...])
    k = kv[:, :MEM_WIDTH]
    ms = _split_dot(k * k, _head_ones(MEM_WIDTH)) * (1.0 / HEAD_DIM)
    k_ref[0] = (k * lax.rsqrt(ms + RMS_EPS) * kw_ref[...]).astype(BF16)
    v_ref[0] = kv[:, MEM_WIDTH:].astype(BF16)


def _memkv(mem, nw, w_kv, kw):
    b, m, d = mem.shape
    full = lambda a: pl.BlockSpec(a.shape, lambda bi: (0,) * a.ndim)
    out = pl.BlockSpec((1, m, MEM_WIDTH), lambda bi: (bi, 0, 0))
    return pl.pallas_call(
        _memkv_kernel,
        grid=(b,),
        in_specs=[pl.BlockSpec((1, m, d), lambda bi: (bi, 0, 0)), full(nw), full(w_kv), full(kw)],
        out_specs=[out, out],
        out_shape=[jax.ShapeDtypeStruct((b, m, MEM_WIDTH), BF16)] * 2,
        compiler_params=_cparams(("parallel",)),
    )(mem, nw, w_kv, kw)


MIX_PARTS = 2


def _mixout_kernel(x_ref, na_ref, yf_ref, yb_ref, bonus_ref, g_ref, mq_ref, mk_ref, mv_ref,
                   lnw_ref, lnb_ref, wo_ref, fnw_ref, whi_ref, wlo_ref, rb_ref,
                   o_ref, h_ref, route_ref, nch_ref):
    part = x_ref.shape[1] // MIX_PARTS
    parts = [slice(j * part, (j + 1) * part) for j in range(MIX_PARTS)]
    ones = _head_ones(RW_WIDTH)
    y_rws = []
    for rows in parts:
        y = yf_ref[0, rows, :] + yb_ref[0, rows, :]
        mu = _split_dot(y, ones) * (1.0 / HEAD_DIM)
        yc = y - mu
        var = _split_dot(yc * yc, ones) * (1.0 / HEAD_DIM)
        yn = yc * lax.rsqrt(var + RW_GN_EPS) * lnw_ref[...] + lnb_ref[...]
        y_rws.append(((yn + bonus_ref[0, rows, :]) * g_ref[0, rows, :]).astype(BF16))

    first = lax.broadcasted_iota(jnp.int32, (1, LANES), 1) < HEAD_DIM
    mems = [[] for _ in parts]
    for p in range(MEM_HEADS // 2):
        ls = slice(p * LANES, (p + 1) * LANES)
        kp = mk_ref[0, :, ls]
        vp = mv_ref[0, :, ls]
        for j, rows in enumerate(parts):
            qp = mq_ref[0, rows, ls]
            outs = []
            for hh in range(2):
                sel = first if hh == 0 else jnp.logical_not(first)
                s = _dot_nt(jnp.where(sel, qp, jnp.zeros_like(qp)), kp)
                m = jnp.max(s, axis=-1, keepdims=True)
                e = jnp.exp(s - m)
                l = jnp.sum(e, axis=-1, keepdims=True)
                outs.append(_dot(e.astype(BF16), vp) / l)
            mems[j].append(jnp.where(first, outs[0], outs[1]).astype(BF16))

    x_mids = []
    for j, rows in enumerate(parts):
        acc = _dot(na_ref[0, rows, :], wo_ref[0:NA_WIDTH, :])
        acc = acc + _dot(y_rws[j], wo_ref[NA_WIDTH:NA_WIDTH + RW_WIDTH, :])
        acc = acc + _dot(jnp.concatenate(mems[j], axis=1), wo_ref[NA_WIDTH + RW_WIDTH:, :])
        x_mids.append(x_ref[0, rows, :] + acc)
    for j, rows in enumerate(parts):
        o_ref[0, rows, :] = x_mids[j]
    for j, rows in enumerate(parts):
        _route_tile(x_mids[j], rows, j * (part // MOE_SUB), fnw_ref, whi_ref, wlo_ref, rb_ref,
                    h_ref, route_ref, nch_ref)


def _mixout(x, y_na, y_f, y_b, bonus, g, mq, mk, mv, ln_w, ln_b, w_out, ffn_nw, router, tm):
    b, s, d = x.shape
    m = mk.shape[1]
    sub = tm // MOE_SUB
    tok = lambda w: pl.BlockSpec((1, tm, w), lambda bi, i: (bi, i, 0))
    full = lambda a: pl.BlockSpec(a.shape, lambda bi, i: (0,) * a.ndim)
    memb = pl.BlockSpec((1, m, MEM_WIDTH), lambda bi, i: (bi, 0, 0))
    x_mid, h2, route, nch = pl.pallas_call(
        _mixout_kernel,
        grid=(b, s // tm),
        in_specs=[tok(d), tok(NA_WIDTH), tok(RW_WIDTH), tok(RW_WIDTH), tok(RW_WIDTH), tok(RW_WIDTH),
                  tok(MEM_WIDTH), memb, memb, full(ln_w), full(ln_b), full(w_out), full(ffn_nw)]
        + [full(p) for p in router],
        out_specs=[tok(d), tok(d),
                   pl.BlockSpec((1, sub, ROUTE_ROWS, MOE_SUB), lambda bi, i: (bi, i, 0, 0)),
                   pl.BlockSpec((1, sub, 1, ROUTER_LANES), lambda bi, i: (bi, i, 0, 0))],
        out_shape=[jax.ShapeDtypeStruct((b, s, d), F32), jax.ShapeDtypeStruct((b, s, d), BF16),
                   jax.ShapeDtypeStruct((b, s // MOE_SUB, ROUTE_ROWS, MOE_SUB), F32),
                   jax.ShapeDtypeStruct((b, s // MOE_SUB, 1, ROUTER_LANES), F32)],
        compiler_params=_cparams(("parallel", "parallel")),
    )(x, y_na, y_f, y_b, bonus, g, mq, mk, mv, ln_w, ln_b, w_out, ffn_nw, *router)
    ns = b * s // MOE_SUB
    return (x_mid, h2.reshape(b * s, d), route.reshape(ns, ROUTE_ROWS, MOE_SUB),
            nch.reshape(ns, 1, ROUTER_LANES))


ROUTER_LANES = 128


MOE_SUB = 256
MOE_CHUNK = 16
MOE_LOCAL_CHUNKS = 48
MOE_BLOCK_CHUNKS = 32
MOE_STEP_BLOCKS = 2
MOE_STEP_SUBS = 4
MOE_TOP_K = 2
ROUTE_ROWS = 8
assert MOE_LOCAL_CHUNKS >= MOE_TOP_K * MOE_SUB // MOE_CHUNK + MOE_EXPERTS - 1


def _route_subtile(lg):
    t = lg.shape[0]
    lane = lax.broadcasted_iota(jnp.int32, (1, ROUTER_LANES), 1)
    lanef = lane.astype(F32)
    big = float(ROUTER_LANES)

    def argmax_lane(v):
        m = jnp.max(v, axis=-1, keepdims=True)
        return m, jnp.min(jnp.where(v == m, lanef, big), axis=-1, keepdims=True)

    is_group = lane < MOE_GROUPS
    gmax, gidx = argmax_lane(jnp.where(is_group, lg, NEG_BIG))
    gsum = jnp.sum(jnp.where(is_group, jnp.exp(lg - gmax), 0.0), axis=-1, keepdims=True)
    lo = MOE_GROUPS + MOE_EXPERTS_PER_GROUP * gidx
    el = jnp.where((lanef >= lo) & (lanef < lo + MOE_EXPERTS_PER_GROUP), lg, NEG_BIG)
    m1, i1 = argmax_lane(el)
    m2, i2 = argmax_lane(jnp.where(lanef == i1, NEG_BIG, el))
    ratio = jnp.exp(m2 - m1)
    w0 = 1.0 / (gsum * (1.0 + ratio))
    w1 = w0 * ratio

    oh0 = (lanef == i1 - MOE_GROUPS).astype(F32)
    oh1 = (lanef == i2 - MOE_GROUPS).astype(F32)
    row = lax.broadcasted_iota(jnp.int32, (t, t), 0)
    col = lax.broadcasted_iota(jnp.int32, (t, t), 1)
    before = (col < row).astype(BF16)
    cnt0 = jnp.sum(oh0, axis=0, keepdims=True)
    cnt1 = jnp.sum(oh1, axis=0, keepdims=True)
    rank0 = _dot(before, oh0.astype(BF16))
    rank1 = _dot(before, oh1.astype(BF16)) + cnt0
    nch = jnp.floor((cnt0 + cnt1 + (MOE_CHUNK - 1.0)) * (1.0 / MOE_CHUNK))
    li = lax.broadcasted_iota(jnp.int32, (ROUTER_LANES, ROUTER_LANES), 0)
    lj = lax.broadcasted_iota(jnp.int32, (ROUTER_LANES, ROUTER_LANES), 1)
    start = _dot(jnp.broadcast_to(nch, (8, ROUTER_LANES)).astype(BF16), (li < lj).astype(BF16))[0:1]
    pos0 = jnp.sum((start * MOE_CHUNK + rank0) * oh0, axis=-1, keepdims=True)
    pos1 = jnp.sum((start * MOE_CHUNK + rank1) * oh1, axis=-1, keepdims=True)
    rec = (jnp.where(lane == 0, pos0, 0.0) + jnp.where(lane == 1, pos1, 0.0)
           + jnp.where(lane == 2, w0, 0.0) + jnp.where(lane == 3, w1, 0.0))
    return jnp.transpose(rec)[0:ROUTE_ROWS], nch


def _route_tile(x, rows, sub0, nw_ref, whi_ref, wlo_ref, b_ref, h_ref, route_ref, nch_ref):
    ms = jnp.mean(x * x, axis=-1, keepdims=True)
    h = x * lax.rsqrt(ms + RMS_EPS) * nw_ref[...]
    hi = h.astype(BF16)
    lo = (h - hi.astype(F32)).astype(BF16)
    h_ref[0, rows, :] = hi
    lg = _dot(hi, whi_ref[...]) + _dot(hi, wlo_ref[...]) + _dot(lo, whi_ref[...]) + b_ref[...]
    for j in range(x.shape[0] // MOE_SUB):
        route_ref[0, sub0 + j], nch_ref[0, sub0 + j] = _route_subtile(lg[j * MOE_SUB:(j + 1) * MOE_SUB])


def _router_params(w_group, b_group, w_expert, b_expert):
    d = w_group.shape[0]
    pad = ROUTER_LANES - MOE_GROUPS - MOE_EXPERTS
    w = jnp.concatenate([w_group, w_expert, jnp.zeros((d, pad), F32)], axis=1)
    whi = w.astype(BF16)
    wlo = (w - whi.astype(F32)).astype(BF16)
    bias = jnp.concatenate([b_group, b_expert, jnp.zeros((pad,), F32)]).reshape(1, -1)
    return whi, wlo, bias


def _moe_plan(nch, n):
    nch = nch[:, 0, :MOE_EXPERTS].astype(jnp.int32)
    lc_end = jnp.cumsum(nch, axis=1)
    lc_start = lc_end - nch
    nloc = lc_end[:, -1]
    nblk = (jnp.sum(nch, axis=0) + MOE_BLOCK_CHUNKS - 1) // MOE_BLOCK_CHUNKS
    bend = jnp.cumsum(nblk)
    gch = ((bend - nblk) * MOE_BLOCK_CHUNKS)[None, :] + jnp.cumsum(nch, axis=0) - nch
    c = jnp.arange(MOE_LOCAL_CHUNKS, dtype=jnp.int32)
    in_run = (c[None, :, None] >= lc_start[:, None, :]) & (c[None, :, None] < lc_end[:, None, :])
    dest = jnp.sum(jnp.where(in_run, (gch - lc_start)[:, None, :] + c[None, :, None], 0), axis=-1)

    nblocks = _moe_blocks(n)
    b = jnp.arange(nblocks, dtype=jnp.int32)
    block_e = jnp.minimum(jnp.sum(b[:, None] >= bend[None, :], axis=-1), MOE_EXPERTS - 1)
    tot = jnp.sum(nch, axis=0)
    tail_start = (bend - nblk) * MOE_BLOCK_CHUNKS + tot
    ntail = nblk * MOE_BLOCK_CHUNKS - tot
    i32 = lambda a: a.astype(jnp.int32)
    return i32(dest), i32(nloc), i32(block_e), i32(bend[-1:]), i32(tail_start), i32(ntail)


def _moe_blocks(n):
    ns = n // MOE_SUB
    chunks = ns * (MOE_TOP_K * MOE_SUB // MOE_CHUNK + MOE_EXPERTS - 1) + MOE_EXPERTS * (MOE_BLOCK_CHUNKS - 1)
    per_step = MOE_BLOCK_CHUNKS * MOE_STEP_BLOCKS
    return -(-chunks // per_step) * MOE_STEP_BLOCKS


def _local_onehot(pos_rows, shape, row_axis):
    idx = lax.broadcasted_iota(jnp.int32, shape, row_axis)
    return (idx == pos_rows[0]) | (idx == pos_rows[1])


def _dispatch_kernel(dest_ref, nloc_ref, tail_ref, ntail_ref, nv_ref, h_ref, pos_ref, xs_ref,
                     buf, zbuf, sem, zsem):
    s = pl.program_id(0)
    ns = pl.num_programs(0)
    slot = s % 2
    bm = MOE_BLOCK_CHUNKS * MOE_CHUNK
    nblocks = xs_ref.shape[0] // bm

    def zero_fill(wait):
        def tail_copy(e, i):
            dst = xs_ref.at[pl.ds(pl.multiple_of((tail_ref[e] + i) * MOE_CHUNK, MOE_CHUNK), MOE_CHUNK)]
            return pltpu.make_async_copy(zbuf.at[pl.ds(0, MOE_CHUNK)], dst, zsem.at[0])

        def block_copy(b):
            return pltpu.make_async_copy(zbuf, xs_ref.at[pl.ds(pl.multiple_of(b * bm, bm), bm)], zsem.at[1])

        def run(cp):
            return cp.wait() if wait else cp.start()

        for e in range(MOE_EXPERTS):
            def tail_body(i, carry, e=e):
                run(tail_copy(e, i))
                return carry
            lax.fori_loop(0, ntail_ref[e], tail_body, 0)

        def block_body(b, carry):
            run(block_copy(b))
            return carry
        lax.fori_loop(nv_ref[0], nblocks, block_body, 0)

    @pl.when(s == 0)
    def _():
        zbuf[...] = jnp.zeros_like(zbuf)
        zero_fill(wait=False)

    def chunk_copy(sl, k, c, sub):
        src = buf.at[sl, k, pl.ds(pl.multiple_of(c * MOE_CHUNK, MOE_CHUNK), MOE_CHUNK)]
        dst = xs_ref.at[pl.ds(pl.multiple_of(dest_ref[sub, c] * MOE_CHUNK, MOE_CHUNK), MOE_CHUNK)]
        return pltpu.make_async_copy(src, dst, sem.at[sl])

    def for_chunks(step, sl, wait):
        for k in range(MOE_STEP_SUBS):
            sub = step * MOE_STEP_SUBS + k

            def body(c, carry, k=k, sub=sub):
                cp = chunk_copy(sl, k, c, sub)
                cp.wait() if wait else cp.start()
                return carry
            lax.fori_loop(0, nloc_ref[sub], body, 0)

    def wait_step(step, sl):
        for_chunks(step, sl, wait=True)

    @pl.when(s >= 2)
    def _():
        wait_step(s - 2, slot)

    onehots = []
    for k in range(MOE_STEP_SUBS):
        pos = pos_ref[k, 0:MOE_TOP_K].astype(jnp.int32)
        onehot = _local_onehot((pos[0:1], pos[1:2]), (MOE_LOCAL_CHUNKS * MOE_CHUNK, MOE_SUB), 0)
        onehots.append(jnp.where(onehot, 1.0, 0.0).astype(BF16))
    for k in range(MOE_STEP_SUBS):
        buf[slot, k] = _dot(onehots[k], h_ref[k * MOE_SUB:(k + 1) * MOE_SUB, :]).astype(BF16)
    for_chunks(s, slot, wait=False)

    @pl.when(s == ns - 1)
    def _():
        @pl.when(s >= 1)
        def _():
            wait_step(s - 1, 1 - slot)
        wait_step(s, slot)
        zero_fill(wait=True)


def _dispatch(h2, pos, dest, nloc, tail_start, ntail, nvalid):
    n, d = h2.shape
    ns = n // MOE_SUB
    bm = MOE_BLOCK_CHUNKS * MOE_CHUNK
    rows = MOE_LOCAL_CHUNKS * MOE_CHUNK
    return pl.pallas_call(
        _dispatch_kernel,
        grid_spec=pltpu.PrefetchScalarGridSpec(
            num_scalar_prefetch=5, grid=(ns // MOE_STEP_SUBS,),
            in_specs=[pl.BlockSpec((MOE_STEP_SUBS * MOE_SUB, d), lambda s, *_: (s, 0)),
                      pl.BlockSpec((MOE_STEP_SUBS, ROUTE_ROWS, MOE_SUB), lambda s, *_: (s, 0, 0))],
            out_specs=pl.BlockSpec(memory_space=pl.ANY),
            scratch_shapes=[pltpu.VMEM((2, MOE_STEP_SUBS, rows, d), BF16), pltpu.VMEM((bm, d), BF16),
                            pltpu.SemaphoreType.DMA((2,)), pltpu.SemaphoreType.DMA((2,))]),
        out_shape=jax.ShapeDtypeStruct((_moe_blocks(n) * bm, d), BF16),
        compiler_params=_cparams(("arbitrary",)),
    )(dest, nloc, tail_start, ntail, nvalid, h2, pos)


def _experts_kernel(be_ref, nv_ref, x_ref, *refs):
    del be_ref
    w_refs, o_ref = refs[:-1], refs[-1]
    bm = MOE_BLOCK_CHUNKS * MOE_CHUNK
    valid = pl.program_id(0) * MOE_STEP_BLOCKS < nv_ref[0]

    @pl.when(valid)
    def _():
        rows = [slice(j * bm, (j + 1) * bm) for j in range(MOE_STEP_BLOCKS)]
        gated = []
        for j, r in enumerate(rows):
            x = x_ref[r, :]
            gated.append((_dot(x, w_refs[3 * j][0]), _dot(x, w_refs[3 * j + 1][0])))
        hids = [(jax.nn.silu(gate) * up).astype(BF16) for gate, up in gated]
        for j, r in enumerate(rows):
            o_ref[r, :] = _dot(hids[j], w_refs[3 * j + 2][0]).astype(o_ref.dtype)

    @pl.when(jnp.logical_not(valid))
    def _():
        o_ref[...] = jnp.zeros_like(o_ref)


def _experts(xs, block_e, nvalid, w_gate, w_up, w_down):
    cap, d = xs.shape
    rows = MOE_STEP_BLOCKS * MOE_BLOCK_CHUNKS * MOE_CHUNK
    step = lambda p, be, nv: (jnp.minimum(p, (nv[0] - 1) // MOE_STEP_BLOCKS), 0)
    w_specs, w_args = [], []
    for j in range(MOE_STEP_BLOCKS):
        wsel = lambda p, be, nv, j=j: (be[jnp.minimum(p * MOE_STEP_BLOCKS + j, nv[0] - 1)], 0, 0)
        w_specs += [pl.BlockSpec((1, d, MOE_D_FF), wsel), pl.BlockSpec((1, d, MOE_D_FF), wsel),
                    pl.BlockSpec((1, MOE_D_FF, d), wsel)]
        w_args += [w_gate, w_up, w_down]
    return pl.pallas_call(
        _experts_kernel,
        grid_spec=pltpu.PrefetchScalarGridSpec(
            num_scalar_prefetch=2, grid=(cap // rows,),
            in_specs=[pl.BlockSpec((rows, d), step)] + w_specs,
            out_specs=pl.BlockSpec((rows, d), lambda p, be, nv: (p, 0))),
        out_shape=jax.ShapeDtypeStruct((cap, d), BF16),
        compiler_params=_cparams(("arbitrary",)),
    )(block_e, nvalid, xs, *w_args)


def _combine_kernel(dest_ref, nloc_ref, x_ref, route_ref, ys_ref, o_ref, buf, sem):
    s = pl.program_id(0)
    ns = pl.num_programs(0)
    slot = s % 2

    def chunk_copy(sl, k, c, sub):
        src = ys_ref.at[pl.ds(pl.multiple_of(dest_ref[sub, c] * MOE_CHUNK, MOE_CHUNK), MOE_CHUNK)]
        dst = buf.at[sl, k, pl.ds(pl.multiple_of(c * MOE_CHUNK, MOE_CHUNK), MOE_CHUNK)]
        return pltpu.make_async_copy(src, dst, sem.at[sl])

    def for_chunks(step, sl, wait):
        for k in range(MOE_STEP_SUBS):
            sub = step * MOE_STEP_SUBS + k

            def body(c, carry, k=k, sub=sub):
                cp = chunk_copy(sl, k, c, sub)
                cp.wait() if wait else cp.start()
                return carry
            lax.fori_loop(0, nloc_ref[sub], body, 0)

    @pl.when(s == 0)
    def _():
        buf[...] = jnp.zeros_like(buf)
        for_chunks(0, 0, wait=False)

    @pl.when(s + 1 < ns)
    def _():
        for_chunks(s + 1, 1 - slot, wait=False)

    for_chunks(s, slot, wait=True)

    idx = lax.broadcasted_iota(jnp.int32, (MOE_SUB, MOE_LOCAL_CHUNKS * MOE_CHUNK), 1)
    splits = []
    for k in range(MOE_STEP_SUBS):
        rec = jnp.transpose(route_ref[k])
        pos = rec[:, 0:MOE_TOP_K].astype(jnp.int32)
        pw = (jnp.where(idx == pos[:, 0:1], rec[:, 2:3], 0.0)
              + jnp.where(idx == pos[:, 1:2], rec[:, 3:4], 0.0))
        p_hi = pw.astype(BF16)
        splits.append((p_hi, (pw - p_hi.astype(F32)).astype(BF16)))
    for k, (p_hi, p_lo) in enumerate(splits):
        rows = slice(k * MOE_SUB, (k + 1) * MOE_SUB)
        o_ref[rows, :] = x_ref[rows, :] + _dot(p_hi, buf[slot, k]) + _dot(p_lo, buf[slot, k])


def _combine(x2, ys, route, dest, nloc):
    n, d = x2.shape
    ns = n // MOE_SUB
    rows = MOE_LOCAL_CHUNKS * MOE_CHUNK
    return pl.pallas_call(
        _combine_kernel,
        grid_spec=pltpu.PrefetchScalarGridSpec(
            num_scalar_prefetch=2, grid=(ns // MOE_STEP_SUBS,),
            in_specs=[pl.BlockSpec((MOE_STEP_SUBS * MOE_SUB, d), lambda s, *_: (s, 0)),
                      pl.BlockSpec((MOE_STEP_SUBS, ROUTE_ROWS, MOE_SUB), lambda s, *_: (s, 0, 0)),
                      pl.BlockSpec(memory_space=pl.ANY)],
            out_specs=pl.BlockSpec((MOE_STEP_SUBS * MOE_SUB, d), lambda s, *_: (s, 0)),
            scratch_shapes=[pltpu.VMEM((2, MOE_STEP_SUBS, rows, d), BF16), pltpu.SemaphoreType.DMA((2,))]),
        out_shape=jax.ShapeDtypeStruct((n, d), F32),
        compiler_params=_cparams(("arbitrary",)),
    )(dest, nloc, x2, route, ys)


def _moe(x2, h2, route, nch, w_gate, w_up, w_down):
    dest, nloc, block_e, nvalid, tail_start, ntail = _moe_plan(nch, x2.shape[0])
    xs = _dispatch(h2, route, dest, nloc, tail_start, ntail, nvalid)
    ys = _experts(xs, block_e, nvalid, w_gate, w_up, w_down)
    return _combine(x2, ys, route, dest, nloc)


def _tile(n, want):
    t = min(n, want)
    assert n % t == 0
    return t


def kernel(x, mem, attn_norm_w, w_in, na_q_norm_w, na_k_norm_w, na_rpb, rw_mu_prev, rw_mu_next, rw_w0, rw_w2, rw_a0, rw_a2, rw_g2, rw_k_k, rw_k_a, rw_r_k, rw_ln_w, rw_ln_b, mem_norm_w, w_mem_kv, mem_q_norm_w, mem_k_norm_w, w_out, ffn_norm_w, moe_w_group, moe_b_group, moe_w_expert, moe_b_expert, moe_w_gate, moe_w_up, moe_w_down):
    b, s, d = x.shape
    n = b * s
    depth = w_in.shape[0]
    assert s % (NA_BLOCK_ROWS * GRID_W) == 0 and s // GRID_W >= 2 * NA_KH
    tm = _tile(n, 512)
    ts = _tile(s, 512)
    for l in range(depth):
        q, k, v, rw, mq = _proj(
            x.reshape(n, d), _row(attn_norm_w[l]), w_in[l].astype(BF16),
            _row(jnp.tile(na_q_norm_w[l], NA_HEADS)), _row(jnp.tile(na_k_norm_w[l], NA_HEADS)),
            _row(jnp.tile(mem_q_norm_w[l], MEM_HEADS)), tm)
        y_na = _na(q.reshape(b, s, NA_WIDTH), k.reshape(b, s, NA_WIDTH), v.reshape(b, s, NA_WIDTH),
                   _na_bias_table(na_rpb[l]))
        r, vv, kk, kd, bb, lw, g, bonus = _rwprep(
            rw.reshape(b, s, RW_PROJ), rw_mu_prev[l], rw_mu_next[l], rw_w0[l], rw_w2[l], rw_a0[l],
            rw_a2[l], rw_g2[l], rw_k_k[l], rw_k_a[l], rw_r_k[l], ts)
        y_f, y_b = _wkv2(r, vv, kk, kd, bb, lw)
        mk, mv = _memkv(mem, _row(mem_norm_w[l]), w_mem_kv[l].astype(BF16),
                        _row(jnp.tile(mem_k_norm_w[l], MEM_HEADS)))
        x, h2, route, nch = _mixout(
            x, y_na, y_f, y_b, bonus, g, mq.reshape(b, s, MEM_WIDTH), mk, mv,
            _row(rw_ln_w[l]), _row(rw_ln_b[l]), w_out[l].astype(BF16), _row(ffn_norm_w[l]),
            _router_params(moe_w_group[l], moe_b_group[l], moe_w_expert[l], moe_b_expert[l]),
            _tile(s, MIX_PARTS * 512))
        x = _moe(x.reshape(n, d), h2, route, nch, moe_w_gate[l].astype(BF16),
                 moe_w_up[l].astype(BF16), moe_w_down[l].astype(BF16)).reshape(b, s, d)
    return x
```

```python
import functools

import jax
import jax.numpy as jnp
import numpy as np
from jax import lax
from jax.experimental import pallas as pl
from jax.experimental.pallas import tpu as pltpu

F32 = jnp.float32
BF16 = jnp.bfloat16

GRID_W = 64
HEAD_DIM = 64
NA_HEADS = 8
NA_WIDTH = NA_HEADS * HEAD_DIM
NA_KH = 8
NA_KW = 16
RW_HEADS = 4
RW_WIDTH = RW_HEADS * HEAD_DIM
RW_LORA_W = 64
RW_LORA_A = 64
RW_LORA_G = 128
RW_PROJ = 3 * RW_WIDTH + 2 * RW_LORA_W + 2 * RW_LORA_A + RW_LORA_G
MEM_HEADS = 4
MEM_WIDTH = MEM_HEADS * HEAD_DIM
MOE_GROUPS = 4
MOE_EXPERTS_PER_GROUP = 4
MOE_EXPERTS = MOE_GROUPS * MOE_EXPERTS_PER_GROUP
MOE_D_FF = 512
RMS_EPS = 1e-6
RW_GN_EPS = 64e-5

LANES = 128
WKV_CHUNK = 64
NEG_BIG = -1e30
VMEM_LIMIT = 56 * 1024 * 1024


def _cparams(sem):
    return pltpu.CompilerParams(dimension_semantics=sem, vmem_limit_bytes=VMEM_LIMIT)


def _dot(a, b):
    return jnp.dot(a, b, preferred_element_type=F32)


def _dot_nt(a, b):
    return lax.dot_general(a, b, (((1,), (1,)), ((), ())), preferred_element_type=F32)


def _dot_tn(a, b):
    return lax.dot_general(a, b, (((0,), (0,)), ((), ())), preferred_element_type=F32)


def _split_dot(t, m):
    hi = t.astype(BF16)
    lo = (t - hi.astype(F32)).astype(BF16)
    return _dot(hi, m) + _dot(lo, m)


def _head_ones(width):
    i = lax.broadcasted_iota(jnp.int32, (width, width), 0) // HEAD_DIM
    j = lax.broadcasted_iota(jnp.int32, (width, width), 1) // HEAD_DIM
    return (i == j).astype(BF16)


def _row(v):
    return v.reshape(1, -1).astype(F32)


def _proj_kernel(x_ref, nw_ref, w_ref, qw_ref, kw_ref, mw_ref, q_ref, k_ref, v_ref, rw_ref, mq_ref):
    x = x_ref[...]
    ms = jnp.mean(x * x, axis=-1, keepdims=True)
    h = (x * lax.rsqrt(ms + RMS_EPS) * nw_ref[...]).astype(BF16)
    ones_na = _head_ones(NA_WIDTH)

    def head_norm(t, w, ones):
        ms = _dot((t * t).astype(BF16), ones) * (1.0 / HEAD_DIM)
        return t * lax.rsqrt(ms + RMS_EPS) * w

    o = 0
    q = _dot(h, w_ref[:, o:o + NA_WIDTH])
    q_ref[...] = (head_norm(q, qw_ref[...], ones_na) * (HEAD_DIM ** -0.5)).astype(BF16)
    o += NA_WIDTH
    k = _dot(h, w_ref[:, o:o + NA_WIDTH])
    k_ref[...] = head_norm(k, kw_ref[...], ones_na).astype(BF16)
    o += NA_WIDTH
    v_ref[...] = _dot(h, w_ref[:, o:o + NA_WIDTH]).astype(BF16)
    o += NA_WIDTH
    rw_ref[...] = _dot(h, w_ref[:, o:o + RW_PROJ])
    o += RW_PROJ
    mq = _dot(h, w_ref[:, o:o + MEM_WIDTH])
    mq_ref[...] = (head_norm(mq, mw_ref[...], _head_ones(MEM_WIDTH)) * (HEAD_DIM ** -0.5)).astype(BF16)


def _proj(x2, nw, w_in, qw, kw, mw, tm):
    n, d = x2.shape
    p_in = w_in.shape[1]
    tok = lambda w: pl.BlockSpec((tm, w), lambda i: (i, 0))
    full = lambda a: pl.BlockSpec(a.shape, lambda i: (0,) * a.ndim)
    return pl.pallas_call(
        _proj_kernel,
        grid=(n // tm,),
        in_specs=[tok(d), full(nw), full(w_in), full(qw), full(kw), full(mw)],
        out_specs=[tok(NA_WIDTH), tok(NA_WIDTH), tok(NA_WIDTH), tok(RW_PROJ), tok(MEM_WIDTH)],
        out_shape=[jax.ShapeDtypeStruct((n, NA_WIDTH), BF16)] * 3
        + [jax.ShapeDtypeStruct((n, RW_PROJ), F32), jax.ShapeDtypeStruct((n, MEM_WIDTH), BF16)],
        compiler_params=_cparams(("parallel",)),
    )(x2, nw, w_in, qw, kw, mw)


NA_BLOCK_ROWS = 8
NA_HALO_ROWS = NA_KH // 2


def _na_bias_table(rpb):
    col = np.arange(GRID_W)
    cs = np.clip(col - NA_KW // 2, 0, GRID_W - NA_KW)
    dc = col[None, :] - col[:, None]
    inside = (col[None, :] >= cs[:, None]) & (col[None, :] < cs[:, None] + NA_KW)
    dci = np.clip(dc + NA_KW - 1, 0, 2 * NA_KW - 2)
    off = np.arange(NA_KH) - (NA_KH - 1)
    dri = off[:, None] + np.arange(NA_KH)[None, :] + NA_KH - 1
    row_sel = (dri[:, :, None] == np.arange(2 * NA_KH - 1)).astype(np.float32)
    col_sel = (dci[:, :, None] == np.arange(2 * NA_KW - 1)).astype(np.float32)
    t = jnp.einsum("hrq,onr,cdq->ohcnd", rpb.astype(F32), row_sel, col_sel,
                   precision=lax.Precision.HIGHEST)
    t = jnp.where(inside[None, None, :, None, :], t, NEG_BIG)
    return t.reshape(NA_KH, NA_HEADS // 2, 2 * GRID_W, NA_KH * GRID_W).astype(F32)


def _na_kernel(q_ref, kp_ref, kc_ref, kn_ref, vp_ref, vc_ref, vn_ref, bias_ref, o_ref, kwin, vwin, *, rows):
    j = pl.program_id(1)
    blk = NA_BLOCK_ROWS * GRID_W
    halo = NA_HALO_ROWS * GRID_W
    kwin[0:halo] = kp_ref[0]
    kwin[halo:halo + blk] = kc_ref[0]
    kwin[halo + blk:2 * halo + blk] = kn_ref[0]
    vwin[0:halo] = vp_ref[0]
    vwin[halo:halo + blk] = vc_ref[0]
    vwin[halo + blk:2 * halo + blk] = vn_ref[0]
    lane = lax.broadcasted_iota(jnp.int32, (1, LANES), 1)
    first = lane < HEAD_DIM

    def row_body(i, carry):
        r = j * NA_BLOCK_ROWS + i
        rs = jnp.clip(r - NA_KH // 2, 0, rows - NA_KH)
        oi = rs - r + (NA_KH - 1)
        start = pl.multiple_of((rs - (j * NA_BLOCK_ROWS - NA_HALO_ROWS)) * GRID_W, GRID_W)
        qs = pl.multiple_of(i * GRID_W, GRID_W)
        pairs = [slice(p * LANES, (p + 1) * LANES) for p in range(NA_HEADS // 2)]
        scores = []
        for p, ls in enumerate(pairs):
            qp = q_ref[0, pl.ds(qs, GRID_W), ls]
            zero = jnp.zeros_like(qp)
            qst = jnp.concatenate([jnp.where(first, qp, zero), jnp.where(first, zero, qp)], axis=0)
            scores.append(_dot_nt(qst, kwin[pl.ds(start, NA_KH * GRID_W), ls]) + bias_ref[oi, p])
        probs, norms = [], []
        for s in scores:
            e = jnp.exp(s - jnp.max(s, axis=-1, keepdims=True))
            norms.append(jnp.sum(e, axis=-1, keepdims=True))
            probs.append(e.astype(BF16))
        for ls, e, l in zip(pairs, probs, norms):
            o = _dot(e, vwin[pl.ds(start, NA_KH * GRID_W), ls]) / l
            o_ref[0, pl.ds(qs, GRID_W), ls] = jnp.where(first, o[:GRID_W], o[GRID_W:]).astype(o_ref.dtype)
        return carry

    lax.fori_loop(0, NA_BLOCK_ROWS, row_body, 0, unroll=True)


def _na(q, k, v, bias):
    b, s, w = q.shape
    rows = s // GRID_W
    nblk = rows // NA_BLOCK_ROWS
    blk = NA_BLOCK_ROWS * GRID_W
    halo = NA_HALO_ROWS * GRID_W
    per = blk // halo
    cur = pl.BlockSpec((1, blk, w), lambda bi, j: (bi, j, 0))
    prv = pl.BlockSpec((1, halo, w), lambda bi, j: (bi, jnp.maximum(j * per - 1, 0), 0))
    nxt = pl.BlockSpec((1, halo, w), lambda bi, j: (bi, jnp.minimum((j + 1) * per, nblk * per - 1), 0))
    return pl.pallas_call(
        functools.partial(_na_kernel, rows=rows),
        grid=(b, nblk),
        in_specs=[cur, prv, cur, nxt, prv, cur, nxt,
                  pl.BlockSpec(bias.shape, lambda bi, j: (0, 0, 0, 0))],
        out_specs=cur,
        out_shape=jax.ShapeDtypeStruct((b, s, w), BF16),
        scratch_shapes=[pltpu.VMEM((blk + 2 * halo, w), BF16), pltpu.VMEM((blk + 2 * halo, w), BF16)],
        compiler_params=_cparams(("parallel", "parallel")),
    )(q, k, k, k, v, v, v, bias)


def _rwprep_kernel(rw_ref, prev_ref, next_ref, mup_ref, mun_ref, w0_ref, w2_ref, a0_ref, a2_ref, g2_ref,
                   kk_w_ref, ka_ref, rk_ref,
                   r_ref, v_ref, kk_ref, kd_ref, b_ref, lw_ref, g_ref, bonus_ref, *, nblk):
    i = pl.program_id(1)
    s = rw_ref[0]
    ts = s.shape[0]
    rowi = lax.broadcasted_iota(jnp.int32, (ts, 1), 0)
    halo_p = jnp.where(i > 0, prev_ref[0, 7:8, :], 0.0)
    halo_n = jnp.where(i < nblk - 1, next_ref[0, 0:1, :], 0.0)
    prev = jnp.where(rowi == 0, halo_p, pltpu.roll(s, 1, 0))
    nxt = jnp.where(rowi == ts - 1, halo_n, pltpu.roll(s, ts - 1, 0))
    s = s + mup_ref[...] * (prev - s) + mun_ref[...] * (nxt - s)

    c = RW_WIDTH
    r = s[:, 0:c]
    k = s[:, c:2 * c]
    v = s[:, 2 * c:3 * c]
    lw = s[:, 3 * c:3 * c + 2 * RW_LORA_W]
    la = s[:, 3 * c + 2 * RW_LORA_W:3 * c + 2 * RW_LORA_W + 2 * RW_LORA_A]
    lg = s[:, 3 * c + 2 * RW_LORA_W + 2 * RW_LORA_A:]
    ones = _head_ones(c)

    g_ref[0] = _dot(jax.nn.sigmoid(lg).astype(BF16), g2_ref[...])
    kk = k * kk_w_ref[...]
    nrm = jnp.sqrt(_split_dot(kk * kk, ones))
    kk = kk / jnp.maximum(nrm, 1e-12)
    wl_pre = w0_ref[...] + _dot(jnp.tanh(lw).astype(BF16), w2_ref[...])
    a_all = jax.nn.sigmoid(a0_ref[...] + _dot(la.astype(BF16), a2_ref[...]))
    wl = -jax.nn.softplus(-wl_pre) - 0.5
    logdecay = -jnp.exp(wl)
    r_ref[0] = r
    v_ref[0] = v
    kk_ref[0] = kk
    kd_sum = jnp.zeros_like(k)
    for d in range(2):
        a = a_all[:, d * c:(d + 1) * c]
        kd = k * (1.0 + (a - 1.0) * ka_ref[...])
        kd_ref[d, 0] = kd
        b_ref[d, 0] = kk * a
        lw_ref[d, 0] = logdecay[:, d * c:(d + 1) * c]
        kd_sum = kd_sum + kd
    bonus_ref[0] = _split_dot(r * kd_sum * rk_ref[...], ones) * v


def _blockdiag2(m):
    z = jnp.zeros_like(m[0])
    return jnp.concatenate([jnp.concatenate([m[0], z], 1), jnp.concatenate([z, m[1]], 1)], 0)


def _rwprep(rw, mu_prev, mu_next, w0, w2, a0, a2, g2, k_k, k_a, r_k, ts):
    b, s, pw = rw.shape
    nblk = s // ts
    c = RW_WIDTH
    cur = pl.BlockSpec((1, ts, pw), lambda bi, i: (bi, i, 0))
    prv = pl.BlockSpec((1, 8, pw), lambda bi, i: (bi, jnp.maximum(i * (ts // 8) - 1, 0), 0))
    nxt = pl.BlockSpec((1, 8, pw), lambda bi, i: (bi, jnp.minimum((i + 1) * (ts // 8), s // 8 - 1), 0))
    params = [_row(mu_prev), _row(mu_next), _row(w0), _blockdiag2(w2).astype(BF16), _row(a0),
              _blockdiag2(a2).astype(BF16), g2.astype(BF16), _row(k_k), _row(k_a), _row(r_k)]
    full = lambda a: pl.BlockSpec(a.shape, lambda bi, i: (0,) * a.ndim)
    one = pl.BlockSpec((1, ts, c), lambda bi, i: (bi, i, 0))
    two = pl.BlockSpec((2, 1, ts, c), lambda bi, i: (0, bi, i, 0))
    s1 = jax.ShapeDtypeStruct((b, s, c), F32)
    s2 = jax.ShapeDtypeStruct((2, b, s, c), F32)
    return pl.pallas_call(
        functools.partial(_rwprep_kernel, nblk=nblk),
        grid=(b, nblk),
        in_specs=[cur, prv, nxt] + [full(p) for p in params],
        out_specs=[one, one, one, two, two, two, one, one],
        out_shape=[s1, s1, s1, s2, s2, s2, s1, s1],
        compiler_params=_cparams(("parallel", "parallel")),
    )(rw, rw, rw, *params)


WKV_STEP_CHUNKS = 4


WKV2_BATCH = 4


def _wkv2_kernel(rf, vf, kkf, rb, vb, kkb, kdf, bf, lwf, kdb, bb, lwb, yf_ref, yb_ref, s_ref):
    cc = WKV_CHUNK
    nb = rf.shape[0]
    nchunk = rf.shape[1] // cc

    @pl.when(pl.program_id(1) == 0)
    def _():
        s_ref[...] = jnp.zeros_like(s_ref)

    same_head = ((lax.broadcasted_iota(jnp.int32, (LANES, LANES), 0) < HEAD_DIM)
                 == (lax.broadcasted_iota(jnp.int32, (LANES, LANES), 1) < HEAD_DIM))
    rowc = lax.broadcasted_iota(jnp.int32, (cc, LANES), 0)
    colc = lax.broadcasted_iota(jnp.int32, (cc, LANES), 1) % cc
    eye = (colc == rowc).astype(F32)
    strict = (colc < rowc, colc > rowc)
    incl = (colc <= rowc, colc >= rowc)
    first = lax.broadcasted_iota(jnp.int32, (1, LANES), 1) < HEAD_DIM
    second = jnp.logical_not(first)

    def stack_heads(t):
        z = jnp.zeros_like(t)
        return jnp.concatenate([jnp.where(first, t, z), jnp.where(second, t, z)], axis=0)

    def make_chains(c):
        chains = []
        for bi in range(nb):
            for d in range(2):
                j = nchunk - 1 - c if d else c
                sl = slice(j * cc, (j + 1) * cc)
                src = (rb, vb, kkb, kdb, bb, lwb) if d else (rf, vf, kkf, kdf, bf, lwf)
                for p in range(RW_HEADS // 2):
                    ls = slice(p * LANES, (p + 1) * LANES)
                    chains.append(dict(
                        bi=bi, d=d, p=p, sl=sl, ls=ls,
                        r=src[0][bi, sl, ls], v=src[1][bi, sl, ls], kk=src[2][bi, sl, ls],
                        kd=src[3][0, bi, sl, ls], b=src[4][0, bi, sl, ls], lw=src[5][0, bi, sl, ls]))
        return chains

    def solve_chunks(chains):
        for ch in chains:
            cum = ch["lw"]
            sh = 1
            while sh < cc:
                if ch["d"]:
                    cum = cum + jnp.where(rowc < cc - sh, pltpu.roll(cum, cc - sh, 0), 0.0)
                else:
                    cum = cum + jnp.where(rowc >= sh, pltpu.roll(cum, sh, 0), 0.0)
                sh *= 2
            tot = cum[0:1] if ch["d"] else cum[cc - 1:cc]
            at = (-ch["kk"] * jnp.exp(cum - ch["lw"])).astype(BF16)
            einv = jnp.exp(-cum)
            eh = jnp.exp(tot - cum)
            ch["etot"] = jnp.exp(tot)
            ch["rt"] = (ch["r"] * jnp.exp(cum)).astype(BF16)
            ch["vb"] = ch["v"].astype(BF16)
            ch["v_st"] = stack_heads(ch["vb"])
            ch["at_st"] = stack_heads(at)
            ch["lhs"] = jnp.concatenate([at, ch["rt"]], axis=0)
            ch["rhs"] = jnp.concatenate([stack_heads((ch["b"] * einv).astype(BF16)),
                                         stack_heads((ch["kd"] * einv).astype(BF16))], axis=0)
            ch["kb_hat"] = jnp.concatenate([ch["kd"] * eh, ch["b"] * eh], axis=0).astype(BF16)
        for ch in chains:
            m1 = _dot_nt(ch["lhs"], ch["rhs"])
            d = ch["d"]
            ch["n"] = jnp.where(strict[d], m1[:cc, :LANES], 0.0)
            a_ak = jnp.where(strict[d], m1[:cc, LANES:], 0.0)
            ch["a_rb"] = jnp.where(incl[d], m1[cc:, :LANES], 0.0).astype(BF16)
            a_rk = jnp.where(incl[d], m1[cc:, LANES:], 0.0)
            ch["a_k"] = jnp.concatenate([a_ak, a_rk], axis=0).astype(BF16)
        for ch in chains:
            nbf = ch["n"].astype(BF16)
            ch["pw"] = _dot(nbf, stack_heads(nbf))
            ch["t"] = eye + ch["n"]
        for _ in range(cc.bit_length() - 3):
            for ch in chains:
                pwb = ch["pw"].astype(BF16)
                tp = _dot(jnp.concatenate([ch["t"].astype(BF16), pwb], axis=0), stack_heads(pwb))
                ch["t"] = ch["t"] + tp[:cc]
                ch["pw"] = tp[cc:]
        for ch in chains:
            ch["t"] = (ch["t"] + _dot(ch["t"].astype(BF16), stack_heads(ch["pw"].astype(BF16)))).astype(BF16)
        for ch in chains:
            ch["av"] = _dot(ch["a_k"], ch["v_st"])
        for ch in chains:
            akv = stack_heads(ch["av"][:cc].astype(BF16))
            tx = _dot(ch["t"], jnp.concatenate([ch["at_st"], akv], axis=1))
            ch["atp"] = tx[:, :LANES].astype(BF16)
            ch["u0"] = tx[:, LANES:]

    def advance_state(chains):
        for ch in chains:
            ch["st"] = s_ref[ch["bi"], ch["d"], ch["p"]]
            qs = _dot_nt(jnp.concatenate([ch["atp"], ch["rt"]], axis=0), ch["st"].astype(BF16))
            ch["u"] = (qs[:cc] + ch["u0"]).astype(BF16)
            ch["ys"] = qs[cc:] + ch["av"][cc:]
        for ch in chains:
            upd = _dot_tn(jnp.concatenate([ch["vb"], ch["u"]], axis=0), ch["kb_hat"])
            s_ref[ch["bi"], ch["d"], ch["p"]] = ch["st"] * ch["etot"] + jnp.where(same_head, upd, 0.0)
        for ch in chains:
            y_ref = yb_ref if ch["d"] else yf_ref
            y_ref[ch["bi"], ch["sl"], ch["ls"]] = ch["ys"] + _dot(ch["a_rb"], stack_heads(ch["u"]))

    per_chunk = [make_chains(c) for c in range(nchunk)]
    solve_chunks([ch for chains in per_chunk for ch in chains])
    for chains in per_chunk:
        advance_state(chains)


def _wkv2(r, v, kk, kd, bb, lw):
    b, s, c = r.shape
    tt = WKV_CHUNK * WKV_STEP_CHUNKS
    nblk = s // tt
    nb = WKV2_BATCH if b % WKV2_BATCH == 0 else 1
    fwd = pl.BlockSpec((nb, tt, c), lambda bi, i: (bi, i, 0))
    bwd = pl.BlockSpec((nb, tt, c), lambda bi, i: (bi, nblk - 1 - i, 0))
    fwd2 = pl.BlockSpec((1, nb, tt, c), lambda bi, i: (0, bi, i, 0))
    bwd2 = pl.BlockSpec((1, nb, tt, c), lambda bi, i: (1, bi, nblk - 1 - i, 0))
    return pl.pallas_call(
        _wkv2_kernel,
        grid=(b // nb, nblk),
        in_specs=[fwd, fwd, fwd, bwd, bwd, bwd, fwd2, fwd2, fwd2, bwd2, bwd2, bwd2],
        out_specs=[fwd, bwd],
        out_shape=[jax.ShapeDtypeStruct((b, s, c), F32)] * 2,
        scratch_shapes=[pltpu.VMEM((nb, 2, RW_HEADS // 2, LANES, LANES), F32)],
        compiler_params=_cparams(("parallel", "arbitrary")),
    )(r, v, kk, r, v, kk, kd, bb, lw, kd, bb, lw)


def _memkv_kernel(mem_ref, nw_ref, w_ref, kw_ref, k_ref, v_ref):
    x = mem_ref[0]
    ms = jnp.mean(x * x, axis=-1, keepdims=True)
    h = (x * lax.rsqrt(ms + RMS_EPS) * nw_ref[...]).astype(BF16)
    kv = _dot(h, w_ref[...])
    k = kv[:, :MEM_WIDTH]
    ms = _split_dot(k * k, _head_ones(MEM_WIDTH)) * (1.0 / HEAD_DIM)
    k_ref[0] = (k * lax.rsqrt(ms + RMS_EPS) * kw_ref[...]).astype(BF16)
    v_ref[0] = kv[:, MEM_WIDTH:].astype(BF16)


def _memkv(mem, nw, w_kv, kw):
    b, m, d = mem.shape
    full = lambda a: pl.BlockSpec(a.shape, lambda bi: (0,) * a.ndim)
    out = pl.BlockSpec((1, m, MEM_WIDTH), lambda bi: (bi, 0, 0))
    return pl.pallas_call(
        _memkv_kernel,
        grid=(b,),
        in_specs=[pl.BlockSpec((1, m, d), lambda bi: (bi, 0, 0)), full(nw), full(w_kv), full(kw)],
        out_specs=[out, out],
        out_shape=[jax.ShapeDtypeStruct((b, m, MEM_WIDTH), BF16)] * 2,
        compiler_params=_cparams(("parallel",)),
    )(mem, nw, w_kv, kw)


MIX_PARTS = 2


def _mixout_kernel(x_ref, na_ref, yf_ref, yb_ref, bonus_ref, g_ref, mq_ref, mk_ref, mv_ref,
                   lnw_ref, lnb_ref, wo_ref, fnw_ref, whi_ref, wlo_ref, rb_ref,
                   o_ref, h_ref, route_ref, nch_ref):
    part = x_ref.shape[1] // MIX_PARTS
    parts = [slice(j * part, (j + 1) * part) for j in range(MIX_PARTS)]
    ones = _head_ones(RW_WIDTH)
    y_rws = []
    for rows in parts:
        y = yf_ref[0, rows, :] + yb_ref[0, rows, :]
        mu = _split_dot(y, ones) * (1.0 / HEAD_DIM)
        yc = y - mu
        var = _split_dot(yc * yc, ones) * (1.0 / HEAD_DIM)
        yn = yc * lax.rsqrt(var + RW_GN_EPS) * lnw_ref[...] + lnb_ref[...]
        y_rws.append(((yn + bonus_ref[0, rows, :]) * g_ref[0, rows, :]).astype(BF16))

    first = lax.broadcasted_iota(jnp.int32, (1, LANES), 1) < HEAD_DIM
    mems = [[] for _ in parts]
    for p in range(MEM_HEADS // 2):
        ls = slice(p * LANES, (p + 1) * LANES)
        kp = mk_ref[0, :, ls]
        vp = mv_ref[0, :, ls]
        for j, rows in enumerate(parts):
            qp = mq_ref[0, rows, ls]
            outs = []
            for hh in range(2):
                sel = first if hh == 0 else jnp.logical_not(first)
                s = _dot_nt(jnp.where(sel, qp, jnp.zeros_like(qp)), kp)
                m = jnp.max(s, axis=-1, keepdims=True)
                e = jnp.exp(s - m)
                l = jnp.sum(e, axis=-1, keepdims=True)
                outs.append(_dot(e.astype(BF16), vp) / l)
            mems[j].append(jnp.where(first, outs[0], outs[1]).astype(BF16))

    x_mids = []
    for j, rows in enumerate(parts):
        acc = _dot(na_ref[0, rows, :], wo_ref[0:NA_WIDTH, :])
        acc = acc + _dot(y_rws[j], wo_ref[NA_WIDTH:NA_WIDTH + RW_WIDTH, :])
        acc = acc + _dot(jnp.concatenate(mems[j], axis=1), wo_ref[NA_WIDTH + RW_WIDTH:, :])
        x_mids.append(x_ref[0, rows, :] + acc)
    for j, rows in enumerate(parts):
        o_ref[0, rows, :] = x_mids[j]
    for j, rows in enumerate(parts):
        _route_tile(x_mids[j], rows, j * (part // MOE_SUB), fnw_ref, whi_ref, wlo_ref, rb_ref,
                    h_ref, route_ref, nch_ref)


def _mixout(x, y_na, y_f, y_b, bonus, g, mq, mk, mv, ln_w, ln_b, w_out, ffn_nw, router, tm):
    b, s, d = x.shape
    m = mk.shape[1]
    sub = tm // MOE_SUB
    tok = lambda w: pl.BlockSpec((1, tm, w), lambda bi, i: (bi, i, 0))
    full = lambda a: pl.BlockSpec(a.shape, lambda bi, i: (0,) * a.ndim)
    memb = pl.BlockSpec((1, m, MEM_WIDTH), lambda bi, i: (bi, 0, 0))
    x_mid, h2, route, nch = pl.pallas_call(
        _mixout_kernel,
        grid=(b, s // tm),
        in_specs=[tok(d), tok(NA_WIDTH), tok(RW_WIDTH), tok(RW_WIDTH), tok(RW_WIDTH), tok(RW_WIDTH),
                  tok(MEM_WIDTH), memb, memb, full(ln_w), full(ln_b), full(w_out), full(ffn_nw)]
        + [full(p) for p in router],
        out_specs=[tok(d), tok(d),
                   pl.BlockSpec((1, sub, ROUTE_ROWS, MOE_SUB), lambda bi, i: (bi, i, 0, 0)),
                   pl.BlockSpec((1, sub, 1, ROUTER_LANES), lambda bi, i: (bi, i, 0, 0))],
        out_shape=[jax.ShapeDtypeStruct((b, s, d), F32), jax.ShapeDtypeStruct((b, s, d), BF16),
                   jax.ShapeDtypeStruct((b, s // MOE_SUB, ROUTE_ROWS, MOE_SUB), F32),
                   jax.ShapeDtypeStruct((b, s // MOE_SUB, 1, ROUTER_LANES), F32)],
        compiler_params=_cparams(("parallel", "parallel")),
    )(x, y_na, y_f, y_b, bonus, g, mq, mk, mv, ln_w, ln_b, w_out, ffn_nw, *router)
    ns = b * s // MOE_SUB
    return (x_mid, h2.reshape(b * s, d), route.reshape(ns, ROUTE_ROWS, MOE_SUB),
            nch.reshape(ns, 1, ROUTER_LANES))


ROUTER_LANES = 128


MOE_SUB = 256
MOE_CHUNK = 16
MOE_LOCAL_CHUNKS = 48
MOE_BLOCK_CHUNKS = 32
MOE_STEP_BLOCKS = 2
MOE_STEP_SUBS = 4
MOE_TOP_K = 2
ROUTE_ROWS = 8
assert MOE_LOCAL_CHUNKS >= MOE_TOP_K * MOE_SUB // MOE_CHUNK + MOE_EXPERTS - 1


def _route_subtile(lg):
    t = lg.shape[0]
    lane = lax.broadcasted_iota(jnp.int32, (1, ROUTER_LANES), 1)
    lanef = lane.astype(F32)
    big = float(ROUTER_LANES)

    def argmax_lane(v):
        m = jnp.max(v, axis=-1, keepdims=True)
        return m, jnp.min(jnp.where(v == m, lanef, big), axis=-1, keepdims=True)

    is_group = lane < MOE_GROUPS
    gmax, gidx = argmax_lane(jnp.where(is_group, lg, NEG_BIG))
    gsum = jnp.sum(jnp.where(is_group, jnp.exp(lg - gmax), 0.0), axis=-1, keepdims=True)
    lo = MOE_GROUPS + MOE_EXPERTS_PER_GROUP * gidx
    el = jnp.where((lanef >= lo) & (lanef < lo + MOE_EXPERTS_PER_GROUP), lg, NEG_BIG)
    m1, i1 = argmax_lane(el)
    m2, i2 = argmax_lane(jnp.where(lanef == i1, NEG_BIG, el))
    ratio = jnp.exp(m2 - m1)
    w0 = 1.0 / (gsum * (1.0 + ratio))
    w1 = w0 * ratio

    oh0 = (lanef == i1 - MOE_GROUPS).astype(F32)
    oh1 = (lanef == i2 - MOE_GROUPS).astype(F32)
    row = lax.broadcasted_iota(jnp.int32, (t, t), 0)
    col = lax.broadcasted_iota(jnp.int32, (t, t), 1)
    before = (col < row).astype(BF16)
    cnt0 = jnp.sum(oh0, axis=0, keepdims=True)
    cnt1 = jnp.sum(oh1, axis=0, keepdims=True)
    rank0 = _dot(before, oh0.astype(BF16))
    rank1 = _dot(before, oh1.astype(BF16)) + cnt0
    nch = jnp.floor((cnt0 + cnt1 + (MOE_CHUNK - 1.0)) * (1.0 / MOE_CHUNK))
    li = lax.broadcasted_iota(jnp.int32, (ROUTER_LANES, ROUTER_LANES), 0)
    lj = lax.broadcasted_iota(jnp.int32, (ROUTER_LANES, ROUTER_LANES), 1)
    start = _dot(jnp.broadcast_to(nch, (8, ROUTER_LANES)).astype(BF16), (li < lj).astype(BF16))[0:1]
    pos0 = jnp.sum((start * MOE_CHUNK + rank0) * oh0, axis=-1, keepdims=True)
    pos1 = jnp.sum((start * MOE_CHUNK + rank1) * oh1, axis=-1, keepdims=True)
    rec = (jnp.where(lane == 0, pos0, 0.0) + jnp.where(lane == 1, pos1, 0.0)
           + jnp.where(lane == 2, w0, 0.0) + jnp.where(lane == 3, w1, 0.0))
    return jnp.transpose(rec)[0:ROUTE_ROWS], nch


def _route_tile(x, rows, sub0, nw_ref, whi_ref, wlo_ref, b_ref, h_ref, route_ref, nch_ref):
    ms = jnp.mean(x * x, axis=-1, keepdims=True)
    h = x * lax.rsqrt(ms + RMS_EPS) * nw_ref[...]
    hi = h.astype(BF16)
    lo = (h - hi.astype(F32)).astype(BF16)
    h_ref[0, rows, :] = hi
    lg = _dot(hi, whi_ref[...]) + _dot(hi, wlo_ref[...]) + _dot(lo, whi_ref[...]) + b_ref[...]
    for j in range(x.shape[0] // MOE_SUB):
        route_ref[0, sub0 + j], nch_ref[0, sub0 + j] = _route_subtile(lg[j * MOE_SUB:(j + 1) * MOE_SUB])


def _router_params(w_group, b_group, w_expert, b_expert):
    d = w_group.shape[0]
    pad = ROUTER_LANES - MOE_GROUPS - MOE_EXPERTS
    w = jnp.concatenate([w_group, w_expert, jnp.zeros((d, pad), F32)], axis=1)
    whi = w.astype(BF16)
    wlo = (w - whi.astype(F32)).astype(BF16)
    bias = jnp.concatenate([b_group, b_expert, jnp.zeros((pad,), F32)]).reshape(1, -1)
    return whi, wlo, bias


def _moe_plan(nch, n):
    nch = nch[:, 0, :MOE_EXPERTS].astype(jnp.int32)
    lc_end = jnp.cumsum(nch, axis=1)
    lc_start = lc_end - nch
    nloc = lc_end[:, -1]
    nblk = (jnp.sum(nch, axis=0) + MOE_BLOCK_CHUNKS - 1) // MOE_BLOCK_CHUNKS
    bend = jnp.cumsum(nblk)
    gch = ((bend - nblk) * MOE_BLOCK_CHUNKS)[None, :] + jnp.cumsum(nch, axis=0) - nch
    c = jnp.arange(MOE_LOCAL_CHUNKS, dtype=jnp.int32)
    in_run = (c[None, :, None] >= lc_start[:, None, :]) & (c[None, :, None] < lc_end[:, None, :])
    dest = jnp.sum(jnp.where(in_run, (gch - lc_start)[:, None, :] + c[None, :, None], 0), axis=-1)

    nblocks = _moe_blocks(n)
    b = jnp.arange(nblocks, dtype=jnp.int32)
    block_e = jnp.minimum(jnp.sum(b[:, None] >= bend[None, :], axis=-1), MOE_EXPERTS - 1)
    tot = jnp.sum(nch, axis=0)
    tail_start = (bend - nblk) * MOE_BLOCK_CHUNKS + tot
    ntail = nblk * MOE_BLOCK_CHUNKS - tot
    i32 = lambda a: a.astype(jnp.int32)
    return i32(dest), i32(nloc), i32(block_e), i32(bend[-1:]), i32(tail_start), i32(ntail)


def _moe_blocks(n):
    ns = n // MOE_SUB
    chunks = ns * (MOE_TOP_K * MOE_SUB // MOE_CHUNK + MOE_EXPERTS - 1) + MOE_EXPERTS * (MOE_BLOCK_CHUNKS - 1)
    per_step = MOE_BLOCK_CHUNKS * MOE_STEP_BLOCKS
    return -(-chunks // per_step) * MOE_STEP_BLOCKS


def _local_onehot(pos_rows, shape, row_axis):
    idx = lax.broadcasted_iota(jnp.int32, shape, row_axis)
    return (idx == pos_rows[0]) | (idx == pos_rows[1])


def _dispatch_kernel(dest_ref, nloc_ref, tail_ref, ntail_ref, nv_ref, h_ref, pos_ref, xs_ref,
                     buf, zbuf, sem, zsem):
    s = pl.program_id(0)
    ns = pl.num_programs(0)
    slot = s % 2
    bm = MOE_BLOCK_CHUNKS * MOE_CHUNK
    nblocks = xs_ref.shape[0] // bm

    def zero_fill(wait):
        def tail_copy(e, i):
            dst = xs_ref.at[pl.ds(pl.multiple_of((tail_ref[e] + i) * MOE_CHUNK, MOE_CHUNK), MOE_CHUNK)]
            return pltpu.make_async_copy(zbuf.at[pl.ds(0, MOE_CHUNK)], dst, zsem.at[0])

        def block_copy(b):
            return pltpu.make_async_copy(zbuf, xs_ref.at[pl.ds(pl.multiple_of(b * bm, bm), bm)], zsem.at[1])

        def run(cp):
            return cp.wait() if wait else cp.start()

        for e in range(MOE_EXPERTS):
            def tail_body(i, carry, e=e):
                run(tail_copy(e, i))
                return carry
            lax.fori_loop(0, ntail_ref[e], tail_body, 0)

        def block_body(b, carry):
            run(block_copy(b))
            return carry
        lax.fori_loop(nv_ref[0], nblocks, block_body, 0)

    @pl.when(s == 0)
    def _():
        zbuf[...] = jnp.zeros_like(zbuf)
        zero_fill(wait=False)

    def chunk_copy(sl, k, c, sub):
        src = buf.at[sl, k, pl.ds(pl.multiple_of(c * MOE_CHUNK, MOE_CHUNK), MOE_CHUNK)]
        dst = xs_ref.at[pl.ds(pl.multiple_of(dest_ref[sub, c] * MOE_CHUNK, MOE_CHUNK), MOE_CHUNK)]
        return pltpu.make_async_copy(src, dst, sem.at[sl])

    def for_chunks(step, sl, wait):
        for k in range(MOE_STEP_SUBS):
            sub = step * MOE_STEP_SUBS + k

            def body(c, carry, k=k, sub=sub):
                cp = chunk_copy(sl, k, c, sub)
                cp.wait() if wait else cp.start()
                return carry
            lax.fori_loop(0, nloc_ref[sub], body, 0)

    def wait_step(step, sl):
        for_chunks(step, sl, wait=True)

    @pl.when(s >= 2)
    def _():
        wait_step(s - 2, slot)

    onehots = []
    for k in range(MOE_STEP_SUBS):
        pos = pos_ref[k, 0:MOE_TOP_K].astype(jnp.int32)
        onehot = _local_onehot((pos[0:1], pos[1:2]), (MOE_LOCAL_CHUNKS * MOE_CHUNK, MOE_SUB), 0)
        onehots.append(jnp.where(onehot, 1.0, 0.0).astype(BF16))
    for k in range(MOE_STEP_SUBS):
        buf[slot, k] = _dot(onehots[k], h_ref[k * MOE_SUB:(k + 1) * MOE_SUB, :]).astype(BF16)
    for_chunks(s, slot, wait=False)

    @pl.when(s == ns - 1)
    def _():
        @pl.when(s >= 1)
        def _():
            wait_step(s - 1, 1 - slot)
        wait_step(s, slot)
        zero_fill(wait=True)


def _dispatch(h2, pos, dest, nloc, tail_start, ntail, nvalid):
    n, d = h2.shape
    ns = n // MOE_SUB
    bm = MOE_BLOCK_CHUNKS * MOE_CHUNK
    rows = MOE_LOCAL_CHUNKS * MOE_CHUNK
    return pl.pallas_call(
        _dispatch_kernel,
        grid_spec=pltpu.PrefetchScalarGridSpec(
            num_scalar_prefetch=5, grid=(ns // MOE_STEP_SUBS,),
            in_specs=[pl.BlockSpec((MOE_STEP_SUBS * MOE_SUB, d), lambda s, *_: (s, 0)),
                      pl.BlockSpec((MOE_STEP_SUBS, ROUTE_ROWS, MOE_SUB), lambda s, *_: (s, 0, 0))],
            out_specs=pl.BlockSpec(memory_space=pl.ANY),
            scratch_shapes=[pltpu.VMEM((2, MOE_STEP_SUBS, rows, d), BF16), pltpu.VMEM((bm, d), BF16),
                            pltpu.SemaphoreType.DMA((2,)), pltpu.SemaphoreType.DMA((2,))]),
        out_shape=jax.ShapeDtypeStruct((_moe_blocks(n) * bm, d), BF16),
        compiler_params=_cparams(("arbitrary",)),
    )(dest, nloc, tail_start, ntail, nvalid, h2, pos)


def _experts_kernel(be_ref, nv_ref, x_ref, *refs):
    del be_ref
    w_refs, o_ref = refs[:-1], refs[-1]
    bm = MOE_BLOCK_CHUNKS * MOE_CHUNK
    valid = pl.program_id(0) * MOE_STEP_BLOCKS < nv_ref[0]

    @pl.when(valid)
    def _():
        rows = [slice(j * bm, (j + 1) * bm) for j in range(MOE_STEP_BLOCKS)]
        gated = []
        for j, r in enumerate(rows):
            x = x_ref[r, :]
            gated.append((_dot(x, w_refs[3 * j][0]), _dot(x, w_refs[3 * j + 1][0])))
        hids = [(jax.nn.silu(gate) * up).astype(BF16) for gate, up in gated]
        for j, r in enumerate(rows):
            o_ref[r, :] = _dot(hids[j], w_refs[3 * j + 2][0]).astype(o_ref.dtype)

    @pl.when(jnp.logical_not(valid))
    def _():
        o_ref[...] = jnp.zeros_like(o_ref)


def _experts(xs, block_e, nvalid, w_gate, w_up, w_down):
    cap, d = xs.shape
    rows = MOE_STEP_BLOCKS * MOE_BLOCK_CHUNKS * MOE_CHUNK
    step = lambda p, be, nv: (jnp.minimum(p, (nv[0] - 1) // MOE_STEP_BLOCKS), 0)
    w_specs, w_args = [], []
    for j in range(MOE_STEP_BLOCKS):
        wsel = lambda p, be, nv, j=j: (be[jnp.minimum(p * MOE_STEP_BLOCKS + j, nv[0] - 1)], 0, 0)
        w_specs += [pl.BlockSpec((1, d, MOE_D_FF), wsel), pl.BlockSpec((1, d, MOE_D_FF), wsel),
                    pl.BlockSpec((1, MOE_D_FF, d), wsel)]
        w_args += [w_gate, w_up, w_down]
    return pl.pallas_call(
        _experts_kernel,
        grid_spec=pltpu.PrefetchScalarGridSpec(
            num_scalar_prefetch=2, grid=(cap // rows,),
            in_specs=[pl.BlockSpec((rows, d), step)] + w_specs,
            out_specs=pl.BlockSpec((rows, d), lambda p, be, nv: (p, 0))),
        out_shape=jax.ShapeDtypeStruct((cap, d), BF16),
        compiler_params=_cparams(("arbitrary",)),
    )(block_e, nvalid, xs, *w_args)


def _combine_kernel(dest_ref, nloc_ref, x_ref, route_ref, ys_ref, o_ref, buf, sem):
    s = pl.program_id(0)
    ns = pl.num_programs(0)
    slot = s % 2

    def chunk_copy(sl, k, c, sub):
        src = ys_ref.at[pl.ds(pl.multiple_of(dest_ref[sub, c] * MOE_CHUNK, MOE_CHUNK), MOE_CHUNK)]
        dst = buf.at[sl, k, pl.ds(pl.multiple_of(c * MOE_CHUNK, MOE_CHUNK), MOE_CHUNK)]
        return pltpu.make_async_copy(src, dst, sem.at[sl])

    def for_chunks(step, sl, wait):
        for k in range(MOE_STEP_SUBS):
            sub = step * MOE_STEP_SUBS + k

            def body(c, carry, k=k, sub=sub):
                cp = chunk_copy(sl, k, c, sub)
                cp.wait() if wait else cp.start()
                return carry
            lax.fori_loop(0, nloc_ref[sub], body, 0)

    @pl.when(s == 0)
    def _():
        buf[...] = jnp.zeros_like(buf)
        for_chunks(0, 0, wait=False)

    @pl.when(s + 1 < ns)
    def _():
        for_chunks(s + 1, 1 - slot, wait=False)

    for_chunks(s, slot, wait=True)

    idx = lax.broadcasted_iota(jnp.int32, (MOE_SUB, MOE_LOCAL_CHUNKS * MOE_CHUNK), 1)
    splits = []
    for k in range(MOE_STEP_SUBS):
        rec = jnp.transpose(route_ref[k])
        pos = rec[:, 0:MOE_TOP_K].astype(jnp.int32)
        pw = (jnp.where(idx == pos[:, 0:1], rec[:, 2:3], 0.0)
              + jnp.where(idx == pos[:, 1:2], rec[:, 3:4], 0.0))
        p_hi = pw.astype(BF16)
        splits.append((p_hi, (pw - p_hi.astype(F32)).astype(BF16)))
    for k, (p_hi, p_lo) in enumerate(splits):
        rows = slice(k * MOE_SUB, (k + 1) * MOE_SUB)
        o_ref[rows, :] = x_ref[rows, :] + _dot(p_hi, buf[slot, k]) + _dot(p_lo, buf[slot, k])


def _combine(x2, ys, route, dest, nloc):
    n, d = x2.shape
    ns = n // MOE_SUB
    rows = MOE_LOCAL_CHUNKS * MOE_CHUNK
    return pl.pallas_call(
        _combine_kernel,
        grid_spec=pltpu.PrefetchScalarGridSpec(
            num_scalar_prefetch=2, grid=(ns // MOE_STEP_SUBS,),
            in_specs=[pl.BlockSpec((MOE_STEP_SUBS * MOE_SUB, d), lambda s, *_: (s, 0)),
                      pl.BlockSpec((MOE_STEP_SUBS, ROUTE_ROWS, MOE_SUB), lambda s, *_: (s, 0, 0)),
                      pl.BlockSpec(memory_space=pl.ANY)],
            out_specs=pl.BlockSpec((MOE_STEP_SUBS * MOE_SUB, d), lambda s, *_: (s, 0)),
            scratch_shapes=[pltpu.VMEM((2, MOE_STEP_SUBS, rows, d), BF16), pltpu.SemaphoreType.DMA((2,))]),
        out_shape=jax.ShapeDtypeStruct((n, d), F32),
        compiler_params=_cparams(("arbitrary",)),
    )(dest, nloc, x2, route, ys)


def _moe(x2, h2, route, nch, w_gate, w_up, w_down):
    dest, nloc, block_e, nvalid, tail_start, ntail = _moe_plan(nch, x2.shape[0])
    xs = _dispatch(h2, route, dest, nloc, tail_start, ntail, nvalid)
    ys = _experts(xs, block_e, nvalid, w_gate, w_up, w_down)
    return _combine(x2, ys, route, dest, nloc)


def _tile(n, want):
    t = min(n, want)
    assert n % t == 0
    return t


def kernel(x, mem, attn_norm_w, w_in, na_q_norm_w, na_k_norm_w, na_rpb, rw_mu_prev, rw_mu_next, rw_w0, rw_w2, rw_a0, rw_a2, rw_g2, rw_k_k, rw_k_a, rw_r_k, rw_ln_w, rw_ln_b, mem_norm_w, w_mem_kv, mem_q_norm_w, mem_k_norm_w, w_out, ffn_norm_w, moe_w_group, moe_b_group, moe_w_expert, moe_b_expert, moe_w_gate, moe_w_up, moe_w_down):
    b, s, d = x.shape
    n = b * s
    depth = w_in.shape[0]
    assert s % (NA_BLOCK_ROWS * GRID_W) == 0 and s // GRID_W >= 2 * NA_KH
    tm = _tile(n, 512)
    ts = _tile(s, 512)
    for l in range(depth):
        q, k, v, rw, mq = _proj(
            x.reshape(n, d), _row(attn_norm_w[l]), w_in[l].astype(BF16),
            _row(jnp.tile(na_q_norm_w[l], NA_HEADS)), _row(jnp.tile(na_k_norm_w[l], NA_HEADS)),
            _row(jnp.tile(mem_q_norm_w[l], MEM_HEADS)), tm)
        y_na = _na(q.reshape(b, s, NA_WIDTH), k.reshape(b, s, NA_WIDTH), v.reshape(b, s, NA_WIDTH),
                   _na_bias_table(na_rpb[l]))
        r, vv, kk, kd, bb, lw, g, bonus = _rwprep(
            rw.reshape(b, s, RW_PROJ), rw_mu_prev[l], rw_mu_next[l], rw_w0[l], rw_w2[l], rw_a0[l],
            rw_a2[l], rw_g2[l], rw_k_k[l], rw_k_a[l], rw_r_k[l], ts)
        y_f, y_b = _wkv2(r, vv, kk, kd, bb, lw)
        mk, mv = _memkv(mem, _row(mem_norm_w[l]), w_mem_kv[l].astype(BF16),
                        _row(jnp.tile(mem_k_norm_w[l], MEM_HEADS)))
        x, h2, route, nch = _mixout(
            x, y_na, y_f, y_b, bonus, g, mq.reshape(b, s, MEM_WIDTH), mk, mv,
            _row(rw_ln_w[l]), _row(rw_ln_b[l]), w_out[l].astype(BF16), _row(ffn_norm_w[l]),
            _router_params(moe_w_group[l], moe_b_group[l], moe_w_expert[l], moe_b_expert[l]),
            _tile(s, MIX_PARTS * 512))
        x = _moe(x.reshape(n, d), h2, route, nch, moe_w_gate[l].astype(BF16),
                 moe_w_up[l].astype(BF16), moe_w_down[l].astype(BF16)).reshape(b, s, d)
    return x
```

```python
import functools

import jax
import jax.numpy as jnp
import numpy as np
from jax import lax
from jax.experimental import pallas as pl
from jax.experimental.pallas import tpu as pltpu

F32 = jnp.float32
BF16 = jnp.bfloat16

GRID_W = 64
HEAD_DIM = 64
NA_HEADS = 8
NA_WIDTH = NA_HEADS * HEAD_DIM
NA_KH = 8
NA_KW = 16
RW_HEADS = 4
RW_WIDTH = RW_HEADS * HEAD_DIM
RW_LORA_W = 64
RW_LORA_A = 64
RW_LORA_G = 128
RW_PROJ = 3 * RW_WIDTH + 2 * RW_LORA_W + 2 * RW_LORA_A + RW_LORA_G
MEM_HEADS = 4
MEM_WIDTH = MEM_HEADS * HEAD_DIM
MOE_GROUPS = 4
MOE_EXPERTS_PER_GROUP = 4
MOE_EXPERTS = MOE_GROUPS * MOE_EXPERTS_PER_GROUP
MOE_D_FF = 512
RMS_EPS = 1e-6
RW_GN_EPS = 64e-5

LANES = 128
WKV_CHUNK = 64
NEG_BIG = -1e30
VMEM_LIMIT = 56 * 1024 * 1024


def _cparams(sem):
    return pltpu.CompilerParams(dimension_semantics=sem, vmem_limit_bytes=VMEM_LIMIT)


def _dot(a, b):
    return jnp.dot(a, b, preferred_element_type=F32)


def _dot_nt(a, b):
    return lax.dot_general(a, b, (((1,), (1,)), ((), ())), preferred_element_type=F32)


def _dot_tn(a, b):
    return lax.dot_general(a, b, (((0,), (0,)), ((), ())), preferred_element_type=F32)


def _split_dot(t, m):
    hi = t.astype(BF16)
    lo = (t - hi.astype(F32)).astype(BF16)
    return _dot(hi, m) + _dot(lo, m)


def _head_ones(width):
    i = lax.broadcasted_iota(jnp.int32, (width, width), 0) // HEAD_DIM
    j = lax.broadcasted_iota(jnp.int32, (width, width), 1) // HEAD_DIM
    return (i == j).astype(BF16)


def _row(v):
    return v.reshape(1, -1).astype(F32)


def _proj_kernel(x_ref, nw_ref, w_ref, qw_ref, kw_ref, mw_ref, q_ref, k_ref, v_ref, rw_ref, mq_ref):
    x = x_ref[...]
    ms = jnp.mean(x * x, axis=-1, keepdims=True)
    h = (x * lax.rsqrt(ms + RMS_EPS) * nw_ref[...]).astype(BF16)
    ones_na = _head_ones(NA_WIDTH)

    def head_norm(t, w, ones):
        ms = _dot((t * t).astype(BF16), ones) * (1.0 / HEAD_DIM)
        return t * lax.rsqrt(ms + RMS_EPS) * w

    o = 0
    q = _dot(h, w_ref[:, o:o + NA_WIDTH])
    q_ref[...] = (head_norm(q, qw_ref[...], ones_na) * (HEAD_DIM ** -0.5)).astype(BF16)
    o += NA_WIDTH
    k = _dot(h, w_ref[:, o:o + NA_WIDTH])
    k_ref[...] = head_norm(k, kw_ref[...], ones_na).astype(BF16)
    o += NA_WIDTH
    v_ref[...] = _dot(h, w_ref[:, o:o + NA_WIDTH]).astype(BF16)
    o += NA_WIDTH
    rw_ref[...] = _dot(h, w_ref[:, o:o + RW_PROJ])
    o += RW_PROJ
    mq = _dot(h, w_ref[:, o:o + MEM_WIDTH])
    mq_ref[...] = (head_norm(mq, mw_ref[...], _head_ones(MEM_WIDTH)) * (HEAD_DIM ** -0.5)).astype(BF16)


def _proj(x2, nw, w_in, qw, kw, mw, tm):
    n, d = x2.shape
    p_in = w_in.shape[1]
    tok = lambda w: pl.BlockSpec((tm, w), lambda i: (i, 0))
    full = lambda a: pl.BlockSpec(a.shape, lambda i: (0,) * a.ndim)
    return pl.pallas_call(
        _proj_kernel,
        grid=(n // tm,),
        in_specs=[tok(d), full(nw), full(w_in), full(qw), full(kw), full(mw)],
        out_specs=[tok(NA_WIDTH), tok(NA_WIDTH), tok(NA_WIDTH), tok(RW_PROJ), tok(MEM_WIDTH)],
        out_shape=[jax.ShapeDtypeStruct((n, NA_WIDTH), BF16)] * 3
        + [jax.ShapeDtypeStruct((n, RW_PROJ), F32), jax.ShapeDtypeStruct((n, MEM_WIDTH), BF16)],
        compiler_params=_cparams(("parallel",)),
    )(x2, nw, w_in, qw, kw, mw)


NA_BLOCK_ROWS = 8


def _na_bias_table(rpb):
    col = np.arange(GRID_W)
    cs = np.clip(col - NA_KW // 2, 0, GRID_W - NA_KW)
    dc = col[None, :] - col[:, None]
    inside = (col[None, :] >= cs[:, None]) & (col[None, :] < cs[:, None] + NA_KW)
    dci = np.clip(dc + NA_KW - 1, 0, 2 * NA_KW - 2)
    off = np.arange(NA_KH) - (NA_KH - 1)
    dri = off[:, None] + np.arange(NA_KH)[None, :] + NA_KH - 1
    row_sel = (dri[:, :, None] == np.arange(2 * NA_KH - 1)).astype(np.float32)
    col_sel = (dci[:, :, None] == np.arange(2 * NA_KW - 1)).astype(np.float32)
    t = jnp.einsum("hrq,onr,cdq->ohcnd", rpb.astype(F32), row_sel, col_sel,
                   precision=lax.Precision.HIGHEST)
    t = jnp.where(inside[None, None, :, None, :], t, NEG_BIG)
    return t.reshape(NA_KH, NA_HEADS // 2, 2 * GRID_W, NA_KH * GRID_W).astype(F32)


def _na_kernel(q_ref, kp_ref, kc_ref, kn_ref, vp_ref, vc_ref, vn_ref, bias_ref, o_ref, kwin, vwin, *, rows):
    j = pl.program_id(1)
    blk = NA_BLOCK_ROWS * GRID_W
    kwin[0:blk] = kp_ref[0]
    kwin[blk:2 * blk] = kc_ref[0]
    kwin[2 * blk:3 * blk] = kn_ref[0]
    vwin[0:blk] = vp_ref[0]
    vwin[blk:2 * blk] = vc_ref[0]
    vwin[2 * blk:3 * blk] = vn_ref[0]
    lane = lax.broadcasted_iota(jnp.int32, (1, LANES), 1)
    first = lane < HEAD_DIM

    def row_body(i, carry):
        r = j * NA_BLOCK_ROWS + i
        rs = jnp.clip(r - NA_KH // 2, 0, rows - NA_KH)
        oi = rs - r + (NA_KH - 1)
        start = pl.multiple_of((rs - (j - 1) * NA_BLOCK_ROWS) * GRID_W, GRID_W)
        qs = pl.multiple_of(i * GRID_W, GRID_W)
        pairs = [slice(p * LANES, (p + 1) * LANES) for p in range(NA_HEADS // 2)]
        scores = []
        for p, ls in enumerate(pairs):
            qp = q_ref[0, pl.ds(qs, GRID_W), ls]
            zero = jnp.zeros_like(qp)
            qst = jnp.concatenate([jnp.where(first, qp, zero), jnp.where(first, zero, qp)], axis=0)
            scores.append(_dot_nt(qst, kwin[pl.ds(start, NA_KH * GRID_W), ls]) + bias_ref[oi, p])
        probs, norms = [], []
        for s in scores:
            e = jnp.exp(s - jnp.max(s, axis=-1, keepdims=True))
            norms.append(jnp.sum(e, axis=-1, keepdims=True))
            probs.append(e.astype(BF16))
        for ls, e, l in zip(pairs, probs, norms):
            o = _dot(e, vwin[pl.ds(start, NA_KH * GRID_W), ls]) / l
            o_ref[0, pl.ds(qs, GRID_W), ls] = jnp.where(first, o[:GRID_W], o[GRID_W:]).astype(o_ref.dtype)
        return carry

    lax.fori_loop(0, NA_BLOCK_ROWS, row_body, 0, unroll=True)


def _na(q, k, v, bias):
    b, s, w = q.shape
    rows = s // GRID_W
    nblk = rows // NA_BLOCK_ROWS
    blk = NA_BLOCK_ROWS * GRID_W
    cur = pl.BlockSpec((1, blk, w), lambda bi, j: (bi, j, 0))
    prv = pl.BlockSpec((1, blk, w), lambda bi, j: (bi, jnp.maximum(j - 1, 0), 0))
    nxt = pl.BlockSpec((1, blk, w), lambda bi, j: (bi, jnp.minimum(j + 1, nblk - 1), 0))
    return pl.pallas_call(
        functools.partial(_na_kernel, rows=rows),
        grid=(b, nblk),
        in_specs=[cur, prv, cur, nxt, prv, cur, nxt,
                  pl.BlockSpec(bias.shape, lambda bi, j: (0, 0, 0, 0))],
        out_specs=cur,
        out_shape=jax.ShapeDtypeStruct((b, s, w), BF16),
        scratch_shapes=[pltpu.VMEM((3 * blk, w), BF16), pltpu.VMEM((3 * blk, w), BF16)],
        compiler_params=_cparams(("parallel", "parallel")),
    )(q, k, k, k, v, v, v, bias)


def _rwprep_kernel(rw_ref, prev_ref, next_ref, mup_ref, mun_ref, w0_ref, w2_ref, a0_ref, a2_ref, g2_ref,
                   kk_w_ref, ka_ref, rk_ref,
                   r_ref, v_ref, kk_ref, kd_ref, b_ref, lw_ref, g_ref, bonus_ref, *, nblk):
    i = pl.program_id(1)
    s = rw_ref[0]
    ts = s.shape[0]
    rowi = lax.broadcasted_iota(jnp.int32, (ts, 1), 0)
    halo_p = jnp.where(i > 0, prev_ref[0, 7:8, :], 0.0)
    halo_n = jnp.where(i < nblk - 1, next_ref[0, 0:1, :], 0.0)
    prev = jnp.where(rowi == 0, halo_p, pltpu.roll(s, 1, 0))
    nxt = jnp.where(rowi == ts - 1, halo_n, pltpu.roll(s, ts - 1, 0))
    s = s + mup_ref[...] * (prev - s) + mun_ref[...] * (nxt - s)

    c = RW_WIDTH
    r = s[:, 0:c]
    k = s[:, c:2 * c]
    v = s[:, 2 * c:3 * c]
    lw = s[:, 3 * c:3 * c + 2 * RW_LORA_W]
    la = s[:, 3 * c + 2 * RW_LORA_W:3 * c + 2 * RW_LORA_W + 2 * RW_LORA_A]
    lg = s[:, 3 * c + 2 * RW_LORA_W + 2 * RW_LORA_A:]
    ones = _head_ones(c)

    g_ref[0] = _dot(jax.nn.sigmoid(lg).astype(BF16), g2_ref[...])
    kk = k * kk_w_ref[...]
    nrm = jnp.sqrt(_split_dot(kk * kk, ones))
    kk = kk / jnp.maximum(nrm, 1e-12)
    wl_pre = w0_ref[...] + _dot(jnp.tanh(lw).astype(BF16), w2_ref[...])
    a_all = jax.nn.sigmoid(a0_ref[...] + _dot(la.astype(BF16), a2_ref[...]))
    wl = -jax.nn.softplus(-wl_pre) - 0.5
    logdecay = -jnp.exp(wl)
    r_ref[0] = r
    v_ref[0] = v
    kk_ref[0] = kk
    kd_sum = jnp.zeros_like(k)
    for d in range(2):
        a = a_all[:, d * c:(d + 1) * c]
        kd = k * (1.0 + (a - 1.0) * ka_ref[...])
        kd_ref[d, 0] = kd
        b_ref[d, 0] = kk * a
        lw_ref[d, 0] = logdecay[:, d * c:(d + 1) * c]
        kd_sum = kd_sum + kd
    bonus_ref[0] = _split_dot(r * kd_sum * rk_ref[...], ones) * v


def _blockdiag2(m):
    z = jnp.zeros_like(m[0])
    return jnp.concatenate([jnp.concatenate([m[0], z], 1), jnp.concatenate([z, m[1]], 1)], 0)


def _rwprep(rw, mu_prev, mu_next, w0, w2, a0, a2, g2, k_k, k_a, r_k, ts):
    b, s, pw = rw.shape
    nblk = s // ts
    c = RW_WIDTH
    cur = pl.BlockSpec((1, ts, pw), lambda bi, i: (bi, i, 0))
    prv = pl.BlockSpec((1, 8, pw), lambda bi, i: (bi, jnp.maximum(i * (ts // 8) - 1, 0), 0))
    nxt = pl.BlockSpec((1, 8, pw), lambda bi, i: (bi, jnp.minimum((i + 1) * (ts // 8), s // 8 - 1), 0))
    params = [_row(mu_prev), _row(mu_next), _row(w0), _blockdiag2(w2).astype(BF16), _row(a0),
              _blockdiag2(a2).astype(BF16), g2.astype(BF16), _row(k_k), _row(k_a), _row(r_k)]
    full = lambda a: pl.BlockSpec(a.shape, lambda bi, i: (0,) * a.ndim)
    one = pl.BlockSpec((1, ts, c), lambda bi, i: (bi, i, 0))
    two = pl.BlockSpec((2, 1, ts, c), lambda bi, i: (0, bi, i, 0))
    s1 = jax.ShapeDtypeStruct((b, s, c), F32)
    s2 = jax.ShapeDtypeStruct((2, b, s, c), F32)
    return pl.pallas_call(
        functools.partial(_rwprep_kernel, nblk=nblk),
        grid=(b, nblk),
        in_specs=[cur, prv, nxt] + [full(p) for p in params],
        out_specs=[one, one, one, two, two, two, one, one],
        out_shape=[s1, s1, s1, s2, s2, s2, s1, s1],
        compiler_params=_cparams(("parallel", "parallel")),
    )(rw, rw, rw, *params)


WKV_STEP_CHUNKS = 4


WKV2_BATCH = 4


def _wkv2_kernel(rf, vf, kkf, rb, vb, kkb, kdf, bf, lwf, kdb, bb, lwb, yf_ref, yb_ref, s_ref):
    cc = WKV_CHUNK
    nb = rf.shape[0]
    nchunk = rf.shape[1] // cc

    @pl.when(pl.program_id(1) == 0)
    def _():
        s_ref[...] = jnp.zeros_like(s_ref)

    same_head = ((lax.broadcasted_iota(jnp.int32, (LANES, LANES), 0) < HEAD_DIM)
                 == (lax.broadcasted_iota(jnp.int32, (LANES, LANES), 1) < HEAD_DIM))
    rowc = lax.broadcasted_iota(jnp.int32, (cc, LANES), 0)
    colc = lax.broadcasted_iota(jnp.int32, (cc, LANES), 1) % cc
    eye = (colc == rowc).astype(F32)
    strict = (colc < rowc, colc > rowc)
    incl = (colc <= rowc, colc >= rowc)
    first = lax.broadcasted_iota(jnp.int32, (1, LANES), 1) < HEAD_DIM
    second = jnp.logical_not(first)

    def stack_heads(t):
        z = jnp.zeros_like(t)
        return jnp.concatenate([jnp.where(first, t, z), jnp.where(second, t, z)], axis=0)

    def make_chains(c):
        chains = []
        for bi in range(nb):
            for d in range(2):
                j = nchunk - 1 - c if d else c
                sl = slice(j * cc, (j + 1) * cc)
                src = (rb, vb, kkb, kdb, bb, lwb) if d else (rf, vf, kkf, kdf, bf, lwf)
                for p in range(RW_HEADS // 2):
                    ls = slice(p * LANES, (p + 1) * LANES)
                    chains.append(dict(
                        bi=bi, d=d, p=p, sl=sl, ls=ls,
                        r=src[0][bi, sl, ls], v=src[1][bi, sl, ls], kk=src[2][bi, sl, ls],
                        kd=src[3][0, bi, sl, ls], b=src[4][0, bi, sl, ls], lw=src[5][0, bi, sl, ls]))
        return chains

    def solve_chunks(chains):
        for ch in chains:
            cum = ch["lw"]
            sh = 1
            while sh < cc:
                if ch["d"]:
                    cum = cum + jnp.where(rowc < cc - sh, pltpu.roll(cum, cc - sh, 0), 0.0)
                else:
                    cum = cum + jnp.where(rowc >= sh, pltpu.roll(cum, sh, 0), 0.0)
                sh *= 2
            tot = cum[0:1] if ch["d"] else cum[cc - 1:cc]
            at = (-ch["kk"] * jnp.exp(cum - ch["lw"])).astype(BF16)
            einv = jnp.exp(-cum)
            eh = jnp.exp(tot - cum)
            ch["etot"] = jnp.exp(tot)
            ch["rt"] = (ch["r"] * jnp.exp(cum)).astype(BF16)
            ch["vb"] = ch["v"].astype(BF16)
            ch["v_st"] = stack_heads(ch["vb"])
            ch["at_st"] = stack_heads(at)
            ch["lhs"] = jnp.concatenate([at, ch["rt"]], axis=0)
            ch["rhs"] = jnp.concatenate([stack_heads((ch["b"] * einv).astype(BF16)),
                                         stack_heads((ch["kd"] * einv).astype(BF16))], axis=0)
            ch["kb_hat"] = jnp.concatenate([ch["kd"] * eh, ch["b"] * eh], axis=0).astype(BF16)
        for ch in chains:
            m1 = _dot_nt(ch["lhs"], ch["rhs"])
            d = ch["d"]
            ch["n"] = jnp.where(strict[d], m1[:cc, :LANES], 0.0)
            a_ak = jnp.where(strict[d], m1[:cc, LANES:], 0.0)
            ch["a_rb"] = jnp.where(incl[d], m1[cc:, :LANES], 0.0).astype(BF16)
            a_rk = jnp.where(incl[d], m1[cc:, LANES:], 0.0)
            ch["a_k"] = jnp.concatenate([a_ak, a_rk], axis=0).astype(BF16)
        for ch in chains:
            nbf = ch["n"].astype(BF16)
            ch["pw"] = _dot(nbf, stack_heads(nbf))
            ch["t"] = eye + ch["n"]
        for _ in range(cc.bit_length() - 3):
            for ch in chains:
                pwb = ch["pw"].astype(BF16)
                tp = _dot(jnp.concatenate([ch["t"].astype(BF16), pwb], axis=0), stack_heads(pwb))
                ch["t"] = ch["t"] + tp[:cc]
                ch["pw"] = tp[cc:]
        for ch in chains:
            ch["t"] = (ch["t"] + _dot(ch["t"].astype(BF16), stack_heads(ch["pw"].astype(BF16)))).astype(BF16)
        for ch in chains:
            ch["av"] = _dot(ch["a_k"], ch["v_st"])
        for ch in chains:
            akv = stack_heads(ch["av"][:cc].astype(BF16))
            tx = _dot(ch["t"], jnp.concatenate([ch["at_st"], akv], axis=1))
            ch["atp"] = tx[:, :LANES].astype(BF16)
            ch["u0"] = tx[:, LANES:]

    def advance_state(chains):
        for ch in chains:
            ch["st"] = s_ref[ch["bi"], ch["d"], ch["p"]]
            qs = _dot_nt(jnp.concatenate([ch["atp"], ch["rt"]], axis=0), ch["st"].astype(BF16))
            ch["u"] = (qs[:cc] + ch["u0"]).astype(BF16)
            ch["ys"] = qs[cc:] + ch["av"][cc:]
        for ch in chains:
            upd = _dot_tn(jnp.concatenate([ch["vb"], ch["u"]], axis=0), ch["kb_hat"])
            s_ref[ch["bi"], ch["d"], ch["p"]] = ch["st"] * ch["etot"] + jnp.where(same_head, upd, 0.0)
        for ch in chains:
            y_ref = yb_ref if ch["d"] else yf_ref
            y_ref[ch["bi"], ch["sl"], ch["ls"]] = ch["ys"] + _dot(ch["a_rb"], stack_heads(ch["u"]))

    group = max(nchunk // 2, 1)
    for c0 in range(0, nchunk, group):
        per_chunk = [make_chains(c) for c in range(c0, min(c0 + group, nchunk))]
        solve_chunks([ch for chains in per_chunk for ch in chains])
        for chains in per_chunk:
            advance_state(chains)


def _wkv2(r, v, kk, kd, bb, lw):
    b, s, c = r.shape
    tt = WKV_CHUNK * WKV_STEP_CHUNKS
    nblk = s // tt
    nb = WKV2_BATCH if b % WKV2_BATCH == 0 else 1
    fwd = pl.BlockSpec((nb, tt, c), lambda bi, i: (bi, i, 0))
    bwd = pl.BlockSpec((nb, tt, c), lambda bi, i: (bi, nblk - 1 - i, 0))
    fwd2 = pl.BlockSpec((1, nb, tt, c), lambda bi, i: (0, bi, i, 0))
    bwd2 = pl.BlockSpec((1, nb, tt, c), lambda bi, i: (1, bi, nblk - 1 - i, 0))
    return pl.pallas_call(
        _wkv2_kernel,
        grid=(b // nb, nblk),
        in_specs=[fwd, fwd, fwd, bwd, bwd, bwd, fwd2, fwd2, fwd2, bwd2, bwd2, bwd2],
        out_specs=[fwd, bwd],
        out_shape=[jax.ShapeDtypeStruct((b, s, c), F32)] * 2,
        scratch_shapes=[pltpu.VMEM((nb, 2, RW_HEADS // 2, LANES, LANES), F32)],
        compiler_params=_cparams(("parallel", "arbitrary")),
    )(r, v, kk, r, v, kk, kd, bb, lw, kd, bb, lw)


def _memkv_kernel(mem_ref, nw_ref, w_ref, kw_ref, k_ref, v_ref):
    x = mem_ref[0]
    ms = jnp.mean(x * x, axis=-1, keepdims=True)
    h = (x * lax.rsqrt(ms + RMS_EPS) * nw_ref[...]).astype(BF16)
    kv = _dot(h, w_ref[...])
    k = kv[:, :MEM_WIDTH]
    ms = _split_dot(k * k, _head_ones(MEM_WIDTH)) * (1.0 / HEAD_DIM)
    k_ref[0] = (k * lax.rsqrt(ms + RMS_EPS) * kw_ref[...]).astype(BF16)
    v_ref[0] = kv[:, MEM_WIDTH:].astype(BF16)


def _memkv(mem, nw, w_kv, kw):
    b, m, d = mem.shape
    full = lambda a: pl.BlockSpec(a.shape, lambda bi: (0,) * a.ndim)
    out = pl.BlockSpec((1, m, MEM_WIDTH), lambda bi: (bi, 0, 0))
    return pl.pallas_call(
        _memkv_kernel,
        grid=(b,),
        in_specs=[pl.BlockSpec((1, m, d), lambda bi: (bi, 0, 0)), full(nw), full(w_kv), full(kw)],
        out_specs=[out, out],
        out_shape=[jax.ShapeDtypeStruct((b, m, MEM_WIDTH), BF16)] * 2,
        compiler_params=_cparams(("parallel",)),
    )(mem, nw, w_kv, kw)


MIX_PARTS = 2


def _mixout_kernel(x_ref, na_ref, yf_ref, yb_ref, bonus_ref, g_ref, mq_ref, mk_ref, mv_ref,
                   lnw_ref, lnb_ref, wo_ref, fnw_ref, whi_ref, wlo_ref, rb_ref,
                   o_ref, h_ref, route_ref, nch_ref):
    part = x_ref.shape[1] // MIX_PARTS
    parts = [slice(j * part, (j + 1) * part) for j in range(MIX_PARTS)]
    ones = _head_ones(RW_WIDTH)
    y_rws = []
    for rows in parts:
        y = yf_ref[0, rows, :] + yb_ref[0, rows, :]
        mu = _split_dot(y, ones) * (1.0 / HEAD_DIM)
        yc = y - mu
        var = _split_dot(yc * yc, ones) * (1.0 / HEAD_DIM)
        yn = yc * lax.rsqrt(var + RW_GN_EPS) * lnw_ref[...] + lnb_ref[...]
        y_rws.append(((yn + bonus_ref[0, rows, :]) * g_ref[0, rows, :]).astype(BF16))

    first = lax.broadcasted_iota(jnp.int32, (1, LANES), 1) < HEAD_DIM
    mems = [[] for _ in parts]
    for p in range(MEM_HEADS // 2):
        ls = slice(p * LANES, (p + 1) * LANES)
        kp = mk_ref[0, :, ls]
        vp = mv_ref[0, :, ls]
        for j, rows in enumerate(parts):
            qp = mq_ref[0, rows, ls]
            outs = []
            for hh in range(2):
                sel = first if hh == 0 else jnp.logical_not(first)
                s = _dot_nt(jnp.where(sel, qp, jnp.zeros_like(qp)), kp)
                m = jnp.max(s, axis=-1, keepdims=True)
                e = jnp.exp(s - m)
                l = jnp.sum(e, axis=-1, keepdims=True)
                outs.append(_dot(e.astype(BF16), vp) / l)
            mems[j].append(jnp.where(first, outs[0], outs[1]).astype(BF16))

    x_mids = []
    for j, rows in enumerate(parts):
        acc = _dot(na_ref[0, rows, :], wo_ref[0:NA_WIDTH, :])
        acc = acc + _dot(y_rws[j], wo_ref[NA_WIDTH:NA_WIDTH + RW_WIDTH, :])
        acc = acc + _dot(jnp.concatenate(mems[j], axis=1), wo_ref[NA_WIDTH + RW_WIDTH:, :])
        x_mids.append(x_ref[0, rows, :] + acc)
    for j, rows in enumerate(parts):
        o_ref[0, rows, :] = x_mids[j]
    for j, rows in enumerate(parts):
        _route_tile(x_mids[j], rows, j * (part // MOE_SUB), fnw_ref, whi_ref, wlo_ref, rb_ref,
                    h_ref, route_ref, nch_ref)


def _mixout(x, y_na, y_f, y_b, bonus, g, mq, mk, mv, ln_w, ln_b, w_out, ffn_nw, router, tm):
    b, s, d = x.shape
    m = mk.shape[1]
    sub = tm // MOE_SUB
    tok = lambda w: pl.BlockSpec((1, tm, w), lambda bi, i: (bi, i, 0))
    full = lambda a: pl.BlockSpec(a.shape, lambda bi, i: (0,) * a.ndim)
    memb = pl.BlockSpec((1, m, MEM_WIDTH), lambda bi, i: (bi, 0, 0))
    x_mid, h2, route, nch = pl.pallas_call(
        _mixout_kernel,
        grid=(b, s // tm),
        in_specs=[tok(d), tok(NA_WIDTH), tok(RW_WIDTH), tok(RW_WIDTH), tok(RW_WIDTH), tok(RW_WIDTH),
                  tok(MEM_WIDTH), memb, memb, full(ln_w), full(ln_b), full(w_out), full(ffn_nw)]
        + [full(p) for p in router],
        out_specs=[tok(d), tok(d),
                   pl.BlockSpec((1, sub, ROUTE_ROWS, MOE_SUB), lambda bi, i: (bi, i, 0, 0)),
                   pl.BlockSpec((1, sub, 1, ROUTER_LANES), lambda bi, i: (bi, i, 0, 0))],
        out_shape=[jax.ShapeDtypeStruct((b, s, d), F32), jax.ShapeDtypeStruct((b, s, d), BF16),
                   jax.ShapeDtypeStruct((b, s // MOE_SUB, ROUTE_ROWS, MOE_SUB), F32),
                   jax.ShapeDtypeStruct((b, s // MOE_SUB, 1, ROUTER_LANES), F32)],
        compiler_params=_cparams(("parallel", "parallel")),
    )(x, y_na, y_f, y_b, bonus, g, mq, mk, mv, ln_w, ln_b, w_out, ffn_nw, *router)
    ns = b * s // MOE_SUB
    return (x_mid, h2.reshape(b * s, d), route.reshape(ns, ROUTE_ROWS, MOE_SUB),
            nch.reshape(ns, 1, ROUTER_LANES))


ROUTER_LANES = 128


MOE_SUB = 256
MOE_CHUNK = 16
MOE_LOCAL_CHUNKS = 48
MOE_BLOCK_CHUNKS = 32
MOE_STEP_BLOCKS = 2
MOE_STEP_SUBS = 4
MOE_TOP_K = 2
ROUTE_ROWS = 8
assert MOE_LOCAL_CHUNKS >= MOE_TOP_K * MOE_SUB // MOE_CHUNK + MOE_EXPERTS - 1


def _route_subtile(lg):
    t = lg.shape[0]
    lane = lax.broadcasted_iota(jnp.int32, (1, ROUTER_LANES), 1)
    lanef = lane.astype(F32)
    big = float(ROUTER_LANES)

    def argmax_lane(v):
        m = jnp.max(v, axis=-1, keepdims=True)
        return m, jnp.min(jnp.where(v == m, lanef, big), axis=-1, keepdims=True)

    is_group = lane < MOE_GROUPS
    gmax, gidx = argmax_lane(jnp.where(is_group, lg, NEG_BIG))
    gsum = jnp.sum(jnp.where(is_group, jnp.exp(lg - gmax), 0.0), axis=-1, keepdims=True)
    lo = MOE_GROUPS + MOE_EXPERTS_PER_GROUP * gidx
    el = jnp.where((lanef >= lo) & (lanef < lo + MOE_EXPERTS_PER_GROUP), lg, NEG_BIG)
    m1, i1 = argmax_lane(el)
    m2, i2 = argmax_lane(jnp.where(lanef == i1, NEG_BIG, el))
    ratio = jnp.exp(m2 - m1)
    w0 = 1.0 / (gsum * (1.0 + ratio))
    w1 = w0 * ratio

    oh0 = (lanef == i1 - MOE_GROUPS).astype(F32)
    oh1 = (lanef == i2 - MOE_GROUPS).astype(F32)
    row = lax.broadcasted_iota(jnp.int32, (t, t), 0)
    col = lax.broadcasted_iota(jnp.int32, (t, t), 1)
    before = (col < row).astype(BF16)
    cnt0 = jnp.sum(oh0, axis=0, keepdims=True)
    cnt1 = jnp.sum(oh1, axis=0, keepdims=True)
    rank0 = _dot(before, oh0.astype(BF16))
    rank1 = _dot(before, oh1.astype(BF16)) + cnt0
    nch = jnp.floor((cnt0 + cnt1 + (MOE_CHUNK - 1.0)) * (1.0 / MOE_CHUNK))
    li = lax.broadcasted_iota(jnp.int32, (ROUTER_LANES, ROUTER_LANES), 0)
    lj = lax.broadcasted_iota(jnp.int32, (ROUTER_LANES, ROUTER_LANES), 1)
    start = _dot(jnp.broadcast_to(nch, (8, ROUTER_LANES)).astype(BF16), (li < lj).astype(BF16))[0:1]
    pos0 = jnp.sum((start * MOE_CHUNK + rank0) * oh0, axis=-1, keepdims=True)
    pos1 = jnp.sum((start * MOE_CHUNK + rank1) * oh1, axis=-1, keepdims=True)
    rec = (jnp.where(lane == 0, pos0, 0.0) + jnp.where(lane == 1, pos1, 0.0)
           + jnp.where(lane == 2, w0, 0.0) + jnp.where(lane == 3, w1, 0.0))
    return jnp.transpose(rec)[0:ROUTE_ROWS], nch


def _route_tile(x, rows, sub0, nw_ref, whi_ref, wlo_ref, b_ref, h_ref, route_ref, nch_ref):
    ms = jnp.mean(x * x, axis=-1, keepdims=True)
    h = x * lax.rsqrt(ms + RMS_EPS) * nw_ref[...]
    hi = h.astype(BF16)
    lo = (h - hi.astype(F32)).astype(BF16)
    h_ref[0, rows, :] = hi
    lg = _dot(hi, whi_ref[...]) + _dot(hi, wlo_ref[...]) + _dot(lo, whi_ref[...]) + b_ref[...]
    for j in range(x.shape[0] // MOE_SUB):
        route_ref[0, sub0 + j], nch_ref[0, sub0 + j] = _route_subtile(lg[j * MOE_SUB:(j + 1) * MOE_SUB])


def _router_params(w_group, b_group, w_expert, b_expert):
    d = w_group.shape[0]
    pad = ROUTER_LANES - MOE_GROUPS - MOE_EXPERTS
    w = jnp.concatenate([w_group, w_expert, jnp.zeros((d, pad), F32)], axis=1)
    whi = w.astype(BF16)
    wlo = (w - whi.astype(F32)).astype(BF16)
    bias = jnp.concatenate([b_group, b_expert, jnp.zeros((pad,), F32)]).reshape(1, -1)
    return whi, wlo, bias


def _moe_plan(nch, n):
    nch = nch[:, 0, :MOE_EXPERTS].astype(jnp.int32)
    lc_end = jnp.cumsum(nch, axis=1)
    lc_start = lc_end - nch
    nloc = lc_end[:, -1]
    nblk = (jnp.sum(nch, axis=0) + MOE_BLOCK_CHUNKS - 1) // MOE_BLOCK_CHUNKS
    bend = jnp.cumsum(nblk)
    gch = ((bend - nblk) * MOE_BLOCK_CHUNKS)[None, :] + jnp.cumsum(nch, axis=0) - nch
    c = jnp.arange(MOE_LOCAL_CHUNKS, dtype=jnp.int32)
    in_run = (c[None, :, None] >= lc_start[:, None, :]) & (c[None, :, None] < lc_end[:, None, :])
    dest = jnp.sum(jnp.where(in_run, (gch - lc_start)[:, None, :] + c[None, :, None], 0), axis=-1)

    nblocks = _moe_blocks(n)
    b = jnp.arange(nblocks, dtype=jnp.int32)
    block_e = jnp.minimum(jnp.sum(b[:, None] >= bend[None, :], axis=-1), MOE_EXPERTS - 1)
    tot = jnp.sum(nch, axis=0)
    tail_start = (bend - nblk) * MOE_BLOCK_CHUNKS + tot
    ntail = nblk * MOE_BLOCK_CHUNKS - tot
    i32 = lambda a: a.astype(jnp.int32)
    return i32(dest), i32(nloc), i32(block_e), i32(bend[-1:]), i32(tail_start), i32(ntail)


def _moe_blocks(n):
    ns = n // MOE_SUB
    chunks = ns * (MOE_TOP_K * MOE_SUB // MOE_CHUNK + MOE_EXPERTS - 1) + MOE_EXPERTS * (MOE_BLOCK_CHUNKS - 1)
    per_step = MOE_BLOCK_CHUNKS * MOE_STEP_BLOCKS
    return -(-chunks // per_step) * MOE_STEP_BLOCKS


def _local_onehot(pos_rows, shape, row_axis):
    idx = lax.broadcasted_iota(jnp.int32, shape, row_axis)
    return (idx == pos_rows[0]) | (idx == pos_rows[1])


def _dispatch_kernel(dest_ref, nloc_ref, tail_ref, ntail_ref, nv_ref, h_ref, pos_ref, xs_ref,
                     buf, zbuf, sem, zsem):
    s = pl.program_id(0)
    ns = pl.num_programs(0)
    slot = s % 2
    bm = MOE_BLOCK_CHUNKS * MOE_CHUNK
    nblocks = xs_ref.shape[0] // bm

    def zero_fill(wait):
        def tail_copy(e, i):
            dst = xs_ref.at[pl.ds(pl.multiple_of((tail_ref[e] + i) * MOE_CHUNK, MOE_CHUNK), MOE_CHUNK)]
            return pltpu.make_async_copy(zbuf.at[pl.ds(0, MOE_CHUNK)], dst, zsem.at[0])

        def block_copy(b):
            return pltpu.make_async_copy(zbuf, xs_ref.at[pl.ds(pl.multiple_of(b * bm, bm), bm)], zsem.at[1])

        def run(cp):
            return cp.wait() if wait else cp.start()

        for e in range(MOE_EXPERTS):
            def tail_body(i, carry, e=e):
                run(tail_copy(e, i))
                return carry
            lax.fori_loop(0, ntail_ref[e], tail_body, 0)

        def block_body(b, carry):
            run(block_copy(b))
            return carry
        lax.fori_loop(nv_ref[0], nblocks, block_body, 0)

    @pl.when(s == 0)
    def _():
        zbuf[...] = jnp.zeros_like(zbuf)
        zero_fill(wait=False)

    def chunk_copy(sl, k, c, sub):
        src = buf.at[sl, k, pl.ds(pl.multiple_of(c * MOE_CHUNK, MOE_CHUNK), MOE_CHUNK)]
        dst = xs_ref.at[pl.ds(pl.multiple_of(dest_ref[sub, c] * MOE_CHUNK, MOE_CHUNK), MOE_CHUNK)]
        return pltpu.make_async_copy(src, dst, sem.at[sl])

    def for_chunks(step, sl, wait):
        for k in range(MOE_STEP_SUBS):
            sub = step * MOE_STEP_SUBS + k

            def body(c, carry, k=k, sub=sub):
                cp = chunk_copy(sl, k, c, sub)
                cp.wait() if wait else cp.start()
                return carry
            lax.fori_loop(0, nloc_ref[sub], body, 0)

    def wait_step(step, sl):
        for_chunks(step, sl, wait=True)

    @pl.when(s >= 2)
    def _():
        wait_step(s - 2, slot)

    onehots = []
    for k in range(MOE_STEP_SUBS):
        pos = pos_ref[k, 0:MOE_TOP_K].astype(jnp.int32)
        onehot = _local_onehot((pos[0:1], pos[1:2]), (MOE_LOCAL_CHUNKS * MOE_CHUNK, MOE_SUB), 0)
        onehots.append(jnp.where(onehot, 1.0, 0.0).astype(BF16))
    for k in range(MOE_STEP_SUBS):
        buf[slot, k] = _dot(onehots[k], h_ref[k * MOE_SUB:(k + 1) * MOE_SUB, :]).astype(BF16)
    for_chunks(s, slot, wait=False)

    @pl.when(s == ns - 1)
    def _():
        @pl.when(s >= 1)
        def _():
            wait_step(s - 1, 1 - slot)
        wait_step(s, slot)
        zero_fill(wait=True)


def _dispatch(h2, pos, dest, nloc, tail_start, ntail, nvalid):
    n, d = h2.shape
    ns = n // MOE_SUB
    bm = MOE_BLOCK_CHUNKS * MOE_CHUNK
    rows = MOE_LOCAL_CHUNKS * MOE_CHUNK
    return pl.pallas_call(
        _dispatch_kernel,
        grid_spec=pltpu.PrefetchScalarGridSpec(
            num_scalar_prefetch=5, grid=(ns // MOE_STEP_SUBS,),
            in_specs=[pl.BlockSpec((MOE_STEP_SUBS * MOE_SUB, d), lambda s, *_: (s, 0)),
                      pl.BlockSpec((MOE_STEP_SUBS, ROUTE_ROWS, MOE_SUB), lambda s, *_: (s, 0, 0))],
            out_specs=pl.BlockSpec(memory_space=pl.ANY),
            scratch_shapes=[pltpu.VMEM((2, MOE_STEP_SUBS, rows, d), BF16), pltpu.VMEM((bm, d), BF16),
                            pltpu.SemaphoreType.DMA((2,)), pltpu.SemaphoreType.DMA((2,))]),
        out_shape=jax.ShapeDtypeStruct((_moe_blocks(n) * bm, d), BF16),
        compiler_params=_cparams(("arbitrary",)),
    )(dest, nloc, tail_start, ntail, nvalid, h2, pos)


def _experts_kernel(be_ref, nv_ref, x_ref, *refs):
    del be_ref
    w_refs, o_ref = refs[:-1], refs[-1]
    bm = MOE_BLOCK_CHUNKS * MOE_CHUNK
    valid = pl.program_id(0) * MOE_STEP_BLOCKS < nv_ref[0]

    @pl.when(valid)
    def _():
        rows = [slice(j * bm, (j + 1) * bm) for j in range(MOE_STEP_BLOCKS)]
        gated = []
        for j, r in enumerate(rows):
            x = x_ref[r, :]
            gated.append((_dot(x, w_refs[3 * j][0]), _dot(x, w_refs[3 * j + 1][0])))
        hids = [(jax.nn.silu(gate) * up).astype(BF16) for gate, up in gated]
        for j, r in enumerate(rows):
            o_ref[r, :] = _dot(hids[j], w_refs[3 * j + 2][0]).astype(o_ref.dtype)

    @pl.when(jnp.logical_not(valid))
    def _():
        o_ref[...] = jnp.zeros_like(o_ref)


def _experts(xs, block_e, nvalid, w_gate, w_up, w_down):
    cap, d = xs.shape
    rows = MOE_STEP_BLOCKS * MOE_BLOCK_CHUNKS * MOE_CHUNK
    step = lambda p, be, nv: (jnp.minimum(p, (nv[0] - 1) // MOE_STEP_BLOCKS), 0)
    w_specs, w_args = [], []
    for j in range(MOE_STEP_BLOCKS):
        wsel = lambda p, be, nv, j=j: (be[jnp.minimum(p * MOE_STEP_BLOCKS + j, nv[0] - 1)], 0, 0)
        w_specs += [pl.BlockSpec((1, d, MOE_D_FF), wsel), pl.BlockSpec((1, d, MOE_D_FF), wsel),
                    pl.BlockSpec((1, MOE_D_FF, d), wsel)]
        w_args += [w_gate, w_up, w_down]
    return pl.pallas_call(
        _experts_kernel,
        grid_spec=pltpu.PrefetchScalarGridSpec(
            num_scalar_prefetch=2, grid=(cap // rows,),
            in_specs=[pl.BlockSpec((rows, d), step)] + w_specs,
            out_specs=pl.BlockSpec((rows, d), lambda p, be, nv: (p, 0))),
        out_shape=jax.ShapeDtypeStruct((cap, d), BF16),
        compiler_params=_cparams(("arbitrary",)),
    )(block_e, nvalid, xs, *w_args)


def _combine_kernel(dest_ref, nloc_ref, x_ref, route_ref, ys_ref, o_ref, buf, sem):
    s = pl.program_id(0)
    ns = pl.num_programs(0)
    slot = s % 2

    def chunk_copy(sl, k, c, sub):
        src = ys_ref.at[pl.ds(pl.multiple_of(dest_ref[sub, c] * MOE_CHUNK, MOE_CHUNK), MOE_CHUNK)]
        dst = buf.at[sl, k, pl.ds(pl.multiple_of(c * MOE_CHUNK, MOE_CHUNK), MOE_CHUNK)]
        return pltpu.make_async_copy(src, dst, sem.at[sl])

    def for_chunks(step, sl, wait):
        for k in range(MOE_STEP_SUBS):
            sub = step * MOE_STEP_SUBS + k

            def body(c, carry, k=k, sub=sub):
                cp = chunk_copy(sl, k, c, sub)
                cp.wait() if wait else cp.start()
                return carry
            lax.fori_loop(0, nloc_ref[sub], body, 0)

    @pl.when(s == 0)
    def _():
        buf[...] = jnp.zeros_like(buf)
        for_chunks(0, 0, wait=False)

    @pl.when(s + 1 < ns)
    def _():
        for_chunks(s + 1, 1 - slot, wait=False)

    for_chunks(s, slot, wait=True)

    idx = lax.broadcasted_iota(jnp.int32, (MOE_SUB, MOE_LOCAL_CHUNKS * MOE_CHUNK), 1)
    splits = []
    for k in range(MOE_STEP_SUBS):
        rec = jnp.transpose(route_ref[k])
        pos = rec[:, 0:MOE_TOP_K].astype(jnp.int32)
        pw = (jnp.where(idx == pos[:, 0:1], rec[:, 2:3], 0.0)
              + jnp.where(idx == pos[:, 1:2], rec[:, 3:4], 0.0))
        p_hi = pw.astype(BF16)
        splits.append((p_hi, (pw - p_hi.astype(F32)).astype(BF16)))
    for k, (p_hi, p_lo) in enumerate(splits):
        rows = slice(k * MOE_SUB, (k + 1) * MOE_SUB)
        o_ref[rows, :] = x_ref[rows, :] + _dot(p_hi, buf[slot, k]) + _dot(p_lo, buf[slot, k])


def _combine(x2, ys, route, dest, nloc):
    n, d = x2.shape
    ns = n // MOE_SUB
    rows = MOE_LOCAL_CHUNKS * MOE_CHUNK
    return pl.pallas_call(
        _combine_kernel,
        grid_spec=pltpu.PrefetchScalarGridSpec(
            num_scalar_prefetch=2, grid=(ns // MOE_STEP_SUBS,),
            in_specs=[pl.BlockSpec((MOE_STEP_SUBS * MOE_SUB, d), lambda s, *_: (s, 0)),
                      pl.BlockSpec((MOE_STEP_SUBS, ROUTE_ROWS, MOE_SUB), lambda s, *_: (s, 0, 0)),
                      pl.BlockSpec(memory_space=pl.ANY)],
            out_specs=pl.BlockSpec((MOE_STEP_SUBS * MOE_SUB, d), lambda s, *_: (s, 0)),
            scratch_shapes=[pltpu.VMEM((2, MOE_STEP_SUBS, rows, d), BF16), pltpu.SemaphoreType.DMA((2,))]),
        out_shape=jax.ShapeDtypeStruct((n, d), F32),
        compiler_params=_cparams(("arbitrary",)),
    )(dest, nloc, x2, route, ys)


def _moe(x2, h2, route, nch, w_gate, w_up, w_down):
    dest, nloc, block_e, nvalid, tail_start, ntail = _moe_plan(nch, x2.shape[0])
    xs = _dispatch(h2, route, dest, nloc, tail_start, ntail, nvalid)
    ys = _experts(xs, block_e, nvalid, w_gate, w_up, w_down)
    return _combine(x2, ys, route, dest, nloc)


def _tile(n, want):
    t = min(n, want)
    assert n % t == 0
    return t


def kernel(x, mem, attn_norm_w, w_in, na_q_norm_w, na_k_norm_w, na_rpb, rw_mu_prev, rw_mu_next, rw_w0, rw_w2, rw_a0, rw_a2, rw_g2, rw_k_k, rw_k_a, rw_r_k, rw_ln_w, rw_ln_b, mem_norm_w, w_mem_kv, mem_q_norm_w, mem_k_norm_w, w_out, ffn_norm_w, moe_w_group, moe_b_group, moe_w_expert, moe_b_expert, moe_w_gate, moe_w_up, moe_w_down):
    b, s, d = x.shape
    n = b * s
    depth = w_in.shape[0]
    assert s % (NA_BLOCK_ROWS * GRID_W) == 0 and s // GRID_W >= 2 * NA_KH
    tm = _tile(n, 512)
    ts = _tile(s, 512)
    for l in range(depth):
        q, k, v, rw, mq = _proj(
            x.reshape(n, d), _row(attn_norm_w[l]), w_in[l].astype(BF16),
            _row(jnp.tile(na_q_norm_w[l], NA_HEADS)), _row(jnp.tile(na_k_norm_w[l], NA_HEADS)),
            _row(jnp.tile(mem_q_norm_w[l], MEM_HEADS)), tm)
        y_na = _na(q.reshape(b, s, NA_WIDTH), k.reshape(b, s, NA_WIDTH), v.reshape(b, s, NA_WIDTH),
                   _na_bias_table(na_rpb[l]))
        r, vv, kk, kd, bb, lw, g, bonus = _rwprep(
            rw.reshape(b, s, RW_PROJ), rw_mu_prev[l], rw_mu_next[l], rw_w0[l], rw_w2[l], rw_a0[l],
            rw_a2[l], rw_g2[l], rw_k_k[l], rw_k_a[l], rw_r_k[l], ts)
        y_f, y_b = _wkv2(r, vv, kk, kd, bb, lw)
        mk, mv = _memkv(mem, _row(mem_norm_w[l]), w_mem_kv[l].astype(BF16),
                        _row(jnp.tile(mem_k_norm_w[l], MEM_HEADS)))
        x, h2, route, nch = _mixout(
            x, y_na, y_f, y_b, bonus, g, mq.reshape(b, s, MEM_WIDTH), mk, mv,
            _row(rw_ln_w[l]), _row(rw_ln_b[l]), w_out[l].astype(BF16), _row(ffn_norm_w[l]),
            _router_params(moe_w_group[l], moe_b_group[l], moe_w_expert[l], moe_b_expert[l]),
            _tile(s, MIX_PARTS * 512))
        x = _moe(x.reshape(n, d), h2, route, nch, moe_w_gate[l].astype(BF16),
                 moe_w_up[l].astype(BF16), moe_w_down[l].astype(BF16)).reshape(b, s, d)
    return x
```
